```python
import math
import jax, jax.numpy as jnp
from jax import lax
import numpy as np

D_MODEL = 2048
BATCH = 4
SEQ = 4096
DEPTH = 4

GRID_W = 64
CTX_LEN = 256
N_MIXERS = 2
NORM_EPS = 1e-6
NEG_INF = -1e30

NA_HEADS = 16
NA_HEAD_DIM = D_MODEL // NA_HEADS
NA_WIN_ROWS = 8
NA_WIN_COLS = 16

HYENA_ORDER = 2
HYENA_PROJ = HYENA_ORDER + 1
HYENA_SHORT = 3
HYENA_EMB = 33
HYENA_FILTER_HIDDEN = 64
HYENA_FAST_DECAY = 0.3
HYENA_SLOW_DECAY = 1.5
HYENA_TARGET = 1e-2

N_EXPERTS = 16
EXPERT_HIDDEN = D_MODEL // 2
CAPACITY_FACTOR = 2

kernel_name = "hybrid_natten_hyena_ecmoe_block"


def _layer_count(m):
    return sum(1 for i in range(DEPTH) if i % N_MIXERS == m)


def rmsnorm(x, g):
    xf = x.astype(jnp.float32)
    y = xf * lax.rsqrt(jnp.mean(xf * xf, axis=-1, keepdims=True) + NORM_EPS)
    return (y * g.astype(jnp.float32)).astype(x.dtype)


def adaln(cond, w, b):
    m = jnp.einsum('bd,de->be', jax.nn.silu(cond), w) + b
    return [t[:, None, :] for t in jnp.split(m, 6, axis=-1)]


def modulate(x, shift, scale):
    return x * (1 + scale) + shift


def _col_window():
    cols = np.arange(GRID_W)
    cs = np.clip(cols - NA_WIN_COLS // 2, 0, GRID_W - NA_WIN_COLS)
    in_win = (cols[None, :] >= cs[:, None]) & (cols[None, :] < cs[:, None] + NA_WIN_COLS)
    dc = np.clip(cols[None, :] - cols[:, None] + NA_WIN_COLS - 1, 0, 2 * NA_WIN_COLS - 2)
    return in_win, dc


def neighbourhood_attention(a, ac, w_qkv, w_o, q_g, k_g, rpb, with_ctx_out):
    B, N, D = a.shape
    rows = N // GRID_W
    kh = min(NA_WIN_ROWS, rows)
    scale = NA_HEAD_DIM ** -0.5

    def project(t):
        p = jnp.einsum('bnd,de->bne', t, w_qkv).reshape(t.shape[0], t.shape[1], 3, NA_HEADS, NA_HEAD_DIM)
        return rmsnorm(p[:, :, 0], q_g), rmsnorm(p[:, :, 1], k_g), p[:, :, 2]

    q, k, v = project(a)
    qc, kc, vc = project(ac)
    grid = (B, rows, GRID_W, NA_HEADS, NA_HEAD_DIM)
    q_grid, k_grid, v_grid = q.reshape(grid), k.reshape(grid), v.reshape(grid)

    in_win, dc = _col_window()
    band_mask = np.broadcast_to(in_win[:, None, :], (GRID_W, kh, GRID_W)).reshape(GRID_W, kh * GRID_W)
    rpb_f = rpb.astype(jnp.float32)
    ctx_bias = jnp.zeros((NA_HEADS, GRID_W, kc.shape[1]), jnp.float32)

    def row_block(r):
        rs = jnp.clip(r - kh // 2, 0, rows - kh)
        q_r = lax.dynamic_index_in_dim(q_grid, r, axis=1, keepdims=False)
        k_band = lax.dynamic_slice_in_dim(k_grid, rs, kh, axis=1).reshape(B, kh * GRID_W, NA_HEADS, NA_HEAD_DIM)
        v_band = lax.dynamic_slice_in_dim(v_grid, rs, kh, axis=1).reshape(B, kh * GRID_W, NA_HEADS, NA_HEAD_DIM)
        k_all = jnp.concatenate([k_band, kc], axis=1)
        v_all = jnp.concatenate([v_band, vc], axis=1)
        dr = rs + jnp.arange(kh) - r + NA_WIN_ROWS - 1
        bias = rpb_f[:, dr][:, :, dc]
        bias = bias.transpose(0, 2, 1, 3).reshape(NA_HEADS, GRID_W, kh * GRID_W)
        bias = jnp.concatenate([jnp.where(band_mask, bias, NEG_INF), ctx_bias], axis=-1)
        s = jnp.einsum('bqhd,bkhd->bhqk', q_r, k_all).astype(jnp.float32) * scale + bias
        p = jax.nn.softmax(s, axis=-1).astype(v_all.dtype)
        return jnp.einsum('bhqk,bkhd->bqhd', p, v_all)

    o = lax.map(row_block, jnp.arange(rows))
    o = o.transpose(1, 0, 2, 3, 4).reshape(B, N, D)
    y = jnp.einsum('bnd,de->bne', o, w_o)
    yc = None
    if with_ctx_out:
        s = jnp.einsum('bqhd,bkhd->bhqk', qc, kc).astype(jnp.float32) * scale
        p = jax.nn.softmax(s, axis=-1).astype(vc.dtype)
        oc = jnp.einsum('bhqk,bkhd->bqhd', p, vc).reshape(ac.shape[0], ac.shape[1], D)
        yc = jnp.einsum('bnd,de->bne', oc, w_o)
    return y, yc


def hyena_filter(L, w1, b1, f1, w2, b2, f2, w3, b3, f3, w_out):
    t01 = jnp.linspace(0.0, 1.0, L, dtype=jnp.float32)[:, None]
    bands = (HYENA_EMB - 1) // 2
    w = 2 * math.pi * jnp.arange(L, dtype=jnp.float32)[:, None] / L
    f = jnp.linspace(1e-4, bands - 1, bands, dtype=jnp.float32)[None, :]
    z = jnp.concatenate([t01, jnp.cos(f * w), -jnp.sin(f * w)], axis=-1)
    h = jnp.sin(f1 * (z @ w1 + b1))
    h = jnp.sin(f2 * (h @ w2 + b2))
    h = jnp.sin(f3 * (h @ w3 + b3))
    h = (h @ w_out).astype(jnp.float32)
    d = h.shape[-1] // 2
    max_decay = math.log(HYENA_TARGET) / HYENA_FAST_DECAY
    min_decay = math.log(HYENA_TARGET) / HYENA_SLOW_DECAY
    deltas = jnp.abs(jnp.linspace(min_decay, max_decay, d, dtype=jnp.float32))
    decay = jnp.exp(-t01 * deltas[None, :])
    h_fwd = h[:, :d] * decay
    h_bwd = h[:, d:] * decay
    k = jnp.concatenate([h_fwd, jnp.zeros((1, d), jnp.float32), h_bwd[:0:-1]], axis=0)
    return k / jnp.sum(jnp.abs(k), axis=0, keepdims=True)


def short_conv(p, w, b):
    L = p.shape[1]
    pad = HYENA_SHORT // 2
    pp = jnp.pad(p, ((0, 0), (pad, HYENA_SHORT - 1 - pad), (0, 0)))
    return sum(pp[:, j:j + L] * w[j] for j in range(HYENA_SHORT)) + b


def hyena(a, w_in, b_in, conv_w, conv_b, fw1, fb1, ff1, fw2, fb2, ff2, fw3, fb3, ff3, fwout, bias, w_out, b_out):
    B, L, D = a.shape
    p = short_conv(jnp.einsum('bld,de->ble', a, w_in) + b_in, conv_w, conv_b)
    x0, x1, v = jnp.split(p, HYENA_PROJ, axis=-1)
    v = v * x1
    k = hyena_filter(L, fw1, fb1, ff1, fw2, fb2, ff2, fw3, fb3, ff3, fwout)
    vf = jnp.fft.rfft(v.astype(jnp.float32), n=2 * L, axis=1)
    kf = jnp.fft.rfft(k, n=2 * L, axis=0)
    y = jnp.fft.irfft(vf * kf[None], n=2 * L, axis=1)[:, :L].astype(a.dtype)
    y = (y + v * bias) * x0
    return jnp.einsum('bld,de->ble', y, w_out) + b_out


def ec_moe(a, router, w1, w3, w2):
    B, N, D = a.shape
    cap = CAPACITY_FACTOR * N // N_EXPERTS
    aff = jax.nn.softmax(jnp.einsum('bnd,de->bne', a, router).astype(jnp.float32), axis=-1)
    gate, idx = lax.top_k(jnp.swapaxes(aff, 1, 2), cap)
    xs = jax.vmap(lambda ab, ib: ab[ib])(a, idx)
    hid = jax.nn.silu(jnp.einsum('becd,edf->becf', xs, w1)) * jnp.einsum('becd,edf->becf', xs, w3)
    ys = jnp.einsum('becf,efd->becd', hid, w2) * gate[..., None].astype(a.dtype)
    return jax.vmap(lambda ib, yb: jnp.zeros((N, D), yb.dtype).at[ib.reshape(-1)].add(yb.reshape(-1, D)))(idx, ys)


def setup_inputs(seed: int = 0) -> dict:
    key = jax.random.key(seed)
    ks = iter(jax.random.split(key, 48))
    D = D_MODEL
    nA, nB = _layer_count(0), _layer_count(1)

    def nrm(shape, scale):
        return jax.random.normal(next(ks), shape, jnp.float32) * scale

    def ones_noise(shape):
        return 1.0 + nrm(shape, 0.02)

    FH = HYENA_FILTER_HIDDEN
    return {
        'x': nrm((BATCH, SEQ, D), 1.0),
        'c': nrm((BATCH, D), 1.0),
        'ctx': nrm((BATCH, CTX_LEN, D), 1.0),
        'c_ctx': nrm((D,), 1.0),
        'ada_w': nrm((DEPTH, D, 6 * D), D ** -0.5),
        'ada_b': nrm((DEPTH, 6 * D), 0.02),
        'norm1_g': ones_noise((DEPTH, D)),
        'norm2_g': ones_noise((DEPTH, D)),
        'na_w_qkv': nrm((nA, D, 3 * D), D ** -0.5),
        'na_w_o': nrm((nA, D, D), D ** -0.5),
        'na_q_g': ones_noise((nA, NA_HEAD_DIM)),
        'na_k_g': ones_noise((nA, NA_HEAD_DIM)),
        'na_rpb': nrm((nA, NA_HEADS, 2 * NA_WIN_ROWS - 1, 2 * NA_WIN_COLS - 1), 0.1),
        'hy_w_in': nrm((nB, D, HYENA_PROJ * D), D ** -0.5),
        'hy_b_in': nrm((nB, HYENA_PROJ * D), 0.02),
        'hy_conv_w': nrm((nB, HYENA_SHORT, HYENA_PROJ * D), HYENA_SHORT ** -0.5),
        'hy_conv_b': nrm((nB, HYENA_PROJ * D), 0.02),
        'hy_f_w1': nrm((nB, HYENA_EMB, FH), HYENA_EMB ** -0.5),
        'hy_f_b1': nrm((nB, FH), 0.02),
        'hy_f_freq1': ones_noise((nB, FH)),
        'hy_f_w2': nrm((nB, FH, FH), FH ** -0.5),
        'hy_f_b2': nrm((nB, FH), 0.02),
        'hy_f_freq2': ones_noise((nB, FH)),
        'hy_f_w3': nrm((nB, FH, FH), FH ** -0.5),
        'hy_f_b3': nrm((nB, FH), 0.02),
        'hy_f_freq3': ones_noise((nB, FH)),
        'hy_f_wout': nrm((nB, FH, 2 * D), FH ** -0.5),
        'hy_bias': nrm((nB, D), 1.0),
        'hy_w_out': nrm((nB, D, D), D ** -0.5),
        'hy_b_out': nrm((nB, D), 0.02),
        'moe_router': nrm((DEPTH, D, N_EXPERTS), D ** -0.5),
        'moe_w1': nrm((DEPTH, N_EXPERTS, D, EXPERT_HIDDEN), D ** -0.5),
        'moe_w3': nrm((DEPTH, N_EXPERTS, D, EXPERT_HIDDEN), D ** -0.5),
        'moe_w2': nrm((DEPTH, N_EXPERTS, EXPERT_HIDDEN, D), EXPERT_HIDDEN ** -0.5),
    }


def reference(x, c, ctx, c_ctx, ada_w, ada_b, norm1_g, norm2_g,
              na_w_qkv, na_w_o, na_q_g, na_k_g, na_rpb,
              hy_w_in, hy_b_in, hy_conv_w, hy_conv_b,
              hy_f_w1, hy_f_b1, hy_f_freq1, hy_f_w2, hy_f_b2, hy_f_freq2,
              hy_f_w3, hy_f_b3, hy_f_freq3, hy_f_wout,
              hy_bias, hy_w_out, hy_b_out,
              moe_router, moe_w1, moe_w3, moe_w2):
    mixer = [i % N_MIXERS for i in range(DEPTH)]
    h, hc = x, ctx
    for i in range(DEPTH):
        j = i // N_MIXERS
        ctx_stream = any(mixer[l] == 0 for l in range(i + 1, DEPTH))
        ctx_in = ctx_stream or mixer[i] == 0
        sh1, sc1, g1, sh2, sc2, g2 = adaln(c, ada_w[i], ada_b[i])
        a = modulate(rmsnorm(h, norm1_g[i]), sh1, sc1)
        ac = None
        if ctx_in:
            csh1, csc1, cg1, csh2, csc2, cg2 = adaln(c_ctx[None, :], ada_w[i], ada_b[i])
            ac = modulate(rmsnorm(hc, norm1_g[i]), csh1, csc1)
        if mixer[i] == 0:
            y, yc = neighbourhood_attention(a, ac, na_w_qkv[j], na_w_o[j], na_q_g[j], na_k_g[j],
                                            na_rpb[j], ctx_stream)
        else:
            hp = (hy_w_in[j], hy_b_in[j], hy_conv_w[j], hy_conv_b[j],
                  hy_f_w1[j], hy_f_b1[j], hy_f_freq1[j], hy_f_w2[j], hy_f_b2[j], hy_f_freq2[j],
                  hy_f_w3[j], hy_f_b3[j], hy_f_freq3[j], hy_f_wout[j],
                  hy_bias[j], hy_w_out[j], hy_b_out[j])
            y = hyena(a, *hp)
            yc = hyena(ac, *hp) if ctx_stream else None
        h = h + g1 * y
        h = h + g2 * ec_moe(modulate(rmsnorm(h, norm2_g[i]), sh2, sc2),
                            moe_router[i], moe_w1[i], moe_w3[i], moe_w2[i])
        if ctx_stream:
            hc = hc + cg1 * yc
            hc = hc + cg2 * ec_moe(modulate(rmsnorm(hc, norm2_g[i]), csh2, csc2),
                                   moe_router[i], moe_w1[i], moe_w3[i], moe_w2[i])
    return h
```

```python
import functools
import math

import jax
import jax.numpy as jnp
import numpy as np
from jax import lax
from jax.experimental import pallas as pl
from jax.experimental.pallas import tpu as pltpu

F32 = jnp.float32
BF16 = jnp.bfloat16

GRID_W = 64
N_MIXERS = 2
NORM_EPS = 1e-6
NEG_INF = -1e30
HEAD_DIM = 128
WIN_ROWS = 8
WIN_COLS = 16
HYENA_SHORT = 3
HYENA_EMB = 33
HYENA_FAST_DECAY = 0.3
HYENA_SLOW_DECAY = 1.5
HYENA_TARGET = 1e-2
N_EXPERTS = 16
CAPACITY_FACTOR = 2

Q_ROWS = 4
BAND_ROWS = 12

VMEM_LIMIT = 56 * 1024 * 1024


def _cparams(*sem):
    return pltpu.CompilerParams(dimension_semantics=sem, vmem_limit_bytes=VMEM_LIMIT)


def _tile(n, want, unit=128):
    if n <= want:
        return n
    t = (want // unit) * unit
    while n % t:
        t -= unit
    return t


def _ada_kernel(c_ref, w_ref, b_ref, o_ref):
    c = c_ref[...]
    s = c * jax.nn.sigmoid(c)
    o_ref[...] = jnp.dot(s.astype(BF16), w_ref[...].astype(BF16),
                         preferred_element_type=F32) + b_ref[...]


def ada_all(cond, ada_w, ada_b, tn=1024):
    L, D, E = ada_w.shape
    tn = _tile(E, tn)
    return pl.pallas_call(
        _ada_kernel,
        grid=(L, E // tn),
        in_specs=[
            pl.BlockSpec((8, D), lambda l, j: (0, 0)),
            pl.BlockSpec((None, D, tn), lambda l, j: (l, 0, j)),
            pl.BlockSpec((None, 1, tn), lambda l, j: (l, 0, j)),
        ],
        out_specs=pl.BlockSpec((None, 8, tn), lambda l, j: (l, 0, j)),
        out_shape=jax.ShapeDtypeStruct((L, 8, E), F32),
        compiler_params=_cparams("parallel", "parallel"),
        name="ada",
    )(cond, ada_w, ada_b.reshape(L, 1, E))


def _norm_mod(x, g, sh, sc):
    y = x * lax.rsqrt(jnp.mean(x * x, axis=-1, keepdims=True) + NORM_EPS) * g
    return y * (1.0 + sc) + sh


def _nm_matmul_kernel(x_ref, g_ref, sh_ref, sc_ref, w_ref, b_ref, hg_ref, o_ref, a_ref, *,
                      n_norm_sections, section_cols, q_scale):
    j = pl.program_id(1)

    @pl.when(j == 0)
    def _():
        a_ref[...] = _norm_mod(x_ref[...], g_ref[...], sh_ref[...], sc_ref[...]).astype(BF16)

    acc = jnp.dot(a_ref[...], w_ref[...], preferred_element_type=F32) + b_ref[...]
    tn = acc.shape[1]
    if n_norm_sections == 0:
        o_ref[...] = acc.astype(o_ref.dtype)
        return
    sec = (j * tn) // section_cols

    @pl.when(sec < n_norm_sections)
    def _():
        hg = hg_ref[...]
        mult = jnp.where(sec == 0, q_scale, 1.0).astype(F32)
        for h in range(tn // HEAD_DIM):
            c = acc[:, h * HEAD_DIM:(h + 1) * HEAD_DIM]
            c = c * lax.rsqrt(jnp.mean(c * c, axis=-1, keepdims=True) + NORM_EPS) * hg
            o_ref[:, h * HEAD_DIM:(h + 1) * HEAD_DIM] = (c * mult).astype(o_ref.dtype)

    @pl.when(sec >= n_norm_sections)
    def _():
        o_ref[...] = acc.astype(o_ref.dtype)


def nm_matmul(x, g, mods, k_shift, k_scale, group_rows, group_base, w, b, head_g=None,
              q_scale=1.0, out_dtype=BF16, tm=1024, tn=512):
    R, D = x.shape
    E = w.shape[1]
    tm = _tile(group_rows, tm, 8)
    tn = _tile(D, tn)
    n_norm = 0 if head_g is None else 2
    if head_g is None:
        head_g = jnp.ones((2, 1, HEAD_DIM), F32)

    def grp(i):
        return group_base + (i * tm) // group_rows

    def hg_map(i, j):
        return (jnp.minimum((j * tn) // D, 1), 0, 0)

    kern = functools.partial(_nm_matmul_kernel, n_norm_sections=n_norm, section_cols=D,
                             q_scale=q_scale)
    return pl.pallas_call(
        kern,
        grid=(R // tm, E // tn),
        in_specs=[
            pl.BlockSpec((tm, D), lambda i, j: (i, 0)),
            pl.BlockSpec((1, D), lambda i, j: (0, 0)),
            pl.BlockSpec((None, None, 1, D), lambda i, j: (k_shift, grp(i), 0, 0)),
            pl.BlockSpec((None, None, 1, D), lambda i, j: (k_scale, grp(i), 0, 0)),
            pl.BlockSpec((D, tn), lambda i, j: (0, j)),
            pl.BlockSpec((1, tn), lambda i, j: (0, j)),
            pl.BlockSpec((None, 1, HEAD_DIM), hg_map),
        ],
        out_specs=pl.BlockSpec((tm, tn), lambda i, j: (i, j)),
        out_shape=jax.ShapeDtypeStruct((R, E), out_dtype),
        scratch_shapes=[pltpu.VMEM((tm, D), BF16)],
        compiler_params=_cparams("parallel", "arbitrary"),
        name="nm_matmul",
    )(x, g.reshape(1, D), mods, mods, w, b.reshape(1, E), head_g)


def _mm_res_kernel(x_ref, w_ref, b_ref, gate_ref, res_ref, o_ref):
    y = jnp.dot(x_ref[...], w_ref[...], preferred_element_type=F32) + b_ref[...]
    o_ref[...] = res_ref[...] + gate_ref[...] * y


def matmul_residual(x, w, b, mods, k_gate, group_rows, group_base, res, tm=1024, tn=512):
    R, K = x.shape
    E = w.shape[1]
    tm = _tile(group_rows, tm, 8)
    tn = _tile(E, tn)

    def grp(i):
        return group_base + (i * tm) // group_rows

    return pl.pallas_call(
        _mm_res_kernel,
        grid=(R // tm, E // tn),
        in_specs=[
            pl.BlockSpec((tm, K), lambda i, j: (i, 0)),
            pl.BlockSpec((K, tn), lambda i, j: (0, j)),
            pl.BlockSpec((1, tn), lambda i, j: (0, j)),
            pl.BlockSpec((None, None, 1, tn), lambda i, j: (k_gate, grp(i), 0, j)),
            pl.BlockSpec((tm, tn), lambda i, j: (i, j)),
        ],
        out_specs=pl.BlockSpec((tm, tn), lambda i, j: (i, j)),
        out_shape=jax.ShapeDtypeStruct((R, E), F32),
        compiler_params=_cparams("parallel", "parallel"),
        name="matmul_residual",
    )(x, w, b.reshape(1, E), mods, res)


def _na_bias_tables(rpb, rows):
    W = GRID_W
    cols = np.arange(W)
    cs = np.clip(cols - WIN_COLS // 2, 0, W - WIN_COLS)
    in_win = (cols[None, :] >= cs[:, None]) & (cols[None, :] < cs[:, None] + WIN_COLS)
    dc = np.clip(cols[None, :] - cols[:, None] + WIN_COLS - 1, 0, 2 * WIN_COLS - 2)
    dr_idx = np.zeros((3, Q_ROWS * W, BAND_ROWS * W), np.int32)
    dc_idx = np.zeros_like(dr_idx)
    mask = np.zeros(dr_idx.shape, bool)
    for var, r0 in enumerate((0, Q_ROWS, rows - Q_ROWS)):
        bs = _band_start(r0, rows)
        for i in range(Q_ROWS):
            r = r0 + i
            rs = int(np.clip(r - WIN_ROWS // 2, 0, rows - WIN_ROWS))
            for jj in range(BAND_ROWS):
                kr = bs + jj
                valid = rs <= kr < rs + WIN_ROWS
                dr = int(np.clip(kr - r + WIN_ROWS - 1, 0, 2 * WIN_ROWS - 2))
                blk = (slice(i * W, (i + 1) * W), slice(jj * W, (jj + 1) * W))
                dr_idx[var][blk] = dr
                dc_idx[var][blk] = dc
                mask[var][blk] = in_win & valid
    bias = rpb.astype(F32)[:, dr_idx, dc_idx]
    return jnp.where(mask[None], bias, NEG_INF)


def _band_start(r0, rows):
    return int(np.clip(r0 - WIN_ROWS // 2, 0, rows - BAND_ROWS))


def _na_kernel(q_ref, k_ref, v_ref, qc_ref, kc_ref, vc_ref, bias_ref, o_ref, oc_ref, *, rows):
    W = GRID_W
    nblk = rows // Q_ROWS
    kc = kc_ref[...]
    vc = vc_ref[...]
    nt = (((1,), (1,)), ((), ()))

    def attend(q, parts):
        ss = [lax.dot_general(q, k, nt, preferred_element_type=F32) + (0.0 if bias is None else bias)
              for k, _, bias in parts]
        m = functools.reduce(jnp.maximum, [jnp.max(s, axis=-1, keepdims=True) for s in ss])
        ps = [jnp.exp(s - m) for s in ss]
        den = sum(jnp.sum(p, axis=-1, keepdims=True) for p in ps)
        o = sum(jnp.dot(p.astype(BF16), v, preferred_element_type=F32)
                for p, (_, v, _) in zip(ps, parts))
        return o / den

    def body(blk, carry):
        r0 = blk * Q_ROWS
        bs = jnp.clip(r0 - WIN_ROWS // 2, 0, rows - BAND_ROWS)
        var = jnp.where(blk == 0, 0, jnp.where(blk == nblk - 1, 2, 1))
        q0 = pl.multiple_of(r0 * W, Q_ROWS * W)
        k0 = pl.multiple_of(bs * W, W)
        q = q_ref[pl.ds(q0, Q_ROWS * W), :]
        kb = k_ref[pl.ds(k0, BAND_ROWS * W), :]
        vb = v_ref[pl.ds(k0, BAND_ROWS * W), :]
        o = attend(q, [(kb, vb, bias_ref[var]), (kc, vc, None)])
        o_ref[pl.ds(q0, Q_ROWS * W), :] = o.astype(o_ref.dtype)
        return carry

    lax.fori_loop(0, nblk, body, 0)
    oc_ref[...] = attend(qc_ref[...], [(kc, vc, None)]).astype(oc_ref.dtype)


def na_attention(qkv, qkv_c, bias, B, N, CTX, D):
    H = D // HEAD_DIM
    rows = N // GRID_W
    kern = functools.partial(_na_kernel, rows=rows)

    def spec(n, sec):
        return pl.BlockSpec((n, HEAD_DIM), lambda h, b: (b, sec * H + h))

    return pl.pallas_call(
        kern,
        grid=(H, B),
        in_specs=[spec(N, 0), spec(N, 1), spec(N, 2), spec(CTX, 0), spec(CTX, 1), spec(CTX, 2),
                  pl.BlockSpec((None, 3, Q_ROWS * GRID_W, BAND_ROWS * GRID_W),
                               lambda h, b: (h, 0, 0, 0))],
        out_specs=[pl.BlockSpec((N, HEAD_DIM), lambda h, b: (b, h)),
                   pl.BlockSpec((CTX, HEAD_DIM), lambda h, b: (b, h))],
        out_shape=[jax.ShapeDtypeStruct((B * N, D), BF16),
                   jax.ShapeDtypeStruct((B * CTX, D), BF16)],
        compiler_params=_cparams("parallel", "parallel"),
        name="na_attention",
    )(qkv, qkv, qkv, qkv_c, qkv_c, qkv_c, bias)


def _expert_kernel(x_ref, w1_ref, w3_ref, w2_ref, gate_ref, o_ref):
    x = x_ref[...]
    h1 = jnp.dot(x, w1_ref[...], preferred_element_type=F32)
    h3 = jnp.dot(x, w3_ref[...], preferred_element_type=F32)
    hid = (h1 * jax.nn.sigmoid(h1) * h3).astype(BF16)
    y = jnp.dot(hid, w2_ref[...], preferred_element_type=F32)
    o_ref[...] = y * gate_ref[...]


def expert_ffn(xs, w1, w3, w2, gate):
    B, E, C, D = xs.shape
    Fh = w1.shape[2]
    return pl.pallas_call(
        _expert_kernel,
        grid=(E, B),
        in_specs=[
            pl.BlockSpec((None, None, C, D), lambda e, b: (b, e, 0, 0)),
            pl.BlockSpec((None, D, Fh), lambda e, b: (e, 0, 0)),
            pl.BlockSpec((None, D, Fh), lambda e, b: (e, 0, 0)),
            pl.BlockSpec((None, Fh, D), lambda e, b: (e, 0, 0)),
            pl.BlockSpec((None, None, C, 1), lambda e, b: (b, e, 0, 0)),
        ],
        out_specs=pl.BlockSpec((None, None, C, D), lambda e, b: (b, e, 0, 0)),
        out_shape=jax.ShapeDtypeStruct((B, E, C, D), F32),
        compiler_params=_cparams("parallel", "parallel"),
        name="expert_ffn",
    )(xs, w1, w3, w2, gate)


def _moe_prep_kernel(x_ref, g_ref, sh_ref, sc_ref, r_ref, a_ref, aff_ref):
    a = _norm_mod(x_ref[...], g_ref[...], sh_ref[...], sc_ref[...])
    a_ref[...] = a.astype(a_ref.dtype)
    logits = lax.dot_general(r_ref[...], a.astype(BF16), (((1,), (1,)), ((), ())),
                             preferred_element_type=F32)
    m = jnp.max(logits, axis=0, keepdims=True)
    p = jnp.exp(logits - m)
    aff_ref[...] = p / jnp.sum(p, axis=0, keepdims=True)


def moe_prep(x, g, mods, k_shift, k_scale, group_rows, group_base, router_t, tm=512):
    R, D = x.shape
    E = router_t.shape[0]
    tm = _tile(group_rows, tm, 128)

    def grp(i):
        return group_base + (i * tm) // group_rows

    return pl.pallas_call(
        _moe_prep_kernel,
        grid=(R // tm,),
        in_specs=[
            pl.BlockSpec((tm, D), lambda i: (i, 0)),
            pl.BlockSpec((1, D), lambda i: (0, 0)),
            pl.BlockSpec((None, None, 1, D), lambda i: (k_shift, grp(i), 0, 0)),
            pl.BlockSpec((None, None, 1, D), lambda i: (k_scale, grp(i), 0, 0)),
            pl.BlockSpec((E, D), lambda i: (0, 0)),
        ],
        out_specs=[pl.BlockSpec((tm, D), lambda i: (i, 0)),
                   pl.BlockSpec((E, tm), lambda i: (0, i))],
        out_shape=[jax.ShapeDtypeStruct((R, D), F32), jax.ShapeDtypeStruct((E, R), F32)],
        compiler_params=_cparams("parallel"),
        name="moe_prep",
    )(x, g.reshape(1, D), mods, mods, router_t)


def ec_moe(h, g, mods, k_shift, k_scale, k_gate, group_rows, group_base, B, router, w1, w3, w2):
    R, D = h.shape
    N = R // B
    E = router.shape[1]
    cap = CAPACITY_FACTOR * N // E
    a, aff_t = moe_prep(h, g, mods, k_shift, k_scale, group_rows, group_base, router.T.astype(BF16))
    aff = aff_t.reshape(E, B, N).transpose(1, 0, 2)
    gate, idx = lax.top_k(aff, cap)
    a3 = a.reshape(B, N, D)
    xs = jax.vmap(lambda ab, ib: ab[ib])(a3, idx).astype(BF16)
    ys = expert_ffn(xs, w1, w3, w2, gate[..., None])
    moe = jax.vmap(lambda ib, yb: jnp.zeros((N, D), F32).at[ib.reshape(-1)].add(yb.reshape(-1, D)))(idx, ys)
    g2 = mods[k_gate, group_base:group_base + B] if group_base == 0 else mods[k_gate, group_base:group_base + 1]
    return (h.reshape(B, N, D) + g2 * moe).reshape(R, D)


def _hyena_filter(L, w1, b1, f1, w2, b2, f2, w3, b3, f3, w_out):
    t01 = jnp.linspace(0.0, 1.0, L, dtype=F32)[:, None]
    bands = (HYENA_EMB - 1) // 2
    w = 2 * math.pi * jnp.arange(L, dtype=F32)[:, None] / L
    f = jnp.linspace(1e-4, bands - 1, bands, dtype=F32)[None, :]
    z = jnp.concatenate([t01, jnp.cos(f * w), -jnp.sin(f * w)], axis=-1)
    hp = lax.Precision.HIGHEST
    h = jnp.sin(f1 * (jnp.dot(z, w1, precision=hp) + b1))
    h = jnp.sin(f2 * (jnp.dot(h, w2, precision=hp) + b2))
    h = jnp.sin(f3 * (jnp.dot(h, w3, precision=hp) + b3))
    h = jnp.dot(h, w_out, precision=hp)
    d = h.shape[-1] // 2
    max_decay = math.log(HYENA_TARGET) / HYENA_FAST_DECAY
    min_decay = math.log(HYENA_TARGET) / HYENA_SLOW_DECAY
    deltas = jnp.abs(jnp.linspace(min_decay, max_decay, d, dtype=F32))
    decay = jnp.exp(-t01 * deltas[None, :])
    h_fwd = h[:, :d] * decay
    h_bwd = h[:, d:] * decay
    k = jnp.concatenate([h_fwd, jnp.zeros((1, d), F32), h_bwd[:0:-1]], axis=0)
    return k / jnp.sum(jnp.abs(k), axis=0, keepdims=True)


def _short_conv(p, w, b):
    L = p.shape[1]
    pad = HYENA_SHORT // 2
    pp = jnp.pad(p, ((0, 0), (pad, HYENA_SHORT - 1 - pad), (0, 0)))
    return sum(pp[:, j:j + L] * w[j] for j in range(HYENA_SHORT)) + b


def hyena_mix(p, B, L, D, conv_w, conv_b, filt, bias):
    p = _short_conv(p.reshape(B, L, 3 * D), conv_w, conv_b)
    x0, x1, v = jnp.split(p, 3, axis=-1)
    v = v * x1
    k = _hyena_filter(L, *filt)
    vf = jnp.fft.rfft(v, n=2 * L, axis=1)
    kf = jnp.fft.rfft(k, n=2 * L, axis=0)
    y = jnp.fft.irfft(vf * kf[None], n=2 * L, axis=1)[:, :L]
    y = (y + v * bias) * x0
    return y.reshape(B * L, D).astype(BF16)


def kernel(x, c, ctx, c_ctx, ada_w, ada_b, norm1_g, norm2_g, na_w_qkv, na_w_o, na_q_g, na_k_g, na_rpb, hy_w_in, hy_b_in, hy_conv_w, hy_conv_b, hy_f_w1, hy_f_b1, hy_f_freq1, hy_f_w2, hy_f_b2, hy_f_freq2, hy_f_w3, hy_f_b3, hy_f_freq3, hy_f_wout, hy_bias, hy_w_out, hy_b_out, moe_router, moe_w1, moe_w3, moe_w2):
    B, N, D = x.shape
    CTX = ctx.shape[1]
    depth = ada_w.shape[0]
    mixer = [i % N_MIXERS for i in range(depth)]

    cond = jnp.concatenate([c, c_ctx[None, :], jnp.zeros((8 - B - 1, D), F32)], axis=0)
    mods_all = ada_all(cond, ada_w, ada_b)
    mods_all = mods_all.reshape(depth, 8, 6, 1, D).transpose(0, 2, 1, 3, 4)

    h = x.reshape(B * N, D)
    hc = ctx.reshape(B * CTX, D)
    zeros_d = jnp.zeros((D,), F32)
    for i in range(depth):
        j = i // N_MIXERS
        ctx_stream = any(mixer[l] == 0 for l in range(i + 1, depth))
        ctx_in = ctx_stream or mixer[i] == 0
        mods = mods_all[i]
        w1 = moe_w1[i].astype(BF16)
        w3 = moe_w3[i].astype(BF16)
        w2 = moe_w2[i].astype(BF16)
        if mixer[i] == 0:
            wqkv = na_w_qkv[j].astype(BF16)
            wo = na_w_o[j].astype(BF16)
            hg = jnp.stack([na_q_g[j], na_k_g[j]]).reshape(2, 1, HEAD_DIM)
            zeros_e = jnp.zeros((3 * D,), F32)
            qs = HEAD_DIM ** -0.5
            qkv = nm_matmul(h, norm1_g[i], mods, 0, 1, N, 0, wqkv, zeros_e, hg, qs)
            qkv_c = nm_matmul(hc, norm1_g[i], mods, 0, 1, B * CTX, B, wqkv, zeros_e, hg, qs)
            bias = _na_bias_tables(na_rpb[j], N // GRID_W)
            o, oc = na_attention(qkv, qkv_c, bias, B, N, CTX, D)
            h = matmul_residual(o, wo, zeros_d, mods, 2, N, 0, h)
            if ctx_stream:
                hc = matmul_residual(oc, wo, zeros_d, mods, 2, B * CTX, B, hc)
        else:
            win = hy_w_in[j].astype(BF16)
            wout = hy_w_out[j].astype(BF16)
            filt = (hy_f_w1[j], hy_f_b1[j], hy_f_freq1[j], hy_f_w2[j], hy_f_b2[j], hy_f_freq2[j],
                    hy_f_w3[j], hy_f_b3[j], hy_f_freq3[j], hy_f_wout[j])
            p = nm_matmul(h, norm1_g[i], mods, 0, 1, N, 0, win, hy_b_in[j], out_dtype=F32)
            z = hyena_mix(p, B, N, D, hy_conv_w[j], hy_conv_b[j], filt, hy_bias[j])
            h = matmul_residual(z, wout, hy_b_out[j], mods, 2, N, 0, h)
            if ctx_stream:
                pc = nm_matmul(hc, norm1_g[i], mods, 0, 1, B * CTX, B, win, hy_b_in[j], out_dtype=F32)
                zc = hyena_mix(pc, B, CTX, D, hy_conv_w[j], hy_conv_b[j], filt, hy_bias[j])
                hc = matmul_residual(zc, wout, hy_b_out[j], mods, 2, B * CTX, B, hc)
        h = ec_moe(h, norm2_g[i], mods, 3, 4, 5, N, 0, B, moe_router[i], w1, w3, w2)
        if ctx_stream:
            hc = ec_moe(hc, norm2_g[i], mods, 3, 4, 5, B * CTX, B, B, moe_router[i], w1, w3, w2)
    return h.reshape(B, N, D)
```

```python
import functools
import math

import jax
import jax.numpy as jnp
import numpy as np
from jax import lax
from jax.experimental import pallas as pl
from jax.experimental.pallas import tpu as pltpu

F32 = jnp.float32
BF16 = jnp.bfloat16

GRID_W = 64
N_MIXERS = 2
NORM_EPS = 1e-6
NEG_INF = -1e30
HEAD_DIM = 128
WIN_ROWS = 8
WIN_COLS = 16
HYENA_SHORT = 3
HYENA_EMB = 33
HYENA_FAST_DECAY = 0.3
HYENA_SLOW_DECAY = 1.5
HYENA_TARGET = 1e-2
N_EXPERTS = 16
CAPACITY_FACTOR = 2

Q_ROWS = 4
BAND_ROWS = 12

VMEM_LIMIT = 56 * 1024 * 1024


def _cparams(*sem):
    return pltpu.CompilerParams(dimension_semantics=sem, vmem_limit_bytes=VMEM_LIMIT)


def _tile(n, want, unit=128):
    if n <= want:
        return n
    t = (want // unit) * unit
    while n % t:
        t -= unit
    return t


def _ada_kernel(c_ref, w_ref, b_ref, o_ref):
    c = c_ref[...]
    s = c * jax.nn.sigmoid(c)
    o_ref[...] = jnp.dot(s.astype(BF16), w_ref[...].astype(BF16),
                         preferred_element_type=F32) + b_ref[...]


def ada_all(cond, ada_w, ada_b, tn=1024):
    L, D, E = ada_w.shape
    tn = _tile(E, tn)
    return pl.pallas_call(
        _ada_kernel,
        grid=(L, E // tn),
        in_specs=[
            pl.BlockSpec((8, D), lambda l, j: (0, 0)),
            pl.BlockSpec((None, D, tn), lambda l, j: (l, 0, j)),
            pl.BlockSpec((None, 1, tn), lambda l, j: (l, 0, j)),
        ],
        out_specs=pl.BlockSpec((None, 8, tn), lambda l, j: (l, 0, j)),
        out_shape=jax.ShapeDtypeStruct((L, 8, E), F32),
        compiler_params=_cparams("parallel", "parallel"),
        name="ada",
    )(cond, ada_w, ada_b.reshape(L, 1, E))


def _norm_mod(x, g, sh, sc):
    y = x * lax.rsqrt(jnp.mean(x * x, axis=-1, keepdims=True) + NORM_EPS) * g
    return y * (1.0 + sc) + sh


def _nm_matmul_kernel(x_ref, g_ref, sh_ref, sc_ref, w_ref, b_ref, hg_ref, o_ref, a_ref, *,
                      n_norm_sections, section_cols, q_scale):
    j = pl.program_id(1)

    @pl.when(j == 0)
    def _():
        a_ref[...] = _norm_mod(x_ref[...], g_ref[...], sh_ref[...], sc_ref[...]).astype(BF16)

    acc = jnp.dot(a_ref[...], w_ref[...], preferred_element_type=F32) + b_ref[...]
    tn = acc.shape[1]
    if n_norm_sections == 0:
        o_ref[...] = acc.astype(o_ref.dtype)
        return
    sec = (j * tn) // section_cols

    @pl.when(sec < n_norm_sections)
    def _():
        hg = hg_ref[...]
        mult = jnp.where(sec == 0, q_scale, 1.0).astype(F32)
        for h in range(tn // HEAD_DIM):
            c = acc[:, h * HEAD_DIM:(h + 1) * HEAD_DIM]
            c = c * lax.rsqrt(jnp.mean(c * c, axis=-1, keepdims=True) + NORM_EPS) * hg
            o_ref[:, h * HEAD_DIM:(h + 1) * HEAD_DIM] = (c * mult).astype(o_ref.dtype)

    @pl.when(sec >= n_norm_sections)
    def _():
        o_ref[...] = acc.astype(o_ref.dtype)


def nm_matmul(x, g, mods, k_shift, k_scale, group_rows, group_base, w, b, head_g=None,
              q_scale=1.0, out_dtype=BF16, tm=1024, tn=512):
    R, D = x.shape
    E = w.shape[1]
    tm = _tile(group_rows, tm, 8)
    tn = _tile(D, tn)
    n_norm = 0 if head_g is None else 2
    if head_g is None:
        head_g = jnp.ones((2, 1, HEAD_DIM), F32)

    def grp(i):
        return group_base + (i * tm) // group_rows

    def hg_map(i, j):
        return (jnp.minimum((j * tn) // D, 1), 0, 0)

    kern = functools.partial(_nm_matmul_kernel, n_norm_sections=n_norm, section_cols=D,
                             q_scale=q_scale)
    return pl.pallas_call(
        kern,
        grid=(R // tm, E // tn),
        in_specs=[
            pl.BlockSpec((tm, D), lambda i, j: (i, 0)),
            pl.BlockSpec((1, D), lambda i, j: (0, 0)),
            pl.BlockSpec((None, None, 1, D), lambda i, j: (k_shift, grp(i), 0, 0)),
            pl.BlockSpec((None, None, 1, D), lambda i, j: (k_scale, grp(i), 0, 0)),
            pl.BlockSpec((D, tn), lambda i, j: (0, j)),
            pl.BlockSpec((1, tn), lambda i, j: (0, j)),
            pl.BlockSpec((None, 1, HEAD_DIM), hg_map),
        ],
        out_specs=pl.BlockSpec((tm, tn), lambda i, j: (i, j)),
        out_shape=jax.ShapeDtypeStruct((R, E), out_dtype),
        scratch_shapes=[pltpu.VMEM((tm, D), BF16)],
        compiler_params=_cparams("parallel", "arbitrary"),
        name="nm_matmul",
    )(x, g.reshape(1, D), mods, mods, w, b.reshape(1, E), head_g)


def _mm_res_kernel(x_ref, w_ref, b_ref, gate_ref, res_ref, o_ref):
    y = jnp.dot(x_ref[...], w_ref[...], preferred_element_type=F32) + b_ref[...]
    o_ref[...] = res_ref[...] + gate_ref[...] * y


def matmul_residual(x, w, b, mods, k_gate, group_rows, group_base, res, tm=1024, tn=512):
    R, K = x.shape
    E = w.shape[1]
    tm = _tile(group_rows, tm, 8)
    tn = _tile(E, tn)

    def grp(i):
        return group_base + (i * tm) // group_rows

    return pl.pallas_call(
        _mm_res_kernel,
        grid=(R // tm, E // tn),
        in_specs=[
            pl.BlockSpec((tm, K), lambda i, j: (i, 0)),
            pl.BlockSpec((K, tn), lambda i, j: (0, j)),
            pl.BlockSpec((1, tn), lambda i, j: (0, j)),
            pl.BlockSpec((None, None, 1, tn), lambda i, j: (k_gate, grp(i), 0, j)),
            pl.BlockSpec((tm, tn), lambda i, j: (i, j)),
        ],
        out_specs=pl.BlockSpec((tm, tn), lambda i, j: (i, j)),
        out_shape=jax.ShapeDtypeStruct((R, E), F32),
        compiler_params=_cparams("parallel", "parallel"),
        name="matmul_residual",
    )(x, w, b.reshape(1, E), mods, res)


def _na_bias_tables(rpb, rows):
    W = GRID_W
    cols = np.arange(W)
    cs = np.clip(cols - WIN_COLS // 2, 0, W - WIN_COLS)
    in_win = (cols[None, :] >= cs[:, None]) & (cols[None, :] < cs[:, None] + WIN_COLS)
    dc = np.clip(cols[None, :] - cols[:, None] + WIN_COLS - 1, 0, 2 * WIN_COLS - 2)
    n_dr, n_dc = 2 * WIN_ROWS - 1, 2 * WIN_COLS - 1
    onehot_r = np.zeros((3, Q_ROWS, BAND_ROWS, n_dr), np.float32)
    mask = np.zeros((3, Q_ROWS, W, BAND_ROWS, W), bool)
    for var, r0 in enumerate((0, Q_ROWS, rows - Q_ROWS)):
        bs = _band_start(r0, rows)
        for i in range(Q_ROWS):
            r = r0 + i
            rs = int(np.clip(r - WIN_ROWS // 2, 0, rows - WIN_ROWS))
            for jj in range(BAND_ROWS):
                kr = bs + jj
                if rs <= kr < rs + WIN_ROWS:
                    onehot_r[var, i, jj, kr - r + WIN_ROWS - 1] = 1.0
                    mask[var, i, :, jj, :] = in_win
    onehot_c = (dc[:, :, None] == np.arange(n_dc)).astype(np.float32)
    hp = lax.Precision.HIGHEST
    t = jnp.einsum('hrc,vijr->hvijc', rpb.astype(F32), onehot_r, precision=hp)
    t = jnp.einsum('hvijc,qkc->hviqjk', t, onehot_c, precision=hp)
    t = jnp.where(mask[None], t, NEG_INF)
    return t.reshape(rpb.shape[0], 3, Q_ROWS * W, BAND_ROWS * W)


def _band_start(r0, rows):
    return int(np.clip(r0 - WIN_ROWS // 2, 0, rows - BAND_ROWS))


def _na_kernel(q_ref, k_ref, v_ref, qc_ref, kc_ref, vc_ref, bias_ref, o_ref, oc_ref, *, rows):
    W = GRID_W
    nblk = rows // Q_ROWS
    kc = kc_ref[...]
    vc = vc_ref[...]
    nt = (((1,), (1,)), ((), ()))

    def attend(q, parts):
        ss = [lax.dot_general(q, k, nt, preferred_element_type=F32) + (0.0 if bias is None else bias)
              for k, _, bias in parts]
        m = functools.reduce(jnp.maximum, [jnp.max(s, axis=-1, keepdims=True) for s in ss])
        ps = [jnp.exp(s - m) for s in ss]
        den = sum(jnp.sum(p, axis=-1, keepdims=True) for p in ps)
        o = sum(jnp.dot(p.astype(BF16), v, preferred_element_type=F32)
                for p, (_, v, _) in zip(ps, parts))
        return o / den

    def body(blk, carry):
        r0 = blk * Q_ROWS
        bs = jnp.clip(r0 - WIN_ROWS // 2, 0, rows - BAND_ROWS)
        var = jnp.where(blk == 0, 0, jnp.where(blk == nblk - 1, 2, 1))
        q0 = pl.multiple_of(r0 * W, Q_ROWS * W)
        k0 = pl.multiple_of(bs * W, W)
        q = q_ref[pl.ds(q0, Q_ROWS * W), :]
        kb = k_ref[pl.ds(k0, BAND_ROWS * W), :]
        vb = v_ref[pl.ds(k0, BAND_ROWS * W), :]
        o = attend(q, [(kb, vb, bias_ref[var]), (kc, vc, None)])
        o_ref[pl.ds(q0, Q_ROWS * W), :] = o.astype(o_ref.dtype)
        return carry

    lax.fori_loop(0, nblk, body, 0)
    oc_ref[...] = attend(qc_ref[...], [(kc, vc, None)]).astype(oc_ref.dtype)


def na_attention(qkv, qkv_c, bias, B, N, CTX, D):
    H = D // HEAD_DIM
    rows = N // GRID_W
    kern = functools.partial(_na_kernel, rows=rows)

    def spec(n, sec):
        return pl.BlockSpec((n, HEAD_DIM), lambda h, b: (b, sec * H + h))

    return pl.pallas_call(
        kern,
        grid=(H, B),
        in_specs=[spec(N, 0), spec(N, 1), spec(N, 2), spec(CTX, 0), spec(CTX, 1), spec(CTX, 2),
                  pl.BlockSpec((None, 3, Q_ROWS * GRID_W, BAND_ROWS * GRID_W),
                               lambda h, b: (h, 0, 0, 0))],
        out_specs=[pl.BlockSpec((N, HEAD_DIM), lambda h, b: (b, h)),
                   pl.BlockSpec((CTX, HEAD_DIM), lambda h, b: (b, h))],
        out_shape=[jax.ShapeDtypeStruct((B * N, D), BF16),
                   jax.ShapeDtypeStruct((B * CTX, D), BF16)],
        compiler_params=_cparams("parallel", "parallel"),
        name="na_attention",
    )(qkv, qkv, qkv, qkv_c, qkv_c, qkv_c, bias)


def _expert_kernel(x_ref, w1_ref, w3_ref, w2_ref, gate_ref, o_ref):
    x = x_ref[...]
    h1 = jnp.dot(x, w1_ref[...], preferred_element_type=F32)
    h3 = jnp.dot(x, w3_ref[...], preferred_element_type=F32)
    hid = (h1 * jax.nn.sigmoid(h1) * h3).astype(BF16)
    y = jnp.dot(hid, w2_ref[...], preferred_element_type=F32)
    o_ref[...] = y * gate_ref[...]


def expert_ffn(xs, w1, w3, w2, gate):
    B, E, C, D = xs.shape
    Fh = w1.shape[2]
    return pl.pallas_call(
        _expert_kernel,
        grid=(E, B),
        in_specs=[
            pl.BlockSpec((None, None, C, D), lambda e, b: (b, e, 0, 0)),
            pl.BlockSpec((None, D, Fh), lambda e, b: (e, 0, 0)),
            pl.BlockSpec((None, D, Fh), lambda e, b: (e, 0, 0)),
            pl.BlockSpec((None, Fh, D), lambda e, b: (e, 0, 0)),
            pl.BlockSpec((None, None, C, 1), lambda e, b: (b, e, 0, 0)),
        ],
        out_specs=pl.BlockSpec((None, None, C, D), lambda e, b: (b, e, 0, 0)),
        out_shape=jax.ShapeDtypeStruct((B, E, C, D), F32),
        compiler_params=_cparams("parallel", "parallel"),
        name="expert_ffn",
    )(xs, w1, w3, w2, gate)


def _moe_prep_kernel(x_ref, g_ref, sh_ref, sc_ref, r_ref, a_ref, aff_ref):
    a = _norm_mod(x_ref[...], g_ref[...], sh_ref[...], sc_ref[...])
    a_ref[...] = a.astype(a_ref.dtype)
    logits = lax.dot_general(r_ref[...], a.astype(BF16), (((1,), (1,)), ((), ())),
                             preferred_element_type=F32)
    m = jnp.max(logits, axis=0, keepdims=True)
    p = jnp.exp(logits - m)
    aff_ref[...] = p / jnp.sum(p, axis=0, keepdims=True)


def moe_prep(x, g, mods, k_shift, k_scale, group_rows, group_base, router_t, tm=512):
    R, D = x.shape
    E = router_t.shape[0]
    tm = _tile(group_rows, tm, 128)

    def grp(i):
        return group_base + (i * tm) // group_rows

    return pl.pallas_call(
        _moe_prep_kernel,
        grid=(R // tm,),
        in_specs=[
            pl.BlockSpec((tm, D), lambda i: (i, 0)),
            pl.BlockSpec((1, D), lambda i: (0, 0)),
            pl.BlockSpec((None, None, 1, D), lambda i: (k_shift, grp(i), 0, 0)),
            pl.BlockSpec((None, None, 1, D), lambda i: (k_scale, grp(i), 0, 0)),
            pl.BlockSpec((E, D), lambda i: (0, 0)),
        ],
        out_specs=[pl.BlockSpec((tm, D), lambda i: (i, 0)),
                   pl.BlockSpec((E, tm), lambda i: (0, i))],
        out_shape=[jax.ShapeDtypeStruct((R, D), F32), jax.ShapeDtypeStruct((E, R), F32)],
        compiler_params=_cparams("parallel"),
        name="moe_prep",
    )(x, g.reshape(1, D), mods, mods, router_t)


def ec_moe(h, g, mods, k_shift, k_scale, k_gate, group_rows, group_base, B, router, w1, w3, w2):
    R, D = h.shape
    N = R // B
    E = router.shape[1]
    cap = CAPACITY_FACTOR * N // E
    a, aff_t = moe_prep(h, g, mods, k_shift, k_scale, group_rows, group_base, router.T.astype(BF16))
    aff = aff_t.reshape(E, B, N).transpose(1, 0, 2)
    gate, idx = lax.top_k(aff, cap)
    a3 = a.reshape(B, N, D)
    xs = jax.vmap(lambda ab, ib: ab[ib])(a3, idx).astype(BF16)
    ys = expert_ffn(xs, w1, w3, w2, gate[..., None])
    moe = jax.vmap(lambda ib, yb: jnp.zeros((N, D), F32).at[ib.reshape(-1)].add(yb.reshape(-1, D)))(idx, ys)
    g2 = mods[k_gate, group_base:group_base + B] if group_base == 0 else mods[k_gate, group_base:group_base + 1]
    return (h.reshape(B, N, D) + g2 * moe).reshape(R, D)


FFT_MINOR = 256
FFT_GROUP = 16


def _dft_tables(L):
    n = 2 * L
    n1 = n // FFT_MINOR
    nb = n1 // 2
    two_pi = 2.0 * math.pi
    a = jnp.arange(n1, dtype=jnp.int32)
    ang = two_pi * ((a[:, None] * a[None, :]) % n1).astype(F32) / n1
    cos1, sin1 = jnp.cos(ang), jnp.sin(ang)
    eye = jnp.eye(FFT_GROUP, dtype=F32)

    def kron(m):
        return jnp.kron(m, eye)

    f1 = jnp.arange(n1, dtype=jnp.int32)[:, None, None]
    f2 = jnp.arange(FFT_MINOR, dtype=jnp.int32)[None, :, None]
    n2 = jnp.arange(FFT_MINOR, dtype=jnp.int32)[None, None, :]
    ang = two_pi * ((n2 * (f1 + n1 * f2)) % n).astype(F32) / n
    gr, gi = jnp.cos(ang), -jnp.sin(ang)
    return dict(
        n1=n1, nb=nb,
        m_fwd_full=jnp.concatenate([kron(cos1), kron(-sin1)], axis=0).astype(BF16),
        m_fwd_half=jnp.concatenate([kron(cos1[:, :nb]), kron(-sin1[:, :nb])], axis=0).astype(BF16),
        m_inv=jnp.concatenate([kron(cos1[:nb]), kron(sin1[:nb])], axis=0).astype(BF16),
        g=jnp.concatenate([gr, gi], axis=1).astype(BF16),
        gt=jnp.concatenate([gr.transpose(0, 2, 1), gi.transpose(0, 2, 1)], axis=1).astype(BF16),
    )


def _conv_gate_kernel(x0_ref, x1_ref, v_ref, w_ref, b_ref, vo_ref, x0o_ref):
    L, C = x0_ref.shape
    row = lax.broadcasted_iota(jnp.int32, (L, C), 0)

    def conv(ref, k):
        x = ref[...]
        w = w_ref[k]
        prev = jnp.where(row == 0, 0.0, pltpu.roll(x, 1, 0))
        nxt = jnp.where(row == L - 1, 0.0, pltpu.roll(x, L - 1, 0))
        return prev * w[0:1] + x * w[1:2] + nxt * w[2:3] + b_ref[k]

    x0 = conv(x0_ref, 0)
    x1 = conv(x1_ref, 1)
    v = conv(v_ref, 2)
    vo_ref[...] = (v * x1).astype(vo_ref.dtype)
    x0o_ref[...] = x0.astype(x0o_ref.dtype)


def conv_gate(p, B, L, D, conv_w, conv_b, tc=256):
    tc = _tile(D, tc)
    nc = D // tc
    w = conv_w.reshape(HYENA_SHORT, 3, D).transpose(1, 0, 2)
    b = conv_b.reshape(3, 1, D)

    def spec(k):
        return pl.BlockSpec((L, tc), lambda bi, c: (bi, k * nc + c))

    v, x0 = pl.pallas_call(
        _conv_gate_kernel,
        grid=(B, nc),
        in_specs=[spec(0), spec(1), spec(2),
                  pl.BlockSpec((3, HYENA_SHORT, tc), lambda bi, c: (0, 0, c)),
                  pl.BlockSpec((3, 1, tc), lambda bi, c: (0, 0, c))],
        out_specs=[pl.BlockSpec((L, tc), lambda bi, c: (bi, c))] * 2,
        out_shape=[jax.ShapeDtypeStruct((B * L, D), BF16)] * 2,
        compiler_params=_cparams("parallel", "parallel"),
        name="hyena_conv_gate",
    )(p, p, p, w, b)
    return v.reshape(B, L, D), x0.reshape(B, L, D)


def _filter_kernel(emb_ref, w1_ref, b1_ref, f1_ref, w2_ref, b2_ref, f2_ref, w3_ref, b3_ref, f3_ref,
                   wo_ref, dl_ref, k_ref, norm_ref, *, L):
    i = pl.program_id(0)
    tr = emb_ref.shape[0]
    hp = lax.Precision.HIGHEST
    emb = emb_ref[...]
    h = jnp.sin(f1_ref[...] * (jnp.dot(emb, w1_ref[...], precision=hp, preferred_element_type=F32) + b1_ref[...]))
    h = jnp.sin(f2_ref[...] * (jnp.dot(h, w2_ref[...], precision=hp, preferred_element_type=F32) + b2_ref[...]))
    h = jnp.sin(f3_ref[...] * (jnp.dot(h, w3_ref[...], precision=hp, preferred_element_type=F32) + b3_ref[...]))
    k = jnp.dot(h, wo_ref[...], precision=hp, preferred_element_type=F32)
    k = k * jnp.exp(-emb[:, 0:1] * dl_ref[...])
    row = i * tr + lax.broadcasted_iota(jnp.int32, (tr, 1), 0)
    k = jnp.where(row == L, 0.0, k)
    k_ref[...] = k

    @pl.when(i == 0)
    def _():
        norm_ref[...] = jnp.zeros_like(norm_ref)

    norm_ref[...] += jnp.sum(jnp.abs(k), axis=0, keepdims=True)


def hyena_filter(L, D, w1, b1, f1, w2, b2, f2, w3, b3, f3, w_out, tr=512):
    n = 2 * L
    tr = _tile(L, tr, 8)
    P = 128
    bands = (HYENA_EMB - 1) // 2
    d = np.arange(n)
    d = np.where(d <= L, d, n - d).clip(0, L - 1)
    t01 = np.linspace(0.0, 1.0, L)[d]
    wang = 2 * math.pi * d / L
    fr = np.linspace(1e-4, bands - 1, bands)
    emb = np.zeros((n, P), np.float32)
    emb[:, 0] = t01
    emb[:, 1:1 + bands] = np.cos(fr[None, :] * wang[:, None])
    emb[:, 1 + bands:1 + 2 * bands] = -np.sin(fr[None, :] * wang[:, None])

    def padw(w):
        return jnp.zeros((P, P), F32).at[:w.shape[0], :w.shape[1]].set(w)

    def padv(v):
        return jnp.zeros((1, P), F32).at[0, :v.shape[0]].set(v)

    wo = jnp.zeros((P, 2 * D), F32).at[:w_out.shape[0]].set(w_out)
    max_decay = math.log(HYENA_TARGET) / HYENA_FAST_DECAY
    min_decay = math.log(HYENA_TARGET) / HYENA_SLOW_DECAY
    deltas = jnp.abs(jnp.linspace(min_decay, max_decay, D, dtype=F32)).reshape(1, D)
    small = pl.BlockSpec((P, P), lambda i: (0, 0))
    vec = pl.BlockSpec((1, P), lambda i: (0, 0))
    kern = functools.partial(_filter_kernel, L=L)
    return pl.pallas_call(
        kern,
        grid=(n // tr,),
        in_specs=[pl.BlockSpec((tr, P), lambda i: (i, 0)),
                  small, vec, vec, small, vec, vec, small, vec, vec,
                  pl.BlockSpec((P, D), lambda i: (0, (i * tr) // L)),
                  pl.BlockSpec((1, D), lambda i: (0, 0))],
        out_specs=[pl.BlockSpec((tr, D), lambda i: (i, 0)), pl.BlockSpec((1, D), lambda i: (0, 0))],
        out_shape=[jax.ShapeDtypeStruct((n, D), F32), jax.ShapeDtypeStruct((1, D), F32)],
        compiler_params=_cparams("arbitrary"),
        name="hyena_filter",
    )(jnp.asarray(emb), padw(w1), padv(b1), padv(f1), padw(w2), padv(b2), padv(f2),
      padw(w3), padv(b3), padv(f3), wo, deltas)


def _fwd_major_kernel(*refs, n1, nb, has_imag):
    if has_imag:
        zr_ref, zi_ref, m_ref, ar_ref, ai_ref = refs
    else:
        zr_ref, m_ref, ar_ref, ai_ref = refs
    m = m_ref[...]
    half = n1 * FFT_GROUP

    def body(g, carry):
        def rows(blk):
            return pl.ds(pl.multiple_of(blk * FFT_MINOR + g * FFT_GROUP, FFT_GROUP), FFT_GROUP)

        def slab(ref):
            return jnp.concatenate([ref[rows(blk), :] for blk in range(nb)], axis=0).astype(BF16)

        pr = jnp.dot(m, slab(zr_ref), preferred_element_type=F32)
        if has_imag:
            pi = jnp.dot(m, slab(zi_ref), preferred_element_type=F32)
            ar, ai = pr[:half] - pi[half:], pi[:half] + pr[half:]
        else:
            ar, ai = pr[:half], pr[half:]
        for f1 in range(n1):
            sl = slice(f1 * FFT_GROUP, (f1 + 1) * FFT_GROUP)
            ar_ref[rows(f1), :] = ar[sl].astype(BF16)
            ai_ref[rows(f1), :] = ai[sl].astype(BF16)
        return carry

    lax.fori_loop(0, FFT_MINOR // FFT_GROUP, body, 0)


def fwd_major(z, tabs, pairs, tc=256):
    n1, nb = tabs["n1"], tabs["nb"]
    n = n1 * FFT_MINOR
    Bz, Lz, D = z.shape
    tc = _tile(D, tc)
    if pairs:
        P = Bz // 2
        m = tabs["m_fwd_half"]
        ins = [z, z, m]
        in_specs = [pl.BlockSpec((None, Lz, tc), lambda p, c: (2 * p, 0, c)),
                    pl.BlockSpec((None, Lz, tc), lambda p, c: (2 * p + 1, 0, c))]
        nblk = nb
    else:
        P = 1
        m = tabs["m_fwd_full"]
        ins = [z, m]
        in_specs = [pl.BlockSpec((None, Lz, tc), lambda p, c: (0, 0, c))]
        nblk = n1
    in_specs.append(pl.BlockSpec(m.shape, lambda p, c: (0, 0)))
    kern = functools.partial(_fwd_major_kernel, n1=n1, nb=nblk, has_imag=pairs)
    return pl.pallas_call(
        kern,
        grid=(P, D // tc),
        in_specs=in_specs,
        out_specs=[pl.BlockSpec((None, n, tc), lambda p, c: (p, 0, c))] * 2,
        out_shape=[jax.ShapeDtypeStruct((P, n, D), BF16)] * 2,
        compiler_params=_cparams("parallel", "parallel"),
        name="hyena_fwd_major",
    )(*ins)


def _spectrum_kernel(ar_ref, ai_ref, g_ref, s_ref, kr_ref, ki_ref):
    g = g_ref[...]
    h = FFT_MINOR
    pr = jnp.dot(g, ar_ref[...], preferred_element_type=F32)
    pi = jnp.dot(g, ai_ref[...], preferred_element_type=F32)
    s = s_ref[...]
    kr_ref[...] = (pr[:h] - pi[h:]) * s
    ki_ref[...] = (pi[:h] + pr[h:]) * s


def filter_spectrum(ar, ai, tabs, scale, tc=1024):
    _, n, D = ar.shape
    tc = _tile(D, tc)
    blk = pl.BlockSpec((None, FFT_MINOR, tc), lambda f, c: (0, f, c))
    out = pl.BlockSpec((FFT_MINOR, tc), lambda f, c: (f, c))
    return pl.pallas_call(
        _spectrum_kernel,
        grid=(tabs["n1"], D // tc),
        in_specs=[blk, blk, pl.BlockSpec((None, 2 * FFT_MINOR, FFT_MINOR), lambda f, c: (f, 0, 0)),
                  pl.BlockSpec((1, tc), lambda f, c: (0, c))],
        out_specs=[out, out],
        out_shape=[jax.ShapeDtypeStruct((n, D), F32)] * 2,
        compiler_params=_cparams("parallel", "parallel"),
        name="hyena_filter_spectrum",
    )(ar, ai, tabs["g"], scale)


def _minor_kernel(ar_ref, ai_ref, g_ref, gt_ref, kr_ref, ki_ref, br_ref, bi_ref):
    h = FFT_MINOR
    g = g_ref[...]
    pr = jnp.dot(g, ar_ref[...], preferred_element_type=F32)
    pi = jnp.dot(g, ai_ref[...], preferred_element_type=F32)
    xr, xi = pr[:h] - pi[h:], pi[:h] + pr[h:]
    kr, ki = kr_ref[...], ki_ref[...]
    yr = (xr * kr - xi * ki).astype(BF16)
    yi = (xr * ki + xi * kr).astype(BF16)
    gt = gt_ref[...]
    qr = jnp.dot(gt, yr, preferred_element_type=F32)
    qi = jnp.dot(gt, yi, preferred_element_type=F32)
    br_ref[...] = (qr[:h] + qi[h:]).astype(BF16)
    bi_ref[...] = (qi[:h] - qr[h:]).astype(BF16)


def minor_conv(ar, ai, tabs, kr, ki, tc=1024):
    P, n, D = ar.shape
    tc = _tile(D, tc)
    blk = pl.BlockSpec((None, FFT_MINOR, tc), lambda f, c, p: (p, f, c))
    tab = pl.BlockSpec((None, 2 * FFT_MINOR, FFT_MINOR), lambda f, c, p: (f, 0, 0))
    kblk = pl.BlockSpec((FFT_MINOR, tc), lambda f, c, p: (f, c))
    return pl.pallas_call(
        _minor_kernel,
        grid=(tabs["n1"], D // tc, P),
        in_specs=[blk, blk, tab, tab, kblk, kblk],
        out_specs=[blk, blk],
        out_shape=[jax.ShapeDtypeStruct((P, n, D), BF16)] * 2,
        compiler_params=_cparams("parallel", "parallel", "parallel"),
        name="hyena_minor_conv",
    )(ar, ai, tabs["g"], tabs["gt"], kr, ki)


def _inv_major_kernel(br_ref, bi_ref, m_ref, v_ref, x0_ref, bias_ref, o_ref, *, n1, nb):
    m = m_ref[...]
    half = nb * FFT_GROUP
    bias = bias_ref[...]

    def body(g, carry):
        def rows(blk):
            return pl.ds(pl.multiple_of(blk * FFT_MINOR + g * FFT_GROUP, FFT_GROUP), FFT_GROUP)

        def slab(ref):
            return jnp.concatenate([ref[rows(f1), :] for f1 in range(n1)], axis=0)

        pr = jnp.dot(m, slab(br_ref), preferred_element_type=F32)
        pi = jnp.dot(m, slab(bi_ref), preferred_element_type=F32)
        ys = (pr[:half] - pi[half:], pi[:half] + pr[half:])
        for s in range(2):
            for blk in range(nb):
                y = ys[s][blk * FFT_GROUP:(blk + 1) * FFT_GROUP]
                v = v_ref[s, rows(blk), :].astype(F32)
                x0 = x0_ref[s, rows(blk), :].astype(F32)
                o_ref[s, rows(blk), :] = ((y + v * bias) * x0).astype(o_ref.dtype)
        return carry

    lax.fori_loop(0, FFT_MINOR // FFT_GROUP, body, 0)


def inv_major(br, bi, tabs, v, x0, bias, tc=256):
    n1, nb = tabs["n1"], tabs["nb"]
    P, n, D = br.shape
    B, L, _ = v.shape
    tc = _tile(D, tc)
    m = tabs["m_inv"]
    blk = pl.BlockSpec((None, n, tc), lambda p, c: (p, 0, c))
    pair = pl.BlockSpec((2, L, tc), lambda p, c: (p, 0, c))
    kern = functools.partial(_inv_major_kernel, n1=n1, nb=nb)
    return pl.pallas_call(
        kern,
        grid=(P, D // tc),
        in_specs=[blk, blk, pl.BlockSpec(m.shape, lambda p, c: (0, 0)), pair, pair,
                  pl.BlockSpec((1, tc), lambda p, c: (0, c))],
        out_specs=pair,
        out_shape=jax.ShapeDtypeStruct((B, L, D), BF16),
        compiler_params=_cparams("parallel", "parallel"),
        name="hyena_inv_major",
    )(br, bi, m, v, x0, bias.reshape(1, D))


def hyena_mix(p, B, L, D, conv_w, conv_b, filt, bias):
    assert B % 2 == 0 and L % FFT_MINOR == 0
    tabs = _dft_tables(L)
    v, x0 = conv_gate(p, B, L, D, conv_w, conv_b)
    k_raw, k_norm = hyena_filter(L, D, *filt)
    kar, kai = fwd_major(k_raw[None], tabs, pairs=False)
    kr, ki = filter_spectrum(kar, kai, tabs, 1.0 / (2 * L * k_norm))
    ar, ai = fwd_major(v, tabs, pairs=True)
    br, bi = minor_conv(ar, ai, tabs, kr, ki)
    z = inv_major(br, bi, tabs, v, x0, bias)
    return z.reshape(B * L, D)


def kernel(x, c, ctx, c_ctx, ada_w, ada_b, norm1_g, norm2_g, na_w_qkv, na_w_o, na_q_g, na_k_g, na_rpb, hy_w_in, hy_b_in, hy_conv_w, hy_conv_b, hy_f_w1, hy_f_b1, hy_f_freq1, hy_f_w2, hy_f_b2, hy_f_freq2, hy_f_w3, hy_f_b3, hy_f_freq3, hy_f_wout, hy_bias, hy_w_out, hy_b_out, moe_router, moe_w1, moe_w3, moe_w2):
    B, N, D = x.shape
    CTX = ctx.shape[1]
    depth = ada_w.shape[0]
    mixer = [i % N_MIXERS for i in range(depth)]

    cond = jnp.concatenate([c, c_ctx[None, :], jnp.zeros((8 - B - 1, D), F32)], axis=0)
    mods_all = ada_all(cond, ada_w, ada_b)
    mods_all = mods_all.reshape(depth, 8, 6, 1, D).transpose(0, 2, 1, 3, 4)

    h = x.reshape(B * N, D)
    hc = ctx.reshape(B * CTX, D)
    zeros_d = jnp.zeros((D,), F32)
    for i in range(depth):
        j = i // N_MIXERS
        ctx_stream = any(mixer[l] == 0 for l in range(i + 1, depth))
        ctx_in = ctx_stream or mixer[i] == 0
        mods = mods_all[i]
        w1 = moe_w1[i].astype(BF16)
        w3 = moe_w3[i].astype(BF16)
        w2 = moe_w2[i].astype(BF16)
        if mixer[i] == 0:
            wqkv = na_w_qkv[j].astype(BF16)
            wo = na_w_o[j].astype(BF16)
            hg = jnp.stack([na_q_g[j], na_k_g[j]]).reshape(2, 1, HEAD_DIM)
            zeros_e = jnp.zeros((3 * D,), F32)
            qs = HEAD_DIM ** -0.5
            qkv = nm_matmul(h, norm1_g[i], mods, 0, 1, N, 0, wqkv, zeros_e, hg, qs)
            qkv_c = nm_matmul(hc, norm1_g[i], mods, 0, 1, B * CTX, B, wqkv, zeros_e, hg, qs)
            bias = _na_bias_tables(na_rpb[j], N // GRID_W)
            o, oc = na_attention(qkv, qkv_c, bias, B, N, CTX, D)
            h = matmul_residual(o, wo, zeros_d, mods, 2, N, 0, h)
            if ctx_stream:
                hc = matmul_residual(oc, wo, zeros_d, mods, 2, B * CTX, B, hc)
        else:
            win = hy_w_in[j].astype(BF16)
            wout = hy_w_out[j].astype(BF16)
            filt = (hy_f_w1[j], hy_f_b1[j], hy_f_freq1[j], hy_f_w2[j], hy_f_b2[j], hy_f_freq2[j],
                    hy_f_w3[j], hy_f_b3[j], hy_f_freq3[j], hy_f_wout[j])
            p = nm_matmul(h, norm1_g[i], mods, 0, 1, N, 0, win, hy_b_in[j], out_dtype=F32)
            z = hyena_mix(p, B, N, D, hy_conv_w[j], hy_conv_b[j], filt, hy_bias[j])
            h = matmul_residual(z, wout, hy_b_out[j], mods, 2, N, 0, h)
            if ctx_stream:
                pc = nm_matmul(hc, norm1_g[i], mods, 0, 1, B * CTX, B, win, hy_b_in[j], out_dtype=F32)
                zc = hyena_mix(pc, B, CTX, D, hy_conv_w[j], hy_conv_b[j], filt, hy_bias[j])
                hc = matmul_residual(zc, wout, hy_b_out[j], mods, 2, B * CTX, B, hc)
        h = ec_moe(h, norm2_g[i], mods, 3, 4, 5, N, 0, B, moe_router[i], w1, w3, w2)
        if ctx_stream:
            hc = ec_moe(hc, norm2_g[i], mods, 3, 4, 5, B * CTX, B, B, moe_router[i], w1, w3, w2)
    return h.reshape(B, N, D)
```

```python
import functools
import math

import jax
import jax.numpy as jnp
import numpy as np
from jax import lax
from jax.experimental import pallas as pl
from jax.experimental.pallas import tpu as pltpu

F32 = jnp.float32
BF16 = jnp.bfloat16

GRID_W = 64
N_MIXERS = 2
NORM_EPS = 1e-6
NEG_INF = -1e30
HEAD_DIM = 128
WIN_ROWS = 8
WIN_COLS = 16
HYENA_SHORT = 3
HYENA_EMB = 33
HYENA_FAST_DECAY = 0.3
HYENA_SLOW_DECAY = 1.5
HYENA_TARGET = 1e-2
N_EXPERTS = 16
CAPACITY_FACTOR = 2

Q_ROWS = 4
BAND_ROWS = 12

VMEM_LIMIT = 56 * 1024 * 1024


def _cparams(*sem):
    return pltpu.CompilerParams(dimension_semantics=sem, vmem_limit_bytes=VMEM_LIMIT)


def _tile(n, want, unit=128):
    if n <= want:
        return n
    t = (want // unit) * unit
    while n % t:
        t -= unit
    return t


def _ada_kernel(c_ref, w_ref, b_ref, o_ref):
    c = c_ref[...]
    s = c * jax.nn.sigmoid(c)
    o_ref[...] = jnp.dot(s.astype(BF16), w_ref[...].astype(BF16),
                         preferred_element_type=F32) + b_ref[...]


def ada_all(cond, ada_w, ada_b, tn=1024):
    L, D, E = ada_w.shape
    tn = _tile(E, tn)
    return pl.pallas_call(
        _ada_kernel,
        grid=(L, E // tn),
        in_specs=[
            pl.BlockSpec((8, D), lambda l, j: (0, 0)),
            pl.BlockSpec((None, D, tn), lambda l, j: (l, 0, j)),
            pl.BlockSpec((None, 1, tn), lambda l, j: (l, 0, j)),
        ],
        out_specs=pl.BlockSpec((None, 8, tn), lambda l, j: (l, 0, j)),
        out_shape=jax.ShapeDtypeStruct((L, 8, E), F32),
        compiler_params=_cparams("parallel", "parallel"),
        name="ada",
    )(cond, ada_w, ada_b.reshape(L, 1, E))


def _norm_mod(x, g, sh, sc):
    y = x * lax.rsqrt(jnp.mean(x * x, axis=-1, keepdims=True) + NORM_EPS) * g
    return y * (1.0 + sc) + sh


def _nm_matmul_kernel(x_ref, g_ref, sh_ref, sc_ref, w_ref, b_ref, hg_ref, o_ref, a_ref, *,
                      n_norm_sections, section_cols, q_scale):
    j = pl.program_id(1)

    @pl.when(j == 0)
    def _():
        a_ref[...] = _norm_mod(x_ref[...], g_ref[...], sh_ref[...], sc_ref[...]).astype(BF16)

    acc = jnp.dot(a_ref[...], w_ref[...], preferred_element_type=F32) + b_ref[...]
    tn = acc.shape[1]
    if n_norm_sections == 0:
        o_ref[...] = acc.astype(o_ref.dtype)
        return
    sec = (j * tn) // section_cols

    @pl.when(sec < n_norm_sections)
    def _():
        hg = hg_ref[...]
        mult = jnp.where(sec == 0, q_scale, 1.0).astype(F32)
        for h in range(tn // HEAD_DIM):
            c = acc[:, h * HEAD_DIM:(h + 1) * HEAD_DIM]
            c = c * lax.rsqrt(jnp.mean(c * c, axis=-1, keepdims=True) + NORM_EPS) * hg
            o_ref[:, h * HEAD_DIM:(h + 1) * HEAD_DIM] = (c * mult).astype(o_ref.dtype)

    @pl.when(sec >= n_norm_sections)
    def _():
        o_ref[...] = acc.astype(o_ref.dtype)


def nm_matmul(x, g, mods, k_shift, k_scale, group_rows, group_base, w, b, head_g=None,
              q_scale=1.0, out_dtype=BF16, tm=1024, tn=512):
    R, D = x.shape
    E = w.shape[1]
    tm = _tile(group_rows, tm, 8)
    tn = _tile(D, tn)
    n_norm = 0 if head_g is None else 2
    if head_g is None:
        head_g = jnp.ones((2, 1, HEAD_DIM), F32)

    def grp(i):
        return group_base + (i * tm) // group_rows

    def hg_map(i, j):
        return (jnp.minimum((j * tn) // D, 1), 0, 0)

    kern = functools.partial(_nm_matmul_kernel, n_norm_sections=n_norm, section_cols=D,
                             q_scale=q_scale)
    return pl.pallas_call(
        kern,
        grid=(R // tm, E // tn),
        in_specs=[
            pl.BlockSpec((tm, D), lambda i, j: (i, 0)),
            pl.BlockSpec((1, D), lambda i, j: (0, 0)),
            pl.BlockSpec((None, None, 1, D), lambda i, j: (k_shift, grp(i), 0, 0)),
            pl.BlockSpec((None, None, 1, D), lambda i, j: (k_scale, grp(i), 0, 0)),
            pl.BlockSpec((D, tn), lambda i, j: (0, j)),
            pl.BlockSpec((1, tn), lambda i, j: (0, j)),
            pl.BlockSpec((None, 1, HEAD_DIM), hg_map),
        ],
        out_specs=pl.BlockSpec((tm, tn), lambda i, j: (i, j)),
        out_shape=jax.ShapeDtypeStruct((R, E), out_dtype),
        scratch_shapes=[pltpu.VMEM((tm, D), BF16)],
        compiler_params=_cparams("parallel", "arbitrary"),
        name="nm_matmul",
    )(x, g.reshape(1, D), mods, mods, w, b.reshape(1, E), head_g)


def _mm_res_kernel(x_ref, w_ref, b_ref, gate_ref, res_ref, o_ref):
    y = jnp.dot(x_ref[...], w_ref[...], preferred_element_type=F32) + b_ref[...]
    o_ref[...] = res_ref[...] + gate_ref[...] * y


def matmul_residual(x, w, b, mods, k_gate, group_rows, group_base, res, tm=1024, tn=512):
    R, K = x.shape
    E = w.shape[1]
    tm = _tile(group_rows, tm, 8)
    tn = _tile(E, tn)

    def grp(i):
        return group_base + (i * tm) // group_rows

    return pl.pallas_call(
        _mm_res_kernel,
        grid=(R // tm, E // tn),
        in_specs=[
            pl.BlockSpec((tm, K), lambda i, j: (i, 0)),
            pl.BlockSpec((K, tn), lambda i, j: (0, j)),
            pl.BlockSpec((1, tn), lambda i, j: (0, j)),
            pl.BlockSpec((None, None, 1, tn), lambda i, j: (k_gate, grp(i), 0, j)),
            pl.BlockSpec((tm, tn), lambda i, j: (i, j)),
        ],
        out_specs=pl.BlockSpec((tm, tn), lambda i, j: (i, j)),
        out_shape=jax.ShapeDtypeStruct((R, E), F32),
        compiler_params=_cparams("parallel", "parallel"),
        name="matmul_residual",
    )(x, w, b.reshape(1, E), mods, res)


def _na_bias_tables(rpb, rows):
    W = GRID_W
    cols = np.arange(W)
    cs = np.clip(cols - WIN_COLS // 2, 0, W - WIN_COLS)
    in_win = (cols[None, :] >= cs[:, None]) & (cols[None, :] < cs[:, None] + WIN_COLS)
    dc = np.clip(cols[None, :] - cols[:, None] + WIN_COLS - 1, 0, 2 * WIN_COLS - 2)
    n_dr, n_dc = 2 * WIN_ROWS - 1, 2 * WIN_COLS - 1
    onehot_r = np.zeros((3, Q_ROWS, BAND_ROWS, n_dr), np.float32)
    mask = np.zeros((3, Q_ROWS, W, BAND_ROWS, W), bool)
    for var, r0 in enumerate((0, Q_ROWS, rows - Q_ROWS)):
        bs = _band_start(r0, rows)
        for i in range(Q_ROWS):
            r = r0 + i
            rs = int(np.clip(r - WIN_ROWS // 2, 0, rows - WIN_ROWS))
            for jj in range(BAND_ROWS):
                kr = bs + jj
                if rs <= kr < rs + WIN_ROWS:
                    onehot_r[var, i, jj, kr - r + WIN_ROWS - 1] = 1.0
                    mask[var, i, :, jj, :] = in_win
    onehot_c = (dc[:, :, None] == np.arange(n_dc)).astype(np.float32)
    hp = lax.Precision.HIGHEST
    t = jnp.einsum('hrc,vijr->hvijc', rpb.astype(F32), onehot_r, precision=hp)
    t = jnp.einsum('hvijc,qkc->hviqjk', t, onehot_c, precision=hp)
    t = jnp.where(mask[None], t, NEG_INF)
    return t.reshape(rpb.shape[0], 3, Q_ROWS * W, BAND_ROWS * W)


def _band_start(r0, rows):
    return int(np.clip(r0 - WIN_ROWS // 2, 0, rows - BAND_ROWS))


def _na_kernel(q_ref, k_ref, v_ref, qc_ref, kc_ref, vc_ref, bias_ref, o_ref, oc_ref, *, rows):
    W = GRID_W
    nblk = rows // Q_ROWS
    kc = kc_ref[...]
    vc = vc_ref[...]
    nt = (((1,), (1,)), ((), ()))

    def attend(q, parts):
        ss = [lax.dot_general(q, k, nt, preferred_element_type=F32) + (0.0 if bias is None else bias)
              for k, _, bias in parts]
        m = functools.reduce(jnp.maximum, [jnp.max(s, axis=-1, keepdims=True) for s in ss])
        ps = [jnp.exp(s - m) for s in ss]
        den = sum(jnp.sum(p, axis=-1, keepdims=True) for p in ps)
        o = sum(jnp.dot(p.astype(BF16), v, preferred_element_type=F32)
                for p, (_, v, _) in zip(ps, parts))
        return o / den

    def body(blk, carry):
        r0 = blk * Q_ROWS
        bs = jnp.clip(r0 - WIN_ROWS // 2, 0, rows - BAND_ROWS)
        var = jnp.where(blk == 0, 0, jnp.where(blk == nblk - 1, 2, 1))
        q0 = pl.multiple_of(r0 * W, Q_ROWS * W)
        k0 = pl.multiple_of(bs * W, W)
        q = q_ref[pl.ds(q0, Q_ROWS * W), :]
        kb = k_ref[pl.ds(k0, BAND_ROWS * W), :]
        vb = v_ref[pl.ds(k0, BAND_ROWS * W), :]
        o = attend(q, [(kb, vb, bias_ref[var]), (kc, vc, None)])
        o_ref[pl.ds(q0, Q_ROWS * W), :] = o.astype(o_ref.dtype)
        return carry

    lax.fori_loop(0, nblk, body, 0)
    oc_ref[...] = attend(qc_ref[...], [(kc, vc, None)]).astype(oc_ref.dtype)


def na_attention(qkv, qkv_c, bias, B, N, CTX, D):
    H = D // HEAD_DIM
    rows = N // GRID_W
    kern = functools.partial(_na_kernel, rows=rows)

    def spec(n, sec):
        return pl.BlockSpec((n, HEAD_DIM), lambda h, b: (b, sec * H + h))

    return pl.pallas_call(
        kern,
        grid=(H, B),
        in_specs=[spec(N, 0), spec(N, 1), spec(N, 2), spec(CTX, 0), spec(CTX, 1), spec(CTX, 2),
                  pl.BlockSpec((None, 3, Q_ROWS * GRID_W, BAND_ROWS * GRID_W),
                               lambda h, b: (h, 0, 0, 0))],
        out_specs=[pl.BlockSpec((N, HEAD_DIM), lambda h, b: (b, h)),
                   pl.BlockSpec((CTX, HEAD_DIM), lambda h, b: (b, h))],
        out_shape=[jax.ShapeDtypeStruct((B * N, D), BF16),
                   jax.ShapeDtypeStruct((B * CTX, D), BF16)],
        compiler_params=_cparams("parallel", "parallel"),
        name="na_attention",
    )(qkv, qkv, qkv, qkv_c, qkv_c, qkv_c, bias)


def _expert_kernel(idx_ref, a_hbm, w1_ref, w3_ref, w2_ref, gate_ref, o_ref, xbuf, sem, *, n_tok, cap):
    e, b = pl.program_id(0), pl.program_id(1)
    nb = pl.num_programs(1)
    step = e * nb + b
    slot = step % 2

    def gather(ee, bb, sl):
        base = (bb * pl.num_programs(0) + ee) * cap

        def body(c, carry):
            row = bb * n_tok + idx_ref[base + c]
            pltpu.make_async_copy(a_hbm.at[pl.ds(row, 1), :], xbuf.at[sl, pl.ds(c, 1), :],
                                  sem.at[sl]).start()
            return carry

        lax.fori_loop(0, cap, body, 0, unroll=8)

    @pl.when(step == 0)
    def _():
        gather(e, b, slot)

    @pl.when(step + 1 < pl.num_programs(0) * nb)
    def _():
        nxt = step + 1
        gather(nxt // nb, nxt % nb, 1 - slot)

    pltpu.make_async_copy(a_hbm.at[pl.ds(0, cap), :], xbuf.at[slot], sem.at[slot]).wait()
    x = xbuf[slot].astype(BF16)
    h1 = jnp.dot(x, w1_ref[...], preferred_element_type=F32)
    h3 = jnp.dot(x, w3_ref[...], preferred_element_type=F32)
    hid = (h1 * jax.nn.sigmoid(h1) * h3).astype(BF16)
    y = jnp.dot(hid, w2_ref[...], preferred_element_type=F32)
    o_ref[...] = y * gate_ref[...]


def expert_ffn(a, idx_flat, w1, w3, w2, gate, B, n_tok):
    E, D, Fh = w1.shape
    C = gate.shape[2]
    kern = functools.partial(_expert_kernel, n_tok=n_tok, cap=C)
    grid_spec = pltpu.PrefetchScalarGridSpec(
        num_scalar_prefetch=1,
        grid=(E, B),
        in_specs=[
            pl.BlockSpec(memory_space=pl.ANY),
            pl.BlockSpec((None, D, Fh), lambda e, b, idx: (e, 0, 0)),
            pl.BlockSpec((None, D, Fh), lambda e, b, idx: (e, 0, 0)),
            pl.BlockSpec((None, Fh, D), lambda e, b, idx: (e, 0, 0)),
            pl.BlockSpec((None, None, C, 1), lambda e, b, idx: (b, e, 0, 0)),
        ],
        out_specs=pl.BlockSpec((None, None, C, D), lambda e, b, idx: (b, e, 0, 0)),
        scratch_shapes=[pltpu.VMEM((2, C, D), F32), pltpu.SemaphoreType.DMA((2,))],
    )
    return pl.pallas_call(
        kern,
        grid_spec=grid_spec,
        out_shape=jax.ShapeDtypeStruct((B, E, C, D), F32),
        compiler_params=_cparams("arbitrary", "arbitrary"),
        name="expert_ffn",
    )(idx_flat, a, w1, w3, w2, gate)


def _moe_prep_kernel(x_ref, g_ref, sh_ref, sc_ref, r_ref, a_ref, aff_ref):
    a = _norm_mod(x_ref[...], g_ref[...], sh_ref[...], sc_ref[...])
    a_ref[...] = a.astype(a_ref.dtype)
    logits = lax.dot_general(r_ref[...], a.astype(BF16), (((1,), (1,)), ((), ())),
                             preferred_element_type=F32)
    m = jnp.max(logits, axis=0, keepdims=True)
    p = jnp.exp(logits - m)
    aff_ref[...] = p / jnp.sum(p, axis=0, keepdims=True)


def moe_prep(x, g, mods, k_shift, k_scale, group_rows, group_base, router_t, tm=512):
    R, D = x.shape
    E = router_t.shape[0]
    tm = _tile(group_rows, tm, 128)

    def grp(i):
        return group_base + (i * tm) // group_rows

    return pl.pallas_call(
        _moe_prep_kernel,
        grid=(R // tm,),
        in_specs=[
            pl.BlockSpec((tm, D), lambda i: (i, 0)),
            pl.BlockSpec((1, D), lambda i: (0, 0)),
            pl.BlockSpec((None, None, 1, D), lambda i: (k_shift, grp(i), 0, 0)),
            pl.BlockSpec((None, None, 1, D), lambda i: (k_scale, grp(i), 0, 0)),
            pl.BlockSpec((E, D), lambda i: (0, 0)),
        ],
        out_specs=[pl.BlockSpec((tm, D), lambda i: (i, 0)),
                   pl.BlockSpec((E, tm), lambda i: (0, i))],
        out_shape=[jax.ShapeDtypeStruct((R, D), F32), jax.ShapeDtypeStruct((E, R), F32)],
        compiler_params=_cparams("parallel"),
        name="moe_prep",
    )(x, g.reshape(1, D), mods, mods, router_t)


COMBINE_TOKENS = 64
CUMSUM_CHUNK = 256
TILE_TABLE_LANES = 128


def _route_kernel(aff_ref, tri_ref, ltri_ref, lt_ref, in_ref,
                  idx_ref, gate_ref, rank_ref, p0_ref, kmax_ref,
                  posi_s, key_s, erank_s, *, cap):
    E, N = aff_ref.shape
    aff = aff_ref[...]
    bits = pltpu.bitcast(aff, jnp.int32)
    cur = jnp.zeros((E, 1), jnp.int32)
    for bit in range(30, -1, -1):
        cand = cur | (1 << bit)
        cnt = jnp.sum(jnp.where(bits >= cand, 1.0, 0.0), axis=1, keepdims=True)
        cur = jnp.where(cnt >= cap, cand, cur)
    gt = bits > cur
    eq = bits == cur
    need = cap - jnp.sum(jnp.where(gt, 1.0, 0.0), axis=1, keepdims=True)

    def cumsum_incl(x):
        tri = tri_ref[...]
        w = min(CUMSUM_CHUNK, N)
        off = jnp.zeros((E, 1), F32)
        outs = []
        for c in range(N // w):
            xc = x[:, c * w:(c + 1) * w]
            outs.append(jnp.dot(xc.astype(BF16), tri[:w, :w], preferred_element_type=F32) + off)
            off = off + jnp.sum(xc, axis=1, keepdims=True)
        return jnp.concatenate(outs, axis=1) if len(outs) > 1 else outs[0]

    eqf = jnp.where(eq, 1.0, 0.0)
    tie_rank = cumsum_incl(eqf) - eqf
    sel = gt | (eq & (tie_rank < need))
    self_ = jnp.where(sel, 1.0, 0.0)
    pos_incl = cumsum_incl(self_)
    posi_s[...] = pos_incl
    key_s[...] = jnp.where(sel, pos_incl - self_, -1.0)
    selb = self_.astype(BF16)
    erank = jnp.dot(ltri_ref[...], selb, preferred_element_type=F32)
    erank_s[...] = erank
    p0_ref[...] = jnp.dot(selb, lt_ref[...], preferred_element_type=F32).astype(jnp.int32)
    count = jnp.sum(self_, axis=0, keepdims=True)
    kk = lax.broadcasted_iota(jnp.int32, (E, 1), 0).astype(F32)
    over = jnp.where(count > kk, 1.0, 0.0).astype(BF16)
    per_tile = jnp.dot(over, in_ref[...], preferred_element_type=F32)
    kmax_ref[...] = jnp.sum(jnp.where(per_tile > 0.0, 1.0, 0.0), axis=0, keepdims=True).astype(jnp.int32)

    ch = min(64, cap)

    def e_body(e, carry):
        pi = posi_s[pl.ds(e, 1), :]
        ky = key_s[pl.ds(e, 1), :]
        af = aff_ref[pl.ds(e, 1), :]
        rk = erank_s[pl.ds(e, 1), :]

        def c_body(cc, carry2):
            c0 = pl.multiple_of(cc * ch, ch)
            ccol = (c0 + lax.broadcasted_iota(jnp.int32, (ch, 1), 0)).astype(F32)
            hit = ky == ccol
            idx_ref[e, pl.ds(c0, ch), :] = jnp.sum(jnp.where(pi <= ccol, 1.0, 0.0), axis=1,
                                                   keepdims=True).astype(jnp.int32)
            gate_ref[e, pl.ds(c0, ch), :] = jnp.sum(jnp.where(hit, af, 0.0), axis=1, keepdims=True)
            rank_ref[e, pl.ds(c0, ch), :] = jnp.sum(jnp.where(hit, rk, 0.0), axis=1,
                                                    keepdims=True).astype(jnp.int32)
            return carry2

        lax.fori_loop(0, cap // ch, c_body, 0)
        return carry

    lax.fori_loop(0, E, e_body, 0)


def moe_route(aff_t, B, N, cap):
    E = aff_t.shape[0]
    tn = COMBINE_TOKENS
    lanes = TILE_TABLE_LANES
    assert N // tn + 1 <= lanes and N % tn == 0 and N % min(CUMSUM_CHUNK, N) == 0
    w = min(CUMSUM_CHUNK, N)
    tri = (np.arange(w)[:, None] <= np.arange(w)[None, :]).astype(np.float32)
    ltri = (np.arange(E)[None, :] < np.arange(E)[:, None]).astype(np.float32)
    tok = np.arange(N)[:, None]
    tile = np.arange(lanes)[None, :]
    before = (tok < tile * tn).astype(np.float32)
    inside = (tok // tn == tile).astype(np.float32)
    kern = functools.partial(_route_kernel, cap=cap)
    const = lambda shape: pl.BlockSpec(shape, lambda b: (0,) * len(shape))
    sel3 = pl.BlockSpec((None, E, cap, 1), lambda b: (b, 0, 0, 0))
    return pl.pallas_call(
        kern,
        grid=(B,),
        in_specs=[pl.BlockSpec((E, N), lambda b: (0, b)), const((w, w)), const((E, E)),
                  const((N, lanes)), const((N, lanes))],
        out_specs=[sel3, sel3, sel3,
                   pl.BlockSpec((None, E, lanes), lambda b: (b, 0, 0)),
                   pl.BlockSpec((None, 1, lanes), lambda b: (b, 0, 0))],
        out_shape=[jax.ShapeDtypeStruct((B, E, cap, 1), jnp.int32),
                   jax.ShapeDtypeStruct((B, E, cap, 1), F32),
                   jax.ShapeDtypeStruct((B, E, cap, 1), jnp.int32),
                   jax.ShapeDtypeStruct((B, E, lanes), jnp.int32),
                   jax.ShapeDtypeStruct((B, 1, lanes), jnp.int32)],
        scratch_shapes=[pltpu.VMEM((E, N), F32)] * 3,
        compiler_params=_cparams("parallel"),
        name="moe_route",
    )(aff_t, jnp.asarray(tri, BF16), jnp.asarray(ltri, BF16), jnp.asarray(before, BF16),
      jnp.asarray(inside, BF16))


def _combine_kernel(idx_ref, rank_ref, p0_ref, kmax_ref, ys_hbm, h_ref, g2_ref, o_ref, rbuf, acc, sem, *,
                    n_exp, cap, tn, lanes):
    b, t = pl.program_id(0), pl.program_id(1)
    nt = pl.num_programs(1)
    step = b * nt + t
    slot = step % 2
    D = h_ref.shape[1]

    def tile_hits(bb, tt):
        def body(e, tot):
            q = (bb * n_exp + e) * lanes + tt
            return tot + p0_ref[q + 1] - p0_ref[q]
        return lax.fori_loop(0, n_exp, body, 0)

    def fetch(bb, tt, sl):
        km = kmax_ref[bb * lanes + tt]
        for k in range(n_exp):
            @pl.when(k < km)
            def _():
                rbuf[sl, k * tn:(k + 1) * tn, :] = jnp.zeros((tn, D), F32)

        def e_body(e, carry):
            q = (bb * n_exp + e) * lanes + tt
            base = (bb * n_exp + e) * cap

            def c_body(c, carry2):
                src = base + c
                dst = rank_ref[src] * tn + idx_ref[src] - tt * tn
                pltpu.make_async_copy(ys_hbm.at[pl.ds(src, 1), :], rbuf.at[sl, pl.ds(dst, 1), :],
                                      sem.at[sl]).start()
                return carry2

            lax.fori_loop(p0_ref[q], p0_ref[q + 1], c_body, 0)
            return carry

        lax.fori_loop(0, n_exp, e_body, 0)

    @pl.when(step == 0)
    def _():
        fetch(b, t, slot)

    @pl.when(step + 1 < pl.num_programs(0) * nt)
    def _():
        nxt = step + 1
        fetch(nxt // nt, nxt % nt, 1 - slot)

    hits = tile_hits(b, t)

    bulk = pl.multiple_of((hits // 8) * 8, 8)

    @pl.when(bulk > 0)
    def _():
        pltpu.make_async_copy(ys_hbm.at[pl.ds(0, bulk), :], rbuf.at[slot, pl.ds(0, bulk), :],
                              sem.at[slot]).wait()

    def wait_row(i, carry):
        pltpu.make_async_copy(ys_hbm.at[pl.ds(0, 1), :], rbuf.at[slot, pl.ds(0, 1), :], sem.at[slot]).wait()
        return carry

    lax.fori_loop(0, hits - bulk, wait_row, 0)

    acc[...] = jnp.zeros_like(acc)
    km = kmax_ref[b * lanes + t]
    for k in range(n_exp):
        @pl.when(k < km)
        def _():
            acc[...] += rbuf[slot, k * tn:(k + 1) * tn, :]
    o_ref[...] = h_ref[...] + g2_ref[...] * acc[...]


def moe_combine(ys, h, mods, k_gate, group_rows, group_base, idx, rank, p0, kmax, B, N):
    R, D = h.shape
    E, cap = idx.shape[1], idx.shape[2]
    tn = COMBINE_TOKENS
    lanes = TILE_TABLE_LANES
    nt = N // tn

    def grp(b, t):
        return group_base + ((b * nt + t) * tn) // group_rows

    kern = functools.partial(_combine_kernel, n_exp=E, cap=cap, tn=tn, lanes=lanes)
    grid_spec = pltpu.PrefetchScalarGridSpec(
        num_scalar_prefetch=4,
        grid=(B, nt),
        in_specs=[
            pl.BlockSpec(memory_space=pl.ANY),
            pl.BlockSpec((tn, D), lambda b, t, *_: (b * nt + t, 0)),
            pl.BlockSpec((None, None, 1, D), lambda b, t, *_: (k_gate, grp(b, t), 0, 0)),
        ],
        out_specs=pl.BlockSpec((tn, D), lambda b, t, *_: (b * nt + t, 0)),
        scratch_shapes=[pltpu.VMEM((2, E * tn, D), F32), pltpu.VMEM((tn, D), F32),
                        pltpu.SemaphoreType.DMA((2,))],
    )
    return pl.pallas_call(
        kern,
        grid_spec=grid_spec,
        out_shape=jax.ShapeDtypeStruct((R, D), F32),
        compiler_params=_cparams("arbitrary", "arbitrary"),
        name="moe_combine",
    )(idx.reshape(-1), rank.reshape(-1), p0.reshape(-1), kmax.reshape(-1), ys, h, mods)


def ec_moe(h, g, mods, k_shift, k_scale, k_gate, group_rows, group_base, B, router, w1, w3, w2):
    R, D = h.shape
    N = R // B
    E = router.shape[1]
    cap = CAPACITY_FACTOR * N // E
    a, aff_t = moe_prep(h, g, mods, k_shift, k_scale, group_rows, group_base, router.T.astype(BF16))
    idx, gate, rank, p0, kmax = moe_route(aff_t, B, N, cap)
    ys = expert_ffn(a, idx.reshape(-1), w1, w3, w2, gate, B, N)
    return moe_combine(ys.reshape(B * E * cap, D), h, mods, k_gate, group_rows, group_base,
                       idx, rank, p0, kmax, B, N)


FFT_MINOR = 256
FFT_GROUP = 16


def _dft_tables(L):
    n = 2 * L
    n1 = n // FFT_MINOR
    nb = n1 // 2
    two_pi = 2.0 * math.pi
    a = jnp.arange(n1, dtype=jnp.int32)
    ang = two_pi * ((a[:, None] * a[None, :]) % n1).astype(F32) / n1
    cos1, sin1 = jnp.cos(ang), jnp.sin(ang)
    eye = jnp.eye(FFT_GROUP, dtype=F32)

    def kron(m):
        return jnp.kron(m, eye)

    f1 = jnp.arange(n1, dtype=jnp.int32)[:, None, None]
    f2 = jnp.arange(FFT_MINOR, dtype=jnp.int32)[None, :, None]
    n2 = jnp.arange(FFT_MINOR, dtype=jnp.int32)[None, None, :]
    ang = two_pi * ((n2 * (f1 + n1 * f2)) % n).astype(F32) / n
    gr, gi = jnp.cos(ang), -jnp.sin(ang)
    return dict(
        n1=n1, nb=nb,
        m_fwd_full=jnp.concatenate([kron(cos1), kron(-sin1)], axis=0).astype(BF16),
        m_fwd_half=jnp.concatenate([kron(cos1[:, :nb]), kron(-sin1[:, :nb])], axis=0).astype(BF16),
        m_inv=jnp.concatenate([kron(cos1[:nb]), kron(sin1[:nb])], axis=0).astype(BF16),
        g=jnp.concatenate([gr, gi], axis=1).astype(BF16),
        gt=jnp.concatenate([gr.transpose(0, 2, 1), gi.transpose(0, 2, 1)], axis=1).astype(BF16),
    )


def _conv_gate_kernel(x0_ref, x1_ref, v_ref, w_ref, b_ref, vo_ref, x0o_ref):
    L, C = x0_ref.shape
    row = lax.broadcasted_iota(jnp.int32, (L, C), 0)

    def conv(ref, k):
        x = ref[...]
        w = w_ref[k]
        prev = jnp.where(row == 0, 0.0, pltpu.roll(x, 1, 0))
        nxt = jnp.where(row == L - 1, 0.0, pltpu.roll(x, L - 1, 0))
        return prev * w[0:1] + x * w[1:2] + nxt * w[2:3] + b_ref[k]

    x0 = conv(x0_ref, 0)
    x1 = conv(x1_ref, 1)
    v = conv(v_ref, 2)
    vo_ref[...] = (v * x1).astype(vo_ref.dtype)
    x0o_ref[...] = x0.astype(x0o_ref.dtype)


def conv_gate(p, B, L, D, conv_w, conv_b, tc=256):
    tc = _tile(D, tc)
    nc = D // tc
    w = conv_w.reshape(HYENA_SHORT, 3, D).transpose(1, 0, 2)
    b = conv_b.reshape(3, 1, D)

    def spec(k):
        return pl.BlockSpec((L, tc), lambda bi, c: (bi, k * nc + c))

    v, x0 = pl.pallas_call(
        _conv_gate_kernel,
        grid=(B, nc),
        in_specs=[spec(0), spec(1), spec(2),
                  pl.BlockSpec((3, HYENA_SHORT, tc), lambda bi, c: (0, 0, c)),
                  pl.BlockSpec((3, 1, tc), lambda bi, c: (0, 0, c))],
        out_specs=[pl.BlockSpec((L, tc), lambda bi, c: (bi, c))] * 2,
        out_shape=[jax.ShapeDtypeStruct((B * L, D), BF16)] * 2,
        compiler_params=_cparams("parallel", "parallel"),
        name="hyena_conv_gate",
    )(p, p, p, w, b)
    return v.reshape(B, L, D), x0.reshape(B, L, D)


def _filter_kernel(emb_ref, w1_ref, b1_ref, f1_ref, w2_ref, b2_ref, f2_ref, w3_ref, b3_ref, f3_ref,
                   wo_ref, dl_ref, k_ref, norm_ref, *, L):
    i = pl.program_id(0)
    tr = emb_ref.shape[0]
    hp = lax.Precision.HIGHEST
    emb = emb_ref[...]
    h = jnp.sin(f1_ref[...] * (jnp.dot(emb, w1_ref[...], precision=hp, preferred_element_type=F32) + b1_ref[...]))
    h = jnp.sin(f2_ref[...] * (jnp.dot(h, w2_ref[...], precision=hp, preferred_element_type=F32) + b2_ref[...]))
    h = jnp.sin(f3_ref[...] * (jnp.dot(h, w3_ref[...], precision=hp, preferred_element_type=F32) + b3_ref[...]))
    k = jnp.dot(h, wo_ref[...], precision=hp, preferred_element_type=F32)
    k = k * jnp.exp(-emb[:, 0:1] * dl_ref[...])
    row = i * tr + lax.broadcasted_iota(jnp.int32, (tr, 1), 0)
    k = jnp.where(row == L, 0.0, k)
    k_ref[...] = k

    @pl.when(i == 0)
    def _():
        norm_ref[...] = jnp.zeros_like(norm_ref)

    norm_ref[...] += jnp.sum(jnp.abs(k), axis=0, keepdims=True)


def hyena_filter(L, D, w1, b1, f1, w2, b2, f2, w3, b3, f3, w_out, tr=512):
    n = 2 * L
    tr = _tile(L, tr, 8)
    P = 128
    bands = (HYENA_EMB - 1) // 2
    d = np.arange(n)
    d = np.where(d <= L, d, n - d).clip(0, L - 1)
    t01 = np.linspace(0.0, 1.0, L)[d]
    wang = 2 * math.pi * d / L
    fr = np.linspace(1e-4, bands - 1, bands)
    emb = np.zeros((n, P), np.float32)
    emb[:, 0] = t01
    emb[:, 1:1 + bands] = np.cos(fr[None, :] * wang[:, None])
    emb[:, 1 + bands:1 + 2 * bands] = -np.sin(fr[None, :] * wang[:, None])

    def padw(w):
        return jnp.zeros((P, P), F32).at[:w.shape[0], :w.shape[1]].set(w)

    def padv(v):
        return jnp.zeros((1, P), F32).at[0, :v.shape[0]].set(v)

    wo = jnp.zeros((P, 2 * D), F32).at[:w_out.shape[0]].set(w_out)
    max_decay = math.log(HYENA_TARGET) / HYENA_FAST_DECAY
    min_decay = math.log(HYENA_TARGET) / HYENA_SLOW_DECAY
    deltas = jnp.abs(jnp.linspace(min_decay, max_decay, D, dtype=F32)).reshape(1, D)
    small = pl.BlockSpec((P, P), lambda i: (0, 0))
    vec = pl.BlockSpec((1, P), lambda i: (0, 0))
    kern = functools.partial(_filter_kernel, L=L)
    return pl.pallas_call(
        kern,
        grid=(n // tr,),
        in_specs=[pl.BlockSpec((tr, P), lambda i: (i, 0)),
                  small, vec, vec, small, vec, vec, small, vec, vec,
                  pl.BlockSpec((P, D), lambda i: (0, (i * tr) // L)),
                  pl.BlockSpec((1, D), lambda i: (0, 0))],
        out_specs=[pl.BlockSpec((tr, D), lambda i: (i, 0)), pl.BlockSpec((1, D), lambda i: (0, 0))],
        out_shape=[jax.ShapeDtypeStruct((n, D), F32), jax.ShapeDtypeStruct((1, D), F32)],
        compiler_params=_cparams("arbitrary"),
        name="hyena_filter",
    )(jnp.asarray(emb), padw(w1), padv(b1), padv(f1), padw(w2), padv(b2), padv(f2),
      padw(w3), padv(b3), padv(f3), wo, deltas)


def _fwd_major_kernel(*refs, n1, nb, has_imag):
    if has_imag:
        zr_ref, zi_ref, m_ref, ar_ref, ai_ref = refs
    else:
        zr_ref, m_ref, ar_ref, ai_ref = refs
    m = m_ref[...]
    half = n1 * FFT_GROUP

    def body(g, carry):
        def rows(blk):
            return pl.ds(pl.multiple_of(blk * FFT_MINOR + g * FFT_GROUP, FFT_GROUP), FFT_GROUP)

        def slab(ref):
            return jnp.concatenate([ref[rows(blk), :] for blk in range(nb)], axis=0).astype(BF16)

        pr = jnp.dot(m, slab(zr_ref), preferred_element_type=F32)
        if has_imag:
            pi = jnp.dot(m, slab(zi_ref), preferred_element_type=F32)
            ar, ai = pr[:half] - pi[half:], pi[:half] + pr[half:]
        else:
            ar, ai = pr[:half], pr[half:]
        for f1 in range(n1):
            sl = slice(f1 * FFT_GROUP, (f1 + 1) * FFT_GROUP)
            ar_ref[rows(f1), :] = ar[sl].astype(BF16)
            ai_ref[rows(f1), :] = ai[sl].astype(BF16)
        return carry

    lax.fori_loop(0, FFT_MINOR // FFT_GROUP, body, 0)


def fwd_major(z, tabs, pairs, tc=256):
    n1, nb = tabs["n1"], tabs["nb"]
    n = n1 * FFT_MINOR
    Bz, Lz, D = z.shape
    tc = _tile(D, tc)
    if pairs:
        P = Bz // 2
        m = tabs["m_fwd_half"]
        ins = [z, z, m]
        in_specs = [pl.BlockSpec((None, Lz, tc), lambda p, c: (2 * p, 0, c)),
                    pl.BlockSpec((None, Lz, tc), lambda p, c: (2 * p + 1, 0, c))]
        nblk = nb
    else:
        P = 1
        m = tabs["m_fwd_full"]
        ins = [z, m]
        in_specs = [pl.BlockSpec((None, Lz, tc), lambda p, c: (0, 0, c))]
        nblk = n1
    in_specs.append(pl.BlockSpec(m.shape, lambda p, c: (0, 0)))
    kern = functools.partial(_fwd_major_kernel, n1=n1, nb=nblk, has_imag=pairs)
    return pl.pallas_call(
        kern,
        grid=(P, D // tc),
        in_specs=in_specs,
        out_specs=[pl.BlockSpec((None, n, tc), lambda p, c: (p, 0, c))] * 2,
        out_shape=[jax.ShapeDtypeStruct((P, n, D), BF16)] * 2,
        compiler_params=_cparams("parallel", "parallel"),
        name="hyena_fwd_major",
    )(*ins)


def _spectrum_kernel(ar_ref, ai_ref, g_ref, s_ref, kr_ref, ki_ref):
    g = g_ref[...]
    h = FFT_MINOR
    pr = jnp.dot(g, ar_ref[...], preferred_element_type=F32)
    pi = jnp.dot(g, ai_ref[...], preferred_element_type=F32)
    s = s_ref[...]
    kr_ref[...] = (pr[:h] - pi[h:]) * s
    ki_ref[...] = (pi[:h] + pr[h:]) * s


def filter_spectrum(ar, ai, tabs, scale, tc=1024):
    _, n, D = ar.shape
    tc = _tile(D, tc)
    blk = pl.BlockSpec((None, FFT_MINOR, tc), lambda f, c: (0, f, c))
    out = pl.BlockSpec((FFT_MINOR, tc), lambda f, c: (f, c))
    return pl.pallas_call(
        _spectrum_kernel,
        grid=(tabs["n1"], D // tc),
        in_specs=[blk, blk, pl.BlockSpec((None, 2 * FFT_MINOR, FFT_MINOR), lambda f, c: (f, 0, 0)),
                  pl.BlockSpec((1, tc), lambda f, c: (0, c))],
        out_specs=[out, out],
        out_shape=[jax.ShapeDtypeStruct((n, D), F32)] * 2,
        compiler_params=_cparams("parallel", "parallel"),
        name="hyena_filter_spectrum",
    )(ar, ai, tabs["g"], scale)


def _minor_kernel(ar_ref, ai_ref, g_ref, gt_ref, kr_ref, ki_ref, br_ref, bi_ref):
    h = FFT_MINOR
    g = g_ref[...]
    pr = jnp.dot(g, ar_ref[...], preferred_element_type=F32)
    pi = jnp.dot(g, ai_ref[...], preferred_element_type=F32)
    xr, xi = pr[:h] - pi[h:], pi[:h] + pr[h:]
    kr, ki = kr_ref[...], ki_ref[...]
    yr = (xr * kr - xi * ki).astype(BF16)
    yi = (xr * ki + xi * kr).astype(BF16)
    gt = gt_ref[...]
    qr = jnp.dot(gt, yr, preferred_element_type=F32)
    qi = jnp.dot(gt, yi, preferred_element_type=F32)
    br_ref[...] = (qr[:h] + qi[h:]).astype(BF16)
    bi_ref[...] = (qi[:h] - qr[h:]).astype(BF16)


def minor_conv(ar, ai, tabs, kr, ki, tc=1024):
    P, n, D = ar.shape
    tc = _tile(D, tc)
    blk = pl.BlockSpec((None, FFT_MINOR, tc), lambda f, c, p: (p, f, c))
    tab = pl.BlockSpec((None, 2 * FFT_MINOR, FFT_MINOR), lambda f, c, p: (f, 0, 0))
    kblk = pl.BlockSpec((FFT_MINOR, tc), lambda f, c, p: (f, c))
    return pl.pallas_call(
        _minor_kernel,
        grid=(tabs["n1"], D // tc, P),
        in_specs=[blk, blk, tab, tab, kblk, kblk],
        out_specs=[blk, blk],
        out_shape=[jax.ShapeDtypeStruct((P, n, D), BF16)] * 2,
        compiler_params=_cparams("parallel", "parallel", "parallel"),
        name="hyena_minor_conv",
    )(ar, ai, tabs["g"], tabs["gt"], kr, ki)


def _inv_major_kernel(br_ref, bi_ref, m_ref, v_ref, x0_ref, bias_ref, o_ref, *, n1, nb):
    m = m_ref[...]
    half = nb * FFT_GROUP
    bias = bias_ref[...]

    def body(g, carry):
        def rows(blk):
            return pl.ds(pl.multiple_of(blk * FFT_MINOR + g * FFT_GROUP, FFT_GROUP), FFT_GROUP)

        def slab(ref):
            return jnp.concatenate([ref[rows(f1), :] for f1 in range(n1)], axis=0)

        pr = jnp.dot(m, slab(br_ref), preferred_element_type=F32)
        pi = jnp.dot(m, slab(bi_ref), preferred_element_type=F32)
        ys = (pr[:half] - pi[half:], pi[:half] + pr[half:])
        for s in range(2):
            for blk in range(nb):
                y = ys[s][blk * FFT_GROUP:(blk + 1) * FFT_GROUP]
                v = v_ref[s, rows(blk), :].astype(F32)
                x0 = x0_ref[s, rows(blk), :].astype(F32)
                o_ref[s, rows(blk), :] = ((y + v * bias) * x0).astype(o_ref.dtype)
        return carry

    lax.fori_loop(0, FFT_MINOR // FFT_GROUP, body, 0)


def inv_major(br, bi, tabs, v, x0, bias, tc=256):
    n1, nb = tabs["n1"], tabs["nb"]
    P, n, D = br.shape
    B, L, _ = v.shape
    tc = _tile(D, tc)
    m = tabs["m_inv"]
    blk = pl.BlockSpec((None, n, tc), lambda p, c: (p, 0, c))
    pair = pl.BlockSpec((2, L, tc), lambda p, c: (p, 0, c))
    kern = functools.partial(_inv_major_kernel, n1=n1, nb=nb)
    return pl.pallas_call(
        kern,
        grid=(P, D // tc),
        in_specs=[blk, blk, pl.BlockSpec(m.shape, lambda p, c: (0, 0)), pair, pair,
                  pl.BlockSpec((1, tc), lambda p, c: (0, c))],
        out_specs=pair,
        out_shape=jax.ShapeDtypeStruct((B, L, D), BF16),
        compiler_params=_cparams("parallel", "parallel"),
        name="hyena_inv_major",
    )(br, bi, m, v, x0, bias.reshape(1, D))


def hyena_mix(p, B, L, D, conv_w, conv_b, filt, bias):
    assert B % 2 == 0 and L % FFT_MINOR == 0
    tabs = _dft_tables(L)
    v, x0 = conv_gate(p, B, L, D, conv_w, conv_b)
    k_raw, k_norm = hyena_filter(L, D, *filt)
    kar, kai = fwd_major(k_raw[None], tabs, pairs=False)
    kr, ki = filter_spectrum(kar, kai, tabs, 1.0 / (2 * L * k_norm))
    ar, ai = fwd_major(v, tabs, pairs=True)
    br, bi = minor_conv(ar, ai, tabs, kr, ki)
    z = inv_major(br, bi, tabs, v, x0, bias)
    return z.reshape(B * L, D)


def kernel(x, c, ctx, c_ctx, ada_w, ada_b, norm1_g, norm2_g, na_w_qkv, na_w_o, na_q_g, na_k_g, na_rpb, hy_w_in, hy_b_in, hy_conv_w, hy_conv_b, hy_f_w1, hy_f_b1, hy_f_freq1, hy_f_w2, hy_f_b2, hy_f_freq2, hy_f_w3, hy_f_b3, hy_f_freq3, hy_f_wout, hy_bias, hy_w_out, hy_b_out, moe_router, moe_w1, moe_w3, moe_w2):
    B, N, D = x.shape
    CTX = ctx.shape[1]
    depth = ada_w.shape[0]
    mixer = [i % N_MIXERS for i in range(depth)]

    cond = jnp.concatenate([c, c_ctx[None, :], jnp.zeros((8 - B - 1, D), F32)], axis=0)
    mods_all = ada_all(cond, ada_w, ada_b)
    mods_all = mods_all.reshape(depth, 8, 6, 1, D).transpose(0, 2, 1, 3, 4)

    h = x.reshape(B * N, D)
    hc = ctx.reshape(B * CTX, D)
    zeros_d = jnp.zeros((D,), F32)
    for i in range(depth):
        j = i // N_MIXERS
        ctx_stream = any(mixer[l] == 0 for l in range(i + 1, depth))
        ctx_in = ctx_stream or mixer[i] == 0
        mods = mods_all[i]
        w1 = moe_w1[i].astype(BF16)
        w3 = moe_w3[i].astype(BF16)
        w2 = moe_w2[i].astype(BF16)
        if mixer[i] == 0:
            wqkv = na_w_qkv[j].astype(BF16)
            wo = na_w_o[j].astype(BF16)
            hg = jnp.stack([na_q_g[j], na_k_g[j]]).reshape(2, 1, HEAD_DIM)
            zeros_e = jnp.zeros((3 * D,), F32)
            qs = HEAD_DIM ** -0.5
            qkv = nm_matmul(h, norm1_g[i], mods, 0, 1, N, 0, wqkv, zeros_e, hg, qs)
            qkv_c = nm_matmul(hc, norm1_g[i], mods, 0, 1, B * CTX, B, wqkv, zeros_e, hg, qs)
            bias = _na_bias_tables(na_rpb[j], N // GRID_W)
            o, oc = na_attention(qkv, qkv_c, bias, B, N, CTX, D)
            h = matmul_residual(o, wo, zeros_d, mods, 2, N, 0, h)
            if ctx_stream:
                hc = matmul_residual(oc, wo, zeros_d, mods, 2, B * CTX, B, hc)
        else:
            win = hy_w_in[j].astype(BF16)
            wout = hy_w_out[j].astype(BF16)
            filt = (hy_f_w1[j], hy_f_b1[j], hy_f_freq1[j], hy_f_w2[j], hy_f_b2[j], hy_f_freq2[j],
                    hy_f_w3[j], hy_f_b3[j], hy_f_freq3[j], hy_f_wout[j])
            p = nm_matmul(h, norm1_g[i], mods, 0, 1, N, 0, win, hy_b_in[j], out_dtype=F32)
            z = hyena_mix(p, B, N, D, hy_conv_w[j], hy_conv_b[j], filt, hy_bias[j])
            h = matmul_residual(z, wout, hy_b_out[j], mods, 2, N, 0, h)
            if ctx_stream:
                pc = nm_matmul(hc, norm1_g[i], mods, 0, 1, B * CTX, B, win, hy_b_in[j], out_dtype=F32)
                zc = hyena_mix(pc, B, CTX, D, hy_conv_w[j], hy_conv_b[j], filt, hy_bias[j])
                hc = matmul_residual(zc, wout, hy_b_out[j], mods, 2, B * CTX, B, hc)
        h = ec_moe(h, norm2_g[i], mods, 3, 4, 5, N, 0, B, moe_router[i], w1, w3, w2)
        if ctx_stream:
            hc = ec_moe(hc, norm2_g[i], mods, 3, 4, 5, B * CTX, B, B, moe_router[i], w1, w3, w2)
    return h.reshape(B, N, D)
```

```python
import functools
import math

import jax
import jax.numpy as jnp
import numpy as np
from jax import lax
from jax.experimental import pallas as pl
from jax.experimental.pallas import tpu as pltpu

F32 = jnp.float32
BF16 = jnp.bfloat16

GRID_W = 64
N_MIXERS = 2
NORM_EPS = 1e-6
NEG_INF = -1e30
HEAD_DIM = 128
WIN_ROWS = 8
WIN_COLS = 16
HYENA_SHORT = 3
HYENA_EMB = 33
HYENA_FAST_DECAY = 0.3
HYENA_SLOW_DECAY = 1.5
HYENA_TARGET = 1e-2
N_EXPERTS = 16
CAPACITY_FACTOR = 2

Q_ROWS = 4
BAND_ROWS = 12

VMEM_LIMIT = 56 * 1024 * 1024


def _cparams(*sem):
    return pltpu.CompilerParams(dimension_semantics=sem, vmem_limit_bytes=VMEM_LIMIT)


def _tile(n, want, unit=128):
    if n <= want:
        return n
    t = (want // unit) * unit
    while n % t:
        t -= unit
    return t


def _ada_kernel(c_ref, w_ref, b_ref, o_ref):
    c = c_ref[...]
    s = c * jax.nn.sigmoid(c)
    o_ref[...] = jnp.dot(s.astype(BF16), w_ref[...].astype(BF16),
                         preferred_element_type=F32) + b_ref[...]


def ada_all(cond, ada_w, ada_b, tn=1024):
    L, D, E = ada_w.shape
    tn = _tile(E, tn)
    return pl.pallas_call(
        _ada_kernel,
        grid=(L, E // tn),
        in_specs=[
            pl.BlockSpec((8, D), lambda l, j: (0, 0)),
            pl.BlockSpec((None, D, tn), lambda l, j: (l, 0, j)),
            pl.BlockSpec((None, 1, tn), lambda l, j: (l, 0, j)),
        ],
        out_specs=pl.BlockSpec((None, 8, tn), lambda l, j: (l, 0, j)),
        out_shape=jax.ShapeDtypeStruct((L, 8, E), F32),
        compiler_params=_cparams("parallel", "parallel"),
        name="ada",
    )(cond, ada_w, ada_b.reshape(L, 1, E))


def _norm_mod(x, g, sh, sc):
    y = x * lax.rsqrt(jnp.mean(x * x, axis=-1, keepdims=True) + NORM_EPS) * g
    return y * (1.0 + sc) + sh


def _nm_matmul_kernel(x_ref, g_ref, sh_ref, sc_ref, w_ref, b_ref, hg_ref, o_ref, a_ref, *,
                      n_norm_sections, section_cols, q_scale):
    j = pl.program_id(1)

    @pl.when(j == 0)
    def _():
        a_ref[...] = _norm_mod(x_ref[...], g_ref[...], sh_ref[...], sc_ref[...]).astype(BF16)

    acc = jnp.dot(a_ref[...], w_ref[...], preferred_element_type=F32) + b_ref[...]
    tn = acc.shape[1]
    if n_norm_sections == 0:
        o_ref[...] = acc.astype(o_ref.dtype)
        return
    sec = (j * tn) // section_cols

    @pl.when(sec < n_norm_sections)
    def _():
        hg = hg_ref[...]
        mult = jnp.where(sec == 0, q_scale, 1.0).astype(F32)
        for h in range(tn // HEAD_DIM):
            c = acc[:, h * HEAD_DIM:(h + 1) * HEAD_DIM]
            c = c * lax.rsqrt(jnp.mean(c * c, axis=-1, keepdims=True) + NORM_EPS) * hg
            o_ref[:, h * HEAD_DIM:(h + 1) * HEAD_DIM] = (c * mult).astype(o_ref.dtype)

    @pl.when(sec >= n_norm_sections)
    def _():
        o_ref[...] = acc.astype(o_ref.dtype)


def nm_matmul(x, g, mods, k_shift, k_scale, group_rows, group_base, w, b, head_g=None,
              q_scale=1.0, out_dtype=BF16, tm=1024, tn=512):
    R, D = x.shape
    E = w.shape[1]
    tm = _tile(group_rows, tm, 8)
    tn = _tile(D, tn)
    n_norm = 0 if head_g is None else 2
    if head_g is None:
        head_g = jnp.ones((2, 1, HEAD_DIM), F32)

    def grp(i):
        return group_base + (i * tm) // group_rows

    def hg_map(i, j):
        return (jnp.minimum((j * tn) // D, 1), 0, 0)

    kern = functools.partial(_nm_matmul_kernel, n_norm_sections=n_norm, section_cols=D,
                             q_scale=q_scale)
    return pl.pallas_call(
        kern,
        grid=(R // tm, E // tn),
        in_specs=[
            pl.BlockSpec((tm, D), lambda i, j: (i, 0)),
            pl.BlockSpec((1, D), lambda i, j: (0, 0)),
            pl.BlockSpec((None, None, 1, D), lambda i, j: (k_shift, grp(i), 0, 0)),
            pl.BlockSpec((None, None, 1, D), lambda i, j: (k_scale, grp(i), 0, 0)),
            pl.BlockSpec((D, tn), lambda i, j: (0, j)),
            pl.BlockSpec((1, tn), lambda i, j: (0, j)),
            pl.BlockSpec((None, 1, HEAD_DIM), hg_map),
        ],
        out_specs=pl.BlockSpec((tm, tn), lambda i, j: (i, j)),
        out_shape=jax.ShapeDtypeStruct((R, E), out_dtype),
        scratch_shapes=[pltpu.VMEM((tm, D), BF16)],
        compiler_params=_cparams("parallel", "arbitrary"),
        name="nm_matmul",
    )(x, g.reshape(1, D), mods, mods, w, b.reshape(1, E), head_g)


def _mm_res_kernel(x_ref, w_ref, b_ref, gate_ref, res_ref, o_ref):
    y = jnp.dot(x_ref[...], w_ref[...], preferred_element_type=F32) + b_ref[...]
    o_ref[...] = res_ref[...] + gate_ref[...] * y


def matmul_residual(x, w, b, mods, k_gate, group_rows, group_base, res, tm=1024, tn=512):
    R, K = x.shape
    E = w.shape[1]
    tm = _tile(group_rows, tm, 8)
    tn = _tile(E, tn)

    def grp(i):
        return group_base + (i * tm) // group_rows

    return pl.pallas_call(
        _mm_res_kernel,
        grid=(R // tm, E // tn),
        in_specs=[
            pl.BlockSpec((tm, K), lambda i, j: (i, 0)),
            pl.BlockSpec((K, tn), lambda i, j: (0, j)),
            pl.BlockSpec((1, tn), lambda i, j: (0, j)),
            pl.BlockSpec((None, None, 1, tn), lambda i, j: (k_gate, grp(i), 0, j)),
            pl.BlockSpec((tm, tn), lambda i, j: (i, j)),
        ],
        out_specs=pl.BlockSpec((tm, tn), lambda i, j: (i, j)),
        out_shape=jax.ShapeDtypeStruct((R, E), F32),
        compiler_params=_cparams("parallel", "parallel"),
        name="matmul_residual",
    )(x, w, b.reshape(1, E), mods, res)


def _na_bias_tables(rpb, rows):
    W = GRID_W
    cols = np.arange(W)
    cs = np.clip(cols - WIN_COLS // 2, 0, W - WIN_COLS)
    in_win = (cols[None, :] >= cs[:, None]) & (cols[None, :] < cs[:, None] + WIN_COLS)
    dc = np.clip(cols[None, :] - cols[:, None] + WIN_COLS - 1, 0, 2 * WIN_COLS - 2)
    n_dr, n_dc = 2 * WIN_ROWS - 1, 2 * WIN_COLS - 1
    onehot_r = np.zeros((3, Q_ROWS, BAND_ROWS, n_dr), np.float32)
    mask = np.zeros((3, Q_ROWS, W, BAND_ROWS, W), bool)
    for var, r0 in enumerate((0, Q_ROWS, rows - Q_ROWS)):
        bs = _band_start(r0, rows)
        for i in range(Q_ROWS):
            r = r0 + i
            rs = int(np.clip(r - WIN_ROWS // 2, 0, rows - WIN_ROWS))
            for jj in range(BAND_ROWS):
                kr = bs + jj
                if rs <= kr < rs + WIN_ROWS:
                    onehot_r[var, i, jj, kr - r + WIN_ROWS - 1] = 1.0
                    mask[var, i, :, jj, :] = in_win
    onehot_c = (dc[:, :, None] == np.arange(n_dc)).astype(np.float32)
    hp = lax.Precision.HIGHEST
    t = jnp.einsum('hrc,vijr->hvijc', rpb.astype(F32), onehot_r, precision=hp)
    t = jnp.einsum('hvijc,qkc->hviqjk', t, onehot_c, precision=hp)
    t = jnp.where(mask[None], t, NEG_INF)
    return t.reshape(rpb.shape[0], 3, Q_ROWS * W, BAND_ROWS * W)


def _band_start(r0, rows):
    return int(np.clip(r0 - WIN_ROWS // 2, 0, rows - BAND_ROWS))


def _na_kernel(q_ref, k_ref, v_ref, qc_ref, kc_ref, vc_ref, bias_ref, o_ref, oc_ref, *, rows):
    W = GRID_W
    nblk = rows // Q_ROWS
    kc = kc_ref[...]
    vc = vc_ref[...]
    nt = (((1,), (1,)), ((), ()))

    def attend(q, parts):
        ss = [lax.dot_general(q, k, nt, preferred_element_type=F32) + (0.0 if bias is None else bias)
              for k, _, bias in parts]
        m = functools.reduce(jnp.maximum, [jnp.max(s, axis=-1, keepdims=True) for s in ss])
        ps = [jnp.exp(s - m) for s in ss]
        den = sum(jnp.sum(p, axis=-1, keepdims=True) for p in ps)
        o = sum(jnp.dot(p.astype(BF16), v, preferred_element_type=F32)
                for p, (_, v, _) in zip(ps, parts))
        return o / den

    def body(blk, carry):
        r0 = blk * Q_ROWS
        bs = jnp.clip(r0 - WIN_ROWS // 2, 0, rows - BAND_ROWS)
        var = jnp.where(blk == 0, 0, jnp.where(blk == nblk - 1, 2, 1))
        q0 = pl.multiple_of(r0 * W, Q_ROWS * W)
        k0 = pl.multiple_of(bs * W, W)
        q = q_ref[pl.ds(q0, Q_ROWS * W), :]
        kb = k_ref[pl.ds(k0, BAND_ROWS * W), :]
        vb = v_ref[pl.ds(k0, BAND_ROWS * W), :]
        o = attend(q, [(kb, vb, bias_ref[var]), (kc, vc, None)])
        o_ref[pl.ds(q0, Q_ROWS * W), :] = o.astype(o_ref.dtype)
        return carry

    lax.fori_loop(0, nblk, body, 0, unroll=2)
    oc_ref[...] = attend(qc_ref[...], [(kc, vc, None)]).astype(oc_ref.dtype)


def na_attention(qkv, qkv_c, bias, B, N, CTX, D):
    H = D // HEAD_DIM
    rows = N // GRID_W
    kern = functools.partial(_na_kernel, rows=rows)

    def spec(n, sec):
        return pl.BlockSpec((n, HEAD_DIM), lambda h, b: (b, sec * H + h))

    return pl.pallas_call(
        kern,
        grid=(H, B),
        in_specs=[spec(N, 0), spec(N, 1), spec(N, 2), spec(CTX, 0), spec(CTX, 1), spec(CTX, 2),
                  pl.BlockSpec((None, 3, Q_ROWS * GRID_W, BAND_ROWS * GRID_W),
                               lambda h, b: (h, 0, 0, 0))],
        out_specs=[pl.BlockSpec((N, HEAD_DIM), lambda h, b: (b, h)),
                   pl.BlockSpec((CTX, HEAD_DIM), lambda h, b: (b, h))],
        out_shape=[jax.ShapeDtypeStruct((B * N, D), BF16),
                   jax.ShapeDtypeStruct((B * CTX, D), BF16)],
        compiler_params=_cparams("parallel", "parallel"),
        name="na_attention",
    )(qkv, qkv, qkv, qkv_c, qkv_c, qkv_c, bias)


def _expert_kernel(idx_ref, a_hbm, w1_ref, w3_ref, w2_ref, gate_ref, o_ref, xbuf, sem, *, n_tok, cap):
    e, b = pl.program_id(0), pl.program_id(1)
    nb = pl.num_programs(1)
    step = e * nb + b
    slot = step % 2

    def gather(ee, bb, sl):
        base = (bb * pl.num_programs(0) + ee) * cap

        def body(c, carry):
            row = bb * n_tok + idx_ref[base + c]
            pltpu.make_async_copy(a_hbm.at[pl.ds(row, 1), :], xbuf.at[sl, pl.ds(c, 1), :],
                                  sem.at[sl]).start()
            return carry

        lax.fori_loop(0, cap, body, 0, unroll=8)

    @pl.when(step == 0)
    def _():
        gather(e, b, slot)

    @pl.when(step + 1 < pl.num_programs(0) * nb)
    def _():
        nxt = step + 1
        gather(nxt // nb, nxt % nb, 1 - slot)

    pltpu.make_async_copy(a_hbm.at[pl.ds(0, cap), :], xbuf.at[slot], sem.at[slot]).wait()
    x = xbuf[slot].astype(BF16)
    h1 = jnp.dot(x, w1_ref[...], preferred_element_type=F32)
    h3 = jnp.dot(x, w3_ref[...], preferred_element_type=F32)
    hid = (h1 * jax.nn.sigmoid(h1) * h3).astype(BF16)
    y = jnp.dot(hid, w2_ref[...], preferred_element_type=F32) * gate_ref[...]
    o_ref[...] = _pack_bf16_pair(y)


def _pack_bf16_pair(y):
    half = y.shape[1] // 2
    hi = pltpu.bitcast(y[:, :half].astype(BF16).astype(F32), jnp.uint32)
    lo = pltpu.bitcast(y[:, half:].astype(BF16).astype(F32), jnp.uint32)
    return hi | (lo >> 16)


def _unpack_bf16_pair(u):
    hi = pltpu.bitcast(u & jnp.uint32(0xFFFF0000), F32)
    lo = pltpu.bitcast(u << 16, F32)
    return hi, lo


def expert_ffn(a, idx_flat, layer, w1, w3, w2, gate, B, n_tok):
    _, E, D, Fh = w1.shape
    C = gate.shape[2]
    kern = functools.partial(_expert_kernel, n_tok=n_tok, cap=C)
    grid_spec = pltpu.PrefetchScalarGridSpec(
        num_scalar_prefetch=1,
        grid=(E, B),
        in_specs=[
            pl.BlockSpec(memory_space=pl.ANY),
            pl.BlockSpec((None, None, D, Fh), lambda e, b, idx: (layer, e, 0, 0)),
            pl.BlockSpec((None, None, D, Fh), lambda e, b, idx: (layer, e, 0, 0)),
            pl.BlockSpec((None, None, Fh, D), lambda e, b, idx: (layer, e, 0, 0)),
            pl.BlockSpec((None, None, C, 1), lambda e, b, idx: (b, e, 0, 0)),
        ],
        out_specs=pl.BlockSpec((None, None, C, D // 2), lambda e, b, idx: (b, e, 0, 0)),
        scratch_shapes=[pltpu.VMEM((2, C, D), F32), pltpu.SemaphoreType.DMA((2,))],
    )
    return pl.pallas_call(
        kern,
        grid_spec=grid_spec,
        out_shape=jax.ShapeDtypeStruct((B, E, C, D // 2), jnp.uint32),
        compiler_params=_cparams("arbitrary", "arbitrary"),
        name="expert_ffn",
    )(idx_flat, a, w1, w3, w2, gate)


def _moe_prep_kernel(x_ref, g_ref, sh_ref, sc_ref, r_ref, a_ref, aff_ref):
    a = _norm_mod(x_ref[...], g_ref[...], sh_ref[...], sc_ref[...])
    a_ref[...] = a.astype(a_ref.dtype)
    logits = lax.dot_general(r_ref[...], a.astype(BF16), (((1,), (1,)), ((), ())),
                             preferred_element_type=F32)
    m = jnp.max(logits, axis=0, keepdims=True)
    p = jnp.exp(logits - m)
    aff_ref[...] = p / jnp.sum(p, axis=0, keepdims=True)


def moe_prep(x, g, mods, k_shift, k_scale, group_rows, group_base, router_t, tm=512):
    R, D = x.shape
    E = router_t.shape[0]
    tm = _tile(group_rows, tm, 128)

    def grp(i):
        return group_base + (i * tm) // group_rows

    return pl.pallas_call(
        _moe_prep_kernel,
        grid=(R // tm,),
        in_specs=[
            pl.BlockSpec((tm, D), lambda i: (i, 0)),
            pl.BlockSpec((1, D), lambda i: (0, 0)),
            pl.BlockSpec((None, None, 1, D), lambda i: (k_shift, grp(i), 0, 0)),
            pl.BlockSpec((None, None, 1, D), lambda i: (k_scale, grp(i), 0, 0)),
            pl.BlockSpec((E, D), lambda i: (0, 0)),
        ],
        out_specs=[pl.BlockSpec((tm, D), lambda i: (i, 0)),
                   pl.BlockSpec((E, tm), lambda i: (0, i))],
        out_shape=[jax.ShapeDtypeStruct((R, D), F32), jax.ShapeDtypeStruct((E, R), F32)],
        compiler_params=_cparams("parallel"),
        name="moe_prep",
    )(x, g.reshape(1, D), mods, mods, router_t)


COMBINE_TOKENS = 64
HIT_UNROLL = 4
CUMSUM_CHUNK = 256
TILE_TABLE_LANES = 128


def _route_kernel(aff_ref, tri_ref, ltri_ref, lt_ref, in_ref,
                  idx_ref, gate_ref, rank_ref, p0_ref, kmax_ref,
                  posi_s, key_s, erank_s, *, cap):
    E, N = aff_ref.shape
    aff = aff_ref[...]
    bits = pltpu.bitcast(aff, jnp.int32)
    cur = jnp.zeros((E, 1), jnp.int32)
    for bit in range(30, -1, -1):
        cand = cur | (1 << bit)
        cnt = jnp.sum(jnp.where(bits >= cand, 1.0, 0.0), axis=1, keepdims=True)
        cur = jnp.where(cnt >= cap, cand, cur)
    gt = bits > cur
    eq = bits == cur
    need = cap - jnp.sum(jnp.where(gt, 1.0, 0.0), axis=1, keepdims=True)

    def cumsum_incl(x):
        tri = tri_ref[...]
        w = min(CUMSUM_CHUNK, N)
        off = jnp.zeros((E, 1), F32)
        outs = []
        for c in range(N // w):
            xc = x[:, c * w:(c + 1) * w]
            outs.append(jnp.dot(xc.astype(BF16), tri[:w, :w], preferred_element_type=F32) + off)
            off = off + jnp.sum(xc, axis=1, keepdims=True)
        return jnp.concatenate(outs, axis=1) if len(outs) > 1 else outs[0]

    eqf = jnp.where(eq, 1.0, 0.0)
    tie_rank = cumsum_incl(eqf) - eqf
    sel = gt | (eq & (tie_rank < need))
    self_ = jnp.where(sel, 1.0, 0.0)
    pos_incl = cumsum_incl(self_)
    posi_s[...] = pos_incl
    key_s[...] = jnp.where(sel, pos_incl - self_, -1.0)
    selb = self_.astype(BF16)
    erank = jnp.dot(ltri_ref[...], selb, preferred_element_type=F32)
    in_tile = lax.broadcasted_iota(jnp.int32, (E, N), 1) & (COMBINE_TOKENS - 1)
    erank_s[...] = erank * COMBINE_TOKENS + in_tile.astype(F32)
    p0_ref[...] = jnp.dot(selb, lt_ref[...], preferred_element_type=F32).astype(jnp.int32)
    count = jnp.sum(self_, axis=0, keepdims=True)
    kk = lax.broadcasted_iota(jnp.int32, (E, 1), 0).astype(F32)
    over = jnp.where(count > kk, 1.0, 0.0).astype(BF16)
    per_tile = jnp.dot(over, in_ref[...], preferred_element_type=F32)
    kmax_ref[...] = jnp.sum(jnp.where(per_tile > 0.0, 1.0, 0.0), axis=0, keepdims=True).astype(jnp.int32)

    ch = min(64, cap)

    def e_body(e, carry):
        pi = posi_s[pl.ds(e, 1), :]
        ky = key_s[pl.ds(e, 1), :]
        af = aff_ref[pl.ds(e, 1), :]
        rk = erank_s[pl.ds(e, 1), :]

        def c_body(cc, carry2):
            c0 = pl.multiple_of(cc * ch, ch)
            ccol = (c0 + lax.broadcasted_iota(jnp.int32, (ch, 1), 0)).astype(F32)
            hit = ky == ccol
            idx_ref[e, pl.ds(c0, ch), :] = jnp.sum(jnp.where(pi <= ccol, 1.0, 0.0), axis=1,
                                                   keepdims=True).astype(jnp.int32)
            gate_ref[e, pl.ds(c0, ch), :] = jnp.sum(jnp.where(hit, af, 0.0), axis=1, keepdims=True)
            rank_ref[e, pl.ds(c0, ch), :] = jnp.sum(jnp.where(hit, rk, 0.0), axis=1,
                                                    keepdims=True).astype(jnp.int32)
            return carry2

        lax.fori_loop(0, cap // ch, c_body, 0)
        return carry

    lax.fori_loop(0, E, e_body, 0)


def moe_route(aff_t, B, N, cap):
    E = aff_t.shape[0]
    tn = COMBINE_TOKENS
    lanes = TILE_TABLE_LANES
    assert N // tn + 1 <= lanes and N % tn == 0 and N % min(CUMSUM_CHUNK, N) == 0
    w = min(CUMSUM_CHUNK, N)
    tri = (np.arange(w)[:, None] <= np.arange(w)[None, :]).astype(np.float32)
    ltri = (np.arange(E)[None, :] < np.arange(E)[:, None]).astype(np.float32)
    tok = np.arange(N)[:, None]
    tile = np.arange(lanes)[None, :]
    before = (tok < tile * tn).astype(np.float32)
    inside = (tok // tn == tile).astype(np.float32)
    kern = functools.partial(_route_kernel, cap=cap)
    const = lambda shape: pl.BlockSpec(shape, lambda b: (0,) * len(shape))
    sel3 = pl.BlockSpec((None, E, cap, 1), lambda b: (b, 0, 0, 0))
    return pl.pallas_call(
        kern,
        grid=(B,),
        in_specs=[pl.BlockSpec((E, N), lambda b: (0, b)), const((w, w)), const((E, E)),
                  const((N, lanes)), const((N, lanes))],
        out_specs=[sel3, sel3, sel3,
                   pl.BlockSpec((None, E, lanes), lambda b: (b, 0, 0)),
                   pl.BlockSpec((None, 1, lanes), lambda b: (b, 0, 0))],
        out_shape=[jax.ShapeDtypeStruct((B, E, cap, 1), jnp.int32),
                   jax.ShapeDtypeStruct((B, E, cap, 1), F32),
                   jax.ShapeDtypeStruct((B, E, cap, 1), jnp.int32),
                   jax.ShapeDtypeStruct((B, E, lanes), jnp.int32),
                   jax.ShapeDtypeStruct((B, 1, lanes), jnp.int32)],
        scratch_shapes=[pltpu.VMEM((E, N), F32)] * 3,
        compiler_params=_cparams("parallel"),
        name="moe_route",
    )(aff_t, jnp.asarray(tri, BF16), jnp.asarray(ltri, BF16), jnp.asarray(before, BF16),
      jnp.asarray(inside, BF16))


def _combine_kernel(rank_ref, p0_ref, kmax_ref, ys_hbm, h_ref, g2_ref, o_ref, rbuf, sem, *,
                    n_exp, cap, tn, lanes):
    b, t = pl.program_id(0), pl.program_id(1)
    nt = pl.num_programs(1)
    step = b * nt + t
    slot = step % 2
    D = h_ref.shape[1]
    half = D // 2

    def tile_hits(bb, tt):
        def body(e, tot):
            q = (bb * n_exp + e) * lanes + tt
            return tot + p0_ref[q + 1] - p0_ref[q]
        return lax.fori_loop(0, n_exp, body, 0)

    def fetch(bb, tt, sl):
        km = kmax_ref[bb * lanes + tt]
        for k in range(n_exp):
            @pl.when(k < km)
            def _():
                rbuf[sl, k * tn:(k + 1) * tn, :] = jnp.zeros((tn, half), jnp.uint32)

        def copy_row(src):
            pltpu.make_async_copy(ys_hbm.at[pl.ds(src, 1), :], rbuf.at[sl, pl.ds(rank_ref[src], 1), :],
                                  sem.at[sl]).start()

        def e_body(e, carry):
            q = (bb * n_exp + e) * lanes + tt
            lo = (bb * n_exp + e) * cap + p0_ref[q]
            n = p0_ref[q + 1] - p0_ref[q]
            groups = n // HIT_UNROLL

            def g_body(g, carry2):
                for u in range(HIT_UNROLL):
                    copy_row(lo + g * HIT_UNROLL + u)
                return carry2

            def r_body(c, carry2):
                copy_row(lo + c)
                return carry2

            lax.fori_loop(0, groups, g_body, 0)
            lax.fori_loop(groups * HIT_UNROLL, n, r_body, 0)
            return carry

        lax.fori_loop(0, n_exp, e_body, 0)

    @pl.when(step == 0)
    def _():
        fetch(b, t, slot)

    @pl.when(step + 1 < pl.num_programs(0) * nt)
    def _():
        nxt = step + 1
        fetch(nxt // nt, nxt % nt, 1 - slot)

    hits = tile_hits(b, t)

    bulk = pl.multiple_of((hits // 8) * 8, 8)

    @pl.when(bulk > 0)
    def _():
        pltpu.make_async_copy(ys_hbm.at[pl.ds(0, bulk), :], rbuf.at[slot, pl.ds(0, bulk), :],
                              sem.at[slot]).wait()

    def wait_row(i, carry):
        pltpu.make_async_copy(ys_hbm.at[pl.ds(0, 1), :], rbuf.at[slot, pl.ds(0, 1), :], sem.at[slot]).wait()
        return carry

    lax.fori_loop(0, hits - bulk, wait_row, 0)

    km = kmax_ref[b * lanes + t]
    g2 = g2_ref[...]
    for r in range(tn // 8):
        def k_body(k, tot):
            row = pl.multiple_of(k * tn + r * 8, 8)
            hi, lo = _unpack_bf16_pair(rbuf[slot, pl.ds(row, 8), :])
            return tot[0] + hi, tot[1] + lo

        zero = jnp.zeros((8, half), F32)
        hi, lo = lax.fori_loop(0, km, k_body, (zero, zero))
        rows = slice(r * 8, (r + 1) * 8)
        o_ref[rows, :half] = h_ref[rows, :half] + g2[:, :half] * hi
        o_ref[rows, half:] = h_ref[rows, half:] + g2[:, half:] * lo


def moe_combine(ys, h, mods, k_gate, group_rows, group_base, rank, p0, kmax, B, N):
    R, D = h.shape
    E, cap = rank.shape[1], rank.shape[2]
    tn = COMBINE_TOKENS
    lanes = TILE_TABLE_LANES
    nt = N // tn

    def grp(b, t):
        return group_base + ((b * nt + t) * tn) // group_rows

    kern = functools.partial(_combine_kernel, n_exp=E, cap=cap, tn=tn, lanes=lanes)
    grid_spec = pltpu.PrefetchScalarGridSpec(
        num_scalar_prefetch=3,
        grid=(B, nt),
        in_specs=[
            pl.BlockSpec(memory_space=pl.ANY),
            pl.BlockSpec((tn, D), lambda b, t, *_: (b * nt + t, 0)),
            pl.BlockSpec((None, None, 1, D), lambda b, t, *_: (k_gate, grp(b, t), 0, 0)),
        ],
        out_specs=pl.BlockSpec((tn, D), lambda b, t, *_: (b * nt + t, 0)),
        scratch_shapes=[pltpu.VMEM((2, E * tn, D // 2), jnp.uint32), pltpu.SemaphoreType.DMA((2,))],
    )
    return pl.pallas_call(
        kern,
        grid_spec=grid_spec,
        out_shape=jax.ShapeDtypeStruct((R, D), F32),
        compiler_params=_cparams("arbitrary", "arbitrary"),
        name="moe_combine",
    )(rank.reshape(-1), p0.reshape(-1), kmax.reshape(-1), ys, h, mods)


def ec_moe(h, g, mods, k_shift, k_scale, k_gate, group_rows, group_base, B, router, layer, w1, w3, w2):
    R, D = h.shape
    N = R // B
    E = router.shape[1]
    cap = CAPACITY_FACTOR * N // E
    a, aff_t = moe_prep(h, g, mods, k_shift, k_scale, group_rows, group_base, router.T.astype(BF16))
    idx, gate, rank, p0, kmax = moe_route(aff_t, B, N, cap)
    ys = expert_ffn(a, idx.reshape(-1), layer, w1, w3, w2, gate, B, N)
    return moe_combine(ys.reshape(B * E * cap, D // 2), h, mods, k_gate, group_rows, group_base,
                       rank, p0, kmax, B, N)


FFT_MINOR = 256
FFT_GROUP = 16


def _dft_tables(L):
    n = 2 * L
    n1 = n // FFT_MINOR
    nb = n1 // 2
    two_pi = 2.0 * math.pi
    a = jnp.arange(n1, dtype=jnp.int32)
    ang = two_pi * ((a[:, None] * a[None, :]) % n1).astype(F32) / n1
    cos1, sin1 = jnp.cos(ang), jnp.sin(ang)
    eye = jnp.eye(FFT_GROUP, dtype=F32)

    def kron(m):
        return jnp.kron(m, eye)

    f1 = jnp.arange(n1, dtype=jnp.int32)[:, None, None]
    f2 = jnp.arange(FFT_MINOR, dtype=jnp.int32)[None, :, None]
    n2 = jnp.arange(FFT_MINOR, dtype=jnp.int32)[None, None, :]
    ang = two_pi * ((n2 * (f1 + n1 * f2)) % n).astype(F32) / n
    gr, gi = jnp.cos(ang), -jnp.sin(ang)
    return dict(
        n1=n1, nb=nb,
        m_fwd_full=jnp.concatenate([kron(cos1), kron(-sin1)], axis=0).astype(BF16),
        m_fwd_half=jnp.concatenate([kron(cos1[:, :nb]), kron(-sin1[:, :nb])], axis=0).astype(BF16),
        m_inv=jnp.concatenate([kron(cos1[:nb]), kron(sin1[:nb])], axis=0).astype(BF16),
        g=jnp.concatenate([gr, gi], axis=1).astype(BF16),
        gt=jnp.concatenate([gr.transpose(0, 2, 1), gi.transpose(0, 2, 1)], axis=1).astype(BF16),
    )


def _conv_gate_kernel(x0_ref, x1_ref, v_ref, w_ref, b_ref, vo_ref, x0o_ref):
    L, C = x0_ref.shape
    row = lax.broadcasted_iota(jnp.int32, (L, C), 0)

    def conv(ref, k):
        x = ref[...]
        w = w_ref[k]
        prev = jnp.where(row == 0, 0.0, pltpu.roll(x, 1, 0))
        nxt = jnp.where(row == L - 1, 0.0, pltpu.roll(x, L - 1, 0))
        return prev * w[0:1] + x * w[1:2] + nxt * w[2:3] + b_ref[k]

    x0 = conv(x0_ref, 0)
    x1 = conv(x1_ref, 1)
    v = conv(v_ref, 2)
    vo_ref[...] = (v * x1).astype(vo_ref.dtype)
    x0o_ref[...] = x0.astype(x0o_ref.dtype)


def conv_gate(p, B, L, D, conv_w, conv_b, tc=256):
    tc = _tile(D, tc)
    nc = D // tc
    w = conv_w.reshape(HYENA_SHORT, 3, D).transpose(1, 0, 2)
    b = conv_b.reshape(3, 1, D)

    def spec(k):
        return pl.BlockSpec((L, tc), lambda bi, c: (bi, k * nc + c))

    v, x0 = pl.pallas_call(
        _conv_gate_kernel,
        grid=(B, nc),
        in_specs=[spec(0), spec(1), spec(2),
                  pl.BlockSpec((3, HYENA_SHORT, tc), lambda bi, c: (0, 0, c)),
                  pl.BlockSpec((3, 1, tc), lambda bi, c: (0, 0, c))],
        out_specs=[pl.BlockSpec((L, tc), lambda bi, c: (bi, c))] * 2,
        out_shape=[jax.ShapeDtypeStruct((B * L, D), BF16)] * 2,
        compiler_params=_cparams("parallel", "parallel"),
        name="hyena_conv_gate",
    )(p, p, p, w, b)
    return v.reshape(B, L, D), x0.reshape(B, L, D)


def _filter_kernel(emb_ref, w1_ref, b1_ref, f1_ref, w2_ref, b2_ref, f2_ref, w3_ref, b3_ref, f3_ref,
                   wo_ref, dl_ref, k_ref, norm_ref, *, L):
    i = pl.program_id(0)
    tr = emb_ref.shape[0]
    hp = lax.Precision.HIGHEST
    emb = emb_ref[...]
    h = jnp.sin(f1_ref[...] * (jnp.dot(emb, w1_ref[...], precision=hp, preferred_element_type=F32) + b1_ref[...]))
    h = jnp.sin(f2_ref[...] * (jnp.dot(h, w2_ref[...], precision=hp, preferred_element_type=F32) + b2_ref[...]))
    h = jnp.sin(f3_ref[...] * (jnp.dot(h, w3_ref[...], precision=hp, preferred_element_type=F32) + b3_ref[...]))
    k = jnp.dot(h, wo_ref[...], precision=hp, preferred_element_type=F32)
    k = k * jnp.exp(-emb[:, 0:1] * dl_ref[...])
    row = i * tr + lax.broadcasted_iota(jnp.int32, (tr, 1), 0)
    k = jnp.where(row == L, 0.0, k)
    k_ref[...] = k

    @pl.when(i == 0)
    def _():
        norm_ref[...] = jnp.zeros_like(norm_ref)

    norm_ref[...] += jnp.sum(jnp.abs(k), axis=0, keepdims=True)


def hyena_filter(L, D, w1, b1, f1, w2, b2, f2, w3, b3, f3, w_out, tr=512):
    n = 2 * L
    tr = _tile(L, tr, 8)
    P = 128
    bands = (HYENA_EMB - 1) // 2
    d = np.arange(n)
    d = np.where(d <= L, d, n - d).clip(0, L - 1)
    t01 = np.linspace(0.0, 1.0, L)[d]
    wang = 2 * math.pi * d / L
    fr = np.linspace(1e-4, bands - 1, bands)
    emb = np.zeros((n, P), np.float32)
    emb[:, 0] = t01
    emb[:, 1:1 + bands] = np.cos(fr[None, :] * wang[:, None])
    emb[:, 1 + bands:1 + 2 * bands] = -np.sin(fr[None, :] * wang[:, None])

    def padw(w):
        return jnp.zeros((P, P), F32).at[:w.shape[0], :w.shape[1]].set(w)

    def padv(v):
        return jnp.zeros((1, P), F32).at[0, :v.shape[0]].set(v)

    wo = jnp.zeros((P, 2 * D), F32).at[:w_out.shape[0]].set(w_out)
    max_decay = math.log(HYENA_TARGET) / HYENA_FAST_DECAY
    min_decay = math.log(HYENA_TARGET) / HYENA_SLOW_DECAY
    deltas = jnp.abs(jnp.linspace(min_decay, max_decay, D, dtype=F32)).reshape(1, D)
    small = pl.BlockSpec((P, P), lambda i: (0, 0))
    vec = pl.BlockSpec((1, P), lambda i: (0, 0))
    kern = functools.partial(_filter_kernel, L=L)
    return pl.pallas_call(
        kern,
        grid=(n // tr,),
        in_specs=[pl.BlockSpec((tr, P), lambda i: (i, 0)),
                  small, vec, vec, small, vec, vec, small, vec, vec,
                  pl.BlockSpec((P, D), lambda i: (0, (i * tr) // L)),
                  pl.BlockSpec((1, D), lambda i: (0, 0))],
        out_specs=[pl.BlockSpec((tr, D), lambda i: (i, 0)), pl.BlockSpec((1, D), lambda i: (0, 0))],
        out_shape=[jax.ShapeDtypeStruct((n, D), F32), jax.ShapeDtypeStruct((1, D), F32)],
        compiler_params=_cparams("arbitrary"),
        name="hyena_filter",
    )(jnp.asarray(emb), padw(w1), padv(b1), padv(f1), padw(w2), padv(b2), padv(f2),
      padw(w3), padv(b3), padv(f3), wo, deltas)


def _fwd_major_kernel(*refs, n1, nb, has_imag):
    if has_imag:
        zr_ref, zi_ref, m_ref, ar_ref, ai_ref = refs
    else:
        zr_ref, m_ref, ar_ref, ai_ref = refs
    m = m_ref[...]
    half = n1 * FFT_GROUP

    def body(g, carry):
        def rows(blk):
            return pl.ds(pl.multiple_of(blk * FFT_MINOR + g * FFT_GROUP, FFT_GROUP), FFT_GROUP)

        def slab(ref):
            return jnp.concatenate([ref[rows(blk), :] for blk in range(nb)], axis=0).astype(BF16)

        pr = jnp.dot(m, slab(zr_ref), preferred_element_type=F32)
        if has_imag:
            pi = jnp.dot(m, slab(zi_ref), preferred_element_type=F32)
            ar, ai = pr[:half] - pi[half:], pi[:half] + pr[half:]
        else:
            ar, ai = pr[:half], pr[half:]
        for f1 in range(n1):
            sl = slice(f1 * FFT_GROUP, (f1 + 1) * FFT_GROUP)
            ar_ref[rows(f1), :] = ar[sl].astype(BF16)
            ai_ref[rows(f1), :] = ai[sl].astype(BF16)
        return carry

    lax.fori_loop(0, FFT_MINOR // FFT_GROUP, body, 0)


def fwd_major(z, tabs, pairs, tc=256):
    n1, nb = tabs["n1"], tabs["nb"]
    n = n1 * FFT_MINOR
    Bz, Lz, D = z.shape
    tc = _tile(D, tc)
    if pairs:
        P = Bz // 2
        m = tabs["m_fwd_half"]
        ins = [z, z, m]
        in_specs = [pl.BlockSpec((None, Lz, tc), lambda p, c: (2 * p, 0, c)),
                    pl.BlockSpec((None, Lz, tc), lambda p, c: (2 * p + 1, 0, c))]
        nblk = nb
    else:
        P = 1
        m = tabs["m_fwd_full"]
        ins = [z, m]
        in_specs = [pl.BlockSpec((None, Lz, tc), lambda p, c: (0, 0, c))]
        nblk = n1
    in_specs.append(pl.BlockSpec(m.shape, lambda p, c: (0, 0)))
    kern = functools.partial(_fwd_major_kernel, n1=n1, nb=nblk, has_imag=pairs)
    return pl.pallas_call(
        kern,
        grid=(P, D // tc),
        in_specs=in_specs,
        out_specs=[pl.BlockSpec((None, n, tc), lambda p, c: (p, 0, c))] * 2,
        out_shape=[jax.ShapeDtypeStruct((P, n, D), BF16)] * 2,
        compiler_params=_cparams("parallel", "parallel"),
        name="hyena_fwd_major",
    )(*ins)


def _spectrum_kernel(ar_ref, ai_ref, g_ref, s_ref, kr_ref, ki_ref):
    g = g_ref[...]
    h = FFT_MINOR
    pr = jnp.dot(g, ar_ref[...], preferred_element_type=F32)
    pi = jnp.dot(g, ai_ref[...], preferred_element_type=F32)
    s = s_ref[...]
    kr_ref[...] = (pr[:h] - pi[h:]) * s
    ki_ref[...] = (pi[:h] + pr[h:]) * s


def filter_spectrum(ar, ai, tabs, scale, tc=1024):
    _, n, D = ar.shape
    tc = _tile(D, tc)
    blk = pl.BlockSpec((None, FFT_MINOR, tc), lambda f, c: (0, f, c))
    out = pl.BlockSpec((FFT_MINOR, tc), lambda f, c: (f, c))
    return pl.pallas_call(
        _spectrum_kernel,
        grid=(tabs["n1"], D // tc),
        in_specs=[blk, blk, pl.BlockSpec((None, 2 * FFT_MINOR, FFT_MINOR), lambda f, c: (f, 0, 0)),
                  pl.BlockSpec((1, tc), lambda f, c: (0, c))],
        out_specs=[out, out],
        out_shape=[jax.ShapeDtypeStruct((n, D), F32)] * 2,
        compiler_params=_cparams("parallel", "parallel"),
        name="hyena_filter_spectrum",
    )(ar, ai, tabs["g"], scale)


def _minor_kernel(ar_ref, ai_ref, g_ref, gt_ref, kr_ref, ki_ref, br_ref, bi_ref):
    h = FFT_MINOR
    g = g_ref[...]
    pr = jnp.dot(g, ar_ref[...], preferred_element_type=F32)
    pi = jnp.dot(g, ai_ref[...], preferred_element_type=F32)
    xr, xi = pr[:h] - pi[h:], pi[:h] + pr[h:]
    kr, ki = kr_ref[...], ki_ref[...]
    yr = (xr * kr - xi * ki).astype(BF16)
    yi = (xr * ki + xi * kr).astype(BF16)
    gt = gt_ref[...]
    qr = jnp.dot(gt, yr, preferred_element_type=F32)
    qi = jnp.dot(gt, yi, preferred_element_type=F32)
    br_ref[...] = (qr[:h] + qi[h:]).astype(BF16)
    bi_ref[...] = (qi[:h] - qr[h:]).astype(BF16)


def minor_conv(ar, ai, tabs, kr, ki, tc=1024):
    P, n, D = ar.shape
    tc = _tile(D, tc)
    blk = pl.BlockSpec((None, FFT_MINOR, tc), lambda f, c, p: (p, f, c))
    tab = pl.BlockSpec((None, 2 * FFT_MINOR, FFT_MINOR), lambda f, c, p: (f, 0, 0))
    kblk = pl.BlockSpec((FFT_MINOR, tc), lambda f, c, p: (f, c))
    return pl.pallas_call(
        _minor_kernel,
        grid=(tabs["n1"], D // tc, P),
        in_specs=[blk, blk, tab, tab, kblk, kblk],
        out_specs=[blk, blk],
        out_shape=[jax.ShapeDtypeStruct((P, n, D), BF16)] * 2,
        compiler_params=_cparams("parallel", "parallel", "parallel"),
        name="hyena_minor_conv",
    )(ar, ai, tabs["g"], tabs["gt"], kr, ki)


def _inv_major_kernel(br_ref, bi_ref, m_ref, v_ref, x0_ref, bias_ref, o_ref, *, n1, nb):
    m = m_ref[...]
    half = nb * FFT_GROUP
    bias = bias_ref[...]

    def body(g, carry):
        def rows(blk):
            return pl.ds(pl.multiple_of(blk * FFT_MINOR + g * FFT_GROUP, FFT_GROUP), FFT_GROUP)

        def slab(ref):
            return jnp.concatenate([ref[rows(f1), :] for f1 in range(n1)], axis=0)

        pr = jnp.dot(m, slab(br_ref), preferred_element_type=F32)
        pi = jnp.dot(m, slab(bi_ref), preferred_element_type=F32)
        ys = (pr[:half] - pi[half:], pi[:half] + pr[half:])
        for s in range(2):
            for blk in range(nb):
                y = ys[s][blk * FFT_GROUP:(blk + 1) * FFT_GROUP]
                v = v_ref[s, rows(blk), :].astype(F32)
                x0 = x0_ref[s, rows(blk), :].astype(F32)
                o_ref[s, rows(blk), :] = ((y + v * bias) * x0).astype(o_ref.dtype)
        return carry

    lax.fori_loop(0, FFT_MINOR // FFT_GROUP, body, 0)


def inv_major(br, bi, tabs, v, x0, bias, tc=256):
    n1, nb = tabs["n1"], tabs["nb"]
    P, n, D = br.shape
    B, L, _ = v.shape
    tc = _tile(D, tc)
    m = tabs["m_inv"]
    blk = pl.BlockSpec((None, n, tc), lambda p, c: (p, 0, c))
    pair = pl.BlockSpec((2, L, tc), lambda p, c: (p, 0, c))
    kern = functools.partial(_inv_major_kernel, n1=n1, nb=nb)
    return pl.pallas_call(
        kern,
        grid=(P, D // tc),
        in_specs=[blk, blk, pl.BlockSpec(m.shape, lambda p, c: (0, 0)), pair, pair,
                  pl.BlockSpec((1, tc), lambda p, c: (0, c))],
        out_specs=pair,
        out_shape=jax.ShapeDtypeStruct((B, L, D), BF16),
        compiler_params=_cparams("parallel", "parallel"),
        name="hyena_inv_major",
    )(br, bi, m, v, x0, bias.reshape(1, D))


def hyena_mix(p, B, L, D, conv_w, conv_b, filt, bias):
    assert B % 2 == 0 and L % FFT_MINOR == 0
    tabs = _dft_tables(L)
    v, x0 = conv_gate(p, B, L, D, conv_w, conv_b)
    k_raw, k_norm = hyena_filter(L, D, *filt)
    kar, kai = fwd_major(k_raw[None], tabs, pairs=False)
    kr, ki = filter_spectrum(kar, kai, tabs, 1.0 / (2 * L * k_norm))
    ar, ai = fwd_major(v, tabs, pairs=True)
    br, bi = minor_conv(ar, ai, tabs, kr, ki)
    z = inv_major(br, bi, tabs, v, x0, bias)
    return z.reshape(B * L, D)


def kernel(x, c, ctx, c_ctx, ada_w, ada_b, norm1_g, norm2_g, na_w_qkv, na_w_o, na_q_g, na_k_g, na_rpb, hy_w_in, hy_b_in, hy_conv_w, hy_conv_b, hy_f_w1, hy_f_b1, hy_f_freq1, hy_f_w2, hy_f_b2, hy_f_freq2, hy_f_w3, hy_f_b3, hy_f_freq3, hy_f_wout, hy_bias, hy_w_out, hy_b_out, moe_router, moe_w1, moe_w3, moe_w2):
    B, N, D = x.shape
    CTX = ctx.shape[1]
    depth = ada_w.shape[0]
    mixer = [i % N_MIXERS for i in range(depth)]

    cond = jnp.concatenate([c, c_ctx[None, :], jnp.zeros((8 - B - 1, D), F32)], axis=0)
    mods_all = ada_all(cond, ada_w, ada_b)
    mods_all = mods_all.reshape(depth, 8, 6, 1, D).transpose(0, 2, 1, 3, 4)

    h = x.reshape(B * N, D)
    hc = ctx.reshape(B * CTX, D)
    zeros_d = jnp.zeros((D,), F32)
    w1, w3, w2 = moe_w1.astype(BF16), moe_w3.astype(BF16), moe_w2.astype(BF16)
    for i in range(depth):
        j = i // N_MIXERS
        ctx_stream = any(mixer[l] == 0 for l in range(i + 1, depth))
        ctx_in = ctx_stream or mixer[i] == 0
        mods = mods_all[i]
        if mixer[i] == 0:
            wqkv = na_w_qkv[j].astype(BF16)
            wo = na_w_o[j].astype(BF16)
            hg = jnp.stack([na_q_g[j], na_k_g[j]]).reshape(2, 1, HEAD_DIM)
            zeros_e = jnp.zeros((3 * D,), F32)
            qs = HEAD_DIM ** -0.5
            qkv = nm_matmul(h, norm1_g[i], mods, 0, 1, N, 0, wqkv, zeros_e, hg, qs)
            qkv_c = nm_matmul(hc, norm1_g[i], mods, 0, 1, B * CTX, B, wqkv, zeros_e, hg, qs)
            bias = _na_bias_tables(na_rpb[j], N // GRID_W)
            o, oc = na_attention(qkv, qkv_c, bias, B, N, CTX, D)
            h = matmul_residual(o, wo, zeros_d, mods, 2, N, 0, h)
            if ctx_stream:
                hc = matmul_residual(oc, wo, zeros_d, mods, 2, B * CTX, B, hc)
        else:
            win = hy_w_in[j].astype(BF16)
            wout = hy_w_out[j].astype(BF16)
            filt = (hy_f_w1[j], hy_f_b1[j], hy_f_freq1[j], hy_f_w2[j], hy_f_b2[j], hy_f_freq2[j],
                    hy_f_w3[j], hy_f_b3[j], hy_f_freq3[j], hy_f_wout[j])
            p = nm_matmul(h, norm1_g[i], mods, 0, 1, N, 0, win, hy_b_in[j], out_dtype=F32)
            z = hyena_mix(p, B, N, D, hy_conv_w[j], hy_conv_b[j], filt, hy_bias[j])
            h = matmul_residual(z, wout, hy_b_out[j], mods, 2, N, 0, h)
            if ctx_stream:
                pc = nm_matmul(hc, norm1_g[i], mods, 0, 1, B * CTX, B, win, hy_b_in[j], out_dtype=F32)
                zc = hyena_mix(pc, B, CTX, D, hy_conv_w[j], hy_conv_b[j], filt, hy_bias[j])
                hc = matmul_residual(zc, wout, hy_b_out[j], mods, 2, B * CTX, B, hc)
        h = ec_moe(h, norm2_g[i], mods, 3, 4, 5, N, 0, B, moe_router[i], i, w1, w3, w2)
        if ctx_stream:
            hc = ec_moe(hc, norm2_g[i], mods, 3, 4, 5, B * CTX, B, B, moe_router[i], i, w1, w3, w2)
    return h.reshape(B, N, D)
```

```python
import functools
import math

import jax
import jax.numpy as jnp
import numpy as np
from jax import lax
from jax.experimental import pallas as pl
from jax.experimental.pallas import tpu as pltpu

F32 = jnp.float32
BF16 = jnp.bfloat16

GRID_W = 64
N_MIXERS = 2
NORM_EPS = 1e-6
NEG_INF = -1e30
HEAD_DIM = 128
LANES = 128
WIN_ROWS = 8
WIN_COLS = 16
HYENA_SHORT = 3
HYENA_EMB = 33
HYENA_FAST_DECAY = 0.3
HYENA_SLOW_DECAY = 1.5
HYENA_TARGET = 1e-2
N_EXPERTS = 16
CAPACITY_FACTOR = 2

Q_ROWS = 4
BAND_ROWS = 12

VMEM_LIMIT = 56 * 1024 * 1024


def _cparams(*sem):
    return pltpu.CompilerParams(dimension_semantics=sem, vmem_limit_bytes=VMEM_LIMIT)


def _tile(n, want, unit=128):
    if n <= want:
        return n
    t = (want // unit) * unit
    while n % t:
        t -= unit
    return t


def _ada_kernel(c_ref, w_ref, b_ref, o_ref):
    c = c_ref[...]
    s = c * jax.nn.sigmoid(c)
    o_ref[...] = jnp.dot(s.astype(BF16), w_ref[...].astype(BF16),
                         preferred_element_type=F32) + b_ref[...]


def ada_all(cond, ada_w, ada_b, tn=1024):
    L, D, E = ada_w.shape
    tn = _tile(E, tn)
    return pl.pallas_call(
        _ada_kernel,
        grid=(L, E // tn),
        in_specs=[
            pl.BlockSpec((8, D), lambda l, j: (0, 0)),
            pl.BlockSpec((None, D, tn), lambda l, j: (l, 0, j)),
            pl.BlockSpec((None, 1, tn), lambda l, j: (l, 0, j)),
        ],
        out_specs=pl.BlockSpec((None, 8, tn), lambda l, j: (l, 0, j)),
        out_shape=jax.ShapeDtypeStruct((L, 8, E), F32),
        compiler_params=_cparams("parallel", "parallel"),
        name="ada",
    )(cond, ada_w, ada_b.reshape(L, 1, E))


def _norm_mod(x, g, sh, sc):
    y = x * lax.rsqrt(jnp.mean(x * x, axis=-1, keepdims=True) + NORM_EPS) * g
    return y * (1.0 + sc) + sh


def _nm_matmul_kernel(x_ref, g_ref, sh_ref, sc_ref, w_ref, b_ref, hg_ref, o_ref, a_ref, *,
                      n_norm_sections, section_cols, q_scale):
    j = pl.program_id(1)

    @pl.when(j == 0)
    def _():
        a_ref[...] = _norm_mod(x_ref[...], g_ref[...], sh_ref[...], sc_ref[...]).astype(BF16)

    acc = jnp.dot(a_ref[...], w_ref[...], preferred_element_type=F32) + b_ref[...]
    tn = acc.shape[1]
    if n_norm_sections == 0:
        o_ref[...] = acc.astype(o_ref.dtype)
        return
    sec = (j * tn) // section_cols

    @pl.when(sec < n_norm_sections)
    def _():
        hg = hg_ref[...]
        mult = jnp.where(sec == 0, q_scale, 1.0).astype(F32)
        for h in range(tn // HEAD_DIM):
            c = acc[:, h * HEAD_DIM:(h + 1) * HEAD_DIM]
            c = c * lax.rsqrt(jnp.mean(c * c, axis=-1, keepdims=True) + NORM_EPS) * hg
            o_ref[:, h * HEAD_DIM:(h + 1) * HEAD_DIM] = (c * mult).astype(o_ref.dtype)

    @pl.when(sec >= n_norm_sections)
    def _():
        o_ref[...] = acc.astype(o_ref.dtype)


def nm_matmul(x, g, mods, k_shift, k_scale, group_rows, group_base, w, b, head_g=None,
              q_scale=1.0, out_dtype=BF16, tm=1024, tn=1024):
    R, D = x.shape
    E = w.shape[1]
    tm = _tile(group_rows, tm, 8)
    tn = _tile(D, tn)
    n_norm = 0 if head_g is None else 2
    if head_g is None:
        head_g = jnp.ones((2, 1, HEAD_DIM), F32)

    def grp(i):
        return group_base + (i * tm) // group_rows

    def hg_map(i, j):
        return (jnp.minimum((j * tn) // D, 1), 0, 0)

    kern = functools.partial(_nm_matmul_kernel, n_norm_sections=n_norm, section_cols=D,
                             q_scale=q_scale)
    return pl.pallas_call(
        kern,
        grid=(R // tm, E // tn),
        in_specs=[
            pl.BlockSpec((tm, D), lambda i, j: (i, 0)),
            pl.BlockSpec((1, D), lambda i, j: (0, 0)),
            pl.BlockSpec((None, None, 1, D), lambda i, j: (k_shift, grp(i), 0, 0)),
            pl.BlockSpec((None, None, 1, D), lambda i, j: (k_scale, grp(i), 0, 0)),
            pl.BlockSpec((D, tn), lambda i, j: (0, j)),
            pl.BlockSpec((1, tn), lambda i, j: (0, j)),
            pl.BlockSpec((None, 1, HEAD_DIM), hg_map),
        ],
        out_specs=pl.BlockSpec((tm, tn), lambda i, j: (i, j)),
        out_shape=jax.ShapeDtypeStruct((R, E), out_dtype),
        scratch_shapes=[pltpu.VMEM((tm, D), BF16)],
        compiler_params=_cparams("parallel", "arbitrary"),
        name="nm_matmul",
    )(x, g.reshape(1, D), mods, mods, w, b.reshape(1, E), head_g)


def _mm_res_kernel(x_ref, w_ref, b_ref, gate_ref, res_ref, o_ref):
    y = jnp.dot(x_ref[...], w_ref[...], preferred_element_type=F32) + b_ref[...]
    o_ref[...] = res_ref[...] + gate_ref[...] * y


def matmul_residual(x, w, b, mods, k_gate, group_rows, group_base, res, tm=1024, tn=1024):
    R, K = x.shape
    E = w.shape[1]
    tm = _tile(group_rows, tm, 8)
    tn = _tile(E, tn)

    def grp(i):
        return group_base + (i * tm) // group_rows

    return pl.pallas_call(
        _mm_res_kernel,
        grid=(R // tm, E // tn),
        in_specs=[
            pl.BlockSpec((tm, K), lambda i, j: (i, 0)),
            pl.BlockSpec((K, tn), lambda i, j: (0, j)),
            pl.BlockSpec((1, tn), lambda i, j: (0, j)),
            pl.BlockSpec((None, None, 1, tn), lambda i, j: (k_gate, grp(i), 0, j)),
            pl.BlockSpec((tm, tn), lambda i, j: (i, j)),
        ],
        out_specs=pl.BlockSpec((tm, tn), lambda i, j: (i, j)),
        out_shape=jax.ShapeDtypeStruct((R, E), F32),
        compiler_params=_cparams("parallel", "parallel"),
        name="matmul_residual",
    )(x, w, b.reshape(1, E), mods, res)


def _na_bias_tables(rpb, rows):
    W = GRID_W
    cols = np.arange(W)
    cs = np.clip(cols - WIN_COLS // 2, 0, W - WIN_COLS)
    in_win = (cols[None, :] >= cs[:, None]) & (cols[None, :] < cs[:, None] + WIN_COLS)
    dc = np.clip(cols[None, :] - cols[:, None] + WIN_COLS - 1, 0, 2 * WIN_COLS - 2)
    n_dr, n_dc = 2 * WIN_ROWS - 1, 2 * WIN_COLS - 1
    onehot_r = np.zeros((3, Q_ROWS, BAND_ROWS, n_dr), np.float32)
    mask = np.zeros((3, Q_ROWS, W, BAND_ROWS, W), bool)
    for var, r0 in enumerate((0, Q_ROWS, rows - Q_ROWS)):
        bs = _band_start(r0, rows)
        for i in range(Q_ROWS):
            r = r0 + i
            rs = int(np.clip(r - WIN_ROWS // 2, 0, rows - WIN_ROWS))
            for jj in range(BAND_ROWS):
                kr = bs + jj
                if rs <= kr < rs + WIN_ROWS:
                    onehot_r[var, i, jj, kr - r + WIN_ROWS - 1] = 1.0
                    mask[var, i, :, jj, :] = in_win
    onehot_c = (dc[:, :, None] == np.arange(n_dc)).astype(np.float32)
    hp = lax.Precision.HIGHEST
    t = jnp.einsum('hrc,vijr->hvijc', rpb.astype(F32), onehot_r, precision=hp)
    t = jnp.einsum('hvijc,qkc->hviqjk', t, onehot_c, precision=hp)
    t = jnp.where(mask[None], t, NEG_INF)
    return t.reshape(rpb.shape[0], 3, Q_ROWS * W, BAND_ROWS * W)


def _band_start(r0, rows):
    return int(np.clip(r0 - WIN_ROWS // 2, 0, rows - BAND_ROWS))


def _na_kernel(q_ref, k_ref, v_ref, qc_ref, kc_ref, vc_ref, bias_ref, o_ref, oc_ref, *, rows):
    W = GRID_W
    nblk = rows // Q_ROWS
    kc = kc_ref[...]
    vc = vc_ref[...]
    nt = (((1,), (1,)), ((), ()))

    def attend(q, parts):
        ss = [lax.dot_general(q, k, nt, preferred_element_type=F32) + (0.0 if bias is None else bias)
              for k, _, bias in parts]
        m = functools.reduce(jnp.maximum, [jnp.max(s, axis=-1, keepdims=True) for s in ss])
        ps = [jnp.exp(s - m) for s in ss]
        den = sum(jnp.sum(p, axis=-1, keepdims=True) for p in ps)
        o = sum(jnp.dot(p.astype(BF16), v, preferred_element_type=F32)
                for p, (_, v, _) in zip(ps, parts))
        return o / den

    def body(blk, carry):
        r0 = blk * Q_ROWS
        bs = jnp.clip(r0 - WIN_ROWS // 2, 0, rows - BAND_ROWS)
        var = jnp.where(blk == 0, 0, jnp.where(blk == nblk - 1, 2, 1))
        q0 = pl.multiple_of(r0 * W, Q_ROWS * W)
        k0 = pl.multiple_of(bs * W, W)
        q = q_ref[pl.ds(q0, Q_ROWS * W), :]
        kb = k_ref[pl.ds(k0, BAND_ROWS * W), :]
        vb = v_ref[pl.ds(k0, BAND_ROWS * W), :]
        o = attend(q, [(kb, vb, bias_ref[var]), (kc, vc, None)])
        o_ref[pl.ds(q0, Q_ROWS * W), :] = o.astype(o_ref.dtype)
        return carry

    lax.fori_loop(0, nblk, body, 0, unroll=2)
    oc_ref[...] = attend(qc_ref[...], [(kc, vc, None)]).astype(oc_ref.dtype)


def na_attention(qkv, qkv_c, bias, B, N, CTX, D):
    H = D // HEAD_DIM
    rows = N // GRID_W
    kern = functools.partial(_na_kernel, rows=rows)

    def spec(n, sec):
        return pl.BlockSpec((n, HEAD_DIM), lambda h, b: (b, sec * H + h))

    return pl.pallas_call(
        kern,
        grid=(H, B),
        in_specs=[spec(N, 0), spec(N, 1), spec(N, 2), spec(CTX, 0), spec(CTX, 1), spec(CTX, 2),
                  pl.BlockSpec((None, 3, Q_ROWS * GRID_W, BAND_ROWS * GRID_W),
                               lambda h, b: (h, 0, 0, 0))],
        out_specs=[pl.BlockSpec((N, HEAD_DIM), lambda h, b: (b, h)),
                   pl.BlockSpec((CTX, HEAD_DIM), lambda h, b: (b, h))],
        out_shape=[jax.ShapeDtypeStruct((B * N, D), BF16),
                   jax.ShapeDtypeStruct((B * CTX, D), BF16)],
        compiler_params=_cparams("parallel", "parallel"),
        name="na_attention",
    )(qkv, qkv, qkv, qkv_c, qkv_c, qkv_c, bias)


def _expert_kernel(idx_ref, a_hbm, w1_ref, w3_ref, w2_ref, gate_ref, o_ref, xbuf, sem, *, n_tok, cap):
    e, b = pl.program_id(0), pl.program_id(1)
    nb = pl.num_programs(1)
    step = e * nb + b
    slot = step % 2

    nsteps = pl.num_programs(0) * nb
    S = xbuf.shape[1] // cap

    def copy_row(base, bb, sl, c):
        row = bb * n_tok + idx_ref[base + c]
        src = pl.ds(pl.multiple_of(row * S, S), S)
        dst = pl.ds(pl.multiple_of(c * S, S), S)
        pltpu.make_async_copy(a_hbm.at[src, :], xbuf.at[sl, dst, :], sem.at[sl]).start()

    def wait_rows(sl):
        pltpu.make_async_copy(a_hbm.at[pl.ds(0, cap * S), :], xbuf.at[sl], sem.at[sl]).wait()

    @pl.when(step == 0)
    def _():
        base = (b * pl.num_programs(0) + e) * cap

        def body(c, carry):
            copy_row(base, b, slot, c)
            return carry

        lax.fori_loop(0, cap, body, 0, unroll=8)

    wait_rows(slot)
    nxt = (step + 1) % nsteps
    nxt_e, nxt_b = nxt // nb, nxt % nb
    nxt_base = (nxt_b * pl.num_programs(0) + nxt_e) * cap
    hi, lo = _load_token_tiles(xbuf, (slot,), 0, cap, S)
    x = jnp.concatenate([p.astype(BF16) for p in hi + lo], axis=1)
    third = -(-cap // 3)

    def request(part):
        for c in range(part * third, min((part + 1) * third, cap)):
            copy_row(nxt_base, nxt_b, 1 - slot, c)

    request(0)
    h1 = jnp.dot(x, w1_ref[...], preferred_element_type=F32)
    request(1)
    h3 = jnp.dot(x, w3_ref[...], preferred_element_type=F32)
    hid = (h1 * jax.nn.sigmoid(h1) * h3).astype(BF16)
    request(2)
    y = jnp.dot(hid, w2_ref[...], preferred_element_type=F32) * gate_ref[...]
    _store_token_tiles(o_ref, (), _pack_bf16_pair(y))

    @pl.when(step == nsteps - 1)
    def _():
        wait_rows(1 - slot)


def _pack_bf16_pair(y):
    half = y.shape[1] // 2
    hi = pltpu.bitcast(y[:, :half].astype(BF16).astype(F32), jnp.uint32)
    lo = pltpu.bitcast(y[:, half:].astype(BF16).astype(F32), jnp.uint32)
    return hi | (lo >> 16)


def _unpack_bf16_pair(u):
    hi = pltpu.bitcast(u & jnp.uint32(0xFFFF0000), F32)
    lo = pltpu.bitcast(u << 16, F32)
    return hi, lo


def _store_token_tiles(ref, lead, packed):
    rows, S = packed.shape[0], packed.shape[1] // LANES
    for s in range(S):
        ref[(*lead, pl.ds(s, rows, stride=S), slice(None))] = packed[:, s * LANES:(s + 1) * LANES]


def _load_token_tiles(ref, lead, first, rows, S):
    pieces = [_unpack_bf16_pair(ref[(*lead, pl.ds(first * S + s, rows, stride=S), slice(None))])
              for s in range(S)]
    return [p[0] for p in pieces], [p[1] for p in pieces]


def expert_ffn(a, idx_flat, layer, w1, w3, w2, gate, B, n_tok):
    _, E, D, Fh = w1.shape
    C = gate.shape[2]
    S = D // (2 * LANES)
    kern = functools.partial(_expert_kernel, n_tok=n_tok, cap=C)
    grid_spec = pltpu.PrefetchScalarGridSpec(
        num_scalar_prefetch=1,
        grid=(E, B),
        in_specs=[
            pl.BlockSpec(memory_space=pl.ANY),
            pl.BlockSpec((None, None, D, Fh), lambda e, b, idx: (layer, e, 0, 0)),
            pl.BlockSpec((None, None, D, Fh), lambda e, b, idx: (layer, e, 0, 0)),
            pl.BlockSpec((None, None, Fh, D), lambda e, b, idx: (layer, e, 0, 0)),
            pl.BlockSpec((None, None, C, 1), lambda e, b, idx: (b, e, 0, 0)),
        ],
        out_specs=pl.BlockSpec((None, None, C * S, LANES), lambda e, b, idx: (b, e, 0, 0)),
        scratch_shapes=[pltpu.VMEM((2, C * S, LANES), jnp.uint32), pltpu.SemaphoreType.DMA((2,))],
    )
    return pl.pallas_call(
        kern,
        grid_spec=grid_spec,
        out_shape=jax.ShapeDtypeStruct((B, E, C * S, LANES), jnp.uint32),
        compiler_params=_cparams("arbitrary", "arbitrary"),
        name="expert_ffn",
    )(idx_flat, a, w1, w3, w2, gate)


def _moe_prep_kernel(x_ref, g_ref, sh_ref, sc_ref, r_ref, a_ref, aff_ref):
    a = _norm_mod(x_ref[...], g_ref[...], sh_ref[...], sc_ref[...])
    _store_token_tiles(a_ref, (), _pack_bf16_pair(a))
    logits = lax.dot_general(r_ref[...], a.astype(BF16), (((1,), (1,)), ((), ())),
                             preferred_element_type=F32)
    m = jnp.max(logits, axis=0, keepdims=True)
    p = jnp.exp(logits - m)
    aff_ref[...] = p / jnp.sum(p, axis=0, keepdims=True)


def moe_prep(x, g, mods, k_shift, k_scale, group_rows, group_base, router_t, tm=512):
    R, D = x.shape
    E = router_t.shape[0]
    S = D // (2 * LANES)
    tm = _tile(group_rows, tm, 128)

    def grp(i):
        return group_base + (i * tm) // group_rows

    return pl.pallas_call(
        _moe_prep_kernel,
        grid=(R // tm,),
        in_specs=[
            pl.BlockSpec((tm, D), lambda i: (i, 0)),
            pl.BlockSpec((1, D), lambda i: (0, 0)),
            pl.BlockSpec((None, None, 1, D), lambda i: (k_shift, grp(i), 0, 0)),
            pl.BlockSpec((None, None, 1, D), lambda i: (k_scale, grp(i), 0, 0)),
            pl.BlockSpec((E, D), lambda i: (0, 0)),
        ],
        out_specs=[pl.BlockSpec((tm * S, LANES), lambda i: (i, 0)),
                   pl.BlockSpec((E, tm), lambda i: (0, i))],
        out_shape=[jax.ShapeDtypeStruct((R * S, LANES), jnp.uint32), jax.ShapeDtypeStruct((E, R), F32)],
        compiler_params=_cparams("parallel"),
        name="moe_prep",
    )(x, g.reshape(1, D), mods, mods, router_t)


COMBINE_TOKENS = 64
HIT_UNROLL = 4
CUMSUM_CHUNK = 256
TILE_TABLE_LANES = 128


def _route_kernel(aff_ref, tri_ref, ltri_ref, lt_ref, in_ref,
                  idx_ref, gate_ref, rank_ref, p0_ref, kmax_ref,
                  posi_s, key_s, erank_s, *, cap):
    E, N = aff_ref.shape
    aff = aff_ref[...]
    bits = pltpu.bitcast(aff, jnp.int32)
    cur = jnp.zeros((E, 1), jnp.int32)
    for bit in range(30, -1, -1):
        cand = cur | (1 << bit)
        cnt = jnp.sum(jnp.where(bits >= cand, 1.0, 0.0), axis=1, keepdims=True)
        cur = jnp.where(cnt >= cap, cand, cur)
    gt = bits > cur
    eq = bits == cur
    need = cap - jnp.sum(jnp.where(gt, 1.0, 0.0), axis=1, keepdims=True)

    def cumsum_incl(x):
        tri = tri_ref[...]
        w = min(CUMSUM_CHUNK, N)
        off = jnp.zeros((E, 1), F32)
        outs = []
        for c in range(N // w):
            xc = x[:, c * w:(c + 1) * w]
            outs.append(jnp.dot(xc.astype(BF16), tri[:w, :w], preferred_element_type=F32) + off)
            off = off + jnp.sum(xc, axis=1, keepdims=True)
        return jnp.concatenate(outs, axis=1) if len(outs) > 1 else outs[0]

    eqf = jnp.where(eq, 1.0, 0.0)
    tie_rank = cumsum_incl(eqf) - eqf
    sel = gt | (eq & (tie_rank < need))
    self_ = jnp.where(sel, 1.0, 0.0)
    pos_incl = cumsum_incl(self_)
    posi_s[...] = pos_incl
    key_s[...] = jnp.where(sel, pos_incl - self_, -1.0)
    selb = self_.astype(BF16)
    erank = jnp.dot(ltri_ref[...], selb, preferred_element_type=F32)
    in_tile = lax.broadcasted_iota(jnp.int32, (E, N), 1) & (COMBINE_TOKENS - 1)
    erank_s[...] = erank * COMBINE_TOKENS + in_tile.astype(F32)
    p0_ref[...] = jnp.dot(selb, lt_ref[...], preferred_element_type=F32).astype(jnp.int32)
    count = jnp.sum(self_, axis=0, keepdims=True)
    kk = lax.broadcasted_iota(jnp.int32, (E, 1), 0).astype(F32)
    over = jnp.where(count > kk, 1.0, 0.0).astype(BF16)
    per_tile = jnp.dot(over, in_ref[...], preferred_element_type=F32)
    kmax_ref[...] = jnp.sum(jnp.where(per_tile > 0.0, 1.0, 0.0), axis=0, keepdims=True).astype(jnp.int32)

    ch = min(64, cap)

    def e_body(e, carry):
        pi = posi_s[pl.ds(e, 1), :]
        ky = key_s[pl.ds(e, 1), :]
        af = aff_ref[pl.ds(e, 1), :]
        rk = erank_s[pl.ds(e, 1), :]

        def c_body(cc, carry2):
            c0 = pl.multiple_of(cc * ch, ch)
            ccol = (c0 + lax.broadcasted_iota(jnp.int32, (ch, 1), 0)).astype(F32)
            hit = ky == ccol
            idx_ref[e, pl.ds(c0, ch), :] = jnp.sum(jnp.where(pi <= ccol, 1.0, 0.0), axis=1,
                                                   keepdims=True).astype(jnp.int32)
            gate_ref[e, pl.ds(c0, ch), :] = jnp.sum(jnp.where(hit, af, 0.0), axis=1, keepdims=True)
            rank_ref[e, pl.ds(c0, ch), :] = jnp.sum(jnp.where(hit, rk, 0.0), axis=1,
                                                    keepdims=True).astype(jnp.int32)
            return carry2

        lax.fori_loop(0, cap // ch, c_body, 0)
        return carry

    lax.fori_loop(0, E, e_body, 0)


def moe_route(aff_t, B, N, cap):
    E = aff_t.shape[0]
    tn = COMBINE_TOKENS
    lanes = TILE_TABLE_LANES
    assert N // tn + 1 <= lanes and N % tn == 0 and N % min(CUMSUM_CHUNK, N) == 0
    w = min(CUMSUM_CHUNK, N)
    tri = (np.arange(w)[:, None] <= np.arange(w)[None, :]).astype(np.float32)
    ltri = (np.arange(E)[None, :] < np.arange(E)[:, None]).astype(np.float32)
    tok = np.arange(N)[:, None]
    tile = np.arange(lanes)[None, :]
    before = (tok < tile * tn).astype(np.float32)
    inside = (tok // tn == tile).astype(np.float32)
    kern = functools.partial(_route_kernel, cap=cap)
    const = lambda shape: pl.BlockSpec(shape, lambda b: (0,) * len(shape))
    sel3 = pl.BlockSpec((None, E, cap, 1), lambda b: (b, 0, 0, 0))
    return pl.pallas_call(
        kern,
        grid=(B,),
        in_specs=[pl.BlockSpec((E, N), lambda b: (0, b)), const((w, w)), const((E, E)),
                  const((N, lanes)), const((N, lanes))],
        out_specs=[sel3, sel3, sel3,
                   pl.BlockSpec((None, E, lanes), lambda b: (b, 0, 0)),
                   pl.BlockSpec((None, 1, lanes), lambda b: (b, 0, 0))],
        out_shape=[jax.ShapeDtypeStruct((B, E, cap, 1), jnp.int32),
                   jax.ShapeDtypeStruct((B, E, cap, 1), F32),
                   jax.ShapeDtypeStruct((B, E, cap, 1), jnp.int32),
                   jax.ShapeDtypeStruct((B, E, lanes), jnp.int32),
                   jax.ShapeDtypeStruct((B, 1, lanes), jnp.int32)],
        scratch_shapes=[pltpu.VMEM((E, N), F32)] * 3,
        compiler_params=_cparams("parallel"),
        name="moe_route",
    )(aff_t, jnp.asarray(tri, BF16), jnp.asarray(ltri, BF16), jnp.asarray(before, BF16),
      jnp.asarray(inside, BF16))


def _combine_kernel(rank_ref, p0_ref, kmax_ref, ys_hbm, h_ref, g2_ref, o_ref, rbuf, sem, *,
                    n_exp, cap, tn, lanes):
    b, t = pl.program_id(0), pl.program_id(1)
    nt = pl.num_programs(1)
    step = b * nt + t
    slot = step % 2
    S = rbuf.shape[1] // (n_exp * tn)

    def tile_hits(bb, tt):
        def body(e, tot):
            q = (bb * n_exp + e) * lanes + tt
            return tot + p0_ref[q + 1] - p0_ref[q]
        return lax.fori_loop(0, n_exp, body, 0)

    def fetch(bb, tt, sl):
        km = kmax_ref[bb * lanes + tt]
        for k in range(n_exp):
            @pl.when(k < km)
            def _():
                rbuf[sl, k * tn * S:(k + 1) * tn * S, :] = jnp.zeros((tn * S, LANES), jnp.uint32)

        def copy_row(row):
            src = pl.ds(pl.multiple_of(row * S, S), S)
            dst = pl.ds(pl.multiple_of(rank_ref[row] * S, S), S)
            pltpu.make_async_copy(ys_hbm.at[src, :], rbuf.at[sl, dst, :], sem.at[sl]).start()

        def e_body(e, carry):
            q = (bb * n_exp + e) * lanes + tt
            lo = (bb * n_exp + e) * cap + p0_ref[q]
            n = p0_ref[q + 1] - p0_ref[q]
            groups = n // HIT_UNROLL

            def g_body(g, carry2):
                for u in range(HIT_UNROLL):
                    copy_row(lo + g * HIT_UNROLL + u)
                return carry2

            def r_body(c, carry2):
                copy_row(lo + c)
                return carry2

            lax.fori_loop(0, groups, g_body, 0)
            lax.fori_loop(groups * HIT_UNROLL, n, r_body, 0)
            return carry

        lax.fori_loop(0, n_exp, e_body, 0)

    @pl.when(step == 0)
    def _():
        fetch(b, t, slot)

    @pl.when(step + 1 < pl.num_programs(0) * nt)
    def _():
        nxt = step + 1
        fetch(nxt // nt, nxt % nt, 1 - slot)

    hits = tile_hits(b, t)

    nrows = hits * S
    bulk = pl.multiple_of((nrows // 8) * 8, 8)

    @pl.when(bulk > 0)
    def _():
        pltpu.make_async_copy(ys_hbm.at[pl.ds(0, bulk), :], rbuf.at[slot, pl.ds(0, bulk), :],
                              sem.at[slot]).wait()

    def wait_row(i, carry):
        pltpu.make_async_copy(ys_hbm.at[pl.ds(0, 1), :], rbuf.at[slot, pl.ds(0, 1), :], sem.at[slot]).wait()
        return carry

    lax.fori_loop(0, nrows - bulk, wait_row, 0)

    km = kmax_ref[b * lanes + t]
    g2 = g2_ref[...]
    for r in range(tn // 8):
        def k_body(k, tot):
            hi, lo = _load_token_tiles(rbuf, (slot,), k * tn + r * 8, 8, S)
            return tuple(t_ + p for t_, p in zip(tot, hi + lo))

        tot = lax.fori_loop(0, km, k_body, (jnp.zeros((8, LANES), F32),) * (2 * S))
        rows = slice(r * 8, (r + 1) * 8)
        for s in range(2 * S):
            cols = slice(s * LANES, (s + 1) * LANES)
            o_ref[rows, cols] = h_ref[rows, cols] + g2[:, cols] * tot[s]


def moe_combine(ys, h, mods, k_gate, group_rows, group_base, rank, p0, kmax, B, N):
    R, D = h.shape
    E, cap = rank.shape[1], rank.shape[2]
    tn = COMBINE_TOKENS
    lanes = TILE_TABLE_LANES
    nt = N // tn

    def grp(b, t):
        return group_base + ((b * nt + t) * tn) // group_rows

    kern = functools.partial(_combine_kernel, n_exp=E, cap=cap, tn=tn, lanes=lanes)
    grid_spec = pltpu.PrefetchScalarGridSpec(
        num_scalar_prefetch=3,
        grid=(B, nt),
        in_specs=[
            pl.BlockSpec(memory_space=pl.ANY),
            pl.BlockSpec((tn, D), lambda b, t, *_: (b * nt + t, 0)),
            pl.BlockSpec((None, None, 1, D), lambda b, t, *_: (k_gate, grp(b, t), 0, 0)),
        ],
        out_specs=pl.BlockSpec((tn, D), lambda b, t, *_: (b * nt + t, 0)),
        scratch_shapes=[pltpu.VMEM((2, E * tn * (D // (2 * LANES)), LANES), jnp.uint32),
                        pltpu.SemaphoreType.DMA((2,))],
    )
    return pl.pallas_call(
        kern,
        grid_spec=grid_spec,
        out_shape=jax.ShapeDtypeStruct((R, D), F32),
        compiler_params=_cparams("arbitrary", "arbitrary"),
        name="moe_combine",
    )(rank.reshape(-1), p0.reshape(-1), kmax.reshape(-1), ys, h, mods)


def ec_moe(h, g, mods, k_shift, k_scale, k_gate, group_rows, group_base, B, router, layer, w1, w3, w2):
    R, D = h.shape
    N = R // B
    E = router.shape[1]
    cap = CAPACITY_FACTOR * N // E
    a, aff_t = moe_prep(h, g, mods, k_shift, k_scale, group_rows, group_base, router.T.astype(BF16))
    idx, gate, rank, p0, kmax = moe_route(aff_t, B, N, cap)
    ys = expert_ffn(a, idx.reshape(-1), layer, w1, w3, w2, gate, B, N)
    return moe_combine(ys.reshape(-1, LANES), h, mods, k_gate, group_rows, group_base,
                       rank, p0, kmax, B, N)


FFT_MINOR = 256
FFT_GROUP = 16


def _dft_tables(L):
    n = 2 * L
    n1 = n // FFT_MINOR
    nb = n1 // 2
    two_pi = 2.0 * math.pi
    a = jnp.arange(n1, dtype=jnp.int32)
    ang = two_pi * ((a[:, None] * a[None, :]) % n1).astype(F32) / n1
    cos1, sin1 = jnp.cos(ang), jnp.sin(ang)
    eye = jnp.eye(FFT_GROUP, dtype=F32)

    def kron(m):
        return jnp.kron(m, eye)

    f1 = jnp.arange(n1, dtype=jnp.int32)[:, None, None]
    f2 = jnp.arange(FFT_MINOR, dtype=jnp.int32)[None, :, None]
    n2 = jnp.arange(FFT_MINOR, dtype=jnp.int32)[None, None, :]
    ang = two_pi * ((n2 * (f1 + n1 * f2)) % n).astype(F32) / n
    gr, gi = jnp.cos(ang), -jnp.sin(ang)
    return dict(
        n1=n1, nb=nb,
        m_fwd_full=jnp.concatenate([kron(cos1), kron(-sin1)], axis=0).astype(BF16),
        m_fwd_half=jnp.concatenate([kron(cos1[:, :nb]), kron(-sin1[:, :nb])], axis=0).astype(BF16),
        m_inv=jnp.concatenate([kron(cos1[:nb]), kron(sin1[:nb])], axis=0).astype(BF16),
        g=jnp.concatenate([gr, gi], axis=1).astype(BF16),
        gt=jnp.concatenate([gr.transpose(0, 2, 1), gi.transpose(0, 2, 1)], axis=1).astype(BF16),
    )


def _conv_gate_kernel(x0_ref, x1_ref, v_ref, w_ref, b_ref, vo_ref, x0o_ref):
    L, C = x0_ref.shape
    row = lax.broadcasted_iota(jnp.int32, (L, C), 0)

    def conv(ref, k):
        x = ref[...]
        w = w_ref[k]
        prev = jnp.where(row == 0, 0.0, pltpu.roll(x, 1, 0))
        nxt = jnp.where(row == L - 1, 0.0, pltpu.roll(x, L - 1, 0))
        return prev * w[0:1] + x * w[1:2] + nxt * w[2:3] + b_ref[k]

    x0 = conv(x0_ref, 0)
    x1 = conv(x1_ref, 1)
    v = conv(v_ref, 2)
    vo_ref[...] = (v * x1).astype(vo_ref.dtype)
    x0o_ref[...] = x0.astype(x0o_ref.dtype)


def conv_gate(p, B, L, D, conv_w, conv_b, tc=256):
    tc = _tile(D, tc)
    nc = D // tc
    w = conv_w.reshape(HYENA_SHORT, 3, D).transpose(1, 0, 2)
    b = conv_b.reshape(3, 1, D)

    def spec(k):
        return pl.BlockSpec((L, tc), lambda bi, c: (bi, k * nc + c))

    v, x0 = pl.pallas_call(
        _conv_gate_kernel,
        grid=(B, nc),
        in_specs=[spec(0), spec(1), spec(2),
                  pl.BlockSpec((3, HYENA_SHORT, tc), lambda bi, c: (0, 0, c)),
                  pl.BlockSpec((3, 1, tc), lambda bi, c: (0, 0, c))],
        out_specs=[pl.BlockSpec((L, tc), lambda bi, c: (bi, c))] * 2,
        out_shape=[jax.ShapeDtypeStruct((B * L, D), BF16)] * 2,
        compiler_params=_cparams("parallel", "parallel"),
        name="hyena_conv_gate",
    )(p, p, p, w, b)
    return v.reshape(B, L, D), x0.reshape(B, L, D)


def _filter_kernel(emb_ref, w1_ref, b1_ref, f1_ref, w2_ref, b2_ref, f2_ref, w3_ref, b3_ref, f3_ref,
                   wo_ref, dl_ref, k_ref, norm_ref, *, L):
    i = pl.program_id(0)
    tr = emb_ref.shape[0]
    hp = lax.Precision.HIGHEST
    emb = emb_ref[...]
    h = jnp.sin(f1_ref[...] * (jnp.dot(emb, w1_ref[...], precision=hp, preferred_element_type=F32) + b1_ref[...]))
    h = jnp.sin(f2_ref[...] * (jnp.dot(h, w2_ref[...], precision=hp, preferred_element_type=F32) + b2_ref[...]))
    h = jnp.sin(f3_ref[...] * (jnp.dot(h, w3_ref[...], precision=hp, preferred_element_type=F32) + b3_ref[...]))
    k = jnp.dot(h, wo_ref[...], precision=hp, preferred_element_type=F32)
    k = k * jnp.exp(-emb[:, 0:1] * dl_ref[...])
    row = i * tr + lax.broadcasted_iota(jnp.int32, (tr, 1), 0)
    k = jnp.where(row == L, 0.0, k)
    k_ref[...] = k

    @pl.when(i == 0)
    def _():
        norm_ref[...] = jnp.zeros_like(norm_ref)

    norm_ref[...] += jnp.sum(jnp.abs(k), axis=0, keepdims=True)


def hyena_filter(L, D, w1, b1, f1, w2, b2, f2, w3, b3, f3, w_out, tr=512):
    n = 2 * L
    tr = _tile(L, tr, 8)
    P = 128
    bands = (HYENA_EMB - 1) // 2
    d = np.arange(n)
    d = np.where(d <= L, d, n - d).clip(0, L - 1)
    t01 = np.linspace(0.0, 1.0, L)[d]
    wang = 2 * math.pi * d / L
    fr = np.linspace(1e-4, bands - 1, bands)
    emb = np.zeros((n, P), np.float32)
    emb[:, 0] = t01
    emb[:, 1:1 + bands] = np.cos(fr[None, :] * wang[:, None])
    emb[:, 1 + bands:1 + 2 * bands] = -np.sin(fr[None, :] * wang[:, None])

    def padw(w):
        return jnp.zeros((P, P), F32).at[:w.shape[0], :w.shape[1]].set(w)

    def padv(v):
        return jnp.zeros((1, P), F32).at[0, :v.shape[0]].set(v)

    wo = jnp.zeros((P, 2 * D), F32).at[:w_out.shape[0]].set(w_out)
    max_decay = math.log(HYENA_TARGET) / HYENA_FAST_DECAY
    min_decay = math.log(HYENA_TARGET) / HYENA_SLOW_DECAY
    deltas = jnp.abs(jnp.linspace(min_decay, max_decay, D, dtype=F32)).reshape(1, D)
    small = pl.BlockSpec((P, P), lambda i: (0, 0))
    vec = pl.BlockSpec((1, P), lambda i: (0, 0))
    kern = functools.partial(_filter_kernel, L=L)
    return pl.pallas_call(
        kern,
        grid=(n // tr,),
        in_specs=[pl.BlockSpec((tr, P), lambda i: (i, 0)),
                  small, vec, vec, small, vec, vec, small, vec, vec,
                  pl.BlockSpec((P, D), lambda i: (0, (i * tr) // L)),
                  pl.BlockSpec((1, D), lambda i: (0, 0))],
        out_specs=[pl.BlockSpec((tr, D), lambda i: (i, 0)), pl.BlockSpec((1, D), lambda i: (0, 0))],
        out_shape=[jax.ShapeDtypeStruct((n, D), F32), jax.ShapeDtypeStruct((1, D), F32)],
        compiler_params=_cparams("arbitrary"),
        name="hyena_filter",
    )(jnp.asarray(emb), padw(w1), padv(b1), padv(f1), padw(w2), padv(b2), padv(f2),
      padw(w3), padv(b3), padv(f3), wo, deltas)


def _fwd_major_kernel(*refs, n1, nb, has_imag):
    if has_imag:
        zr_ref, zi_ref, m_ref, ar_ref, ai_ref = refs
    else:
        zr_ref, m_ref, ar_ref, ai_ref = refs
    m = m_ref[...]
    half = n1 * FFT_GROUP

    def body(g, carry):
        def rows(blk):
            return pl.ds(pl.multiple_of(blk * FFT_MINOR + g * FFT_GROUP, FFT_GROUP), FFT_GROUP)

        def slab(ref):
            return jnp.concatenate([ref[rows(blk), :] for blk in range(nb)], axis=0).astype(BF16)

        pr = jnp.dot(m, slab(zr_ref), preferred_element_type=F32)
        if has_imag:
            pi = jnp.dot(m, slab(zi_ref), preferred_element_type=F32)
            ar, ai = pr[:half] - pi[half:], pi[:half] + pr[half:]
        else:
            ar, ai = pr[:half], pr[half:]
        for f1 in range(n1):
            sl = slice(f1 * FFT_GROUP, (f1 + 1) * FFT_GROUP)
            ar_ref[rows(f1), :] = ar[sl].astype(BF16)
            ai_ref[rows(f1), :] = ai[sl].astype(BF16)
        return carry

    lax.fori_loop(0, FFT_MINOR // FFT_GROUP, body, 0)


def fwd_major(z, tabs, pairs, tc=256):
    n1, nb = tabs["n1"], tabs["nb"]
    n = n1 * FFT_MINOR
    Bz, Lz, D = z.shape
    tc = _tile(D, tc)
    if pairs:
        P = Bz // 2
        m = tabs["m_fwd_half"]
        ins = [z, z, m]
        in_specs = [pl.BlockSpec((None, Lz, tc), lambda p, c: (2 * p, 0, c)),
                    pl.BlockSpec((None, Lz, tc), lambda p, c: (2 * p + 1, 0, c))]
        nblk = nb
    else:
        P = 1
        m = tabs["m_fwd_full"]
        ins = [z, m]
        in_specs = [pl.BlockSpec((None, Lz, tc), lambda p, c: (0, 0, c))]
        nblk = n1
    in_specs.append(pl.BlockSpec(m.shape, lambda p, c: (0, 0)))
    kern = functools.partial(_fwd_major_kernel, n1=n1, nb=nblk, has_imag=pairs)
    return pl.pallas_call(
        kern,
        grid=(P, D // tc),
        in_specs=in_specs,
        out_specs=[pl.BlockSpec((None, n, tc), lambda p, c: (p, 0, c))] * 2,
        out_shape=[jax.ShapeDtypeStruct((P, n, D), BF16)] * 2,
        compiler_params=_cparams("parallel", "parallel"),
        name="hyena_fwd_major",
    )(*ins)


def _spectrum_kernel(ar_ref, ai_ref, g_ref, s_ref, kr_ref, ki_ref):
    g = g_ref[...]
    h = FFT_MINOR
    pr = jnp.dot(g, ar_ref[...], preferred_element_type=F32)
    pi = jnp.dot(g, ai_ref[...], preferred_element_type=F32)
    s = s_ref[...]
    kr_ref[...] = (pr[:h] - pi[h:]) * s
    ki_ref[...] = (pi[:h] + pr[h:]) * s


def filter_spectrum(ar, ai, tabs, scale, tc=1024):
    _, n, D = ar.shape
    tc = _tile(D, tc)
    blk = pl.BlockSpec((None, FFT_MINOR, tc), lambda f, c: (0, f, c))
    out = pl.BlockSpec((FFT_MINOR, tc), lambda f, c: (f, c))
    return pl.pallas_call(
        _spectrum_kernel,
        grid=(tabs["n1"], D // tc),
        in_specs=[blk, blk, pl.BlockSpec((None, 2 * FFT_MINOR, FFT_MINOR), lambda f, c: (f, 0, 0)),
                  pl.BlockSpec((1, tc), lambda f, c: (0, c))],
        out_specs=[out, out],
        out_shape=[jax.ShapeDtypeStruct((n, D), F32)] * 2,
        compiler_params=_cparams("parallel", "parallel"),
        name="hyena_filter_spectrum",
    )(ar, ai, tabs["g"], scale)


def _minor_kernel(ar_ref, ai_ref, g_ref, gt_ref, kr_ref, ki_ref, br_ref, bi_ref):
    h = FFT_MINOR
    g = g_ref[...]
    pr = jnp.dot(g, ar_ref[...], preferred_element_type=F32)
    pi = jnp.dot(g, ai_ref[...], preferred_element_type=F32)
    xr, xi = pr[:h] - pi[h:], pi[:h] + pr[h:]
    kr, ki = kr_ref[...], ki_ref[...]
    yr = (xr * kr - xi * ki).astype(BF16)
    yi = (xr * ki + xi * kr).astype(BF16)
    gt = gt_ref[...]
    qr = jnp.dot(gt, yr, preferred_element_type=F32)
    qi = jnp.dot(gt, yi, preferred_element_type=F32)
    br_ref[...] = (qr[:h] + qi[h:]).astype(BF16)
    bi_ref[...] = (qi[:h] - qr[h:]).astype(BF16)


def minor_conv(ar, ai, tabs, kr, ki, tc=1024):
    P, n, D = ar.shape
    tc = _tile(D, tc)
    blk = pl.BlockSpec((None, FFT_MINOR, tc), lambda f, c, p: (p, f, c))
    tab = pl.BlockSpec((None, 2 * FFT_MINOR, FFT_MINOR), lambda f, c, p: (f, 0, 0))
    kblk = pl.BlockSpec((FFT_MINOR, tc), lambda f, c, p: (f, c))
    return pl.pallas_call(
        _minor_kernel,
        grid=(tabs["n1"], D // tc, P),
        in_specs=[blk, blk, tab, tab, kblk, kblk],
        out_specs=[blk, blk],
        out_shape=[jax.ShapeDtypeStruct((P, n, D), BF16)] * 2,
        compiler_params=_cparams("parallel", "parallel", "parallel"),
        name="hyena_minor_conv",
    )(ar, ai, tabs["g"], tabs["gt"], kr, ki)


def _inv_major_kernel(br_ref, bi_ref, m_ref, v_ref, x0_ref, bias_ref, o_ref, *, n1, nb):
    m = m_ref[...]
    half = nb * FFT_GROUP
    bias = bias_ref[...]

    def body(g, carry):
        def rows(blk):
            return pl.ds(pl.multiple_of(blk * FFT_MINOR + g * FFT_GROUP, FFT_GROUP), FFT_GROUP)

        def slab(ref):
            return jnp.concatenate([ref[rows(f1), :] for f1 in range(n1)], axis=0)

        pr = jnp.dot(m, slab(br_ref), preferred_element_type=F32)
        pi = jnp.dot(m, slab(bi_ref), preferred_element_type=F32)
        ys = (pr[:half] - pi[half:], pi[:half] + pr[half:])
        for s in range(2):
            for blk in range(nb):
                y = ys[s][blk * FFT_GROUP:(blk + 1) * FFT_GROUP]
                v = v_ref[s, rows(blk), :].astype(F32)
                x0 = x0_ref[s, rows(blk), :].astype(F32)
                o_ref[s, rows(blk), :] = ((y + v * bias) * x0).astype(o_ref.dtype)
        return carry

    lax.fori_loop(0, FFT_MINOR // FFT_GROUP, body, 0)


def inv_major(br, bi, tabs, v, x0, bias, tc=256):
    n1, nb = tabs["n1"], tabs["nb"]
    P, n, D = br.shape
    B, L, _ = v.shape
    tc = _tile(D, tc)
    m = tabs["m_inv"]
    blk = pl.BlockSpec((None, n, tc), lambda p, c: (p, 0, c))
    pair = pl.BlockSpec((2, L, tc), lambda p, c: (p, 0, c))
    kern = functools.partial(_inv_major_kernel, n1=n1, nb=nb)
    return pl.pallas_call(
        kern,
        grid=(P, D // tc),
        in_specs=[blk, blk, pl.BlockSpec(m.shape, lambda p, c: (0, 0)), pair, pair,
                  pl.BlockSpec((1, tc), lambda p, c: (0, c))],
        out_specs=pair,
        out_shape=jax.ShapeDtypeStruct((B, L, D), BF16),
        compiler_params=_cparams("parallel", "parallel"),
        name="hyena_inv_major",
    )(br, bi, m, v, x0, bias.reshape(1, D))


def hyena_mix(p, B, L, D, conv_w, conv_b, filt, bias):
    assert B % 2 == 0 and L % FFT_MINOR == 0
    tabs = _dft_tables(L)
    v, x0 = conv_gate(p, B, L, D, conv_w, conv_b)
    k_raw, k_norm = hyena_filter(L, D, *filt)
    kar, kai = fwd_major(k_raw[None], tabs, pairs=False)
    kr, ki = filter_spectrum(kar, kai, tabs, 1.0 / (2 * L * k_norm))
    ar, ai = fwd_major(v, tabs, pairs=True)
    br, bi = minor_conv(ar, ai, tabs, kr, ki)
    z = inv_major(br, bi, tabs, v, x0, bias)
    return z.reshape(B * L, D)


def kernel(x, c, ctx, c_ctx, ada_w, ada_b, norm1_g, norm2_g, na_w_qkv, na_w_o, na_q_g, na_k_g, na_rpb, hy_w_in, hy_b_in, hy_conv_w, hy_conv_b, hy_f_w1, hy_f_b1, hy_f_freq1, hy_f_w2, hy_f_b2, hy_f_freq2, hy_f_w3, hy_f_b3, hy_f_freq3, hy_f_wout, hy_bias, hy_w_out, hy_b_out, moe_router, moe_w1, moe_w3, moe_w2):
    B, N, D = x.shape
    CTX = ctx.shape[1]
    depth = ada_w.shape[0]
    mixer = [i % N_MIXERS for i in range(depth)]

    cond = jnp.concatenate([c, c_ctx[None, :], jnp.zeros((8 - B - 1, D), F32)], axis=0)
    mods_all = ada_all(cond, ada_w, ada_b)
    mods_all = mods_all.reshape(depth, 8, 6, 1, D).transpose(0, 2, 1, 3, 4)

    h = x.reshape(B * N, D)
    hc = ctx.reshape(B * CTX, D)
    zeros_d = jnp.zeros((D,), F32)
    w1, w3, w2 = moe_w1.astype(BF16), moe_w3.astype(BF16), moe_w2.astype(BF16)
    for i in range(depth):
        j = i // N_MIXERS
        ctx_stream = any(mixer[l] == 0 for l in range(i + 1, depth))
        ctx_in = ctx_stream or mixer[i] == 0
        mods = mods_all[i]
        if mixer[i] == 0:
            wqkv = na_w_qkv[j].astype(BF16)
            wo = na_w_o[j].astype(BF16)
            hg = jnp.stack([na_q_g[j], na_k_g[j]]).reshape(2, 1, HEAD_DIM)
            zeros_e = jnp.zeros((3 * D,), F32)
            qs = HEAD_DIM ** -0.5
            qkv = nm_matmul(h, norm1_g[i], mods, 0, 1, N, 0, wqkv, zeros_e, hg, qs)
            qkv_c = nm_matmul(hc, norm1_g[i], mods, 0, 1, B * CTX, B, wqkv, zeros_e, hg, qs)
            bias = _na_bias_tables(na_rpb[j], N // GRID_W)
            o, oc = na_attention(qkv, qkv_c, bias, B, N, CTX, D)
            h = matmul_residual(o, wo, zeros_d, mods, 2, N, 0, h)
            if ctx_stream:
                hc = matmul_residual(oc, wo, zeros_d, mods, 2, B * CTX, B, hc)
        else:
            win = hy_w_in[j].astype(BF16)
            wout = hy_w_out[j].astype(BF16)
            filt = (hy_f_w1[j], hy_f_b1[j], hy_f_freq1[j], hy_f_w2[j], hy_f_b2[j], hy_f_freq2[j],
                    hy_f_w3[j], hy_f_b3[j], hy_f_freq3[j], hy_f_wout[j])
            p = nm_matmul(h, norm1_g[i], mods, 0, 1, N, 0, win, hy_b_in[j], out_dtype=F32)
            z = hyena_mix(p, B, N, D, hy_conv_w[j], hy_conv_b[j], filt, hy_bias[j])
            h = matmul_residual(z, wout, hy_b_out[j], mods, 2, N, 0, h)
            if ctx_stream:
                pc = nm_matmul(hc, norm1_g[i], mods, 0, 1, B * CTX, B, win, hy_b_in[j], out_dtype=F32)
                zc = hyena_mix(pc, B, CTX, D, hy_conv_w[j], hy_conv_b[j], filt, hy_bias[j])
                hc = matmul_residual(zc, wout, hy_b_out[j], mods, 2, B * CTX, B, hc)
        h = ec_moe(h, norm2_g[i], mods, 3, 4, 5, N, 0, B, moe_router[i], i, w1, w3, w2)
        if ctx_stream:
            hc = ec_moe(hc, norm2_g[i], mods, 3, 4, 5, B * CTX, B, B, moe_router[i], i, w1, w3, w2)
    return h.reshape(B, N, D)
```

```python
import functools
import math

import jax
import jax.numpy as jnp
import numpy as np
from jax import lax
from jax.experimental import pallas as pl
from jax.experimental.pallas import tpu as pltpu

F32 = jnp.float32
BF16 = jnp.bfloat16

GRID_W = 64
N_MIXERS = 2
NORM_EPS = 1e-6
NEG_INF = -1e30
HEAD_DIM = 128
LANES = 128
WIN_ROWS = 8
WIN_COLS = 16
HYENA_SHORT = 3
HYENA_EMB = 33
HYENA_FAST_DECAY = 0.3
HYENA_SLOW_DECAY = 1.5
HYENA_TARGET = 1e-2
N_EXPERTS = 16
CAPACITY_FACTOR = 2

Q_ROWS = 4
BAND_ROWS = 12

VMEM_LIMIT = 56 * 1024 * 1024


def _cparams(*sem):
    return pltpu.CompilerParams(dimension_semantics=sem, vmem_limit_bytes=VMEM_LIMIT)


def _tile(n, want, unit=128):
    if n <= want:
        return n
    t = (want // unit) * unit
    while n % t:
        t -= unit
    return t


def _ada_kernel(c_ref, w_ref, b_ref, o_ref):
    c = c_ref[...]
    s = c * jax.nn.sigmoid(c)
    o_ref[...] = jnp.dot(s.astype(BF16), w_ref[...].astype(BF16),
                         preferred_element_type=F32) + b_ref[...]


def ada_all(cond, ada_w, ada_b, tn=1024):
    L, D, E = ada_w.shape
    tn = _tile(E, tn)
    return pl.pallas_call(
        _ada_kernel,
        grid=(L, E // tn),
        in_specs=[
            pl.BlockSpec((8, D), lambda l, j: (0, 0)),
            pl.BlockSpec((None, D, tn), lambda l, j: (l, 0, j)),
            pl.BlockSpec((None, 1, tn), lambda l, j: (l, 0, j)),
        ],
        out_specs=pl.BlockSpec((None, 8, tn), lambda l, j: (l, 0, j)),
        out_shape=jax.ShapeDtypeStruct((L, 8, E), F32),
        compiler_params=_cparams("parallel", "parallel"),
        name="ada",
    )(cond, ada_w, ada_b.reshape(L, 1, E))


def _norm_mod(x, g, sh, sc):
    y = x * lax.rsqrt(jnp.mean(x * x, axis=-1, keepdims=True) + NORM_EPS) * g
    return y * (1.0 + sc) + sh


def _nm_matmul_kernel(x_ref, g_ref, sh_ref, sc_ref, w_ref, b_ref, hg_ref, o_ref, a_ref, *,
                      n_norm_sections, section_cols, q_scale):
    j = pl.program_id(1)

    @pl.when(j == 0)
    def _():
        a_ref[...] = _norm_mod(x_ref[...], g_ref[...], sh_ref[...], sc_ref[...]).astype(BF16)

    acc = jnp.dot(a_ref[...], w_ref[...], preferred_element_type=F32) + b_ref[...]
    tn = acc.shape[1]
    if n_norm_sections == 0:
        o_ref[...] = acc.astype(o_ref.dtype)
        return
    sec = (j * tn) // section_cols

    @pl.when(sec < n_norm_sections)
    def _():
        hg = hg_ref[...]
        mult = jnp.where(sec == 0, q_scale, 1.0).astype(F32)
        for h in range(tn // HEAD_DIM):
            c = acc[:, h * HEAD_DIM:(h + 1) * HEAD_DIM]
            c = c * lax.rsqrt(jnp.mean(c * c, axis=-1, keepdims=True) + NORM_EPS) * hg
            o_ref[:, h * HEAD_DIM:(h + 1) * HEAD_DIM] = (c * mult).astype(o_ref.dtype)

    @pl.when(sec >= n_norm_sections)
    def _():
        o_ref[...] = acc.astype(o_ref.dtype)


def nm_matmul(x, g, mods, k_shift, k_scale, group_rows, group_base, w, b, head_g=None,
              q_scale=1.0, out_dtype=BF16, tm=1024, tn=1024):
    R, D = x.shape
    E = w.shape[1]
    tm = _tile(group_rows, tm, 8)
    tn = _tile(D, tn)
    n_norm = 0 if head_g is None else 2
    if head_g is None:
        head_g = jnp.ones((2, 1, HEAD_DIM), F32)

    def grp(i):
        return group_base + (i * tm) // group_rows

    def hg_map(i, j):
        return (jnp.minimum((j * tn) // D, 1), 0, 0)

    kern = functools.partial(_nm_matmul_kernel, n_norm_sections=n_norm, section_cols=D,
                             q_scale=q_scale)
    return pl.pallas_call(
        kern,
        grid=(R // tm, E // tn),
        in_specs=[
            pl.BlockSpec((tm, D), lambda i, j: (i, 0)),
            pl.BlockSpec((1, D), lambda i, j: (0, 0)),
            pl.BlockSpec((None, None, 1, D), lambda i, j: (k_shift, grp(i), 0, 0)),
            pl.BlockSpec((None, None, 1, D), lambda i, j: (k_scale, grp(i), 0, 0)),
            pl.BlockSpec((D, tn), lambda i, j: (0, j)),
            pl.BlockSpec((1, tn), lambda i, j: (0, j)),
            pl.BlockSpec((None, 1, HEAD_DIM), hg_map),
        ],
        out_specs=pl.BlockSpec((tm, tn), lambda i, j: (i, j)),
        out_shape=jax.ShapeDtypeStruct((R, E), out_dtype),
        scratch_shapes=[pltpu.VMEM((tm, D), BF16)],
        compiler_params=_cparams("parallel", "arbitrary"),
        name="nm_matmul",
    )(x, g.reshape(1, D), mods, mods, w, b.reshape(1, E), head_g)


def _mm_res_kernel(x_ref, w_ref, b_ref, gate_ref, res_ref, o_ref):
    y = jnp.dot(x_ref[...], w_ref[...], preferred_element_type=F32) + b_ref[...]
    o_ref[...] = res_ref[...] + gate_ref[...] * y


def matmul_residual(x, w, b, mods, k_gate, group_rows, group_base, res, tm=1024, tn=1024):
    R, K = x.shape
    E = w.shape[1]
    tm = _tile(group_rows, tm, 8)
    tn = _tile(E, tn)

    def grp(i):
        return group_base + (i * tm) // group_rows

    return pl.pallas_call(
        _mm_res_kernel,
        grid=(R // tm, E // tn),
        in_specs=[
            pl.BlockSpec((tm, K), lambda i, j: (i, 0)),
            pl.BlockSpec((K, tn), lambda i, j: (0, j)),
            pl.BlockSpec((1, tn), lambda i, j: (0, j)),
            pl.BlockSpec((None, None, 1, tn), lambda i, j: (k_gate, grp(i), 0, j)),
            pl.BlockSpec((tm, tn), lambda i, j: (i, j)),
        ],
        out_specs=pl.BlockSpec((tm, tn), lambda i, j: (i, j)),
        out_shape=jax.ShapeDtypeStruct((R, E), F32),
        compiler_params=_cparams("parallel", "parallel"),
        name="matmul_residual",
    )(x, w, b.reshape(1, E), mods, res)


def _na_bias_tables(rpb, rows):
    W = GRID_W
    cols = np.arange(W)
    cs = np.clip(cols - WIN_COLS // 2, 0, W - WIN_COLS)
    in_win = (cols[None, :] >= cs[:, None]) & (cols[None, :] < cs[:, None] + WIN_COLS)
    dc = np.clip(cols[None, :] - cols[:, None] + WIN_COLS - 1, 0, 2 * WIN_COLS - 2)
    n_dr, n_dc = 2 * WIN_ROWS - 1, 2 * WIN_COLS - 1
    onehot_r = np.zeros((3, Q_ROWS, BAND_ROWS, n_dr), np.float32)
    mask = np.zeros((3, Q_ROWS, W, BAND_ROWS, W), bool)
    for var, r0 in enumerate((0, Q_ROWS, rows - Q_ROWS)):
        bs = _band_start(r0, rows)
        for i in range(Q_ROWS):
            r = r0 + i
            rs = int(np.clip(r - WIN_ROWS // 2, 0, rows - WIN_ROWS))
            for jj in range(BAND_ROWS):
                kr = bs + jj
                if rs <= kr < rs + WIN_ROWS:
                    onehot_r[var, i, jj, kr - r + WIN_ROWS - 1] = 1.0
                    mask[var, i, :, jj, :] = in_win
    onehot_c = (dc[:, :, None] == np.arange(n_dc)).astype(np.float32)
    hp = lax.Precision.HIGHEST
    t = jnp.einsum('hrc,vijr->hvijc', rpb.astype(F32), onehot_r, precision=hp)
    t = jnp.einsum('hvijc,qkc->hviqjk', t, onehot_c, precision=hp)
    t = jnp.where(mask[None], t, NEG_INF)
    return t.reshape(rpb.shape[0], 3, Q_ROWS * W, BAND_ROWS * W)


def _band_start(r0, rows):
    return int(np.clip(r0 - WIN_ROWS // 2, 0, rows - BAND_ROWS))


def _na_kernel(q_ref, k_ref, v_ref, qc_ref, kc_ref, vc_ref, bias_ref, o_ref, oc_ref, *, rows):
    W = GRID_W
    nblk = rows // Q_ROWS
    kc = kc_ref[...]
    vc = vc_ref[...]
    nt = (((1,), (1,)), ((), ()))

    def attend(q, parts):
        ss = [lax.dot_general(q, k, nt, preferred_element_type=F32) + (0.0 if bias is None else bias)
              for k, _, bias in parts]
        m = functools.reduce(jnp.maximum, [jnp.max(s, axis=-1, keepdims=True) for s in ss])
        ps = [jnp.exp(s - m) for s in ss]
        den = sum(jnp.sum(p, axis=-1, keepdims=True) for p in ps)
        o = sum(jnp.dot(p.astype(BF16), v, preferred_element_type=F32)
                for p, (_, v, _) in zip(ps, parts))
        return o / den

    def body(blk, carry):
        r0 = blk * Q_ROWS
        bs = jnp.clip(r0 - WIN_ROWS // 2, 0, rows - BAND_ROWS)
        var = jnp.where(blk == 0, 0, jnp.where(blk == nblk - 1, 2, 1))
        q0 = pl.multiple_of(r0 * W, Q_ROWS * W)
        k0 = pl.multiple_of(bs * W, W)
        q = q_ref[pl.ds(q0, Q_ROWS * W), :]
        kb = k_ref[pl.ds(k0, BAND_ROWS * W), :]
        vb = v_ref[pl.ds(k0, BAND_ROWS * W), :]
        o = attend(q, [(kb, vb, bias_ref[var]), (kc, vc, None)])
        o_ref[pl.ds(q0, Q_ROWS * W), :] = o.astype(o_ref.dtype)
        return carry

    lax.fori_loop(0, nblk, body, 0, unroll=2)
    oc_ref[...] = attend(qc_ref[...], [(kc, vc, None)]).astype(oc_ref.dtype)


def na_attention(qkv, qkv_c, bias, B, N, CTX, D):
    H = D // HEAD_DIM
    rows = N // GRID_W
    kern = functools.partial(_na_kernel, rows=rows)

    def spec(n, sec):
        return pl.BlockSpec((n, HEAD_DIM), lambda h, b: (b, sec * H + h))

    return pl.pallas_call(
        kern,
        grid=(H, B),
        in_specs=[spec(N, 0), spec(N, 1), spec(N, 2), spec(CTX, 0), spec(CTX, 1), spec(CTX, 2),
                  pl.BlockSpec((None, 3, Q_ROWS * GRID_W, BAND_ROWS * GRID_W),
                               lambda h, b: (h, 0, 0, 0))],
        out_specs=[pl.BlockSpec((N, HEAD_DIM), lambda h, b: (b, h)),
                   pl.BlockSpec((CTX, HEAD_DIM), lambda h, b: (b, h))],
        out_shape=[jax.ShapeDtypeStruct((B * N, D), BF16),
                   jax.ShapeDtypeStruct((B * CTX, D), BF16)],
        compiler_params=_cparams("parallel", "parallel"),
        name="na_attention",
    )(qkv, qkv, qkv, qkv_c, qkv_c, qkv_c, bias)


def _expert_kernel(idx_ref, a_hbm, w1_ref, w3_ref, w2_ref, gate_ref, o_ref, xbuf, sem, *, n_tok, cap, bg):
    e, g = pl.program_id(0), pl.program_id(1)
    n_exp, ng = pl.num_programs(0), pl.num_programs(1)
    step = e * ng + g
    slot = step % 2
    nsteps = n_exp * ng
    rows = bg * cap
    S = xbuf.shape[1] // rows

    def copy_row(ee, gg, sl, r, priority=0):
        b = gg * bg + r // cap
        row = b * n_tok + idx_ref[(b * n_exp + ee) * cap + r % cap]
        src = pl.ds(pl.multiple_of(row * S, S), S)
        dst = pl.ds(pl.multiple_of(r * S, S), S)
        pltpu.make_async_copy(a_hbm.at[src, :], xbuf.at[sl, dst, :], sem.at[sl]).start(priority=priority)

    def wait_rows(sl):
        pltpu.make_async_copy(a_hbm.at[pl.ds(0, rows * S), :], xbuf.at[sl], sem.at[sl]).wait()

    @pl.when(step == 0)
    def _():
        def body(r, carry):
            copy_row(e, g, slot, r)
            return carry

        lax.fori_loop(0, rows, body, 0, unroll=8)

    wait_rows(slot)
    hi, lo = _load_token_tiles(xbuf, (slot,), 0, rows, S)
    x = jnp.concatenate([p.astype(BF16) for p in hi + lo], axis=1)
    nxt = (step + 1) % nsteps
    for r in range(rows):
        copy_row(nxt // ng, nxt % ng, 1 - slot, r, priority=r % 2)
    h1 = jnp.dot(x, w1_ref[...], preferred_element_type=F32)
    h3 = jnp.dot(x, w3_ref[...], preferred_element_type=F32)
    hid = (h1 * jax.nn.sigmoid(h1) * h3).astype(BF16)
    y = jnp.dot(hid, w2_ref[...], preferred_element_type=F32) * gate_ref[...].reshape(rows, 1)
    packed = _pack_bf16_pair(y)
    for i in range(bg):
        _store_token_tiles(o_ref, (i,), packed[i * cap:(i + 1) * cap])

    @pl.when(step == nsteps - 1)
    def _():
        wait_rows(1 - slot)


def _pack_bf16_pair(y):
    half = y.shape[1] // 2
    hi = pltpu.bitcast(y[:, :half].astype(BF16).astype(F32), jnp.uint32)
    lo = pltpu.bitcast(y[:, half:].astype(BF16).astype(F32), jnp.uint32)
    return hi | (lo >> 16)


def _unpack_bf16_pair(u):
    hi = pltpu.bitcast(u & jnp.uint32(0xFFFF0000), F32)
    lo = pltpu.bitcast(u << 16, F32)
    return hi, lo


def _store_token_tiles(ref, lead, packed):
    rows, S = packed.shape[0], packed.shape[1] // LANES
    for s in range(S):
        ref[(*lead, pl.ds(s, rows, stride=S), slice(None))] = packed[:, s * LANES:(s + 1) * LANES]


def _load_token_tiles(ref, lead, first, rows, S):
    pieces = [_unpack_bf16_pair(ref[(*lead, pl.ds(first * S + s, rows, stride=S), slice(None))])
              for s in range(S)]
    return [p[0] for p in pieces], [p[1] for p in pieces]


def expert_ffn(a, idx_flat, layer, w1, w3, w2, gate, B, n_tok):
    _, E, D, Fh = w1.shape
    C = gate.shape[2]
    S = D // (2 * LANES)
    bg = max(1, min(B, EXPERT_ROWS // C))
    while B % bg:
        bg -= 1
    kern = functools.partial(_expert_kernel, n_tok=n_tok, cap=C, bg=bg)
    grid_spec = pltpu.PrefetchScalarGridSpec(
        num_scalar_prefetch=1,
        grid=(E, B // bg),
        in_specs=[
            pl.BlockSpec(memory_space=pl.ANY),
            pl.BlockSpec((None, None, D, Fh), lambda e, g, idx: (layer, e, 0, 0)),
            pl.BlockSpec((None, None, D, Fh), lambda e, g, idx: (layer, e, 0, 0)),
            pl.BlockSpec((None, None, Fh, D), lambda e, g, idx: (layer, e, 0, 0)),
            pl.BlockSpec((bg, None, C, 1), lambda e, g, idx: (g, e, 0, 0)),
        ],
        out_specs=pl.BlockSpec((bg, None, C * S, LANES), lambda e, g, idx: (g, e, 0, 0)),
        scratch_shapes=[pltpu.VMEM((2, bg * C * S, LANES), jnp.uint32), pltpu.SemaphoreType.DMA((2,))],
    )
    return pl.pallas_call(
        kern,
        grid_spec=grid_spec,
        out_shape=jax.ShapeDtypeStruct((B, E, C * S, LANES), jnp.uint32),
        compiler_params=_cparams("arbitrary", "arbitrary"),
        name="expert_ffn",
    )(idx_flat, a, w1, w3, w2, gate)


def _moe_prep_kernel(x_ref, g_ref, sh_ref, sc_ref, r_ref, a_ref, aff_ref):
    a = _norm_mod(x_ref[...], g_ref[...], sh_ref[...], sc_ref[...])
    _store_token_tiles(a_ref, (), _pack_bf16_pair(a))
    logits = lax.dot_general(r_ref[...], a.astype(BF16), (((1,), (1,)), ((), ())),
                             preferred_element_type=F32)
    m = jnp.max(logits, axis=0, keepdims=True)
    p = jnp.exp(logits - m)
    aff_ref[...] = p / jnp.sum(p, axis=0, keepdims=True)


def moe_prep(x, g, mods, k_shift, k_scale, group_rows, group_base, router_t, tm=512):
    R, D = x.shape
    E = router_t.shape[0]
    S = D // (2 * LANES)
    tm = _tile(group_rows, tm, 128)

    def grp(i):
        return group_base + (i * tm) // group_rows

    return pl.pallas_call(
        _moe_prep_kernel,
        grid=(R // tm,),
        in_specs=[
            pl.BlockSpec((tm, D), lambda i: (i, 0)),
            pl.BlockSpec((1, D), lambda i: (0, 0)),
            pl.BlockSpec((None, None, 1, D), lambda i: (k_shift, grp(i), 0, 0)),
            pl.BlockSpec((None, None, 1, D), lambda i: (k_scale, grp(i), 0, 0)),
            pl.BlockSpec((E, D), lambda i: (0, 0)),
        ],
        out_specs=[pl.BlockSpec((tm * S, LANES), lambda i: (i, 0)),
                   pl.BlockSpec((E, tm), lambda i: (0, i))],
        out_shape=[jax.ShapeDtypeStruct((R * S, LANES), jnp.uint32), jax.ShapeDtypeStruct((E, R), F32)],
        compiler_params=_cparams("parallel"),
        name="moe_prep",
    )(x, g.reshape(1, D), mods, mods, router_t)


EXPERT_ROWS = 512
COMBINE_TOKENS = 64
HIT_UNROLL = 4
CUMSUM_CHUNK = 256
ROUTE_BLOCK_LANES = 512
TILE_TABLE_LANES = 128


def _route_select_kernel(aff_ref, tri_ref, ltri_ref, lt_ref, in_ref,
                         posi_ref, key_ref, slab_ref, p0_ref, kmax_ref, *, cap):
    E, N = aff_ref.shape
    aff = aff_ref[...]
    bits = pltpu.bitcast(aff, jnp.int32)
    cur = jnp.zeros((E, 1), jnp.int32)
    for bit in range(30, -1, -1):
        cand = cur | (1 << bit)
        cnt = jnp.sum(jnp.where(bits >= cand, 1.0, 0.0), axis=1, keepdims=True)
        cur = jnp.where(cnt >= cap, cand, cur)
    gt = bits > cur
    eq = bits == cur
    need = cap - jnp.sum(jnp.where(gt, 1.0, 0.0), axis=1, keepdims=True)

    def cumsum_incl(x):
        tri = tri_ref[...]
        w = min(CUMSUM_CHUNK, N)
        off = jnp.zeros((E, 1), F32)
        outs = []
        for c in range(N // w):
            xc = x[:, c * w:(c + 1) * w]
            outs.append(jnp.dot(xc.astype(BF16), tri[:w, :w], preferred_element_type=F32) + off)
            off = off + jnp.sum(xc, axis=1, keepdims=True)
        return jnp.concatenate(outs, axis=1) if len(outs) > 1 else outs[0]

    eqf = jnp.where(eq, 1.0, 0.0)
    tie_rank = cumsum_incl(eqf) - eqf
    sel = gt | (eq & (tie_rank < need))
    self_ = jnp.where(sel, 1.0, 0.0)
    pos_incl = cumsum_incl(self_)
    posi_ref[...] = pos_incl
    key_ref[...] = jnp.where(sel, pos_incl - self_, -1.0)
    selb = self_.astype(BF16)
    erank = jnp.dot(ltri_ref[...], selb, preferred_element_type=F32)
    in_tile = lax.broadcasted_iota(jnp.int32, (E, N), 1) & (COMBINE_TOKENS - 1)
    slab_ref[...] = erank * COMBINE_TOKENS + in_tile.astype(F32)
    p0_ref[...] = jnp.dot(selb, lt_ref[...], preferred_element_type=F32).astype(jnp.int32)
    count = jnp.sum(self_, axis=0, keepdims=True)
    kk = lax.broadcasted_iota(jnp.int32, (E, 1), 0).astype(F32)
    over = jnp.where(count > kk, 1.0, 0.0).astype(BF16)
    per_tile = jnp.dot(over, in_ref[...], preferred_element_type=F32)
    kmax_ref[...] = jnp.sum(jnp.where(per_tile > 0.0, 1.0, 0.0), axis=0, keepdims=True).astype(jnp.int32)


def _route_compact_kernel(p0_ref, aff_ref, posi_ref, key_ref, slab_ref, idx_ref, gate_ref, rank_ref, *,
                          cap, tn, lanes, width):
    b = pl.program_id(0)
    E, N = aff_ref.shape
    ch = min(64, cap)
    tpb = width // tn

    def e_body(e, carry):
        q = (b * E + e) * lanes

        def c_body(cc, tiles):
            c0 = pl.multiple_of(cc * ch, ch)
            t_lo = lax.while_loop(lambda t: p0_ref[q + t + 1] <= c0, lambda t: t + 1, tiles[0])
            t_hi = lax.while_loop(lambda t: p0_ref[q + t] < c0 + ch, lambda t: t + 1, tiles[1])
            j_lo = t_lo // tpb
            j_hi = (t_hi + tpb - 1) // tpb
            ccol = (c0 + lax.broadcasted_iota(jnp.int32, (ch, 1), 0)).astype(F32)

            def j_body(j, acc):
                row = pl.ds(e, 1)
                blk = pl.ds(pl.multiple_of(j * width, width), width)
                hit = key_ref[row, blk] == ccol
                return (acc[0] + jnp.sum(jnp.where(posi_ref[row, blk] <= ccol, 1.0, 0.0), axis=1, keepdims=True),
                        acc[1] + jnp.sum(jnp.where(hit, aff_ref[row, blk], 0.0), axis=1, keepdims=True),
                        acc[2] + jnp.sum(jnp.where(hit, slab_ref[row, blk], 0.0), axis=1, keepdims=True))

            zero = jnp.zeros((ch, 1), F32)
            before = zero + (j_lo * width).astype(F32)
            iv, gv, rv = lax.fori_loop(j_lo, j_hi, j_body, (before, zero, zero))
            idx_ref[e, pl.ds(c0, ch), :] = iv.astype(jnp.int32)
            gate_ref[e, pl.ds(c0, ch), :] = gv
            rank_ref[e, pl.ds(c0, ch), :] = rv.astype(jnp.int32)
            return (t_lo, t_hi)

        lax.fori_loop(0, cap // ch, c_body, (0, 0))
        return carry

    lax.fori_loop(0, E, e_body, 0)


def moe_route(aff_t, B, N, cap):
    E = aff_t.shape[0]
    tn = COMBINE_TOKENS
    lanes = TILE_TABLE_LANES
    assert N // tn + 1 <= lanes and N % tn == 0 and N % min(CUMSUM_CHUNK, N) == 0
    w = min(CUMSUM_CHUNK, N)
    tri = (np.arange(w)[:, None] <= np.arange(w)[None, :]).astype(np.float32)
    ltri = (np.arange(E)[None, :] < np.arange(E)[:, None]).astype(np.float32)
    tok = np.arange(N)[:, None]
    tile = np.arange(lanes)[None, :]
    before = (tok < tile * tn).astype(np.float32)
    inside = (tok // tn == tile).astype(np.float32)
    const = lambda shape: pl.BlockSpec(shape, lambda b: (0,) * len(shape))
    per_tok = pl.BlockSpec((None, E, N), lambda b, *_: (b, 0, 0))
    posi, key, slab, p0, kmax = pl.pallas_call(
        functools.partial(_route_select_kernel, cap=cap),
        grid=(B,),
        in_specs=[pl.BlockSpec((E, N), lambda b: (0, b)), const((w, w)), const((E, E)),
                  const((N, lanes)), const((N, lanes))],
        out_specs=[per_tok, per_tok, per_tok,
                   pl.BlockSpec((None, E, lanes), lambda b: (b, 0, 0)),
                   pl.BlockSpec((None, 1, lanes), lambda b: (b, 0, 0))],
        out_shape=[jax.ShapeDtypeStruct((B, E, N), F32)] * 3 + [
            jax.ShapeDtypeStruct((B, E, lanes), jnp.int32), jax.ShapeDtypeStruct((B, 1, lanes), jnp.int32)],
        compiler_params=_cparams("parallel"),
        name="moe_route_select",
    )(aff_t, jnp.asarray(tri, BF16), jnp.asarray(ltri, BF16), jnp.asarray(before, BF16),
      jnp.asarray(inside, BF16))
    width = min(ROUTE_BLOCK_LANES, N)
    sel3 = pl.BlockSpec((None, E, cap, 1), lambda b, *_: (b, 0, 0, 0))
    idx, gate, rank = pl.pallas_call(
        functools.partial(_route_compact_kernel, cap=cap, tn=tn, lanes=lanes, width=width),
        grid_spec=pltpu.PrefetchScalarGridSpec(
            num_scalar_prefetch=1,
            grid=(B,),
            in_specs=[pl.BlockSpec((E, N), lambda b, *_: (0, b)), per_tok, per_tok, per_tok],
            out_specs=[sel3, sel3, sel3],
        ),
        out_shape=[jax.ShapeDtypeStruct((B, E, cap, 1), jnp.int32),
                   jax.ShapeDtypeStruct((B, E, cap, 1), F32),
                   jax.ShapeDtypeStruct((B, E, cap, 1), jnp.int32)],
        compiler_params=_cparams("parallel"),
        name="moe_route_compact",
    )(p0.reshape(-1), aff_t, posi, key, slab)
    return idx, gate, rank, p0, kmax


def _combine_kernel(rank_ref, p0_ref, kmax_ref, ys_hbm, h_ref, g2_ref, o_ref, rbuf, sem, *,
                    n_exp, cap, tn, lanes):
    b, t = pl.program_id(0), pl.program_id(1)
    nt = pl.num_programs(1)
    step = b * nt + t
    slot = step % 2
    S = rbuf.shape[1] // (n_exp * tn)

    def tile_hits(bb, tt):
        def body(e, tot):
            q = (bb * n_exp + e) * lanes + tt
            return tot + p0_ref[q + 1] - p0_ref[q]
        return lax.fori_loop(0, n_exp, body, 0)

    def fetch(bb, tt, sl):
        km = kmax_ref[bb * lanes + tt]
        for k in range(n_exp):
            @pl.when(k < km)
            def _():
                rbuf[sl, k * tn * S:(k + 1) * tn * S, :] = jnp.zeros((tn * S, LANES), jnp.uint32)

        def copy_row(row, priority=0):
            src = pl.ds(pl.multiple_of(row * S, S), S)
            dst = pl.ds(pl.multiple_of(rank_ref[row] * S, S), S)
            pltpu.make_async_copy(ys_hbm.at[src, :], rbuf.at[sl, dst, :], sem.at[sl]).start(priority=priority)

        def e_body(e, carry):
            q = (bb * n_exp + e) * lanes + tt
            lo = (bb * n_exp + e) * cap + p0_ref[q]
            n = p0_ref[q + 1] - p0_ref[q]
            groups = n // HIT_UNROLL

            def g_body(g, carry2):
                for u in range(HIT_UNROLL):
                    copy_row(lo + g * HIT_UNROLL + u, priority=u % 2)
                return carry2

            def r_body(c, carry2):
                copy_row(lo + c)
                return carry2

            lax.fori_loop(0, groups, g_body, 0)
            lax.fori_loop(groups * HIT_UNROLL, n, r_body, 0)
            return carry

        lax.fori_loop(0, n_exp, e_body, 0)

    @pl.when(step == 0)
    def _():
        fetch(b, t, slot)

    @pl.when(step + 1 < pl.num_programs(0) * nt)
    def _():
        nxt = step + 1
        fetch(nxt // nt, nxt % nt, 1 - slot)

    hits = tile_hits(b, t)

    nrows = hits * S
    bulk = pl.multiple_of((nrows // 8) * 8, 8)

    @pl.when(bulk > 0)
    def _():
        pltpu.make_async_copy(ys_hbm.at[pl.ds(0, bulk), :], rbuf.at[slot, pl.ds(0, bulk), :],
                              sem.at[slot]).wait()

    def wait_row(i, carry):
        pltpu.make_async_copy(ys_hbm.at[pl.ds(0, 1), :], rbuf.at[slot, pl.ds(0, 1), :], sem.at[slot]).wait()
        return carry

    lax.fori_loop(0, nrows - bulk, wait_row, 0)

    km = kmax_ref[b * lanes + t]
    g2 = g2_ref[...]
    for r in range(tn // 8):
        def k_body(k, tot):
            hi, lo = _load_token_tiles(rbuf, (slot,), k * tn + r * 8, 8, S)
            return tuple(t_ + p for t_, p in zip(tot, hi + lo))

        tot = lax.fori_loop(0, km, k_body, (jnp.zeros((8, LANES), F32),) * (2 * S))
        rows = slice(r * 8, (r + 1) * 8)
        for s in range(2 * S):
            cols = slice(s * LANES, (s + 1) * LANES)
            o_ref[rows, cols] = h_ref[rows, cols] + g2[:, cols] * tot[s]


def moe_combine(ys, h, mods, k_gate, group_rows, group_base, rank, p0, kmax, B, N):
    R, D = h.shape
    E, cap = rank.shape[1], rank.shape[2]
    tn = COMBINE_TOKENS
    lanes = TILE_TABLE_LANES
    nt = N // tn

    def grp(b, t):
        return group_base + ((b * nt + t) * tn) // group_rows

    kern = functools.partial(_combine_kernel, n_exp=E, cap=cap, tn=tn, lanes=lanes)
    grid_spec = pltpu.PrefetchScalarGridSpec(
        num_scalar_prefetch=3,
        grid=(B, nt),
        in_specs=[
            pl.BlockSpec(memory_space=pl.ANY),
            pl.BlockSpec((tn, D), lambda b, t, *_: (b * nt + t, 0)),
            pl.BlockSpec((None, None, 1, D), lambda b, t, *_: (k_gate, grp(b, t), 0, 0)),
        ],
        out_specs=pl.BlockSpec((tn, D), lambda b, t, *_: (b * nt + t, 0)),
        scratch_shapes=[pltpu.VMEM((2, E * tn * (D // (2 * LANES)), LANES), jnp.uint32),
                        pltpu.SemaphoreType.DMA((2,))],
    )
    return pl.pallas_call(
        kern,
        grid_spec=grid_spec,
        out_shape=jax.ShapeDtypeStruct((R, D), F32),
        compiler_params=_cparams("arbitrary", "arbitrary"),
        name="moe_combine",
    )(rank.reshape(-1), p0.reshape(-1), kmax.reshape(-1), ys, h, mods)


def ec_moe(h, g, mods, k_shift, k_scale, k_gate, group_rows, group_base, B, router, layer, w1, w3, w2):
    R, D = h.shape
    N = R // B
    E = router.shape[1]
    cap = CAPACITY_FACTOR * N // E
    a, aff_t = moe_prep(h, g, mods, k_shift, k_scale, group_rows, group_base, router.T.astype(BF16))
    idx, gate, rank, p0, kmax = moe_route(aff_t, B, N, cap)
    ys = expert_ffn(a, idx.reshape(-1), layer, w1, w3, w2, gate, B, N)
    return moe_combine(ys.reshape(-1, LANES), h, mods, k_gate, group_rows, group_base,
                       rank, p0, kmax, B, N)


FFT_MINOR = 256
FFT_GROUP = 16


def _dft_tables(L):
    n = 2 * L
    n1 = n // FFT_MINOR
    nb = n1 // 2
    two_pi = 2.0 * math.pi
    a = jnp.arange(n1, dtype=jnp.int32)
    ang = two_pi * ((a[:, None] * a[None, :]) % n1).astype(F32) / n1
    cos1, sin1 = jnp.cos(ang), jnp.sin(ang)
    eye = jnp.eye(FFT_GROUP, dtype=F32)

    def kron(m):
        return jnp.kron(m, eye)

    f1 = jnp.arange(n1, dtype=jnp.int32)[:, None, None]
    f2 = jnp.arange(FFT_MINOR, dtype=jnp.int32)[None, :, None]
    n2 = jnp.arange(FFT_MINOR, dtype=jnp.int32)[None, None, :]
    ang = two_pi * ((n2 * (f1 + n1 * f2)) % n).astype(F32) / n
    gr, gi = jnp.cos(ang), -jnp.sin(ang)
    return dict(
        n1=n1, nb=nb,
        m_fwd_full=jnp.concatenate([kron(cos1), kron(-sin1)], axis=0).astype(BF16),
        m_fwd_half=jnp.concatenate([kron(cos1[:, :nb]), kron(-sin1[:, :nb])], axis=0).astype(BF16),
        m_inv=jnp.concatenate([kron(cos1[:nb]), kron(sin1[:nb])], axis=0).astype(BF16),
        g=jnp.concatenate([gr, gi], axis=1).astype(BF16),
        gt=jnp.concatenate([gr.transpose(0, 2, 1), gi.transpose(0, 2, 1)], axis=1).astype(BF16),
    )


def _conv_gate_kernel(x0_ref, x1_ref, v_ref, w_ref, b_ref, vo_ref, x0o_ref):
    L, C = x0_ref.shape
    row = lax.broadcasted_iota(jnp.int32, (L, C), 0)

    def conv(ref, k):
        x = ref[...]
        w = w_ref[k]
        prev = jnp.where(row == 0, 0.0, pltpu.roll(x, 1, 0))
        nxt = jnp.where(row == L - 1, 0.0, pltpu.roll(x, L - 1, 0))
        return prev * w[0:1] + x * w[1:2] + nxt * w[2:3] + b_ref[k]

    x0 = conv(x0_ref, 0)
    x1 = conv(x1_ref, 1)
    v = conv(v_ref, 2)
    vo_ref[...] = (v * x1).astype(vo_ref.dtype)
    x0o_ref[...] = x0.astype(x0o_ref.dtype)


def conv_gate(p, B, L, D, conv_w, conv_b, tc=256):
    tc = _tile(D, tc)
    nc = D // tc
    w = conv_w.reshape(HYENA_SHORT, 3, D).transpose(1, 0, 2)
    b = conv_b.reshape(3, 1, D)

    def spec(k):
        return pl.BlockSpec((L, tc), lambda bi, c: (bi, k * nc + c))

    v, x0 = pl.pallas_call(
        _conv_gate_kernel,
        grid=(B, nc),
        in_specs=[spec(0), spec(1), spec(2),
                  pl.BlockSpec((3, HYENA_SHORT, tc), lambda bi, c: (0, 0, c)),
                  pl.BlockSpec((3, 1, tc), lambda bi, c: (0, 0, c))],
        out_specs=[pl.BlockSpec((L, tc), lambda bi, c: (bi, c))] * 2,
        out_shape=[jax.ShapeDtypeStruct((B * L, D), BF16)] * 2,
        compiler_params=_cparams("parallel", "parallel"),
        name="hyena_conv_gate",
    )(p, p, p, w, b)
    return v.reshape(B, L, D), x0.reshape(B, L, D)


def _filter_kernel(emb_ref, w1_ref, b1_ref, f1_ref, w2_ref, b2_ref, f2_ref, w3_ref, b3_ref, f3_ref,
                   wo_ref, dl_ref, k_ref, norm_ref, *, L):
    i = pl.program_id(0)
    tr = emb_ref.shape[0]
    hp = lax.Precision.HIGHEST
    emb = emb_ref[...]
    h = jnp.sin(f1_ref[...] * (jnp.dot(emb, w1_ref[...], precision=hp, preferred_element_type=F32) + b1_ref[...]))
    h = jnp.sin(f2_ref[...] * (jnp.dot(h, w2_ref[...], precision=hp, preferred_element_type=F32) + b2_ref[...]))
    h = jnp.sin(f3_ref[...] * (jnp.dot(h, w3_ref[...], precision=hp, preferred_element_type=F32) + b3_ref[...]))
    k = jnp.dot(h, wo_ref[...], precision=hp, preferred_element_type=F32)
    k = k * jnp.exp(-emb[:, 0:1] * dl_ref[...])
    row = i * tr + lax.broadcasted_iota(jnp.int32, (tr, 1), 0)
    k = jnp.where(row == L, 0.0, k)
    k_ref[...] = k

    @pl.when(i == 0)
    def _():
        norm_ref[...] = jnp.zeros_like(norm_ref)

    norm_ref[...] += jnp.sum(jnp.abs(k), axis=0, keepdims=True)


def hyena_filter(L, D, w1, b1, f1, w2, b2, f2, w3, b3, f3, w_out, tr=512):
    n = 2 * L
    tr = _tile(L, tr, 8)
    P = 128
    bands = (HYENA_EMB - 1) // 2
    d = np.arange(n)
    d = np.where(d <= L, d, n - d).clip(0, L - 1)
    t01 = np.linspace(0.0, 1.0, L)[d]
    wang = 2 * math.pi * d / L
    fr = np.linspace(1e-4, bands - 1, bands)
    emb = np.zeros((n, P), np.float32)
    emb[:, 0] = t01
    emb[:, 1:1 + bands] = np.cos(fr[None, :] * wang[:, None])
    emb[:, 1 + bands:1 + 2 * bands] = -np.sin(fr[None, :] * wang[:, None])

    def padw(w):
        return jnp.zeros((P, P), F32).at[:w.shape[0], :w.shape[1]].set(w)

    def padv(v):
        return jnp.zeros((1, P), F32).at[0, :v.shape[0]].set(v)

    wo = jnp.zeros((P, 2 * D), F32).at[:w_out.shape[0]].set(w_out)
    max_decay = math.log(HYENA_TARGET) / HYENA_FAST_DECAY
    min_decay = math.log(HYENA_TARGET) / HYENA_SLOW_DECAY
    deltas = jnp.abs(jnp.linspace(min_decay, max_decay, D, dtype=F32)).reshape(1, D)
    small = pl.BlockSpec((P, P), lambda i: (0, 0))
    vec = pl.BlockSpec((1, P), lambda i: (0, 0))
    kern = functools.partial(_filter_kernel, L=L)
    return pl.pallas_call(
        kern,
        grid=(n // tr,),
        in_specs=[pl.BlockSpec((tr, P), lambda i: (i, 0)),
                  small, vec, vec, small, vec, vec, small, vec, vec,
                  pl.BlockSpec((P, D), lambda i: (0, (i * tr) // L)),
                  pl.BlockSpec((1, D), lambda i: (0, 0))],
        out_specs=[pl.BlockSpec((tr, D), lambda i: (i, 0)), pl.BlockSpec((1, D), lambda i: (0, 0))],
        out_shape=[jax.ShapeDtypeStruct((n, D), F32), jax.ShapeDtypeStruct((1, D), F32)],
        compiler_params=_cparams("arbitrary"),
        name="hyena_filter",
    )(jnp.asarray(emb), padw(w1), padv(b1), padv(f1), padw(w2), padv(b2), padv(f2),
      padw(w3), padv(b3), padv(f3), wo, deltas)


def _fwd_major_kernel(*refs, n1, nb, has_imag):
    if has_imag:
        zr_ref, zi_ref, m_ref, ar_ref, ai_ref = refs
    else:
        zr_ref, m_ref, ar_ref, ai_ref = refs
    m = m_ref[...]
    half = n1 * FFT_GROUP

    def body(g, carry):
        def rows(blk):
            return pl.ds(pl.multiple_of(blk * FFT_MINOR + g * FFT_GROUP, FFT_GROUP), FFT_GROUP)

        def slab(ref):
            return jnp.concatenate([ref[rows(blk), :] for blk in range(nb)], axis=0).astype(BF16)

        pr = jnp.dot(m, slab(zr_ref), preferred_element_type=F32)
        if has_imag:
            pi = jnp.dot(m, slab(zi_ref), preferred_element_type=F32)
            ar, ai = pr[:half] - pi[half:], pi[:half] + pr[half:]
        else:
            ar, ai = pr[:half], pr[half:]
        for f1 in range(n1):
            sl = slice(f1 * FFT_GROUP, (f1 + 1) * FFT_GROUP)
            ar_ref[rows(f1), :] = ar[sl].astype(BF16)
            ai_ref[rows(f1), :] = ai[sl].astype(BF16)
        return carry

    lax.fori_loop(0, FFT_MINOR // FFT_GROUP, body, 0)


def fwd_major(z, tabs, pairs, tc=256):
    n1, nb = tabs["n1"], tabs["nb"]
    n = n1 * FFT_MINOR
    Bz, Lz, D = z.shape
    tc = _tile(D, tc)
    if pairs:
        P = Bz // 2
        m = tabs["m_fwd_half"]
        ins = [z, z, m]
        in_specs = [pl.BlockSpec((None, Lz, tc), lambda p, c: (2 * p, 0, c)),
                    pl.BlockSpec((None, Lz, tc), lambda p, c: (2 * p + 1, 0, c))]
        nblk = nb
    else:
        P = 1
        m = tabs["m_fwd_full"]
        ins = [z, m]
        in_specs = [pl.BlockSpec((None, Lz, tc), lambda p, c: (0, 0, c))]
        nblk = n1
    in_specs.append(pl.BlockSpec(m.shape, lambda p, c: (0, 0)))
    kern = functools.partial(_fwd_major_kernel, n1=n1, nb=nblk, has_imag=pairs)
    return pl.pallas_call(
        kern,
        grid=(P, D // tc),
        in_specs=in_specs,
        out_specs=[pl.BlockSpec((None, n, tc), lambda p, c: (p, 0, c))] * 2,
        out_shape=[jax.ShapeDtypeStruct((P, n, D), BF16)] * 2,
        compiler_params=_cparams("parallel", "parallel"),
        name="hyena_fwd_major",
    )(*ins)


def _spectrum_kernel(ar_ref, ai_ref, g_ref, s_ref, kr_ref, ki_ref):
    g = g_ref[...]
    h = FFT_MINOR
    pr = jnp.dot(g, ar_ref[...], preferred_element_type=F32)
    pi = jnp.dot(g, ai_ref[...], preferred_element_type=F32)
    s = s_ref[...]
    kr_ref[...] = (pr[:h] - pi[h:]) * s
    ki_ref[...] = (pi[:h] + pr[h:]) * s


def filter_spectrum(ar, ai, tabs, scale, tc=1024):
    _, n, D = ar.shape
    tc = _tile(D, tc)
    blk = pl.BlockSpec((None, FFT_MINOR, tc), lambda f, c: (0, f, c))
    out = pl.BlockSpec((FFT_MINOR, tc), lambda f, c: (f, c))
    return pl.pallas_call(
        _spectrum_kernel,
        grid=(tabs["n1"], D // tc),
        in_specs=[blk, blk, pl.BlockSpec((None, 2 * FFT_MINOR, FFT_MINOR), lambda f, c: (f, 0, 0)),
                  pl.BlockSpec((1, tc), lambda f, c: (0, c))],
        out_specs=[out, out],
        out_shape=[jax.ShapeDtypeStruct((n, D), F32)] * 2,
        compiler_params=_cparams("parallel", "parallel"),
        name="hyena_filter_spectrum",
    )(ar, ai, tabs["g"], scale)


def _minor_kernel(ar_ref, ai_ref, g_ref, gt_ref, kr_ref, ki_ref, br_ref, bi_ref):
    h = FFT_MINOR
    g = g_ref[...]
    pr = jnp.dot(g, ar_ref[...], preferred_element_type=F32)
    pi = jnp.dot(g, ai_ref[...], preferred_element_type=F32)
    xr, xi = pr[:h] - pi[h:], pi[:h] + pr[h:]
    kr, ki = kr_ref[...], ki_ref[...]
    yr = (xr * kr - xi * ki).astype(BF16)
    yi = (xr * ki + xi * kr).astype(BF16)
    gt = gt_ref[...]
    qr = jnp.dot(gt, yr, preferred_element_type=F32)
    qi = jnp.dot(gt, yi, preferred_element_type=F32)
    br_ref[...] = (qr[:h] + qi[h:]).astype(BF16)
    bi_ref[...] = (qi[:h] - qr[h:]).astype(BF16)


def minor_conv(ar, ai, tabs, kr, ki, tc=1024):
    P, n, D = ar.shape
    tc = _tile(D, tc)
    blk = pl.BlockSpec((None, FFT_MINOR, tc), lambda f, c, p: (p, f, c))
    tab = pl.BlockSpec((None, 2 * FFT_MINOR, FFT_MINOR), lambda f, c, p: (f, 0, 0))
    kblk = pl.BlockSpec((FFT_MINOR, tc), lambda f, c, p: (f, c))
    return pl.pallas_call(
        _minor_kernel,
        grid=(tabs["n1"], D // tc, P),
        in_specs=[blk, blk, tab, tab, kblk, kblk],
        out_specs=[blk, blk],
        out_shape=[jax.ShapeDtypeStruct((P, n, D), BF16)] * 2,
        compiler_params=_cparams("parallel", "parallel", "parallel"),
        name="hyena_minor_conv",
    )(ar, ai, tabs["g"], tabs["gt"], kr, ki)


def _inv_major_kernel(br_ref, bi_ref, m_ref, v_ref, x0_ref, bias_ref, o_ref, *, n1, nb):
    m = m_ref[...]
    half = nb * FFT_GROUP
    bias = bias_ref[...]

    def body(g, carry):
        def rows(blk):
            return pl.ds(pl.multiple_of(blk * FFT_MINOR + g * FFT_GROUP, FFT_GROUP), FFT_GROUP)

        def slab(ref):
            return jnp.concatenate([ref[rows(f1), :] for f1 in range(n1)], axis=0)

        pr = jnp.dot(m, slab(br_ref), preferred_element_type=F32)
        pi = jnp.dot(m, slab(bi_ref), preferred_element_type=F32)
        ys = (pr[:half] - pi[half:], pi[:half] + pr[half:])
        for s in range(2):
            for blk in range(nb):
                y = ys[s][blk * FFT_GROUP:(blk + 1) * FFT_GROUP]
                v = v_ref[s, rows(blk), :].astype(F32)
                x0 = x0_ref[s, rows(blk), :].astype(F32)
                o_ref[s, rows(blk), :] = ((y + v * bias) * x0).astype(o_ref.dtype)
        return carry

    lax.fori_loop(0, FFT_MINOR // FFT_GROUP, body, 0)


def inv_major(br, bi, tabs, v, x0, bias, tc=256):
    n1, nb = tabs["n1"], tabs["nb"]
    P, n, D = br.shape
    B, L, _ = v.shape
    tc = _tile(D, tc)
    m = tabs["m_inv"]
    blk = pl.BlockSpec((None, n, tc), lambda p, c: (p, 0, c))
    pair = pl.BlockSpec((2, L, tc), lambda p, c: (p, 0, c))
    kern = functools.partial(_inv_major_kernel, n1=n1, nb=nb)
    return pl.pallas_call(
        kern,
        grid=(P, D // tc),
        in_specs=[blk, blk, pl.BlockSpec(m.shape, lambda p, c: (0, 0)), pair, pair,
                  pl.BlockSpec((1, tc), lambda p, c: (0, c))],
        out_specs=pair,
        out_shape=jax.ShapeDtypeStruct((B, L, D), BF16),
        compiler_params=_cparams("parallel", "parallel"),
        name="hyena_inv_major",
    )(br, bi, m, v, x0, bias.reshape(1, D))


def hyena_mix(p, B, L, D, conv_w, conv_b, filt, bias):
    assert B % 2 == 0 and L % FFT_MINOR == 0
    tabs = _dft_tables(L)
    v, x0 = conv_gate(p, B, L, D, conv_w, conv_b)
    k_raw, k_norm = hyena_filter(L, D, *filt)
    kar, kai = fwd_major(k_raw[None], tabs, pairs=False)
    kr, ki = filter_spectrum(kar, kai, tabs, 1.0 / (2 * L * k_norm))
    ar, ai = fwd_major(v, tabs, pairs=True)
    br, bi = minor_conv(ar, ai, tabs, kr, ki)
    z = inv_major(br, bi, tabs, v, x0, bias)
    return z.reshape(B * L, D)


def kernel(x, c, ctx, c_ctx, ada_w, ada_b, norm1_g, norm2_g, na_w_qkv, na_w_o, na_q_g, na_k_g, na_rpb, hy_w_in, hy_b_in, hy_conv_w, hy_conv_b, hy_f_w1, hy_f_b1, hy_f_freq1, hy_f_w2, hy_f_b2, hy_f_freq2, hy_f_w3, hy_f_b3, hy_f_freq3, hy_f_wout, hy_bias, hy_w_out, hy_b_out, moe_router, moe_w1, moe_w3, moe_w2):
    B, N, D = x.shape
    CTX = ctx.shape[1]
    depth = ada_w.shape[0]
    mixer = [i % N_MIXERS for i in range(depth)]

    cond = jnp.concatenate([c, c_ctx[None, :], jnp.zeros((8 - B - 1, D), F32)], axis=0)
    mods_all = ada_all(cond, ada_w, ada_b)
    mods_all = mods_all.reshape(depth, 8, 6, 1, D).transpose(0, 2, 1, 3, 4)

    h = x.reshape(B * N, D)
    hc = ctx.reshape(B * CTX, D)
    zeros_d = jnp.zeros((D,), F32)
    w1, w3, w2 = moe_w1.astype(BF16), moe_w3.astype(BF16), moe_w2.astype(BF16)
    for i in range(depth):
        j = i // N_MIXERS
        ctx_stream = any(mixer[l] == 0 for l in range(i + 1, depth))
        ctx_in = ctx_stream or mixer[i] == 0
        mods = mods_all[i]
        if mixer[i] == 0:
            wqkv = na_w_qkv[j].astype(BF16)
            wo = na_w_o[j].astype(BF16)
            hg = jnp.stack([na_q_g[j], na_k_g[j]]).reshape(2, 1, HEAD_DIM)
            zeros_e = jnp.zeros((3 * D,), F32)
            qs = HEAD_DIM ** -0.5
            qkv = nm_matmul(h, norm1_g[i], mods, 0, 1, N, 0, wqkv, zeros_e, hg, qs)
            qkv_c = nm_matmul(hc, norm1_g[i], mods, 0, 1, B * CTX, B, wqkv, zeros_e, hg, qs)
            bias = _na_bias_tables(na_rpb[j], N // GRID_W)
            o, oc = na_attention(qkv, qkv_c, bias, B, N, CTX, D)
            h = matmul_residual(o, wo, zeros_d, mods, 2, N, 0, h)
            if ctx_stream:
                hc = matmul_residual(oc, wo, zeros_d, mods, 2, B * CTX, B, hc)
        else:
            win = hy_w_in[j].astype(BF16)
            wout = hy_w_out[j].astype(BF16)
            filt = (hy_f_w1[j], hy_f_b1[j], hy_f_freq1[j], hy_f_w2[j], hy_f_b2[j], hy_f_freq2[j],
                    hy_f_w3[j], hy_f_b3[j], hy_f_freq3[j], hy_f_wout[j])
            p = nm_matmul(h, norm1_g[i], mods, 0, 1, N, 0, win, hy_b_in[j], out_dtype=F32)
            z = hyena_mix(p, B, N, D, hy_conv_w[j], hy_conv_b[j], filt, hy_bias[j])
            h = matmul_residual(z, wout, hy_b_out[j], mods, 2, N, 0, h)
            if ctx_stream:
                pc = nm_matmul(hc, norm1_g[i], mods, 0, 1, B * CTX, B, win, hy_b_in[j], out_dtype=F32)
                zc = hyena_mix(pc, B, CTX, D, hy_conv_w[j], hy_conv_b[j], filt, hy_bias[j])
                hc = matmul_residual(zc, wout, hy_b_out[j], mods, 2, B * CTX, B, hc)
        h = ec_moe(h, norm2_g[i], mods, 3, 4, 5, N, 0, B, moe_router[i], i, w1, w3, w2)
        if ctx_stream:
            hc = ec_moe(hc, norm2_g[i], mods, 3, 4, 5, B * CTX, B, B, moe_router[i], i, w1, w3, w2)
    return h.reshape(B, N, D)
```

```python
import functools
import math

import jax
import jax.numpy as jnp
import numpy as np
from jax import lax
from jax.experimental import pallas as pl
from jax.experimental.pallas import tpu as pltpu

F32 = jnp.float32
BF16 = jnp.bfloat16

GRID_W = 64
N_MIXERS = 2
NORM_EPS = 1e-6
NEG_INF = -1e30
HEAD_DIM = 128
LANES = 128
WIN_ROWS = 8
WIN_COLS = 16
HYENA_SHORT = 3
HYENA_EMB = 33
HYENA_FAST_DECAY = 0.3
HYENA_SLOW_DECAY = 1.5
HYENA_TARGET = 1e-2
N_EXPERTS = 16
CAPACITY_FACTOR = 2

Q_ROWS = 4
BAND_ROWS = 12

VMEM_LIMIT = 56 * 1024 * 1024


def _cparams(*sem):
    return pltpu.CompilerParams(dimension_semantics=sem, vmem_limit_bytes=VMEM_LIMIT)


def _tile(n, want, unit=128):
    if n <= want:
        return n
    t = (want // unit) * unit
    while n % t:
        t -= unit
    return t


def _ada_kernel(c_ref, w_ref, b_ref, o_ref):
    c = c_ref[...]
    s = c * jax.nn.sigmoid(c)
    o_ref[...] = jnp.dot(s.astype(BF16), w_ref[...].astype(BF16),
                         preferred_element_type=F32) + b_ref[...]


def ada_all(cond, ada_w, ada_b, tn=1024):
    L, D, E = ada_w.shape
    tn = _tile(E, tn)
    return pl.pallas_call(
        _ada_kernel,
        grid=(L, E // tn),
        in_specs=[
            pl.BlockSpec((8, D), lambda l, j: (0, 0)),
            pl.BlockSpec((None, D, tn), lambda l, j: (l, 0, j)),
            pl.BlockSpec((None, 1, tn), lambda l, j: (l, 0, j)),
        ],
        out_specs=pl.BlockSpec((None, 8, tn), lambda l, j: (l, 0, j)),
        out_shape=jax.ShapeDtypeStruct((L, 8, E), F32),
        compiler_params=_cparams("parallel", "parallel"),
        name="ada",
    )(cond, ada_w, ada_b.reshape(L, 1, E))


def _norm_mod(x, g, sh, sc):
    y = x * lax.rsqrt(jnp.mean(x * x, axis=-1, keepdims=True) + NORM_EPS) * g
    return y * (1.0 + sc) + sh


def _nm_matmul_kernel(x_ref, g_ref, sh_ref, sc_ref, w_ref, b_ref, hg_ref, o_ref, a_ref, *,
                      n_norm_sections, section_cols, q_scale):
    j = pl.program_id(1)

    @pl.when(j == 0)
    def _():
        a_ref[...] = _norm_mod(x_ref[...], g_ref[...], sh_ref[...], sc_ref[...]).astype(BF16)

    acc = jnp.dot(a_ref[...], w_ref[...], preferred_element_type=F32) + b_ref[...]
    tn = acc.shape[1]
    if n_norm_sections == 0:
        o_ref[...] = acc.astype(o_ref.dtype)
        return
    sec = (j * tn) // section_cols

    @pl.when(sec < n_norm_sections)
    def _():
        hg = hg_ref[...]
        mult = jnp.where(sec == 0, q_scale, 1.0).astype(F32)
        for h in range(tn // HEAD_DIM):
            c = acc[:, h * HEAD_DIM:(h + 1) * HEAD_DIM]
            c = c * lax.rsqrt(jnp.mean(c * c, axis=-1, keepdims=True) + NORM_EPS) * hg
            o_ref[:, h * HEAD_DIM:(h + 1) * HEAD_DIM] = (c * mult).astype(o_ref.dtype)

    @pl.when(sec >= n_norm_sections)
    def _():
        o_ref[...] = acc.astype(o_ref.dtype)


def nm_matmul(x, g, mods, k_shift, k_scale, group_rows, group_base, w, b, head_g=None,
              q_scale=1.0, out_dtype=BF16, tm=1024, tn=1024):
    R, D = x.shape
    E = w.shape[1]
    tm = _tile(group_rows, tm, 8)
    tn = _tile(D, tn)
    n_norm = 0 if head_g is None else 2
    if head_g is None:
        head_g = jnp.ones((2, 1, HEAD_DIM), F32)

    def grp(i):
        return group_base + (i * tm) // group_rows

    def hg_map(i, j):
        return (jnp.minimum((j * tn) // D, 1), 0, 0)

    kern = functools.partial(_nm_matmul_kernel, n_norm_sections=n_norm, section_cols=D,
                             q_scale=q_scale)
    return pl.pallas_call(
        kern,
        grid=(R // tm, E // tn),
        in_specs=[
            pl.BlockSpec((tm, D), lambda i, j: (i, 0)),
            pl.BlockSpec((1, D), lambda i, j: (0, 0)),
            pl.BlockSpec((None, None, 1, D), lambda i, j: (k_shift, grp(i), 0, 0)),
            pl.BlockSpec((None, None, 1, D), lambda i, j: (k_scale, grp(i), 0, 0)),
            pl.BlockSpec((D, tn), lambda i, j: (0, j)),
            pl.BlockSpec((1, tn), lambda i, j: (0, j)),
            pl.BlockSpec((None, 1, HEAD_DIM), hg_map),
        ],
        out_specs=pl.BlockSpec((tm, tn), lambda i, j: (i, j)),
        out_shape=jax.ShapeDtypeStruct((R, E), out_dtype),
        scratch_shapes=[pltpu.VMEM((tm, D), BF16)],
        compiler_params=_cparams("parallel", "arbitrary"),
        name="nm_matmul",
    )(x, g.reshape(1, D), mods, mods, w, b.reshape(1, E), head_g)


def _mm_res_kernel(x_ref, w_ref, b_ref, gate_ref, res_ref, o_ref):
    y = jnp.dot(x_ref[...], w_ref[...], preferred_element_type=F32) + b_ref[...]
    o_ref[...] = res_ref[...] + gate_ref[...] * y


def matmul_residual(x, w, b, mods, k_gate, group_rows, group_base, res, tm=1024, tn=1024):
    R, K = x.shape
    E = w.shape[1]
    tm = _tile(group_rows, tm, 8)
    tn = _tile(E, tn)

    def grp(i):
        return group_base + (i * tm) // group_rows

    return pl.pallas_call(
        _mm_res_kernel,
        grid=(R // tm, E // tn),
        in_specs=[
            pl.BlockSpec((tm, K), lambda i, j: (i, 0)),
            pl.BlockSpec((K, tn), lambda i, j: (0, j)),
            pl.BlockSpec((1, tn), lambda i, j: (0, j)),
            pl.BlockSpec((None, None, 1, tn), lambda i, j: (k_gate, grp(i), 0, j)),
            pl.BlockSpec((tm, tn), lambda i, j: (i, j)),
        ],
        out_specs=pl.BlockSpec((tm, tn), lambda i, j: (i, j)),
        out_shape=jax.ShapeDtypeStruct((R, E), F32),
        compiler_params=_cparams("parallel", "parallel"),
        name="matmul_residual",
    )(x, w, b.reshape(1, E), mods, res)


def _na_bias_tables(rpb, rows):
    W = GRID_W
    cols = np.arange(W)
    cs = np.clip(cols - WIN_COLS // 2, 0, W - WIN_COLS)
    in_win = (cols[None, :] >= cs[:, None]) & (cols[None, :] < cs[:, None] + WIN_COLS)
    dc = np.clip(cols[None, :] - cols[:, None] + WIN_COLS - 1, 0, 2 * WIN_COLS - 2)
    n_dr, n_dc = 2 * WIN_ROWS - 1, 2 * WIN_COLS - 1
    onehot_r = np.zeros((3, Q_ROWS, BAND_ROWS, n_dr), np.float32)
    mask = np.zeros((3, Q_ROWS, W, BAND_ROWS, W), bool)
    for var, r0 in enumerate((0, Q_ROWS, rows - Q_ROWS)):
        bs = _band_start(r0, rows)
        for i in range(Q_ROWS):
            r = r0 + i
            rs = int(np.clip(r - WIN_ROWS // 2, 0, rows - WIN_ROWS))
            for jj in range(BAND_ROWS):
                kr = bs + jj
                if rs <= kr < rs + WIN_ROWS:
                    onehot_r[var, i, jj, kr - r + WIN_ROWS - 1] = 1.0
                    mask[var, i, :, jj, :] = in_win
    onehot_c = (dc[:, :, None] == np.arange(n_dc)).astype(np.float32)
    hp = lax.Precision.HIGHEST
    t = jnp.einsum('hrc,vijr->hvijc', rpb.astype(F32), onehot_r, precision=hp)
    t = jnp.einsum('hvijc,qkc->hviqjk', t, onehot_c, precision=hp)
    t = jnp.where(mask[None], t, NEG_INF)
    return t.reshape(rpb.shape[0], 3, Q_ROWS * W, BAND_ROWS * W)


def _band_start(r0, rows):
    return int(np.clip(r0 - WIN_ROWS // 2, 0, rows - BAND_ROWS))


def _na_kernel(q_ref, k_ref, v_ref, qc_ref, kc_ref, vc_ref, bias_ref, o_ref, oc_ref, *, rows):
    W = GRID_W
    nblk = rows // Q_ROWS
    kc = kc_ref[...]
    vc = vc_ref[...]
    nt = (((1,), (1,)), ((), ()))

    def attend(q, parts):
        ss = [lax.dot_general(q, k, nt, preferred_element_type=F32) + (0.0 if bias is None else bias)
              for k, _, bias in parts]
        m = functools.reduce(jnp.maximum, [jnp.max(s, axis=-1, keepdims=True) for s in ss])
        ps = [jnp.exp(s - m) for s in ss]
        den = sum(jnp.sum(p, axis=-1, keepdims=True) for p in ps)
        o = sum(jnp.dot(p.astype(BF16), v, preferred_element_type=F32)
                for p, (_, v, _) in zip(ps, parts))
        return o / den

    def body(blk, carry):
        r0 = blk * Q_ROWS
        bs = jnp.clip(r0 - WIN_ROWS // 2, 0, rows - BAND_ROWS)
        var = jnp.where(blk == 0, 0, jnp.where(blk == nblk - 1, 2, 1))
        q0 = pl.multiple_of(r0 * W, Q_ROWS * W)
        k0 = pl.multiple_of(bs * W, W)
        q = q_ref[pl.ds(q0, Q_ROWS * W), :]
        kb = k_ref[pl.ds(k0, BAND_ROWS * W), :]
        vb = v_ref[pl.ds(k0, BAND_ROWS * W), :]
        o = attend(q, [(kb, vb, bias_ref[var]), (kc, vc, None)])
        o_ref[pl.ds(q0, Q_ROWS * W), :] = o.astype(o_ref.dtype)
        return carry

    lax.fori_loop(0, nblk, body, 0, unroll=2)
    oc_ref[...] = attend(qc_ref[...], [(kc, vc, None)]).astype(oc_ref.dtype)


def na_attention(qkv, qkv_c, bias, B, N, CTX, D):
    H = D // HEAD_DIM
    rows = N // GRID_W
    kern = functools.partial(_na_kernel, rows=rows)

    def spec(n, sec):
        return pl.BlockSpec((n, HEAD_DIM), lambda h, b: (b, sec * H + h))

    return pl.pallas_call(
        kern,
        grid=(H, B),
        in_specs=[spec(N, 0), spec(N, 1), spec(N, 2), spec(CTX, 0), spec(CTX, 1), spec(CTX, 2),
                  pl.BlockSpec((None, 3, Q_ROWS * GRID_W, BAND_ROWS * GRID_W),
                               lambda h, b: (h, 0, 0, 0))],
        out_specs=[pl.BlockSpec((N, HEAD_DIM), lambda h, b: (b, h)),
                   pl.BlockSpec((CTX, HEAD_DIM), lambda h, b: (b, h))],
        out_shape=[jax.ShapeDtypeStruct((B * N, D), BF16),
                   jax.ShapeDtypeStruct((B * CTX, D), BF16)],
        compiler_params=_cparams("parallel", "parallel"),
        name="na_attention",
    )(qkv, qkv, qkv, qkv_c, qkv_c, qkv_c, bias)


def _expert_kernel(idx_ref, a_hbm, w1_ref, w3_ref, w2_ref, gate_ref, o_ref, xbuf, sem, *, n_tok, cap, bg):
    e, g = pl.program_id(0), pl.program_id(1)
    n_exp, ng = pl.num_programs(0), pl.num_programs(1)
    step = e * ng + g
    slot = step % 2
    nsteps = n_exp * ng
    rows = bg * cap
    S = xbuf.shape[1] // rows

    def copy_row(ee, gg, sl, r, priority=0):
        b = gg * bg + r // cap
        row = b * n_tok + idx_ref[(b * n_exp + ee) * cap + r % cap]
        src = pl.ds(pl.multiple_of(row * S, S), S)
        dst = pl.ds(pl.multiple_of(r * S, S), S)
        pltpu.make_async_copy(a_hbm.at[src, :], xbuf.at[sl, dst, :], sem.at[sl]).start(priority=priority)

    def wait_rows(sl):
        pltpu.make_async_copy(a_hbm.at[pl.ds(0, rows * S), :], xbuf.at[sl], sem.at[sl]).wait()

    @pl.when(step == 0)
    def _():
        def body(r, carry):
            copy_row(e, g, slot, r)
            return carry

        lax.fori_loop(0, rows, body, 0, unroll=8)

    wait_rows(slot)
    hi, lo = _load_token_tiles(xbuf, (slot,), 0, rows, S)
    x = jnp.concatenate([p.astype(BF16) for p in hi + lo], axis=1)
    nxt = (step + 1) % nsteps
    for r in range(rows):
        copy_row(nxt // ng, nxt % ng, 1 - slot, r, priority=r % 2)
    h1 = jnp.dot(x, w1_ref[...], preferred_element_type=F32)
    h3 = jnp.dot(x, w3_ref[...], preferred_element_type=F32)
    hid = (h1 * jax.nn.sigmoid(h1) * h3).astype(BF16)
    y = jnp.dot(hid, w2_ref[...], preferred_element_type=F32) * gate_ref[...].reshape(rows, 1)
    packed = _pack_bf16_pair(y)
    for i in range(bg):
        _store_token_tiles(o_ref, (i,), packed[i * cap:(i + 1) * cap])

    @pl.when(step == nsteps - 1)
    def _():
        wait_rows(1 - slot)


def _pack_bf16_pair(y):
    half = y.shape[1] // 2
    hi = pltpu.bitcast(y[:, :half].astype(BF16).astype(F32), jnp.uint32)
    lo = pltpu.bitcast(y[:, half:].astype(BF16).astype(F32), jnp.uint32)
    return hi | (lo >> 16)


def _unpack_bf16_pair(u):
    hi = pltpu.bitcast(u & jnp.uint32(0xFFFF0000), F32)
    lo = pltpu.bitcast(u << 16, F32)
    return hi, lo


def _store_token_tiles(ref, lead, packed):
    rows, S = packed.shape[0], packed.shape[1] // LANES
    for s in range(S):
        ref[(*lead, pl.ds(s, rows, stride=S), slice(None))] = packed[:, s * LANES:(s + 1) * LANES]


def _load_token_tiles(ref, lead, first, rows, S):
    pieces = [_unpack_bf16_pair(ref[(*lead, pl.ds(first * S + s, rows, stride=S), slice(None))])
              for s in range(S)]
    return [p[0] for p in pieces], [p[1] for p in pieces]


def expert_ffn(a, idx_flat, layer, w1, w3, w2, gate, B, n_tok):
    _, E, D, Fh = w1.shape
    C = gate.shape[2]
    S = D // (2 * LANES)
    bg = max(1, min(B, EXPERT_ROWS // C))
    while B % bg:
        bg -= 1
    kern = functools.partial(_expert_kernel, n_tok=n_tok, cap=C, bg=bg)
    grid_spec = pltpu.PrefetchScalarGridSpec(
        num_scalar_prefetch=1,
        grid=(E, B // bg),
        in_specs=[
            pl.BlockSpec(memory_space=pl.ANY),
            pl.BlockSpec((None, None, D, Fh), lambda e, g, idx: (layer, e, 0, 0)),
            pl.BlockSpec((None, None, D, Fh), lambda e, g, idx: (layer, e, 0, 0)),
            pl.BlockSpec((None, None, Fh, D), lambda e, g, idx: (layer, e, 0, 0)),
            pl.BlockSpec((bg, None, C, 1), lambda e, g, idx: (g, e, 0, 0)),
        ],
        out_specs=pl.BlockSpec((bg, None, C * S, LANES), lambda e, g, idx: (g, e, 0, 0)),
        scratch_shapes=[pltpu.VMEM((2, bg * C * S, LANES), jnp.uint32), pltpu.SemaphoreType.DMA((2,))],
    )
    return pl.pallas_call(
        kern,
        grid_spec=grid_spec,
        out_shape=jax.ShapeDtypeStruct((B, E, C * S, LANES), jnp.uint32),
        compiler_params=_cparams("arbitrary", "arbitrary"),
        name="expert_ffn",
    )(idx_flat, a, w1, w3, w2, gate)


def _moe_prep_kernel(x_ref, g_ref, sh_ref, sc_ref, r_ref, a_ref, aff_ref):
    a = _norm_mod(x_ref[...], g_ref[...], sh_ref[...], sc_ref[...])
    _store_token_tiles(a_ref, (), _pack_bf16_pair(a))
    logits = lax.dot_general(r_ref[...], a.astype(BF16), (((1,), (1,)), ((), ())),
                             preferred_element_type=F32)
    m = jnp.max(logits, axis=0, keepdims=True)
    p = jnp.exp(logits - m)
    aff_ref[...] = p / jnp.sum(p, axis=0, keepdims=True)


def moe_prep(x, g, mods, k_shift, k_scale, group_rows, group_base, router_t, tm=512):
    R, D = x.shape
    E = router_t.shape[0]
    S = D // (2 * LANES)
    tm = _tile(group_rows, tm, 128)

    def grp(i):
        return group_base + (i * tm) // group_rows

    return pl.pallas_call(
        _moe_prep_kernel,
        grid=(R // tm,),
        in_specs=[
            pl.BlockSpec((tm, D), lambda i: (i, 0)),
            pl.BlockSpec((1, D), lambda i: (0, 0)),
            pl.BlockSpec((None, None, 1, D), lambda i: (k_shift, grp(i), 0, 0)),
            pl.BlockSpec((None, None, 1, D), lambda i: (k_scale, grp(i), 0, 0)),
            pl.BlockSpec((E, D), lambda i: (0, 0)),
        ],
        out_specs=[pl.BlockSpec((tm * S, LANES), lambda i: (i, 0)),
                   pl.BlockSpec((E, tm), lambda i: (0, i))],
        out_shape=[jax.ShapeDtypeStruct((R * S, LANES), jnp.uint32), jax.ShapeDtypeStruct((E, R), F32)],
        compiler_params=_cparams("parallel"),
        name="moe_prep",
    )(x, g.reshape(1, D), mods, mods, router_t)


EXPERT_ROWS = 512
COMBINE_TOKENS = 128
HIT_UNROLL = 4
CUMSUM_CHUNK = 256
ROUTE_BLOCK_LANES = 512
TILE_TABLE_LANES = 128


def _route_select_kernel(aff_ref, tri_ref, ltri_ref, lt_ref, in_ref,
                         posi_ref, key_ref, slab_ref, p0_ref, kmax_ref, *, cap):
    E, N = aff_ref.shape
    aff = aff_ref[...]
    bits = pltpu.bitcast(aff, jnp.int32)
    cur = jnp.zeros((E, 1), jnp.int32)
    for bit in range(30, -1, -1):
        cand = cur | (1 << bit)
        cnt = jnp.sum(jnp.where(bits >= cand, 1.0, 0.0), axis=1, keepdims=True)
        cur = jnp.where(cnt >= cap, cand, cur)
    gt = bits > cur
    eq = bits == cur
    need = cap - jnp.sum(jnp.where(gt, 1.0, 0.0), axis=1, keepdims=True)

    def cumsum_incl(x):
        tri = tri_ref[...]
        w = min(CUMSUM_CHUNK, N)
        off = jnp.zeros((E, 1), F32)
        outs = []
        for c in range(N // w):
            xc = x[:, c * w:(c + 1) * w]
            outs.append(jnp.dot(xc.astype(BF16), tri[:w, :w], preferred_element_type=F32) + off)
            off = off + jnp.sum(xc, axis=1, keepdims=True)
        return jnp.concatenate(outs, axis=1) if len(outs) > 1 else outs[0]

    eqf = jnp.where(eq, 1.0, 0.0)
    tie_rank = cumsum_incl(eqf) - eqf
    sel = gt | (eq & (tie_rank < need))
    self_ = jnp.where(sel, 1.0, 0.0)
    pos_incl = cumsum_incl(self_)
    posi_ref[...] = pos_incl
    key_ref[...] = jnp.where(sel, pos_incl - self_, -1.0)
    selb = self_.astype(BF16)
    erank = jnp.dot(ltri_ref[...], selb, preferred_element_type=F32)
    in_tile = lax.broadcasted_iota(jnp.int32, (E, N), 1) & (COMBINE_TOKENS - 1)
    slab_ref[...] = erank * COMBINE_TOKENS + in_tile.astype(F32)
    p0_ref[...] = jnp.dot(selb, lt_ref[...], preferred_element_type=F32).astype(jnp.int32)
    count = jnp.sum(self_, axis=0, keepdims=True)
    kk = lax.broadcasted_iota(jnp.int32, (E, 1), 0).astype(F32)
    over = jnp.where(count > kk, 1.0, 0.0).astype(BF16)
    per_tile = jnp.dot(over, in_ref[...], preferred_element_type=F32)
    kmax_ref[...] = jnp.sum(jnp.where(per_tile > 0.0, 1.0, 0.0), axis=0, keepdims=True).astype(jnp.int32)


def _route_compact_kernel(p0_ref, aff_ref, posi_ref, key_ref, slab_ref, idx_ref, gate_ref, rank_ref, *,
                          cap, tn, lanes, width):
    b = pl.program_id(0)
    E, N = aff_ref.shape
    ch = min(64, cap)
    tpb = width // tn

    def e_body(e, carry):
        q = (b * E + e) * lanes

        def c_body(cc, tiles):
            c0 = pl.multiple_of(cc * ch, ch)
            t_lo = lax.while_loop(lambda t: p0_ref[q + t + 1] <= c0, lambda t: t + 1, tiles[0])
            t_hi = lax.while_loop(lambda t: p0_ref[q + t] < c0 + ch, lambda t: t + 1, tiles[1])
            j_lo = t_lo // tpb
            j_hi = (t_hi + tpb - 1) // tpb
            ccol = (c0 + lax.broadcasted_iota(jnp.int32, (ch, 1), 0)).astype(F32)

            def j_body(j, acc):
                row = pl.ds(e, 1)
                blk = pl.ds(pl.multiple_of(j * width, width), width)
                pi, ky, af, sl = posi_ref[row, blk], key_ref[row, blk], aff_ref[row, blk], slab_ref[row, blk]
                acc = list(acc)
                for u in range(width // LANES):
                    part = slice(u * LANES, (u + 1) * LANES)
                    hit = ky[:, part] == ccol
                    acc[0] = acc[0] + jnp.where(pi[:, part] <= ccol, 1.0, 0.0)
                    acc[1] = acc[1] + jnp.where(hit, af[:, part], 0.0)
                    acc[2] = acc[2] + jnp.where(hit, sl[:, part], 0.0)
                return tuple(acc)

            zero = jnp.zeros((ch, LANES), F32)
            iv, gv, rv = [jnp.sum(a, axis=1, keepdims=True)
                          for a in lax.fori_loop(j_lo, j_hi, j_body, (zero, zero, zero))]
            idx_ref[e, pl.ds(c0, ch), :] = iv.astype(jnp.int32) + j_lo * width
            gate_ref[e, pl.ds(c0, ch), :] = gv
            rank_ref[e, pl.ds(c0, ch), :] = rv.astype(jnp.int32)
            return (t_lo, t_hi)

        lax.fori_loop(0, cap // ch, c_body, (0, 0))
        return carry

    lax.fori_loop(0, E, e_body, 0)


def moe_route(aff_t, B, N, cap):
    E = aff_t.shape[0]
    tn = COMBINE_TOKENS
    lanes = TILE_TABLE_LANES
    assert N // tn + 1 <= lanes and N % tn == 0 and N % min(CUMSUM_CHUNK, N) == 0
    w = min(CUMSUM_CHUNK, N)
    tri = (np.arange(w)[:, None] <= np.arange(w)[None, :]).astype(np.float32)
    ltri = (np.arange(E)[None, :] < np.arange(E)[:, None]).astype(np.float32)
    tok = np.arange(N)[:, None]
    tile = np.arange(lanes)[None, :]
    before = (tok < tile * tn).astype(np.float32)
    inside = (tok // tn == tile).astype(np.float32)
    const = lambda shape: pl.BlockSpec(shape, lambda b: (0,) * len(shape))
    per_tok = pl.BlockSpec((None, E, N), lambda b, *_: (b, 0, 0))
    posi, key, slab, p0, kmax = pl.pallas_call(
        functools.partial(_route_select_kernel, cap=cap),
        grid=(B,),
        in_specs=[pl.BlockSpec((E, N), lambda b: (0, b)), const((w, w)), const((E, E)),
                  const((N, lanes)), const((N, lanes))],
        out_specs=[per_tok, per_tok, per_tok,
                   pl.BlockSpec((None, E, lanes), lambda b: (b, 0, 0)),
                   pl.BlockSpec((None, 1, lanes), lambda b: (b, 0, 0))],
        out_shape=[jax.ShapeDtypeStruct((B, E, N), F32)] * 3 + [
            jax.ShapeDtypeStruct((B, E, lanes), jnp.int32), jax.ShapeDtypeStruct((B, 1, lanes), jnp.int32)],
        compiler_params=_cparams("parallel"),
        name="moe_route_select",
    )(aff_t, jnp.asarray(tri, BF16), jnp.asarray(ltri, BF16), jnp.asarray(before, BF16),
      jnp.asarray(inside, BF16))
    width = min(ROUTE_BLOCK_LANES, N)
    sel3 = pl.BlockSpec((None, E, cap, 1), lambda b, *_: (b, 0, 0, 0))
    idx, gate, rank = pl.pallas_call(
        functools.partial(_route_compact_kernel, cap=cap, tn=tn, lanes=lanes, width=width),
        grid_spec=pltpu.PrefetchScalarGridSpec(
            num_scalar_prefetch=1,
            grid=(B,),
            in_specs=[pl.BlockSpec((E, N), lambda b, *_: (0, b)), per_tok, per_tok, per_tok],
            out_specs=[sel3, sel3, sel3],
        ),
        out_shape=[jax.ShapeDtypeStruct((B, E, cap, 1), jnp.int32),
                   jax.ShapeDtypeStruct((B, E, cap, 1), F32),
                   jax.ShapeDtypeStruct((B, E, cap, 1), jnp.int32)],
        compiler_params=_cparams("parallel"),
        name="moe_route_compact",
    )(p0.reshape(-1), aff_t, posi, key, slab)
    return idx, gate, rank, p0, kmax


def _combine_kernel(rank_ref, p0_ref, kmax_ref, ys_hbm, h_ref, g2_ref, o_ref, rbuf, sem, *,
                    n_exp, cap, tn, lanes):
    b, t = pl.program_id(0), pl.program_id(1)
    nt = pl.num_programs(1)
    step = b * nt + t
    slot = step % 2
    S = rbuf.shape[1] // (n_exp * tn)

    def tile_hits(bb, tt):
        def body(e, tot):
            q = (bb * n_exp + e) * lanes + tt
            return tot + p0_ref[q + 1] - p0_ref[q]
        return lax.fori_loop(0, n_exp, body, 0)

    def fetch(bb, tt, sl):
        km = kmax_ref[bb * lanes + tt]
        for k in range(n_exp):
            @pl.when(k < km)
            def _():
                rbuf[sl, k * tn * S:(k + 1) * tn * S, :] = jnp.zeros((tn * S, LANES), jnp.uint32)

        def copy_row(row, priority=0):
            src = pl.ds(pl.multiple_of(row * S, S), S)
            dst = pl.ds(pl.multiple_of(rank_ref[row] * S, S), S)
            pltpu.make_async_copy(ys_hbm.at[src, :], rbuf.at[sl, dst, :], sem.at[sl]).start(priority=priority)

        def e_body(e, carry):
            q = (bb * n_exp + e) * lanes + tt
            lo = (bb * n_exp + e) * cap + p0_ref[q]
            n = p0_ref[q + 1] - p0_ref[q]
            groups = n // HIT_UNROLL

            def g_body(g, carry2):
                for u in range(HIT_UNROLL):
                    copy_row(lo + g * HIT_UNROLL + u, priority=u % 2)
                return carry2

            def r_body(c, carry2):
                copy_row(lo + c)
                return carry2

            lax.fori_loop(0, groups, g_body, 0)
            lax.fori_loop(groups * HIT_UNROLL, n, r_body, 0)
            return carry

        lax.fori_loop(0, n_exp, e_body, 0)

    @pl.when(step == 0)
    def _():
        fetch(b, t, slot)

    @pl.when(step + 1 < pl.num_programs(0) * nt)
    def _():
        nxt = step + 1
        fetch(nxt // nt, nxt % nt, 1 - slot)

    hits = tile_hits(b, t)

    nrows = hits * S
    bulk = pl.multiple_of((nrows // 8) * 8, 8)

    @pl.when(bulk > 0)
    def _():
        pltpu.make_async_copy(ys_hbm.at[pl.ds(0, bulk), :], rbuf.at[slot, pl.ds(0, bulk), :],
                              sem.at[slot]).wait()

    def wait_row(i, carry):
        pltpu.make_async_copy(ys_hbm.at[pl.ds(0, 1), :], rbuf.at[slot, pl.ds(0, 1), :], sem.at[slot]).wait()
        return carry

    lax.fori_loop(0, nrows - bulk, wait_row, 0)

    km = kmax_ref[b * lanes + t]
    g2 = g2_ref[...]
    for r in range(tn // 8):
        def k_body(k, tot):
            hi, lo = _load_token_tiles(rbuf, (slot,), k * tn + r * 8, 8, S)
            return tuple(t_ + p for t_, p in zip(tot, hi + lo))

        tot = lax.fori_loop(0, km, k_body, (jnp.zeros((8, LANES), F32),) * (2 * S))
        rows = slice(r * 8, (r + 1) * 8)
        for s in range(2 * S):
            cols = slice(s * LANES, (s + 1) * LANES)
            o_ref[rows, cols] = h_ref[rows, cols] + g2[:, cols] * tot[s]


def moe_combine(ys, h, mods, k_gate, group_rows, group_base, rank, p0, kmax, B, N):
    R, D = h.shape
    E, cap = rank.shape[1], rank.shape[2]
    tn = COMBINE_TOKENS
    lanes = TILE_TABLE_LANES
    nt = N // tn

    def grp(b, t):
        return group_base + ((b * nt + t) * tn) // group_rows

    kern = functools.partial(_combine_kernel, n_exp=E, cap=cap, tn=tn, lanes=lanes)
    grid_spec = pltpu.PrefetchScalarGridSpec(
        num_scalar_prefetch=3,
        grid=(B, nt),
        in_specs=[
            pl.BlockSpec(memory_space=pl.ANY),
            pl.BlockSpec((tn, D), lambda b, t, *_: (b * nt + t, 0)),
            pl.BlockSpec((None, None, 1, D), lambda b, t, *_: (k_gate, grp(b, t), 0, 0)),
        ],
        out_specs=pl.BlockSpec((tn, D), lambda b, t, *_: (b * nt + t, 0)),
        scratch_shapes=[pltpu.VMEM((2, E * tn * (D // (2 * LANES)), LANES), jnp.uint32),
                        pltpu.SemaphoreType.DMA((2,))],
    )
    return pl.pallas_call(
        kern,
        grid_spec=grid_spec,
        out_shape=jax.ShapeDtypeStruct((R, D), F32),
        compiler_params=_cparams("arbitrary", "arbitrary"),
        name="moe_combine",
    )(rank.reshape(-1), p0.reshape(-1), kmax.reshape(-1), ys, h, mods)


def ec_moe(h, g, mods, k_shift, k_scale, k_gate, group_rows, group_base, B, router, layer, w1, w3, w2):
    R, D = h.shape
    N = R // B
    E = router.shape[1]
    cap = CAPACITY_FACTOR * N // E
    a, aff_t = moe_prep(h, g, mods, k_shift, k_scale, group_rows, group_base, router.T.astype(BF16))
    idx, gate, rank, p0, kmax = moe_route(aff_t, B, N, cap)
    ys = expert_ffn(a, idx.reshape(-1), layer, w1, w3, w2, gate, B, N)
    return moe_combine(ys.reshape(-1, LANES), h, mods, k_gate, group_rows, group_base,
                       rank, p0, kmax, B, N)


FFT_MINOR = 256
FFT_GROUP = 16


def _dft_tables(L):
    n = 2 * L
    n1 = n // FFT_MINOR
    nb = n1 // 2
    two_pi = 2.0 * math.pi
    a = jnp.arange(n1, dtype=jnp.int32)
    ang = two_pi * ((a[:, None] * a[None, :]) % n1).astype(F32) / n1
    cos1, sin1 = jnp.cos(ang), jnp.sin(ang)
    eye = jnp.eye(FFT_GROUP, dtype=F32)

    def kron(m):
        return jnp.kron(m, eye)

    f1 = jnp.arange(n1, dtype=jnp.int32)[:, None, None]
    f2 = jnp.arange(FFT_MINOR, dtype=jnp.int32)[None, :, None]
    n2 = jnp.arange(FFT_MINOR, dtype=jnp.int32)[None, None, :]
    ang = two_pi * ((n2 * (f1 + n1 * f2)) % n).astype(F32) / n
    gr, gi = jnp.cos(ang), -jnp.sin(ang)
    return dict(
        n1=n1, nb=nb,
        m_fwd_full=jnp.concatenate([kron(cos1), kron(-sin1)], axis=0).astype(BF16),
        m_fwd_half=jnp.concatenate([kron(cos1[:, :nb]), kron(-sin1[:, :nb])], axis=0).astype(BF16),
        m_inv=jnp.concatenate([kron(cos1[:nb]), kron(sin1[:nb])], axis=0).astype(BF16),
        g=jnp.concatenate([gr, gi], axis=1).astype(BF16),
        gt=jnp.concatenate([gr.transpose(0, 2, 1), gi.transpose(0, 2, 1)], axis=1).astype(BF16),
    )


HALO = 8


def _hyena_in_kernel(x_ref, xp_ref, xn_ref, g_ref, sh_ref, sc_ref, w0_ref, w1_ref, w2_ref, b_ref, cw_ref,
                     cb_ref, v_ref, x0_ref, a_ref, ah_ref, *, seq_len):
    i, j = pl.program_id(0), pl.program_id(1)
    tm = x_ref.shape[0]

    @pl.when(j == 0)
    def _():
        g, sh, sc = g_ref[...], sh_ref[...], sc_ref[...]
        a_ref[...] = _norm_mod(x_ref[...], g, sh, sc).astype(BF16)
        ah_ref[0:HALO, :] = _norm_mod(xp_ref[...], g, sh, sc)
        ah_ref[HALO:, :] = _norm_mod(xn_ref[...], g, sh, sc)

    first = (i * tm) % seq_len == 0
    last = ((i + 1) * tm) % seq_len == 0
    a = a_ref[...]
    ah = ah_ref[...].astype(BF16)
    row = lax.broadcasted_iota(jnp.int32, (tm, v_ref.shape[1]), 0)

    def section(k, w_ref):
        w = w_ref[...]
        p = jnp.dot(a, w, preferred_element_type=F32) + b_ref[k]
        ph = jnp.dot(ah, w, preferred_element_type=F32) + b_ref[k]
        above = jnp.where(first, 0.0, ph[HALO - 1:HALO])
        below = jnp.where(last, 0.0, ph[HALO:HALO + 1])
        cw = cw_ref[k]
        prev = jnp.where(row == 0, above, pltpu.roll(p, 1, 0))
        nxt = jnp.where(row == tm - 1, below, pltpu.roll(p, tm - 1, 0))
        return prev * cw[0:1] + p * cw[1:2] + nxt * cw[2:3] + cb_ref[k]

    x0 = section(0, w0_ref)
    x1 = section(1, w1_ref)
    v = section(2, w2_ref)
    v_ref[...] = (v * x1).astype(v_ref.dtype)
    x0_ref[...] = x0.astype(x0_ref.dtype)


def hyena_in_proj(x, g, mods, k_shift, k_scale, group_rows, group_base, seq_len, w, b, conv_w, conv_b,
                  tm=1024, tc=512):
    R, D = x.shape
    tm = _tile(math.gcd(group_rows, seq_len), tm, HALO)
    tc = _tile(D, tc)
    nc = D // tc
    cw = conv_w.reshape(HYENA_SHORT, 3, D).transpose(1, 0, 2)

    def grp(i):
        return group_base + (i * tm) // group_rows

    def wspec(k):
        return pl.BlockSpec((D, tc), lambda i, j: (0, k * nc + j))

    mod = lambda k: pl.BlockSpec((None, None, 1, D), lambda i, j: (k, grp(i), 0, 0))
    per_proj = lambda rows: pl.BlockSpec((3, rows, tc), lambda i, j: (0, 0, j))
    kern = functools.partial(_hyena_in_kernel, seq_len=seq_len)
    return pl.pallas_call(
        kern,
        grid=(R // tm, nc),
        in_specs=[pl.BlockSpec((tm, D), lambda i, j: (i, 0)),
                  pl.BlockSpec((HALO, D), lambda i, j: (jnp.maximum(i * (tm // HALO) - 1, 0), 0)),
                  pl.BlockSpec((HALO, D), lambda i, j: (jnp.minimum((i + 1) * (tm // HALO), R // HALO - 1), 0)),
                  pl.BlockSpec((1, D), lambda i, j: (0, 0)), mod(k_shift), mod(k_scale),
                  wspec(0), wspec(1), wspec(2), per_proj(1), per_proj(HYENA_SHORT), per_proj(1)],
        out_specs=[pl.BlockSpec((tm, tc), lambda i, j: (i, j))] * 2,
        out_shape=[jax.ShapeDtypeStruct((R, D), BF16)] * 2,
        scratch_shapes=[pltpu.VMEM((tm, D), BF16), pltpu.VMEM((2 * HALO, D), F32)],
        compiler_params=_cparams("parallel", "arbitrary"),
        name="hyena_in_proj",
    )(x, x, x, g.reshape(1, D), mods, mods, w, w, w, b.reshape(3, 1, D), cw, conv_b.reshape(3, 1, D))


def _filter_kernel(emb_ref, w1_ref, b1_ref, f1_ref, w2_ref, b2_ref, f2_ref, w3_ref, b3_ref, f3_ref,
                   wo_ref, dl_ref, k_ref, norm_ref, *, L):
    i = pl.program_id(0)
    tr = emb_ref.shape[0]
    hp = lax.Precision.HIGHEST
    emb = emb_ref[...]
    h = jnp.sin(f1_ref[...] * (jnp.dot(emb, w1_ref[...], precision=hp, preferred_element_type=F32) + b1_ref[...]))
    h = jnp.sin(f2_ref[...] * (jnp.dot(h, w2_ref[...], precision=hp, preferred_element_type=F32) + b2_ref[...]))
    h = jnp.sin(f3_ref[...] * (jnp.dot(h, w3_ref[...], precision=hp, preferred_element_type=F32) + b3_ref[...]))
    k = jnp.dot(h, wo_ref[...], precision=hp, preferred_element_type=F32)
    k = k * jnp.exp(-emb[:, 0:1] * dl_ref[...])
    row = i * tr + lax.broadcasted_iota(jnp.int32, (tr, 1), 0)
    k = jnp.where(row == L, 0.0, k)
    k_ref[...] = k

    @pl.when(i == 0)
    def _():
        norm_ref[...] = jnp.zeros_like(norm_ref)

    norm_ref[...] += jnp.sum(jnp.abs(k), axis=0, keepdims=True)


def hyena_filter(L, D, w1, b1, f1, w2, b2, f2, w3, b3, f3, w_out, tr=512):
    n = 2 * L
    tr = _tile(L, tr, 8)
    P = 128
    bands = (HYENA_EMB - 1) // 2
    d = np.arange(n)
    d = np.where(d <= L, d, n - d).clip(0, L - 1)
    t01 = np.linspace(0.0, 1.0, L)[d]
    wang = 2 * math.pi * d / L
    fr = np.linspace(1e-4, bands - 1, bands)
    emb = np.zeros((n, P), np.float32)
    emb[:, 0] = t01
    emb[:, 1:1 + bands] = np.cos(fr[None, :] * wang[:, None])
    emb[:, 1 + bands:1 + 2 * bands] = -np.sin(fr[None, :] * wang[:, None])

    def padw(w):
        return jnp.zeros((P, P), F32).at[:w.shape[0], :w.shape[1]].set(w)

    def padv(v):
        return jnp.zeros((1, P), F32).at[0, :v.shape[0]].set(v)

    wo = jnp.zeros((P, 2 * D), F32).at[:w_out.shape[0]].set(w_out)
    max_decay = math.log(HYENA_TARGET) / HYENA_FAST_DECAY
    min_decay = math.log(HYENA_TARGET) / HYENA_SLOW_DECAY
    deltas = jnp.abs(jnp.linspace(min_decay, max_decay, D, dtype=F32)).reshape(1, D)
    small = pl.BlockSpec((P, P), lambda i: (0, 0))
    vec = pl.BlockSpec((1, P), lambda i: (0, 0))
    kern = functools.partial(_filter_kernel, L=L)
    return pl.pallas_call(
        kern,
        grid=(n // tr,),
        in_specs=[pl.BlockSpec((tr, P), lambda i: (i, 0)),
                  small, vec, vec, small, vec, vec, small, vec, vec,
                  pl.BlockSpec((P, D), lambda i: (0, (i * tr) // L)),
                  pl.BlockSpec((1, D), lambda i: (0, 0))],
        out_specs=[pl.BlockSpec((tr, D), lambda i: (i, 0)), pl.BlockSpec((1, D), lambda i: (0, 0))],
        out_shape=[jax.ShapeDtypeStruct((n, D), F32), jax.ShapeDtypeStruct((1, D), F32)],
        compiler_params=_cparams("arbitrary"),
        name="hyena_filter",
    )(jnp.asarray(emb), padw(w1), padv(b1), padv(f1), padw(w2), padv(b2), padv(f2),
      padw(w3), padv(b3), padv(f3), wo, deltas)


def _fwd_major_kernel(*refs, n1, nb, has_imag):
    if has_imag:
        zr_ref, zi_ref, m_ref, ar_ref, ai_ref = refs
    else:
        zr_ref, m_ref, ar_ref, ai_ref = refs
    m = m_ref[...]
    half = n1 * FFT_GROUP

    def body(g, carry):
        def rows(blk):
            return pl.ds(pl.multiple_of(blk * FFT_MINOR + g * FFT_GROUP, FFT_GROUP), FFT_GROUP)

        def slab(ref):
            return jnp.concatenate([ref[rows(blk), :] for blk in range(nb)], axis=0).astype(BF16)

        pr = jnp.dot(m, slab(zr_ref), preferred_element_type=F32)
        if has_imag:
            pi = jnp.dot(m, slab(zi_ref), preferred_element_type=F32)
            ar, ai = pr[:half] - pi[half:], pi[:half] + pr[half:]
        else:
            ar, ai = pr[:half], pr[half:]
        for f1 in range(n1):
            sl = slice(f1 * FFT_GROUP, (f1 + 1) * FFT_GROUP)
            ar_ref[rows(f1), :] = ar[sl].astype(BF16)
            ai_ref[rows(f1), :] = ai[sl].astype(BF16)
        return carry

    lax.fori_loop(0, FFT_MINOR // FFT_GROUP, body, 0)


def fwd_major(z, tabs, pairs, tc=256):
    n1, nb = tabs["n1"], tabs["nb"]
    n = n1 * FFT_MINOR
    Bz, Lz, D = z.shape
    tc = _tile(D, tc)
    if pairs:
        P = Bz // 2
        m = tabs["m_fwd_half"]
        ins = [z, z, m]
        in_specs = [pl.BlockSpec((None, Lz, tc), lambda p, c: (2 * p, 0, c)),
                    pl.BlockSpec((None, Lz, tc), lambda p, c: (2 * p + 1, 0, c))]
        nblk = nb
    else:
        P = 1
        m = tabs["m_fwd_full"]
        ins = [z, m]
        in_specs = [pl.BlockSpec((None, Lz, tc), lambda p, c: (0, 0, c))]
        nblk = n1
    in_specs.append(pl.BlockSpec(m.shape, lambda p, c: (0, 0)))
    kern = functools.partial(_fwd_major_kernel, n1=n1, nb=nblk, has_imag=pairs)
    return pl.pallas_call(
        kern,
        grid=(P, D // tc),
        in_specs=in_specs,
        out_specs=[pl.BlockSpec((None, n, tc), lambda p, c: (p, 0, c))] * 2,
        out_shape=[jax.ShapeDtypeStruct((P, n, D), BF16)] * 2,
        compiler_params=_cparams("parallel", "parallel"),
        name="hyena_fwd_major",
    )(*ins)


def _spectrum_kernel(ar_ref, ai_ref, g_ref, s_ref, kr_ref, ki_ref):
    g = g_ref[...]
    h = FFT_MINOR
    pr = jnp.dot(g, ar_ref[...], preferred_element_type=F32)
    pi = jnp.dot(g, ai_ref[...], preferred_element_type=F32)
    s = s_ref[...]
    kr_ref[...] = (pr[:h] - pi[h:]) * s
    ki_ref[...] = (pi[:h] + pr[h:]) * s


def filter_spectrum(ar, ai, tabs, scale, tc=1024):
    _, n, D = ar.shape
    tc = _tile(D, tc)
    blk = pl.BlockSpec((None, FFT_MINOR, tc), lambda f, c: (0, f, c))
    out = pl.BlockSpec((FFT_MINOR, tc), lambda f, c: (f, c))
    return pl.pallas_call(
        _spectrum_kernel,
        grid=(tabs["n1"], D // tc),
        in_specs=[blk, blk, pl.BlockSpec((None, 2 * FFT_MINOR, FFT_MINOR), lambda f, c: (f, 0, 0)),
                  pl.BlockSpec((1, tc), lambda f, c: (0, c))],
        out_specs=[out, out],
        out_shape=[jax.ShapeDtypeStruct((n, D), F32)] * 2,
        compiler_params=_cparams("parallel", "parallel"),
        name="hyena_filter_spectrum",
    )(ar, ai, tabs["g"], scale)


def _minor_kernel(ar_ref, ai_ref, g_ref, gt_ref, kr_ref, ki_ref, br_ref, bi_ref):
    h = FFT_MINOR
    g = g_ref[...]
    pr = jnp.dot(g, ar_ref[...], preferred_element_type=F32)
    pi = jnp.dot(g, ai_ref[...], preferred_element_type=F32)
    xr, xi = pr[:h] - pi[h:], pi[:h] + pr[h:]
    kr, ki = kr_ref[...], ki_ref[...]
    yr = (xr * kr - xi * ki).astype(BF16)
    yi = (xr * ki + xi * kr).astype(BF16)
    gt = gt_ref[...]
    qr = jnp.dot(gt, yr, preferred_element_type=F32)
    qi = jnp.dot(gt, yi, preferred_element_type=F32)
    br_ref[...] = (qr[:h] + qi[h:]).astype(BF16)
    bi_ref[...] = (qi[:h] - qr[h:]).astype(BF16)


def minor_conv(ar, ai, tabs, kr, ki, tc=1024):
    P, n, D = ar.shape
    tc = _tile(D, tc)
    blk = pl.BlockSpec((None, FFT_MINOR, tc), lambda f, c, p: (p, f, c))
    tab = pl.BlockSpec((None, 2 * FFT_MINOR, FFT_MINOR), lambda f, c, p: (f, 0, 0))
    kblk = pl.BlockSpec((FFT_MINOR, tc), lambda f, c, p: (f, c))
    return pl.pallas_call(
        _minor_kernel,
        grid=(tabs["n1"], D // tc, P),
        in_specs=[blk, blk, tab, tab, kblk, kblk],
        out_specs=[blk, blk],
        out_shape=[jax.ShapeDtypeStruct((P, n, D), BF16)] * 2,
        compiler_params=_cparams("parallel", "parallel", "parallel"),
        name="hyena_minor_conv",
    )(ar, ai, tabs["g"], tabs["gt"], kr, ki)


def _inv_major_kernel(br_ref, bi_ref, m_ref, v_ref, x0_ref, bias_ref, o_ref, *, n1, nb):
    m = m_ref[...]
    half = nb * FFT_GROUP
    bias = bias_ref[...]

    def body(g, carry):
        def rows(blk):
            return pl.ds(pl.multiple_of(blk * FFT_MINOR + g * FFT_GROUP, FFT_GROUP), FFT_GROUP)

        def slab(ref):
            return jnp.concatenate([ref[rows(f1), :] for f1 in range(n1)], axis=0)

        pr = jnp.dot(m, slab(br_ref), preferred_element_type=F32)
        pi = jnp.dot(m, slab(bi_ref), preferred_element_type=F32)
        ys = (pr[:half] - pi[half:], pi[:half] + pr[half:])
        for s in range(2):
            for blk in range(nb):
                y = ys[s][blk * FFT_GROUP:(blk + 1) * FFT_GROUP]
                v = v_ref[s, rows(blk), :].astype(F32)
                x0 = x0_ref[s, rows(blk), :].astype(F32)
                o_ref[s, rows(blk), :] = ((y + v * bias) * x0).astype(o_ref.dtype)
        return carry

    lax.fori_loop(0, FFT_MINOR // FFT_GROUP, body, 0)


def inv_major(br, bi, tabs, v, x0, bias, tc=256):
    n1, nb = tabs["n1"], tabs["nb"]
    P, n, D = br.shape
    B, L, _ = v.shape
    tc = _tile(D, tc)
    m = tabs["m_inv"]
    blk = pl.BlockSpec((None, n, tc), lambda p, c: (p, 0, c))
    pair = pl.BlockSpec((2, L, tc), lambda p, c: (p, 0, c))
    kern = functools.partial(_inv_major_kernel, n1=n1, nb=nb)
    return pl.pallas_call(
        kern,
        grid=(P, D // tc),
        in_specs=[blk, blk, pl.BlockSpec(m.shape, lambda p, c: (0, 0)), pair, pair,
                  pl.BlockSpec((1, tc), lambda p, c: (0, c))],
        out_specs=pair,
        out_shape=jax.ShapeDtypeStruct((B, L, D), BF16),
        compiler_params=_cparams("parallel", "parallel"),
        name="hyena_inv_major",
    )(br, bi, m, v, x0, bias.reshape(1, D))


def hyena_mix(v, x0, B, L, D, filt, bias):
    assert B % 2 == 0 and L % FFT_MINOR == 0
    tabs = _dft_tables(L)
    v, x0 = v.reshape(B, L, D), x0.reshape(B, L, D)
    k_raw, k_norm = hyena_filter(L, D, *filt)
    kar, kai = fwd_major(k_raw[None], tabs, pairs=False)
    kr, ki = filter_spectrum(kar, kai, tabs, 1.0 / (2 * L * k_norm))
    ar, ai = fwd_major(v, tabs, pairs=True)
    br, bi = minor_conv(ar, ai, tabs, kr, ki)
    z = inv_major(br, bi, tabs, v, x0, bias)
    return z.reshape(B * L, D)


def kernel(x, c, ctx, c_ctx, ada_w, ada_b, norm1_g, norm2_g, na_w_qkv, na_w_o, na_q_g, na_k_g, na_rpb, hy_w_in, hy_b_in, hy_conv_w, hy_conv_b, hy_f_w1, hy_f_b1, hy_f_freq1, hy_f_w2, hy_f_b2, hy_f_freq2, hy_f_w3, hy_f_b3, hy_f_freq3, hy_f_wout, hy_bias, hy_w_out, hy_b_out, moe_router, moe_w1, moe_w3, moe_w2):
    B, N, D = x.shape
    CTX = ctx.shape[1]
    depth = ada_w.shape[0]
    mixer = [i % N_MIXERS for i in range(depth)]

    cond = jnp.concatenate([c, c_ctx[None, :], jnp.zeros((8 - B - 1, D), F32)], axis=0)
    mods_all = ada_all(cond, ada_w, ada_b)
    mods_all = mods_all.reshape(depth, 8, 6, 1, D).transpose(0, 2, 1, 3, 4)

    h = x.reshape(B * N, D)
    hc = ctx.reshape(B * CTX, D)
    zeros_d = jnp.zeros((D,), F32)
    w1, w3, w2 = moe_w1.astype(BF16), moe_w3.astype(BF16), moe_w2.astype(BF16)
    for i in range(depth):
        j = i // N_MIXERS
        ctx_stream = any(mixer[l] == 0 for l in range(i + 1, depth))
        ctx_in = ctx_stream or mixer[i] == 0
        mods = mods_all[i]
        if mixer[i] == 0:
            wqkv = na_w_qkv[j].astype(BF16)
            wo = na_w_o[j].astype(BF16)
            hg = jnp.stack([na_q_g[j], na_k_g[j]]).reshape(2, 1, HEAD_DIM)
            zeros_e = jnp.zeros((3 * D,), F32)
            qs = HEAD_DIM ** -0.5
            qkv = nm_matmul(h, norm1_g[i], mods, 0, 1, N, 0, wqkv, zeros_e, hg, qs)
            qkv_c = nm_matmul(hc, norm1_g[i], mods, 0, 1, B * CTX, B, wqkv, zeros_e, hg, qs)
            bias = _na_bias_tables(na_rpb[j], N // GRID_W)
            o, oc = na_attention(qkv, qkv_c, bias, B, N, CTX, D)
            h = matmul_residual(o, wo, zeros_d, mods, 2, N, 0, h)
            if ctx_stream:
                hc = matmul_residual(oc, wo, zeros_d, mods, 2, B * CTX, B, hc)
        else:
            win = hy_w_in[j].astype(BF16)
            wout = hy_w_out[j].astype(BF16)
            filt = (hy_f_w1[j], hy_f_b1[j], hy_f_freq1[j], hy_f_w2[j], hy_f_b2[j], hy_f_freq2[j],
                    hy_f_w3[j], hy_f_b3[j], hy_f_freq3[j], hy_f_wout[j])
            v, x0 = hyena_in_proj(h, norm1_g[i], mods, 0, 1, N, 0, N, win, hy_b_in[j],
                                  hy_conv_w[j], hy_conv_b[j])
            z = hyena_mix(v, x0, B, N, D, filt, hy_bias[j])
            h = matmul_residual(z, wout, hy_b_out[j], mods, 2, N, 0, h)
            if ctx_stream:
                vc, x0c = hyena_in_proj(hc, norm1_g[i], mods, 0, 1, B * CTX, B, CTX, win, hy_b_in[j],
                                        hy_conv_w[j], hy_conv_b[j])
                zc = hyena_mix(vc, x0c, B, CTX, D, filt, hy_bias[j])
                hc = matmul_residual(zc, wout, hy_b_out[j], mods, 2, B * CTX, B, hc)
        h = ec_moe(h, norm2_g[i], mods, 3, 4, 5, N, 0, B, moe_router[i], i, w1, w3, w2)
        if ctx_stream:
            hc = ec_moe(hc, norm2_g[i], mods, 3, 4, 5, B * CTX, B, B, moe_router[i], i, w1, w3, w2)
    return h.reshape(B, N, D)
```

```python
import functools
import math

import jax
import jax.numpy as jnp
import numpy as np
from jax import lax
from jax.experimental import pallas as pl
from jax.experimental.pallas import tpu as pltpu

F32 = jnp.float32
BF16 = jnp.bfloat16

GRID_W = 64
N_MIXERS = 2
NORM_EPS = 1e-6
NEG_INF = -1e30
HEAD_DIM = 128
LANES = 128
WIN_ROWS = 8
WIN_COLS = 16
HYENA_SHORT = 3
HYENA_EMB = 33
HYENA_FAST_DECAY = 0.3
HYENA_SLOW_DECAY = 1.5
HYENA_TARGET = 1e-2
N_EXPERTS = 16
CAPACITY_FACTOR = 2

Q_ROWS = 4
BAND_ROWS = 12

VMEM_LIMIT = 56 * 1024 * 1024


def _cparams(*sem):
    return pltpu.CompilerParams(dimension_semantics=sem, vmem_limit_bytes=VMEM_LIMIT)


def _tile(n, want, unit=128):
    if n <= want:
        return n
    t = (want // unit) * unit
    while n % t:
        t -= unit
    return t


def _ada_kernel(c_ref, w_ref, b_ref, o_ref):
    c = c_ref[...]
    s = c * jax.nn.sigmoid(c)
    o_ref[...] = jnp.dot(s.astype(BF16), w_ref[...].astype(BF16),
                         preferred_element_type=F32) + b_ref[...]


def ada_all(cond, ada_w, ada_b, tn=1024):
    L, D, E = ada_w.shape
    tn = _tile(E, tn)
    return pl.pallas_call(
        _ada_kernel,
        grid=(L, E // tn),
        in_specs=[
            pl.BlockSpec((8, D), lambda l, j: (0, 0)),
            pl.BlockSpec((None, D, tn), lambda l, j: (l, 0, j)),
            pl.BlockSpec((None, 1, tn), lambda l, j: (l, 0, j)),
        ],
        out_specs=pl.BlockSpec((None, 8, tn), lambda l, j: (l, 0, j)),
        out_shape=jax.ShapeDtypeStruct((L, 8, E), F32),
        compiler_params=_cparams("parallel", "parallel"),
        name="ada",
    )(cond, ada_w, ada_b.reshape(L, 1, E))


def _norm_mod(x, g, sh, sc):
    y = x * lax.rsqrt(jnp.mean(x * x, axis=-1, keepdims=True) + NORM_EPS) * g
    return y * (1.0 + sc) + sh


def _nm_matmul_kernel(x_ref, g_ref, sh_ref, sc_ref, w_ref, b_ref, hg_ref, o_ref, a_ref, *,
                      n_norm_sections, section_cols, q_scale):
    j = pl.program_id(1)

    @pl.when(j == 0)
    def _():
        a_ref[...] = _norm_mod(x_ref[...], g_ref[...], sh_ref[...], sc_ref[...]).astype(BF16)

    acc = jnp.dot(a_ref[...], w_ref[...], preferred_element_type=F32) + b_ref[...]
    tn = acc.shape[1]
    if n_norm_sections == 0:
        o_ref[...] = acc.astype(o_ref.dtype)
        return
    sec = (j * tn) // section_cols

    @pl.when(sec < n_norm_sections)
    def _():
        hg = hg_ref[...]
        mult = jnp.where(sec == 0, q_scale, 1.0).astype(F32)
        for h in range(tn // HEAD_DIM):
            c = acc[:, h * HEAD_DIM:(h + 1) * HEAD_DIM]
            c = c * lax.rsqrt(jnp.mean(c * c, axis=-1, keepdims=True) + NORM_EPS) * hg
            o_ref[:, h * HEAD_DIM:(h + 1) * HEAD_DIM] = (c * mult).astype(o_ref.dtype)

    @pl.when(sec >= n_norm_sections)
    def _():
        o_ref[...] = acc.astype(o_ref.dtype)


def nm_matmul(x, g, mods, k_shift, k_scale, group_rows, group_base, w, b, head_g=None,
              q_scale=1.0, out_dtype=BF16, tm=1024, tn=1024):
    R, D = x.shape
    E = w.shape[1]
    tm = _tile(group_rows, tm, 8)
    tn = _tile(D, tn)
    n_norm = 0 if head_g is None else 2
    if head_g is None:
        head_g = jnp.ones((2, 1, HEAD_DIM), F32)

    def grp(i):
        return group_base + (i * tm) // group_rows

    def hg_map(i, j):
        return (jnp.minimum((j * tn) // D, 1), 0, 0)

    kern = functools.partial(_nm_matmul_kernel, n_norm_sections=n_norm, section_cols=D,
                             q_scale=q_scale)
    return pl.pallas_call(
        kern,
        grid=(R // tm, E // tn),
        in_specs=[
            pl.BlockSpec((tm, D), lambda i, j: (i, 0)),
            pl.BlockSpec((1, D), lambda i, j: (0, 0)),
            pl.BlockSpec((None, None, 1, D), lambda i, j: (k_shift, grp(i), 0, 0)),
            pl.BlockSpec((None, None, 1, D), lambda i, j: (k_scale, grp(i), 0, 0)),
            pl.BlockSpec((D, tn), lambda i, j: (0, j)),
            pl.BlockSpec((1, tn), lambda i, j: (0, j)),
            pl.BlockSpec((None, 1, HEAD_DIM), hg_map),
        ],
        out_specs=pl.BlockSpec((tm, tn), lambda i, j: (i, j)),
        out_shape=jax.ShapeDtypeStruct((R, E), out_dtype),
        scratch_shapes=[pltpu.VMEM((tm, D), BF16)],
        compiler_params=_cparams("parallel", "arbitrary"),
        name="nm_matmul",
    )(x, g.reshape(1, D), mods, mods, w, b.reshape(1, E), head_g)


def _mm_res_kernel(x_ref, w_ref, b_ref, gate_ref, res_ref, o_ref):
    y = jnp.dot(x_ref[...], w_ref[...], preferred_element_type=F32) + b_ref[...]
    o_ref[...] = res_ref[...] + gate_ref[...] * y


def matmul_residual(x, w, b, mods, k_gate, group_rows, group_base, res, tm=1024, tn=1024):
    R, K = x.shape
    E = w.shape[1]
    tm = _tile(group_rows, tm, 8)
    tn = _tile(E, tn)

    def grp(i):
        return group_base + (i * tm) // group_rows

    return pl.pallas_call(
        _mm_res_kernel,
        grid=(R // tm, E // tn),
        in_specs=[
            pl.BlockSpec((tm, K), lambda i, j: (i, 0)),
            pl.BlockSpec((K, tn), lambda i, j: (0, j)),
            pl.BlockSpec((1, tn), lambda i, j: (0, j)),
            pl.BlockSpec((None, None, 1, tn), lambda i, j: (k_gate, grp(i), 0, j)),
            pl.BlockSpec((tm, tn), lambda i, j: (i, j)),
        ],
        out_specs=pl.BlockSpec((tm, tn), lambda i, j: (i, j)),
        out_shape=jax.ShapeDtypeStruct((R, E), F32),
        compiler_params=_cparams("parallel", "parallel"),
        name="matmul_residual",
    )(x, w, b.reshape(1, E), mods, res)


def _na_bias_tables(rpb, rows):
    W = GRID_W
    cols = np.arange(W)
    cs = np.clip(cols - WIN_COLS // 2, 0, W - WIN_COLS)
    in_win = (cols[None, :] >= cs[:, None]) & (cols[None, :] < cs[:, None] + WIN_COLS)
    dc = np.clip(cols[None, :] - cols[:, None] + WIN_COLS - 1, 0, 2 * WIN_COLS - 2)
    n_dr, n_dc = 2 * WIN_ROWS - 1, 2 * WIN_COLS - 1
    onehot_r = np.zeros((3, Q_ROWS, BAND_ROWS, n_dr), np.float32)
    mask = np.zeros((3, Q_ROWS, W, BAND_ROWS, W), bool)
    for var, r0 in enumerate((0, Q_ROWS, rows - Q_ROWS)):
        bs = _band_start(r0, rows)
        for i in range(Q_ROWS):
            r = r0 + i
            rs = int(np.clip(r - WIN_ROWS // 2, 0, rows - WIN_ROWS))
            for jj in range(BAND_ROWS):
                kr = bs + jj
                if rs <= kr < rs + WIN_ROWS:
                    onehot_r[var, i, jj, kr - r + WIN_ROWS - 1] = 1.0
                    mask[var, i, :, jj, :] = in_win
    onehot_c = (dc[:, :, None] == np.arange(n_dc)).astype(np.float32)
    hp = lax.Precision.HIGHEST
    t = jnp.einsum('hrc,vijr->hvijc', rpb.astype(F32), onehot_r, precision=hp)
    t = jnp.einsum('hvijc,qkc->hviqjk', t, onehot_c, precision=hp)
    t = jnp.where(mask[None], t, NEG_INF)
    return t.reshape(rpb.shape[0], 3, Q_ROWS * W, BAND_ROWS * W)


def _band_start(r0, rows):
    return int(np.clip(r0 - WIN_ROWS // 2, 0, rows - BAND_ROWS))


def _na_kernel(q_ref, k_ref, v_ref, qc_ref, kc_ref, vc_ref, bias_ref, o_ref, oc_ref, *, rows):
    W = GRID_W
    nblk = rows // Q_ROWS
    kc = kc_ref[...]
    vc = vc_ref[...]
    nt = (((1,), (1,)), ((), ()))

    def attend(q, parts):
        ss = [lax.dot_general(q, k, nt, preferred_element_type=F32) + (0.0 if bias is None else bias)
              for k, _, bias in parts]
        m = functools.reduce(jnp.maximum, [jnp.max(s, axis=-1, keepdims=True) for s in ss])
        ps = [jnp.exp(s - m) for s in ss]
        den = sum(jnp.sum(p, axis=-1, keepdims=True) for p in ps)
        o = sum(jnp.dot(p.astype(BF16), v, preferred_element_type=F32)
                for p, (_, v, _) in zip(ps, parts))
        return o / den

    def body(blk, carry):
        r0 = blk * Q_ROWS
        bs = jnp.clip(r0 - WIN_ROWS // 2, 0, rows - BAND_ROWS)
        var = jnp.where(blk == 0, 0, jnp.where(blk == nblk - 1, 2, 1))
        q0 = pl.multiple_of(r0 * W, Q_ROWS * W)
        k0 = pl.multiple_of(bs * W, W)
        q = q_ref[pl.ds(q0, Q_ROWS * W), :]
        kb = k_ref[pl.ds(k0, BAND_ROWS * W), :]
        vb = v_ref[pl.ds(k0, BAND_ROWS * W), :]
        o = attend(q, [(kb, vb, bias_ref[var]), (kc, vc, None)])
        o_ref[pl.ds(q0, Q_ROWS * W), :] = o.astype(o_ref.dtype)
        return carry

    lax.fori_loop(0, nblk, body, 0, unroll=2)
    oc_ref[...] = attend(qc_ref[...], [(kc, vc, None)]).astype(oc_ref.dtype)


def na_attention(qkv, qkv_c, bias, B, N, CTX, D):
    H = D // HEAD_DIM
    rows = N // GRID_W
    kern = functools.partial(_na_kernel, rows=rows)

    def spec(n, sec):
        return pl.BlockSpec((n, HEAD_DIM), lambda h, b: (b, sec * H + h))

    return pl.pallas_call(
        kern,
        grid=(H, B),
        in_specs=[spec(N, 0), spec(N, 1), spec(N, 2), spec(CTX, 0), spec(CTX, 1), spec(CTX, 2),
                  pl.BlockSpec((None, 3, Q_ROWS * GRID_W, BAND_ROWS * GRID_W),
                               lambda h, b: (h, 0, 0, 0))],
        out_specs=[pl.BlockSpec((N, HEAD_DIM), lambda h, b: (b, h)),
                   pl.BlockSpec((CTX, HEAD_DIM), lambda h, b: (b, h))],
        out_shape=[jax.ShapeDtypeStruct((B * N, D), BF16),
                   jax.ShapeDtypeStruct((B * CTX, D), BF16)],
        compiler_params=_cparams("parallel", "parallel"),
        name="na_attention",
    )(qkv, qkv, qkv, qkv_c, qkv_c, qkv_c, bias)


def _expert_kernel(idx_ref, a_hbm, w1_ref, w3_ref, w2_ref, gate_ref, o_ref, xbuf, xs, yacc, sem, *,
                   n_tok, cap, bg):
    e, g, f = pl.program_id(0), pl.program_id(1), pl.program_id(2)
    n_exp, ng, nf = pl.num_programs(0), pl.num_programs(1), pl.num_programs(2)
    step = e * ng + g
    slot = step % 2
    nsteps = n_exp * ng
    rows = bg * cap
    S = xbuf.shape[1] // rows
    part = rows // nf

    def copy_row(ee, gg, sl, r0, u=0, priority=0):
        unit = math.gcd(part, cap)
        b = gg * bg + (r0 + (u // unit) * unit) // cap
        c = (r0 + (u // unit) * unit) % cap + u % unit
        row = b * n_tok + idx_ref[(b * n_exp + ee) * cap + c]
        src = pl.ds(pl.multiple_of(row * S, S), S)
        dst = pl.ds(pl.multiple_of((r0 + u) * S, S), S)
        pltpu.make_async_copy(a_hbm.at[src, :], xbuf.at[sl, dst, :], sem.at[sl]).start(priority=priority)

    def wait_rows(sl):
        pltpu.make_async_copy(a_hbm.at[pl.ds(0, rows * S), :], xbuf.at[sl], sem.at[sl]).wait()

    @pl.when((step == 0) & (f == 0))
    def _():
        def body(r, carry):
            copy_row(e, g, slot, r)
            return carry

        lax.fori_loop(0, rows, body, 0, unroll=8)

    @pl.when(f == 0)
    def _():
        wait_rows(slot)
        hi, lo = _load_token_tiles(xbuf, (slot,), 0, rows, S)
        xs[...] = jnp.concatenate([p.astype(BF16) for p in hi + lo], axis=1)
        yacc[...] = jnp.zeros_like(yacc)

    nxt = (step + 1) % nsteps
    for u in range(part):
        copy_row(nxt // ng, nxt % ng, 1 - slot, f * part, u, priority=u % 2)

    x = xs[...]
    h1 = jnp.dot(x, w1_ref[...].astype(BF16), preferred_element_type=F32)
    h3 = jnp.dot(x, w3_ref[...].astype(BF16), preferred_element_type=F32)
    hid = (h1 * jax.nn.sigmoid(h1) * h3).astype(BF16)
    w2 = w2_ref[...].astype(BF16)
    D = w2.shape[1]
    cw = min(D, EXPERT_OUT_COLS)
    for c in range(D // cw):
        cols = slice(c * cw, (c + 1) * cw)
        yacc[:, cols] += jnp.dot(hid, w2[:, cols], preferred_element_type=F32)

    @pl.when(f == nf - 1)
    def _():
        packed = _pack_bf16_pair(yacc[...] * gate_ref[...].reshape(rows, 1))
        for i in range(bg):
            _store_token_tiles(o_ref, (i,), packed[i * cap:(i + 1) * cap])

    @pl.when((step == nsteps - 1) & (f == nf - 1))
    def _():
        wait_rows(1 - slot)


def _pack_bf16_pair(y):
    half = y.shape[1] // 2
    hi = pltpu.bitcast(y[:, :half].astype(BF16).astype(F32), jnp.uint32)
    lo = pltpu.bitcast(y[:, half:].astype(BF16).astype(F32), jnp.uint32)
    return hi | (lo >> 16)


def _unpack_bf16_pair(u):
    hi = pltpu.bitcast(u & jnp.uint32(0xFFFF0000), F32)
    lo = pltpu.bitcast(u << 16, F32)
    return hi, lo


def _store_token_tiles(ref, lead, packed):
    rows, S = packed.shape[0], packed.shape[1] // LANES
    for s in range(S):
        ref[(*lead, pl.ds(s, rows, stride=S), slice(None))] = packed[:, s * LANES:(s + 1) * LANES]


def _load_token_tiles(ref, lead, first, rows, S):
    pieces = [_unpack_bf16_pair(ref[(*lead, pl.ds(first * S + s, rows, stride=S), slice(None))])
              for s in range(S)]
    return [p[0] for p in pieces], [p[1] for p in pieces]


def expert_ffn(a, idx_flat, layer, w1, w3, w2, gate, B, n_tok):
    _, E, D, Fh = w1.shape
    C = gate.shape[2]
    S = D // (2 * LANES)
    bg = max(1, min(B, EXPERT_ROWS // C))
    while B % bg:
        bg -= 1
    fc = _tile(Fh, EXPERT_HIDDEN_CHUNK)
    assert (bg * C) % (Fh // fc) == 0
    kern = functools.partial(_expert_kernel, n_tok=n_tok, cap=C, bg=bg)
    grid_spec = pltpu.PrefetchScalarGridSpec(
        num_scalar_prefetch=1,
        grid=(E, B // bg, Fh // fc),
        in_specs=[
            pl.BlockSpec(memory_space=pl.ANY),
            pl.BlockSpec((None, None, D, fc), lambda e, g, f, idx: (layer, e, 0, f)),
            pl.BlockSpec((None, None, D, fc), lambda e, g, f, idx: (layer, e, 0, f)),
            pl.BlockSpec((None, None, fc, D), lambda e, g, f, idx: (layer, e, f, 0)),
            pl.BlockSpec((bg, None, C, 1), lambda e, g, f, idx: (g, e, 0, 0)),
        ],
        out_specs=pl.BlockSpec((bg, None, C * S, LANES), lambda e, g, f, idx: (g, e, 0, 0)),
        scratch_shapes=[pltpu.VMEM((2, bg * C * S, LANES), jnp.uint32), pltpu.VMEM((bg * C, D), BF16),
                        pltpu.VMEM((bg * C, D), F32), pltpu.SemaphoreType.DMA((2,))],
    )
    return pl.pallas_call(
        kern,
        grid_spec=grid_spec,
        out_shape=jax.ShapeDtypeStruct((B, E, C * S, LANES), jnp.uint32),
        compiler_params=_cparams("arbitrary", "arbitrary", "arbitrary"),
        name="expert_ffn",
    )(idx_flat, a, w1, w3, w2, gate)


def _moe_prep_kernel(x_ref, g_ref, sh_ref, sc_ref, r_ref, a_ref, aff_ref):
    a = _norm_mod(x_ref[...], g_ref[...], sh_ref[...], sc_ref[...])
    _store_token_tiles(a_ref, (), _pack_bf16_pair(a))
    logits = lax.dot_general(r_ref[...], a.astype(BF16), (((1,), (1,)), ((), ())),
                             preferred_element_type=F32)
    m = jnp.max(logits, axis=0, keepdims=True)
    p = jnp.exp(logits - m)
    aff_ref[...] = p / jnp.sum(p, axis=0, keepdims=True)


def moe_prep(x, g, mods, k_shift, k_scale, group_rows, group_base, router_t, tm=512):
    R, D = x.shape
    E = router_t.shape[0]
    S = D // (2 * LANES)
    tm = _tile(group_rows, tm, 128)

    def grp(i):
        return group_base + (i * tm) // group_rows

    return pl.pallas_call(
        _moe_prep_kernel,
        grid=(R // tm,),
        in_specs=[
            pl.BlockSpec((tm, D), lambda i: (i, 0)),
            pl.BlockSpec((1, D), lambda i: (0, 0)),
            pl.BlockSpec((None, None, 1, D), lambda i: (k_shift, grp(i), 0, 0)),
            pl.BlockSpec((None, None, 1, D), lambda i: (k_scale, grp(i), 0, 0)),
            pl.BlockSpec((E, D), lambda i: (0, 0)),
        ],
        out_specs=[pl.BlockSpec((tm * S, LANES), lambda i: (i, 0)),
                   pl.BlockSpec((E, tm), lambda i: (0, i))],
        out_shape=[jax.ShapeDtypeStruct((R * S, LANES), jnp.uint32), jax.ShapeDtypeStruct((E, R), F32)],
        compiler_params=_cparams("parallel"),
        name="moe_prep",
    )(x, g.reshape(1, D), mods, mods, router_t)


EXPERT_ROWS = 1024
EXPERT_HIDDEN_CHUNK = 256
EXPERT_OUT_COLS = 512
COMBINE_TOKENS = 128
HIT_UNROLL = 4
CUMSUM_CHUNK = 256
ROUTE_BLOCK_LANES = 512
TILE_TABLE_LANES = 128


def _route_select_kernel(aff_ref, tri_ref, ltri_ref, lt_ref, in_ref,
                         posi_ref, key_ref, slab_ref, p0_ref, kmax_ref, *, cap):
    E, N = aff_ref.shape
    aff = aff_ref[...]
    bits = pltpu.bitcast(aff, jnp.int32)
    cur = jnp.zeros((E, 1), jnp.int32)
    for bit in range(30, -1, -1):
        cand = cur | (1 << bit)
        cnt = jnp.sum(jnp.where(bits >= cand, 1.0, 0.0), axis=1, keepdims=True)
        cur = jnp.where(cnt >= cap, cand, cur)
    gt = bits > cur
    eq = bits == cur
    need = cap - jnp.sum(jnp.where(gt, 1.0, 0.0), axis=1, keepdims=True)

    def cumsum_incl(x):
        tri = tri_ref[...]
        w = min(CUMSUM_CHUNK, N)
        off = jnp.zeros((E, 1), F32)
        outs = []
        for c in range(N // w):
            xc = x[:, c * w:(c + 1) * w]
            outs.append(jnp.dot(xc.astype(BF16), tri[:w, :w], preferred_element_type=F32) + off)
            off = off + jnp.sum(xc, axis=1, keepdims=True)
        return jnp.concatenate(outs, axis=1) if len(outs) > 1 else outs[0]

    eqf = jnp.where(eq, 1.0, 0.0)
    tie_rank = cumsum_incl(eqf) - eqf
    sel = gt | (eq & (tie_rank < need))
    self_ = jnp.where(sel, 1.0, 0.0)
    pos_incl = cumsum_incl(self_)
    posi_ref[...] = pos_incl
    key_ref[...] = jnp.where(sel, pos_incl - self_, -1.0)
    selb = self_.astype(BF16)
    erank = jnp.dot(ltri_ref[...], selb, preferred_element_type=F32)
    in_tile = lax.broadcasted_iota(jnp.int32, (E, N), 1) & (COMBINE_TOKENS - 1)
    slab_ref[...] = erank * COMBINE_TOKENS + in_tile.astype(F32)
    p0_ref[...] = jnp.dot(selb, lt_ref[...], preferred_element_type=F32).astype(jnp.int32)
    count = jnp.sum(self_, axis=0, keepdims=True)
    kk = lax.broadcasted_iota(jnp.int32, (E, 1), 0).astype(F32)
    over = jnp.where(count > kk, 1.0, 0.0).astype(BF16)
    per_tile = jnp.dot(over, in_ref[...], preferred_element_type=F32)
    kmax_ref[...] = jnp.sum(jnp.where(per_tile > 0.0, 1.0, 0.0), axis=0, keepdims=True).astype(jnp.int32)


def _route_compact_kernel(p0_ref, aff_ref, posi_ref, key_ref, slab_ref, idx_ref, gate_ref, rank_ref, *,
                          cap, tn, lanes, width):
    b = pl.program_id(0)
    E, N = aff_ref.shape
    ch = min(64, cap)
    tpb = width // tn

    def e_body(e, carry):
        q = (b * E + e) * lanes

        def c_body(cc, tiles):
            c0 = pl.multiple_of(cc * ch, ch)
            t_lo = lax.while_loop(lambda t: p0_ref[q + t + 1] <= c0, lambda t: t + 1, tiles[0])
            t_hi = lax.while_loop(lambda t: p0_ref[q + t] < c0 + ch, lambda t: t + 1, tiles[1])
            j_lo = t_lo // tpb
            j_hi = (t_hi + tpb - 1) // tpb
            ccol = (c0 + lax.broadcasted_iota(jnp.int32, (ch, 1), 0)).astype(F32)

            def j_body(j, acc):
                row = pl.ds(e, 1)
                blk = pl.ds(pl.multiple_of(j * width, width), width)
                pi, ky, af, sl = posi_ref[row, blk], key_ref[row, blk], aff_ref[row, blk], slab_ref[row, blk]
                acc = list(acc)
                for u in range(width // LANES):
                    part = slice(u * LANES, (u + 1) * LANES)
                    hit = ky[:, part] == ccol
                    acc[0] = acc[0] + jnp.where(pi[:, part] <= ccol, 1.0, 0.0)
                    acc[1] = acc[1] + jnp.where(hit, af[:, part], 0.0)
                    acc[2] = acc[2] + jnp.where(hit, sl[:, part], 0.0)
                return tuple(acc)

            zero = jnp.zeros((ch, LANES), F32)
            iv, gv, rv = [jnp.sum(a, axis=1, keepdims=True)
                          for a in lax.fori_loop(j_lo, j_hi, j_body, (zero, zero, zero))]
            idx_ref[e, pl.ds(c0, ch), :] = iv.astype(jnp.int32) + j_lo * width
            gate_ref[e, pl.ds(c0, ch), :] = gv
            rank_ref[e, pl.ds(c0, ch), :] = rv.astype(jnp.int32)
            return (t_lo, t_hi)

        lax.fori_loop(0, cap // ch, c_body, (0, 0))
        return carry

    lax.fori_loop(0, E, e_body, 0)


def moe_route(aff_t, B, N, cap):
    E = aff_t.shape[0]
    tn = COMBINE_TOKENS
    lanes = TILE_TABLE_LANES
    assert N // tn + 1 <= lanes and N % tn == 0 and N % min(CUMSUM_CHUNK, N) == 0
    w = min(CUMSUM_CHUNK, N)
    tri = (np.arange(w)[:, None] <= np.arange(w)[None, :]).astype(np.float32)
    ltri = (np.arange(E)[None, :] < np.arange(E)[:, None]).astype(np.float32)
    tok = np.arange(N)[:, None]
    tile = np.arange(lanes)[None, :]
    before = (tok < tile * tn).astype(np.float32)
    inside = (tok // tn == tile).astype(np.float32)
    const = lambda shape: pl.BlockSpec(shape, lambda b: (0,) * len(shape))
    per_tok = pl.BlockSpec((None, E, N), lambda b, *_: (b, 0, 0))
    posi, key, slab, p0, kmax = pl.pallas_call(
        functools.partial(_route_select_kernel, cap=cap),
        grid=(B,),
        in_specs=[pl.BlockSpec((E, N), lambda b: (0, b)), const((w, w)), const((E, E)),
                  const((N, lanes)), const((N, lanes))],
        out_specs=[per_tok, per_tok, per_tok,
                   pl.BlockSpec((None, E, lanes), lambda b: (b, 0, 0)),
                   pl.BlockSpec((None, 1, lanes), lambda b: (b, 0, 0))],
        out_shape=[jax.ShapeDtypeStruct((B, E, N), F32)] * 3 + [
            jax.ShapeDtypeStruct((B, E, lanes), jnp.int32), jax.ShapeDtypeStruct((B, 1, lanes), jnp.int32)],
        compiler_params=_cparams("parallel"),
        name="moe_route_select",
    )(aff_t, jnp.asarray(tri, BF16), jnp.asarray(ltri, BF16), jnp.asarray(before, BF16),
      jnp.asarray(inside, BF16))
    width = min(ROUTE_BLOCK_LANES, N)
    sel3 = pl.BlockSpec((None, E, cap, 1), lambda b, *_: (b, 0, 0, 0))
    idx, gate, rank = pl.pallas_call(
        functools.partial(_route_compact_kernel, cap=cap, tn=tn, lanes=lanes, width=width),
        grid_spec=pltpu.PrefetchScalarGridSpec(
            num_scalar_prefetch=1,
            grid=(B,),
            in_specs=[pl.BlockSpec((E, N), lambda b, *_: (0, b)), per_tok, per_tok, per_tok],
            out_specs=[sel3, sel3, sel3],
        ),
        out_shape=[jax.ShapeDtypeStruct((B, E, cap, 1), jnp.int32),
                   jax.ShapeDtypeStruct((B, E, cap, 1), F32),
                   jax.ShapeDtypeStruct((B, E, cap, 1), jnp.int32)],
        compiler_params=_cparams("parallel"),
        name="moe_route_compact",
    )(p0.reshape(-1), aff_t, posi, key, slab)
    return idx, gate, rank, p0, kmax


def _combine_kernel(rank_ref, p0_ref, kmax_ref, ys_hbm, h_ref, g2_ref, o_ref, rbuf, sem, *,
                    n_exp, cap, tn, lanes):
    b, t = pl.program_id(0), pl.program_id(1)
    nt = pl.num_programs(1)
    step = b * nt + t
    slot = step % 2
    S = rbuf.shape[1] // (n_exp * tn)

    def tile_hits(bb, tt):
        def body(e, tot):
            q = (bb * n_exp + e) * lanes + tt
            return tot + p0_ref[q + 1] - p0_ref[q]
        return lax.fori_loop(0, n_exp, body, 0)

    def fetch(bb, tt, sl):
        km = kmax_ref[bb * lanes + tt]
        for k in range(n_exp):
            @pl.when(k < km)
            def _():
                rbuf[sl, k * tn * S:(k + 1) * tn * S, :] = jnp.zeros((tn * S, LANES), jnp.uint32)

        def copy_row(row, priority=0):
            src = pl.ds(pl.multiple_of(row * S, S), S)
            dst = pl.ds(pl.multiple_of(rank_ref[row] * S, S), S)
            pltpu.make_async_copy(ys_hbm.at[src, :], rbuf.at[sl, dst, :], sem.at[sl]).start(priority=priority)

        def e_body(e, carry):
            q = (bb * n_exp + e) * lanes + tt
            lo = (bb * n_exp + e) * cap + p0_ref[q]
            n = p0_ref[q + 1] - p0_ref[q]
            groups = n // HIT_UNROLL

            def g_body(g, carry2):
                for u in range(HIT_UNROLL):
                    copy_row(lo + g * HIT_UNROLL + u, priority=u % 2)
                return carry2

            def r_body(c, carry2):
                copy_row(lo + c)
                return carry2

            lax.fori_loop(0, groups, g_body, 0)
            lax.fori_loop(groups * HIT_UNROLL, n, r_body, 0)
            return carry

        lax.fori_loop(0, n_exp, e_body, 0)

    @pl.when(step == 0)
    def _():
        fetch(b, t, slot)

    @pl.when(step + 1 < pl.num_programs(0) * nt)
    def _():
        nxt = step + 1
        fetch(nxt // nt, nxt % nt, 1 - slot)

    hits = tile_hits(b, t)

    nrows = hits * S
    bulk = pl.multiple_of((nrows // 8) * 8, 8)

    @pl.when(bulk > 0)
    def _():
        pltpu.make_async_copy(ys_hbm.at[pl.ds(0, bulk), :], rbuf.at[slot, pl.ds(0, bulk), :],
                              sem.at[slot]).wait()

    def wait_row(i, carry):
        pltpu.make_async_copy(ys_hbm.at[pl.ds(0, 1), :], rbuf.at[slot, pl.ds(0, 1), :], sem.at[slot]).wait()
        return carry

    lax.fori_loop(0, nrows - bulk, wait_row, 0)

    km = kmax_ref[b * lanes + t]
    g2 = g2_ref[...]
    for r in range(tn // 8):
        def k_body(k, tot):
            hi, lo = _load_token_tiles(rbuf, (slot,), k * tn + r * 8, 8, S)
            return tuple(t_ + p for t_, p in zip(tot, hi + lo))

        tot = lax.fori_loop(0, km, k_body, (jnp.zeros((8, LANES), F32),) * (2 * S))
        rows = slice(r * 8, (r + 1) * 8)
        for s in range(2 * S):
            cols = slice(s * LANES, (s + 1) * LANES)
            o_ref[rows, cols] = h_ref[rows, cols] + g2[:, cols] * tot[s]


def moe_combine(ys, h, mods, k_gate, group_rows, group_base, rank, p0, kmax, B, N):
    R, D = h.shape
    E, cap = rank.shape[1], rank.shape[2]
    tn = COMBINE_TOKENS
    lanes = TILE_TABLE_LANES
    nt = N // tn

    def grp(b, t):
        return group_base + ((b * nt + t) * tn) // group_rows

    kern = functools.partial(_combine_kernel, n_exp=E, cap=cap, tn=tn, lanes=lanes)
    grid_spec = pltpu.PrefetchScalarGridSpec(
        num_scalar_prefetch=3,
        grid=(B, nt),
        in_specs=[
            pl.BlockSpec(memory_space=pl.ANY),
            pl.BlockSpec((tn, D), lambda b, t, *_: (b * nt + t, 0)),
            pl.BlockSpec((None, None, 1, D), lambda b, t, *_: (k_gate, grp(b, t), 0, 0)),
        ],
        out_specs=pl.BlockSpec((tn, D), lambda b, t, *_: (b * nt + t, 0)),
        scratch_shapes=[pltpu.VMEM((2, E * tn * (D // (2 * LANES)), LANES), jnp.uint32),
                        pltpu.SemaphoreType.DMA((2,))],
    )
    return pl.pallas_call(
        kern,
        grid_spec=grid_spec,
        out_shape=jax.ShapeDtypeStruct((R, D), F32),
        compiler_params=_cparams("arbitrary", "arbitrary"),
        name="moe_combine",
    )(rank.reshape(-1), p0.reshape(-1), kmax.reshape(-1), ys, h, mods)


def ec_moe(h, g, mods, k_shift, k_scale, k_gate, group_rows, group_base, B, router, layer, w1, w3, w2):
    R, D = h.shape
    N = R // B
    E = router.shape[1]
    cap = CAPACITY_FACTOR * N // E
    a, aff_t = moe_prep(h, g, mods, k_shift, k_scale, group_rows, group_base, router.T.astype(BF16))
    idx, gate, rank, p0, kmax = moe_route(aff_t, B, N, cap)
    ys = expert_ffn(a, idx.reshape(-1), layer, w1, w3, w2, gate, B, N)
    return moe_combine(ys.reshape(-1, LANES), h, mods, k_gate, group_rows, group_base,
                       rank, p0, kmax, B, N)


FFT_MINOR = 256
FFT_GROUP = 16


def _dft_tables(L):
    n = 2 * L
    n1 = n // FFT_MINOR
    nb = n1 // 2
    two_pi = 2.0 * math.pi
    a = jnp.arange(n1, dtype=jnp.int32)
    ang = two_pi * ((a[:, None] * a[None, :]) % n1).astype(F32) / n1
    cos1, sin1 = jnp.cos(ang), jnp.sin(ang)
    eye = jnp.eye(FFT_GROUP, dtype=F32)

    def kron(m):
        return jnp.kron(m, eye)

    f1 = jnp.arange(n1, dtype=jnp.int32)[:, None, None]
    f2 = jnp.arange(FFT_MINOR, dtype=jnp.int32)[None, :, None]
    n2 = jnp.arange(FFT_MINOR, dtype=jnp.int32)[None, None, :]
    ang = two_pi * ((n2 * (f1 + n1 * f2)) % n).astype(F32) / n
    gr, gi = jnp.cos(ang), -jnp.sin(ang)
    return dict(
        n1=n1, nb=nb,
        m_fwd_full=jnp.concatenate([kron(cos1), kron(-sin1)], axis=0).astype(BF16),
        m_fwd_half=jnp.concatenate([kron(cos1[:, :nb]), kron(-sin1[:, :nb])], axis=0).astype(BF16),
        m_inv=jnp.concatenate([kron(cos1[:nb]), kron(sin1[:nb])], axis=0).astype(BF16),
        g=jnp.concatenate([gr, gi], axis=1).astype(BF16),
        gt=jnp.concatenate([gr.transpose(0, 2, 1), gi.transpose(0, 2, 1)], axis=1).astype(BF16),
    )


HALO = 8


def _hyena_in_kernel(x_ref, xp_ref, xn_ref, g_ref, sh_ref, sc_ref, w0_ref, w1_ref, w2_ref, b_ref, cw_ref,
                     cb_ref, v_ref, x0_ref, a_ref, ah_ref, *, seq_len):
    i, j = pl.program_id(0), pl.program_id(1)
    tm = x_ref.shape[0]

    @pl.when(j == 0)
    def _():
        g, sh, sc = g_ref[...], sh_ref[...], sc_ref[...]
        a_ref[...] = _norm_mod(x_ref[...], g, sh, sc).astype(BF16)
        ah_ref[0:HALO, :] = _norm_mod(xp_ref[...], g, sh, sc)
        ah_ref[HALO:, :] = _norm_mod(xn_ref[...], g, sh, sc)

    first = (i * tm) % seq_len == 0
    last = ((i + 1) * tm) % seq_len == 0
    a = a_ref[...]
    ah = ah_ref[...].astype(BF16)
    row = lax.broadcasted_iota(jnp.int32, (tm, v_ref.shape[1]), 0)

    def section(k, w_ref):
        w = w_ref[...]
        p = jnp.dot(a, w, preferred_element_type=F32) + b_ref[k]
        ph = jnp.dot(ah, w, preferred_element_type=F32) + b_ref[k]
        above = jnp.where(first, 0.0, ph[HALO - 1:HALO])
        below = jnp.where(last, 0.0, ph[HALO:HALO + 1])
        cw = cw_ref[k]
        prev = jnp.where(row == 0, above, pltpu.roll(p, 1, 0))
        nxt = jnp.where(row == tm - 1, below, pltpu.roll(p, tm - 1, 0))
        return prev * cw[0:1] + p * cw[1:2] + nxt * cw[2:3] + cb_ref[k]

    x0 = section(0, w0_ref)
    x1 = section(1, w1_ref)
    v = section(2, w2_ref)
    v_ref[...] = (v * x1).astype(v_ref.dtype)
    x0_ref[...] = x0.astype(x0_ref.dtype)


def hyena_in_proj(x, g, mods, k_shift, k_scale, group_rows, group_base, seq_len, w, b, conv_w, conv_b,
                  tm=1024, tc=512):
    R, D = x.shape
    tm = _tile(math.gcd(group_rows, seq_len), tm, HALO)
    tc = _tile(D, tc)
    nc = D // tc
    cw = conv_w.reshape(HYENA_SHORT, 3, D).transpose(1, 0, 2)

    def grp(i):
        return group_base + (i * tm) // group_rows

    def wspec(k):
        return pl.BlockSpec((D, tc), lambda i, j: (0, k * nc + j))

    mod = lambda k: pl.BlockSpec((None, None, 1, D), lambda i, j: (k, grp(i), 0, 0))
    per_proj = lambda rows: pl.BlockSpec((3, rows, tc), lambda i, j: (0, 0, j))
    kern = functools.partial(_hyena_in_kernel, seq_len=seq_len)
    return pl.pallas_call(
        kern,
        grid=(R // tm, nc),
        in_specs=[pl.BlockSpec((tm, D), lambda i, j: (i, 0)),
                  pl.BlockSpec((HALO, D), lambda i, j: (jnp.maximum(i * (tm // HALO) - 1, 0), 0)),
                  pl.BlockSpec((HALO, D), lambda i, j: (jnp.minimum((i + 1) * (tm // HALO), R // HALO - 1), 0)),
                  pl.BlockSpec((1, D), lambda i, j: (0, 0)), mod(k_shift), mod(k_scale),
                  wspec(0), wspec(1), wspec(2), per_proj(1), per_proj(HYENA_SHORT), per_proj(1)],
        out_specs=[pl.BlockSpec((tm, tc), lambda i, j: (i, j))] * 2,
        out_shape=[jax.ShapeDtypeStruct((R, D), BF16)] * 2,
        scratch_shapes=[pltpu.VMEM((tm, D), BF16), pltpu.VMEM((2 * HALO, D), F32)],
        compiler_params=_cparams("parallel", "arbitrary"),
        name="hyena_in_proj",
    )(x, x, x, g.reshape(1, D), mods, mods, w, w, w, b.reshape(3, 1, D), cw, conv_b.reshape(3, 1, D))


def _filter_kernel(emb_ref, w1_ref, b1_ref, f1_ref, w2_ref, b2_ref, f2_ref, w3_ref, b3_ref, f3_ref,
                   wo_ref, dl_ref, k_ref, norm_ref, *, L):
    i = pl.program_id(0)
    tr = emb_ref.shape[0]
    hp = lax.Precision.HIGHEST
    emb = emb_ref[...]
    h = jnp.sin(f1_ref[...] * (jnp.dot(emb, w1_ref[...], precision=hp, preferred_element_type=F32) + b1_ref[...]))
    h = jnp.sin(f2_ref[...] * (jnp.dot(h, w2_ref[...], precision=hp, preferred_element_type=F32) + b2_ref[...]))
    h = jnp.sin(f3_ref[...] * (jnp.dot(h, w3_ref[...], precision=hp, preferred_element_type=F32) + b3_ref[...]))
    k = jnp.dot(h, wo_ref[...], precision=hp, preferred_element_type=F32)
    k = k * jnp.exp(-emb[:, 0:1] * dl_ref[...])
    row = i * tr + lax.broadcasted_iota(jnp.int32, (tr, 1), 0)
    k = jnp.where(row == L, 0.0, k)
    k_ref[...] = k

    @pl.when(i == 0)
    def _():
        norm_ref[...] = jnp.zeros_like(norm_ref)

    norm_ref[...] += jnp.sum(jnp.abs(k), axis=0, keepdims=True)


def hyena_filter(L, D, w1, b1, f1, w2, b2, f2, w3, b3, f3, w_out, tr=512):
    n = 2 * L
    tr = _tile(L, tr, 8)
    P = 128
    bands = (HYENA_EMB - 1) // 2
    d = np.arange(n)
    d = np.where(d <= L, d, n - d).clip(0, L - 1)
    t01 = np.linspace(0.0, 1.0, L)[d]
    wang = 2 * math.pi * d / L
    fr = np.linspace(1e-4, bands - 1, bands)
    emb = np.zeros((n, P), np.float32)
    emb[:, 0] = t01
    emb[:, 1:1 + bands] = np.cos(fr[None, :] * wang[:, None])
    emb[:, 1 + bands:1 + 2 * bands] = -np.sin(fr[None, :] * wang[:, None])

    def padw(w):
        return jnp.zeros((P, P), F32).at[:w.shape[0], :w.shape[1]].set(w)

    def padv(v):
        return jnp.zeros((1, P), F32).at[0, :v.shape[0]].set(v)

    wo = jnp.zeros((P, 2 * D), F32).at[:w_out.shape[0]].set(w_out)
    max_decay = math.log(HYENA_TARGET) / HYENA_FAST_DECAY
    min_decay = math.log(HYENA_TARGET) / HYENA_SLOW_DECAY
    deltas = jnp.abs(jnp.linspace(min_decay, max_decay, D, dtype=F32)).reshape(1, D)
    small = pl.BlockSpec((P, P), lambda i: (0, 0))
    vec = pl.BlockSpec((1, P), lambda i: (0, 0))
    kern = functools.partial(_filter_kernel, L=L)
    return pl.pallas_call(
        kern,
        grid=(n // tr,),
        in_specs=[pl.BlockSpec((tr, P), lambda i: (i, 0)),
                  small, vec, vec, small, vec, vec, small, vec, vec,
                  pl.BlockSpec((P, D), lambda i: (0, (i * tr) // L)),
                  pl.BlockSpec((1, D), lambda i: (0, 0))],
        out_specs=[pl.BlockSpec((tr, D), lambda i: (i, 0)), pl.BlockSpec((1, D), lambda i: (0, 0))],
        out_shape=[jax.ShapeDtypeStruct((n, D), F32), jax.ShapeDtypeStruct((1, D), F32)],
        compiler_params=_cparams("arbitrary"),
        name="hyena_filter",
    )(jnp.asarray(emb), padw(w1), padv(b1), padv(f1), padw(w2), padv(b2), padv(f2),
      padw(w3), padv(b3), padv(f3), wo, deltas)


def _fwd_major_kernel(*refs, n1, nb, has_imag):
    if has_imag:
        zr_ref, zi_ref, m_ref, ar_ref, ai_ref = refs
    else:
        zr_ref, m_ref, ar_ref, ai_ref = refs
    m = m_ref[...]
    half = n1 * FFT_GROUP

    def body(g, carry):
        def rows(blk):
            return pl.ds(pl.multiple_of(blk * FFT_MINOR + g * FFT_GROUP, FFT_GROUP), FFT_GROUP)

        def slab(ref):
            return jnp.concatenate([ref[rows(blk), :] for blk in range(nb)], axis=0).astype(BF16)

        pr = jnp.dot(m, slab(zr_ref), preferred_element_type=F32)
        if has_imag:
            pi = jnp.dot(m, slab(zi_ref), preferred_element_type=F32)
            ar, ai = pr[:half] - pi[half:], pi[:half] + pr[half:]
        else:
            ar, ai = pr[:half], pr[half:]
        for f1 in range(n1):
            sl = slice(f1 * FFT_GROUP, (f1 + 1) * FFT_GROUP)
            ar_ref[rows(f1), :] = ar[sl].astype(BF16)
            ai_ref[rows(f1), :] = ai[sl].astype(BF16)
        return carry

    lax.fori_loop(0, FFT_MINOR // FFT_GROUP, body, 0)


def fwd_major(z, tabs, pairs, tc=256):
    n1, nb = tabs["n1"], tabs["nb"]
    n = n1 * FFT_MINOR
    Bz, Lz, D = z.shape
    tc = _tile(D, tc)
    if pairs:
        P = Bz // 2
        m = tabs["m_fwd_half"]
        ins = [z, z, m]
        in_specs = [pl.BlockSpec((None, Lz, tc), lambda p, c: (2 * p, 0, c)),
                    pl.BlockSpec((None, Lz, tc), lambda p, c: (2 * p + 1, 0, c))]
        nblk = nb
    else:
        P = 1
        m = tabs["m_fwd_full"]
        ins = [z, m]
        in_specs = [pl.BlockSpec((None, Lz, tc), lambda p, c: (0, 0, c))]
        nblk = n1
    in_specs.append(pl.BlockSpec(m.shape, lambda p, c: (0, 0)))
    kern = functools.partial(_fwd_major_kernel, n1=n1, nb=nblk, has_imag=pairs)
    return pl.pallas_call(
        kern,
        grid=(P, D // tc),
        in_specs=in_specs,
        out_specs=[pl.BlockSpec((None, n, tc), lambda p, c: (p, 0, c))] * 2,
        out_shape=[jax.ShapeDtypeStruct((P, n, D), BF16)] * 2,
        compiler_params=_cparams("parallel", "parallel"),
        name="hyena_fwd_major",
    )(*ins)


def _spectrum_kernel(ar_ref, ai_ref, g_ref, s_ref, kr_ref, ki_ref):
    g = g_ref[...]
    h = FFT_MINOR
    pr = jnp.dot(g, ar_ref[...], preferred_element_type=F32)
    pi = jnp.dot(g, ai_ref[...], preferred_element_type=F32)
    s = s_ref[...]
    kr_ref[...] = (pr[:h] - pi[h:]) * s
    ki_ref[...] = (pi[:h] + pr[h:]) * s


def filter_spectrum(ar, ai, tabs, scale, tc=1024):
    _, n, D = ar.shape
    tc = _tile(D, tc)
    blk = pl.BlockSpec((None, FFT_MINOR, tc), lambda f, c: (0, f, c))
    out = pl.BlockSpec((FFT_MINOR, tc), lambda f, c: (f, c))
    return pl.pallas_call(
        _spectrum_kernel,
        grid=(tabs["n1"], D // tc),
        in_specs=[blk, blk, pl.BlockSpec((None, 2 * FFT_MINOR, FFT_MINOR), lambda f, c: (f, 0, 0)),
                  pl.BlockSpec((1, tc), lambda f, c: (0, c))],
        out_specs=[out, out],
        out_shape=[jax.ShapeDtypeStruct((n, D), F32)] * 2,
        compiler_params=_cparams("parallel", "parallel"),
        name="hyena_filter_spectrum",
    )(ar, ai, tabs["g"], scale)


def _minor_kernel(ar_ref, ai_ref, g_ref, gt_ref, kr_ref, ki_ref, br_ref, bi_ref):
    h = FFT_MINOR
    g = g_ref[...]
    pr = jnp.dot(g, ar_ref[...], preferred_element_type=F32)
    pi = jnp.dot(g, ai_ref[...], preferred_element_type=F32)
    xr, xi = pr[:h] - pi[h:], pi[:h] + pr[h:]
    kr, ki = kr_ref[...], ki_ref[...]
    yr = (xr * kr - xi * ki).astype(BF16)
    yi = (xr * ki + xi * kr).astype(BF16)
    gt = gt_ref[...]
    qr = jnp.dot(gt, yr, preferred_element_type=F32)
    qi = jnp.dot(gt, yi, preferred_element_type=F32)
    br_ref[...] = (qr[:h] + qi[h:]).astype(BF16)
    bi_ref[...] = (qi[:h] - qr[h:]).astype(BF16)


def minor_conv(ar, ai, tabs, kr, ki, tc=1024):
    P, n, D = ar.shape
    tc = _tile(D, tc)
    blk = pl.BlockSpec((None, FFT_MINOR, tc), lambda f, c, p: (p, f, c))
    tab = pl.BlockSpec((None, 2 * FFT_MINOR, FFT_MINOR), lambda f, c, p: (f, 0, 0))
    kblk = pl.BlockSpec((FFT_MINOR, tc), lambda f, c, p: (f, c))
    return pl.pallas_call(
        _minor_kernel,
        grid=(tabs["n1"], D // tc, P),
        in_specs=[blk, blk, tab, tab, kblk, kblk],
        out_specs=[blk, blk],
        out_shape=[jax.ShapeDtypeStruct((P, n, D), BF16)] * 2,
        compiler_params=_cparams("parallel", "parallel", "parallel"),
        name="hyena_minor_conv",
    )(ar, ai, tabs["g"], tabs["gt"], kr, ki)


def _inv_major_kernel(br_ref, bi_ref, m_ref, v_ref, x0_ref, bias_ref, o_ref, *, n1, nb):
    m = m_ref[...]
    half = nb * FFT_GROUP
    bias = bias_ref[...]

    def body(g, carry):
        def rows(blk):
            return pl.ds(pl.multiple_of(blk * FFT_MINOR + g * FFT_GROUP, FFT_GROUP), FFT_GROUP)

        def slab(ref):
            return jnp.concatenate([ref[rows(f1), :] for f1 in range(n1)], axis=0)

        pr = jnp.dot(m, slab(br_ref), preferred_element_type=F32)
        pi = jnp.dot(m, slab(bi_ref), preferred_element_type=F32)
        ys = (pr[:half] - pi[half:], pi[:half] + pr[half:])
        for s in range(2):
            for blk in range(nb):
                y = ys[s][blk * FFT_GROUP:(blk + 1) * FFT_GROUP]
                v = v_ref[s, rows(blk), :].astype(F32)
                x0 = x0_ref[s, rows(blk), :].astype(F32)
                o_ref[s, rows(blk), :] = ((y + v * bias) * x0).astype(o_ref.dtype)
        return carry

    lax.fori_loop(0, FFT_MINOR // FFT_GROUP, body, 0)


def inv_major(br, bi, tabs, v, x0, bias, tc=256):
    n1, nb = tabs["n1"], tabs["nb"]
    P, n, D = br.shape
    B, L, _ = v.shape
    tc = _tile(D, tc)
    m = tabs["m_inv"]
    blk = pl.BlockSpec((None, n, tc), lambda p, c: (p, 0, c))
    pair = pl.BlockSpec((2, L, tc), lambda p, c: (p, 0, c))
    kern = functools.partial(_inv_major_kernel, n1=n1, nb=nb)
    return pl.pallas_call(
        kern,
        grid=(P, D // tc),
        in_specs=[blk, blk, pl.BlockSpec(m.shape, lambda p, c: (0, 0)), pair, pair,
                  pl.BlockSpec((1, tc), lambda p, c: (0, c))],
        out_specs=pair,
        out_shape=jax.ShapeDtypeStruct((B, L, D), BF16),
        compiler_params=_cparams("parallel", "parallel"),
        name="hyena_inv_major",
    )(br, bi, m, v, x0, bias.reshape(1, D))


def hyena_mix(v, x0, B, L, D, filt, bias):
    assert B % 2 == 0 and L % FFT_MINOR == 0
    tabs = _dft_tables(L)
    v, x0 = v.reshape(B, L, D), x0.reshape(B, L, D)
    k_raw, k_norm = hyena_filter(L, D, *filt)
    kar, kai = fwd_major(k_raw[None], tabs, pairs=False)
    kr, ki = filter_spectrum(kar, kai, tabs, 1.0 / (2 * L * k_norm))
    ar, ai = fwd_major(v, tabs, pairs=True)
    br, bi = minor_conv(ar, ai, tabs, kr, ki)
    z = inv_major(br, bi, tabs, v, x0, bias)
    return z.reshape(B * L, D)


def kernel(x, c, ctx, c_ctx, ada_w, ada_b, norm1_g, norm2_g, na_w_qkv, na_w_o, na_q_g, na_k_g, na_rpb, hy_w_in, hy_b_in, hy_conv_w, hy_conv_b, hy_f_w1, hy_f_b1, hy_f_freq1, hy_f_w2, hy_f_b2, hy_f_freq2, hy_f_w3, hy_f_b3, hy_f_freq3, hy_f_wout, hy_bias, hy_w_out, hy_b_out, moe_router, moe_w1, moe_w3, moe_w2):
    B, N, D = x.shape
    CTX = ctx.shape[1]
    depth = ada_w.shape[0]
    mixer = [i % N_MIXERS for i in range(depth)]

    cond = jnp.concatenate([c, c_ctx[None, :], jnp.zeros((8 - B - 1, D), F32)], axis=0)
    mods_all = ada_all(cond, ada_w, ada_b)
    mods_all = mods_all.reshape(depth, 8, 6, 1, D).transpose(0, 2, 1, 3, 4)

    h = x.reshape(B * N, D)
    hc = ctx.reshape(B * CTX, D)
    zeros_d = jnp.zeros((D,), F32)
    w1, w3, w2 = moe_w1, moe_w3, moe_w2
    for i in range(depth):
        j = i // N_MIXERS
        ctx_stream = any(mixer[l] == 0 for l in range(i + 1, depth))
        ctx_in = ctx_stream or mixer[i] == 0
        mods = mods_all[i]
        if mixer[i] == 0:
            wqkv = na_w_qkv[j].astype(BF16)
            wo = na_w_o[j].astype(BF16)
            hg = jnp.stack([na_q_g[j], na_k_g[j]]).reshape(2, 1, HEAD_DIM)
            zeros_e = jnp.zeros((3 * D,), F32)
            qs = HEAD_DIM ** -0.5
            qkv = nm_matmul(h, norm1_g[i], mods, 0, 1, N, 0, wqkv, zeros_e, hg, qs)
            qkv_c = nm_matmul(hc, norm1_g[i], mods, 0, 1, B * CTX, B, wqkv, zeros_e, hg, qs)
            bias = _na_bias_tables(na_rpb[j], N // GRID_W)
            o, oc = na_attention(qkv, qkv_c, bias, B, N, CTX, D)
            h = matmul_residual(o, wo, zeros_d, mods, 2, N, 0, h)
            if ctx_stream:
                hc = matmul_residual(oc, wo, zeros_d, mods, 2, B * CTX, B, hc)
        else:
            win = hy_w_in[j].astype(BF16)
            wout = hy_w_out[j].astype(BF16)
            filt = (hy_f_w1[j], hy_f_b1[j], hy_f_freq1[j], hy_f_w2[j], hy_f_b2[j], hy_f_freq2[j],
                    hy_f_w3[j], hy_f_b3[j], hy_f_freq3[j], hy_f_wout[j])
            v, x0 = hyena_in_proj(h, norm1_g[i], mods, 0, 1, N, 0, N, win, hy_b_in[j],
                                  hy_conv_w[j], hy_conv_b[j])
            z = hyena_mix(v, x0, B, N, D, filt, hy_bias[j])
            h = matmul_residual(z, wout, hy_b_out[j], mods, 2, N, 0, h)
            if ctx_stream:
                vc, x0c = hyena_in_proj(hc, norm1_g[i], mods, 0, 1, B * CTX, B, CTX, win, hy_b_in[j],
                                        hy_conv_w[j], hy_conv_b[j])
                zc = hyena_mix(vc, x0c, B, CTX, D, filt, hy_bias[j])
                hc = matmul_residual(zc, wout, hy_b_out[j], mods, 2, B * CTX, B, hc)
        h = ec_moe(h, norm2_g[i], mods, 3, 4, 5, N, 0, B, moe_router[i], i, w1, w3, w2)
        if ctx_stream:
            hc = ec_moe(hc, norm2_g[i], mods, 3, 4, 5, B * CTX, B, B, moe_router[i], i, w1, w3, w2)
    return h.reshape(B, N, D)
```

```python
import functools
import math

import jax
import jax.numpy as jnp
import numpy as np
from jax import lax
from jax.experimental import pallas as pl
from jax.experimental.pallas import tpu as pltpu

F32 = jnp.float32
BF16 = jnp.bfloat16

GRID_W = 64
N_MIXERS = 2
NORM_EPS = 1e-6
NEG_INF = -1e30
HEAD_DIM = 128
LANES = 128
WIN_ROWS = 8
WIN_COLS = 16
HYENA_SHORT = 3
HYENA_EMB = 33
HYENA_FAST_DECAY = 0.3
HYENA_SLOW_DECAY = 1.5
HYENA_TARGET = 1e-2
N_EXPERTS = 16
CAPACITY_FACTOR = 2

Q_ROWS = 4
BAND_ROWS = 12

VMEM_LIMIT = 56 * 1024 * 1024


def _cparams(*sem):
    return pltpu.CompilerParams(dimension_semantics=sem, vmem_limit_bytes=VMEM_LIMIT)


def _tile(n, want, unit=128):
    if n <= want:
        return n
    t = (want // unit) * unit
    while n % t:
        t -= unit
    return t


def _ada_kernel(c_ref, w_ref, b_ref, o_ref):
    c = c_ref[...]
    s = c * jax.nn.sigmoid(c)
    o_ref[...] = jnp.dot(s.astype(BF16), w_ref[...].astype(BF16),
                         preferred_element_type=F32) + b_ref[...]


def ada_all(cond, ada_w, ada_b, tn=1024):
    L, D, E = ada_w.shape
    tn = _tile(E, tn)
    return pl.pallas_call(
        _ada_kernel,
        grid=(L, E // tn),
        in_specs=[
            pl.BlockSpec((8, D), lambda l, j: (0, 0)),
            pl.BlockSpec((None, D, tn), lambda l, j: (l, 0, j)),
            pl.BlockSpec((None, 1, tn), lambda l, j: (l, 0, j)),
        ],
        out_specs=pl.BlockSpec((None, 8, tn), lambda l, j: (l, 0, j)),
        out_shape=jax.ShapeDtypeStruct((L, 8, E), F32),
        compiler_params=_cparams("parallel", "parallel"),
        name="ada",
    )(cond, ada_w, ada_b.reshape(L, 1, E))


def _norm_mod(x, g, sh, sc):
    y = x * lax.rsqrt(jnp.mean(x * x, axis=-1, keepdims=True) + NORM_EPS) * g
    return y * (1.0 + sc) + sh


def _nm_matmul_kernel(x_ref, g_ref, sh_ref, sc_ref, w_ref, b_ref, hg_ref, o_ref, a_ref, *,
                      n_norm_sections, section_cols, q_scale):
    j = pl.program_id(1)

    @pl.when(j == 0)
    def _():
        a_ref[...] = _norm_mod(x_ref[...], g_ref[...], sh_ref[...], sc_ref[...]).astype(BF16)

    acc = jnp.dot(a_ref[...], w_ref[...], preferred_element_type=F32) + b_ref[...]
    tn = acc.shape[1]
    if n_norm_sections == 0:
        o_ref[...] = acc.astype(o_ref.dtype)
        return
    sec = (j * tn) // section_cols

    @pl.when(sec < n_norm_sections)
    def _():
        hg = hg_ref[...]
        mult = jnp.where(sec == 0, q_scale, 1.0).astype(F32)
        for h in range(tn // HEAD_DIM):
            c = acc[:, h * HEAD_DIM:(h + 1) * HEAD_DIM]
            c = c * lax.rsqrt(jnp.mean(c * c, axis=-1, keepdims=True) + NORM_EPS) * hg
            o_ref[:, h * HEAD_DIM:(h + 1) * HEAD_DIM] = (c * mult).astype(o_ref.dtype)

    @pl.when(sec >= n_norm_sections)
    def _():
        o_ref[...] = acc.astype(o_ref.dtype)


def nm_matmul(x, g, mods, k_shift, k_scale, group_rows, group_base, w, b, head_g=None,
              q_scale=1.0, out_dtype=BF16, tm=1024, tn=1024):
    R, D = x.shape
    E = w.shape[1]
    tm = _tile(group_rows, tm, 8)
    tn = _tile(D, tn)
    n_norm = 0 if head_g is None else 2
    if head_g is None:
        head_g = jnp.ones((2, 1, HEAD_DIM), F32)

    def grp(i):
        return group_base + (i * tm) // group_rows

    def hg_map(i, j):
        return (jnp.minimum((j * tn) // D, 1), 0, 0)

    kern = functools.partial(_nm_matmul_kernel, n_norm_sections=n_norm, section_cols=D,
                             q_scale=q_scale)
    return pl.pallas_call(
        kern,
        grid=(R // tm, E // tn),
        in_specs=[
            pl.BlockSpec((tm, D), lambda i, j: (i, 0)),
            pl.BlockSpec((1, D), lambda i, j: (0, 0)),
            pl.BlockSpec((None, None, 1, D), lambda i, j: (k_shift, grp(i), 0, 0)),
            pl.BlockSpec((None, None, 1, D), lambda i, j: (k_scale, grp(i), 0, 0)),
            pl.BlockSpec((D, tn), lambda i, j: (0, j)),
            pl.BlockSpec((1, tn), lambda i, j: (0, j)),
            pl.BlockSpec((None, 1, HEAD_DIM), hg_map),
        ],
        out_specs=pl.BlockSpec((tm, tn), lambda i, j: (i, j)),
        out_shape=jax.ShapeDtypeStruct((R, E), out_dtype),
        scratch_shapes=[pltpu.VMEM((tm, D), BF16)],
        compiler_params=_cparams("parallel", "arbitrary"),
        name="nm_matmul",
    )(x, g.reshape(1, D), mods, mods, w, b.reshape(1, E), head_g)


def _mm_res_kernel(x_ref, w_ref, b_ref, gate_ref, res_ref, o_ref):
    y = jnp.dot(x_ref[...], w_ref[...], preferred_element_type=F32) + b_ref[...]
    o_ref[...] = res_ref[...] + gate_ref[...] * y


def matmul_residual(x, w, b, mods, k_gate, group_rows, group_base, res, tm=1024, tn=1024):
    R, K = x.shape
    E = w.shape[1]
    tm = _tile(group_rows, tm, 8)
    tn = _tile(E, tn)

    def grp(i):
        return group_base + (i * tm) // group_rows

    return pl.pallas_call(
        _mm_res_kernel,
        grid=(R // tm, E // tn),
        in_specs=[
            pl.BlockSpec((tm, K), lambda i, j: (i, 0)),
            pl.BlockSpec((K, tn), lambda i, j: (0, j)),
            pl.BlockSpec((1, tn), lambda i, j: (0, j)),
            pl.BlockSpec((None, None, 1, tn), lambda i, j: (k_gate, grp(i), 0, j)),
            pl.BlockSpec((tm, tn), lambda i, j: (i, j)),
        ],
        out_specs=pl.BlockSpec((tm, tn), lambda i, j: (i, j)),
        out_shape=jax.ShapeDtypeStruct((R, E), F32),
        compiler_params=_cparams("parallel", "parallel"),
        name="matmul_residual",
    )(x, w, b.reshape(1, E), mods, res)


def _na_bias_tables(rpb, rows):
    W = GRID_W
    cols = np.arange(W)
    cs = np.clip(cols - WIN_COLS // 2, 0, W - WIN_COLS)
    in_win = (cols[None, :] >= cs[:, None]) & (cols[None, :] < cs[:, None] + WIN_COLS)
    dc = np.clip(cols[None, :] - cols[:, None] + WIN_COLS - 1, 0, 2 * WIN_COLS - 2)
    n_dr, n_dc = 2 * WIN_ROWS - 1, 2 * WIN_COLS - 1
    onehot_r = np.zeros((3, Q_ROWS, BAND_ROWS, n_dr), np.float32)
    mask = np.zeros((3, Q_ROWS, W, BAND_ROWS, W), bool)
    for var, r0 in enumerate((0, Q_ROWS, rows - Q_ROWS)):
        bs = _band_start(r0, rows)
        for i in range(Q_ROWS):
            r = r0 + i
            rs = int(np.clip(r - WIN_ROWS // 2, 0, rows - WIN_ROWS))
            for jj in range(BAND_ROWS):
                kr = bs + jj
                if rs <= kr < rs + WIN_ROWS:
                    onehot_r[var, i, jj, kr - r + WIN_ROWS - 1] = 1.0
                    mask[var, i, :, jj, :] = in_win
    onehot_c = (dc[:, :, None] == np.arange(n_dc)).astype(np.float32)
    hp = lax.Precision.HIGHEST
    t = jnp.einsum('hrc,vijr->hvijc', rpb.astype(F32), onehot_r, precision=hp)
    t = jnp.einsum('hvijc,qkc->hviqjk', t, onehot_c, precision=hp)
    t = jnp.where(mask[None], t, NEG_INF)
    return t.reshape(rpb.shape[0], 3, Q_ROWS * W, BAND_ROWS * W)


def _band_start(r0, rows):
    return int(np.clip(r0 - WIN_ROWS // 2, 0, rows - BAND_ROWS))


def _na_kernel(q_ref, k_ref, v_ref, qc_ref, kc_ref, vc_ref, bias_ref, o_ref, oc_ref, *, rows):
    W = GRID_W
    nblk = rows // Q_ROWS
    kc = kc_ref[...]
    vc = vc_ref[...]
    nt = (((1,), (1,)), ((), ()))

    def attend(q, parts):
        ss = [lax.dot_general(q, k, nt, preferred_element_type=F32) + (0.0 if bias is None else bias)
              for k, _, bias in parts]
        m = functools.reduce(jnp.maximum, [jnp.max(s, axis=-1, keepdims=True) for s in ss])
        ps = [jnp.exp(s - m) for s in ss]
        den = sum(jnp.sum(p, axis=-1, keepdims=True) for p in ps)
        o = sum(jnp.dot(p.astype(BF16), v, preferred_element_type=F32)
                for p, (_, v, _) in zip(ps, parts))
        return o / den

    def body(blk, carry):
        r0 = blk * Q_ROWS
        bs = jnp.clip(r0 - WIN_ROWS // 2, 0, rows - BAND_ROWS)
        var = jnp.where(blk == 0, 0, jnp.where(blk == nblk - 1, 2, 1))
        q0 = pl.multiple_of(r0 * W, Q_ROWS * W)
        k0 = pl.multiple_of(bs * W, W)
        q = q_ref[pl.ds(q0, Q_ROWS * W), :]
        kb = k_ref[pl.ds(k0, BAND_ROWS * W), :]
        vb = v_ref[pl.ds(k0, BAND_ROWS * W), :]
        o = attend(q, [(kb, vb, bias_ref[var]), (kc, vc, None)])
        o_ref[pl.ds(q0, Q_ROWS * W), :] = o.astype(o_ref.dtype)
        return carry

    lax.fori_loop(0, nblk, body, 0, unroll=2)
    oc_ref[...] = attend(qc_ref[...], [(kc, vc, None)]).astype(oc_ref.dtype)


def na_attention(qkv, qkv_c, bias, B, N, CTX, D):
    H = D // HEAD_DIM
    rows = N // GRID_W
    kern = functools.partial(_na_kernel, rows=rows)

    def spec(n, sec):
        return pl.BlockSpec((n, HEAD_DIM), lambda h, b: (b, sec * H + h))

    return pl.pallas_call(
        kern,
        grid=(H, B),
        in_specs=[spec(N, 0), spec(N, 1), spec(N, 2), spec(CTX, 0), spec(CTX, 1), spec(CTX, 2),
                  pl.BlockSpec((None, 3, Q_ROWS * GRID_W, BAND_ROWS * GRID_W),
                               lambda h, b: (h, 0, 0, 0))],
        out_specs=[pl.BlockSpec((N, HEAD_DIM), lambda h, b: (b, h)),
                   pl.BlockSpec((CTX, HEAD_DIM), lambda h, b: (b, h))],
        out_shape=[jax.ShapeDtypeStruct((B * N, D), BF16),
                   jax.ShapeDtypeStruct((B * CTX, D), BF16)],
        compiler_params=_cparams("parallel", "parallel"),
        name="na_attention",
    )(qkv, qkv, qkv, qkv_c, qkv_c, qkv_c, bias)


def _expert_kernel(idx_ref, a_hbm, w1_ref, w3_ref, w2_ref, gate_ref, o_ref, xbuf, xs, yacc, sem, *,
                   n_tok, cap, bg):
    e, g, f = pl.program_id(0), pl.program_id(1), pl.program_id(2)
    n_exp, ng, nf = pl.num_programs(0), pl.num_programs(1), pl.num_programs(2)
    step = e * ng + g
    slot = step % 2
    nsteps = n_exp * ng
    rows = bg * cap
    S = xbuf.shape[1] // rows
    part = rows // nf

    def copy_row(ee, gg, sl, r0, u=0, priority=0):
        unit = math.gcd(part, cap)
        b = gg * bg + (r0 + (u // unit) * unit) // cap
        c = (r0 + (u // unit) * unit) % cap + u % unit
        row = b * n_tok + idx_ref[(b * n_exp + ee) * cap + c]
        src = pl.ds(pl.multiple_of(row * S, S), S)
        dst = pl.ds(pl.multiple_of((r0 + u) * S, S), S)
        pltpu.make_async_copy(a_hbm.at[src, :], xbuf.at[sl, dst, :], sem.at[sl]).start(priority=priority)

    def wait_rows(sl):
        pltpu.make_async_copy(a_hbm.at[pl.ds(0, rows * S), :], xbuf.at[sl], sem.at[sl]).wait()

    @pl.when((step == 0) & (f == 0))
    def _():
        def body(r, carry):
            copy_row(e, g, slot, r)
            return carry

        lax.fori_loop(0, rows, body, 0, unroll=8)

    @pl.when(f == 0)
    def _():
        wait_rows(slot)
        hi, lo = _load_token_tiles(xbuf, (slot,), 0, rows, S)
        xs[...] = jnp.concatenate([p.astype(BF16) for p in hi + lo], axis=1)
        yacc[...] = jnp.zeros_like(yacc)

    nxt = (step + 1) % nsteps
    for u in range(part):
        copy_row(nxt // ng, nxt % ng, 1 - slot, f * part, u, priority=u % 2)

    x = xs[...]
    h1 = jnp.dot(x, w1_ref[...].astype(BF16), preferred_element_type=F32)
    h3 = jnp.dot(x, w3_ref[...].astype(BF16), preferred_element_type=F32)
    hid = (h1 * jax.nn.sigmoid(h1) * h3).astype(BF16)
    w2 = w2_ref[...].astype(BF16)
    D = w2.shape[1]
    cw = min(D, EXPERT_OUT_COLS)
    for c in range(D // cw):
        cols = slice(c * cw, (c + 1) * cw)
        yacc[:, cols] += jnp.dot(hid, w2[:, cols], preferred_element_type=F32)

    @pl.when(f == nf - 1)
    def _():
        packed = _pack_bf16_pair(yacc[...] * gate_ref[...].reshape(rows, 1))
        for i in range(bg):
            _store_token_tiles(o_ref, (i,), packed[i * cap:(i + 1) * cap])

    @pl.when((step == nsteps - 1) & (f == nf - 1))
    def _():
        wait_rows(1 - slot)


def _pack_bf16_pair(y):
    half = y.shape[1] // 2
    hi = pltpu.bitcast(y[:, :half].astype(BF16).astype(F32), jnp.uint32)
    lo = pltpu.bitcast(y[:, half:].astype(BF16).astype(F32), jnp.uint32)
    return hi | (lo >> 16)


def _unpack_bf16_pair(u):
    hi = pltpu.bitcast(u & jnp.uint32(0xFFFF0000), F32)
    lo = pltpu.bitcast(u << 16, F32)
    return hi, lo


def _store_token_tiles(ref, lead, packed):
    rows, S = packed.shape[0], packed.shape[1] // LANES
    for s in range(S):
        ref[(*lead, pl.ds(s, rows, stride=S), slice(None))] = packed[:, s * LANES:(s + 1) * LANES]


def _load_token_tiles(ref, lead, first, rows, S):
    pieces = [_unpack_bf16_pair(ref[(*lead, pl.ds(first * S + s, rows, stride=S), slice(None))])
              for s in range(S)]
    return [p[0] for p in pieces], [p[1] for p in pieces]


def expert_ffn(a, idx_flat, layer, w1, w3, w2, gate, B, n_tok):
    _, E, D, Fh = w1.shape
    C = gate.shape[2]
    S = D // (2 * LANES)
    bg = max(1, min(B, EXPERT_ROWS // C))
    while B % bg:
        bg -= 1
    fc = _tile(Fh, EXPERT_HIDDEN_CHUNK)
    assert (bg * C) % (Fh // fc) == 0
    kern = functools.partial(_expert_kernel, n_tok=n_tok, cap=C, bg=bg)
    grid_spec = pltpu.PrefetchScalarGridSpec(
        num_scalar_prefetch=1,
        grid=(E, B // bg, Fh // fc),
        in_specs=[
            pl.BlockSpec(memory_space=pl.ANY),
            pl.BlockSpec((None, None, D, fc), lambda e, g, f, idx: (layer, e, 0, f)),
            pl.BlockSpec((None, None, D, fc), lambda e, g, f, idx: (layer, e, 0, f)),
            pl.BlockSpec((None, None, fc, D), lambda e, g, f, idx: (layer, e, f, 0)),
            pl.BlockSpec((bg, None, C, 1), lambda e, g, f, idx: (g, e, 0, 0)),
        ],
        out_specs=pl.BlockSpec((bg, None, C * S, LANES), lambda e, g, f, idx: (g, e, 0, 0)),
        scratch_shapes=[pltpu.VMEM((2, bg * C * S, LANES), jnp.uint32), pltpu.VMEM((bg * C, D), BF16),
                        pltpu.VMEM((bg * C, D), F32), pltpu.SemaphoreType.DMA((2,))],
    )
    return pl.pallas_call(
        kern,
        grid_spec=grid_spec,
        out_shape=jax.ShapeDtypeStruct((B, E, C * S, LANES), jnp.uint32),
        compiler_params=_cparams("arbitrary", "arbitrary", "arbitrary"),
        name="expert_ffn",
    )(idx_flat, a, w1, w3, w2, gate)


def _moe_prep_kernel(x_ref, g_ref, sh_ref, sc_ref, r_ref, a_ref, aff_ref):
    a = _norm_mod(x_ref[...], g_ref[...], sh_ref[...], sc_ref[...])
    _store_token_tiles(a_ref, (), _pack_bf16_pair(a))
    logits = lax.dot_general(r_ref[...], a.astype(BF16), (((1,), (1,)), ((), ())),
                             preferred_element_type=F32)
    m = jnp.max(logits, axis=0, keepdims=True)
    p = jnp.exp(logits - m)
    aff_ref[...] = p / jnp.sum(p, axis=0, keepdims=True)


def moe_prep(x, g, mods, k_shift, k_scale, group_rows, group_base, router_t, tm=512):
    R, D = x.shape
    E = router_t.shape[0]
    S = D // (2 * LANES)
    tm = _tile(group_rows, tm, 128)

    def grp(i):
        return group_base + (i * tm) // group_rows

    return pl.pallas_call(
        _moe_prep_kernel,
        grid=(R // tm,),
        in_specs=[
            pl.BlockSpec((tm, D), lambda i: (i, 0)),
            pl.BlockSpec((1, D), lambda i: (0, 0)),
            pl.BlockSpec((None, None, 1, D), lambda i: (k_shift, grp(i), 0, 0)),
            pl.BlockSpec((None, None, 1, D), lambda i: (k_scale, grp(i), 0, 0)),
            pl.BlockSpec((E, D), lambda i: (0, 0)),
        ],
        out_specs=[pl.BlockSpec((tm * S, LANES), lambda i: (i, 0)),
                   pl.BlockSpec((E, tm), lambda i: (0, i))],
        out_shape=[jax.ShapeDtypeStruct((R * S, LANES), jnp.uint32), jax.ShapeDtypeStruct((E, R), F32)],
        compiler_params=_cparams("parallel"),
        name="moe_prep",
    )(x, g.reshape(1, D), mods, mods, router_t)


EXPERT_ROWS = 1024
EXPERT_HIDDEN_CHUNK = 256
EXPERT_OUT_COLS = 512
COMBINE_TOKENS = 128
HIT_UNROLL = 4
COMBINE_ROW_GROUPS = 2
CUMSUM_CHUNK = 256
ROUTE_BLOCK_LANES = 512
TILE_TABLE_LANES = 128


def _route_select_kernel(aff_ref, tri_ref, ltri_ref, lt_ref, in_ref,
                         posi_ref, key_ref, slab_ref, p0_ref, kmax_ref, *, cap):
    E, N = aff_ref.shape
    aff = aff_ref[...]
    bits = pltpu.bitcast(aff, jnp.int32)
    cur = jnp.zeros((E, 1), jnp.int32)
    for bit in range(30, -1, -1):
        cand = cur | (1 << bit)
        cnt = jnp.sum(jnp.where(bits >= cand, 1.0, 0.0), axis=1, keepdims=True)
        cur = jnp.where(cnt >= cap, cand, cur)
    gt = bits > cur
    eq = bits == cur
    need = cap - jnp.sum(jnp.where(gt, 1.0, 0.0), axis=1, keepdims=True)

    def cumsum_incl(x):
        tri = tri_ref[...]
        w = min(CUMSUM_CHUNK, N)
        off = jnp.zeros((E, 1), F32)
        outs = []
        for c in range(N // w):
            xc = x[:, c * w:(c + 1) * w]
            outs.append(jnp.dot(xc.astype(BF16), tri[:w, :w], preferred_element_type=F32) + off)
            off = off + jnp.sum(xc, axis=1, keepdims=True)
        return jnp.concatenate(outs, axis=1) if len(outs) > 1 else outs[0]

    eqf = jnp.where(eq, 1.0, 0.0)
    tie_rank = cumsum_incl(eqf) - eqf
    sel = gt | (eq & (tie_rank < need))
    self_ = jnp.where(sel, 1.0, 0.0)
    pos_incl = cumsum_incl(self_)
    posi_ref[...] = pos_incl
    key_ref[...] = jnp.where(sel, pos_incl - self_, -1.0)
    selb = self_.astype(BF16)
    erank = jnp.dot(ltri_ref[...], selb, preferred_element_type=F32)
    in_tile = lax.broadcasted_iota(jnp.int32, (E, N), 1) & (COMBINE_TOKENS - 1)
    slab_ref[...] = erank * COMBINE_TOKENS + in_tile.astype(F32)
    p0_ref[...] = jnp.dot(selb, lt_ref[...], preferred_element_type=F32).astype(jnp.int32)
    count = jnp.sum(self_, axis=0, keepdims=True)
    kk = lax.broadcasted_iota(jnp.int32, (E, 1), 0).astype(F32)
    over = jnp.where(count > kk, 1.0, 0.0).astype(BF16)
    per_tile = jnp.dot(over, in_ref[...], preferred_element_type=F32)
    kmax_ref[...] = jnp.sum(jnp.where(per_tile > 0.0, 1.0, 0.0), axis=0, keepdims=True).astype(jnp.int32)


def _route_compact_kernel(p0_ref, aff_ref, posi_ref, key_ref, slab_ref, idx_ref, gate_ref, rank_ref, *,
                          cap, tn, lanes, width):
    b = pl.program_id(0)
    E, N = aff_ref.shape
    ch = min(LANES, cap)
    tpb = width // tn
    diag = lax.broadcasted_iota(jnp.int32, (ch, ch), 0) == lax.broadcasted_iota(jnp.int32, (ch, ch), 1)

    def as_row(col):
        return jnp.sum(jnp.where(diag, col, 0.0), axis=0, keepdims=True)

    def e_body(e, carry):
        q = (b * E + e) * lanes

        def c_body(cc, tiles):
            c0 = cc * ch
            t_lo = lax.while_loop(lambda t: p0_ref[q + t + 1] <= c0, lambda t: t + 1, tiles[0])
            t_hi = lax.while_loop(lambda t: p0_ref[q + t] < c0 + ch, lambda t: t + 1, tiles[1])
            j_lo = t_lo // tpb
            j_hi = (t_hi + tpb - 1) // tpb
            ccol = (c0 + lax.broadcasted_iota(jnp.int32, (ch, 1), 0)).astype(F32)

            def j_body(j, acc):
                row = pl.ds(e, 1)
                blk = pl.ds(pl.multiple_of(j * width, width), width)
                pi, ky, af, sl = posi_ref[row, blk], key_ref[row, blk], aff_ref[row, blk], slab_ref[row, blk]
                acc = list(acc)
                for u in range(width // LANES):
                    part = slice(u * LANES, (u + 1) * LANES)
                    hit = ky[:, part] == ccol
                    acc[0] = acc[0] + jnp.where(pi[:, part] <= ccol, 1.0, 0.0)
                    acc[1] = acc[1] + jnp.where(hit, af[:, part], 0.0)
                    acc[2] = acc[2] + jnp.where(hit, sl[:, part], 0.0)
                return tuple(acc)

            zero = jnp.zeros((ch, LANES), F32)
            iv, gv, rv = [jnp.sum(a, axis=1, keepdims=True)
                          for a in lax.fori_loop(j_lo, j_hi, j_body, (zero, zero, zero))]
            idx_ref[pl.ds(e, 1), pl.ds(c0, ch)] = as_row(iv).astype(jnp.int32) + j_lo * width
            gate_ref[e, pl.ds(c0, ch), :] = gv
            rank_ref[pl.ds(e, 1), pl.ds(c0, ch)] = as_row(rv).astype(jnp.int32)
            return (t_lo, t_hi)

        tiles = (0, 0)
        for cc in range(cap // ch):
            tiles = c_body(cc, tiles)
        return carry

    for e in range(E):
        e_body(e, 0)


def moe_route(aff_t, B, N, cap):
    E = aff_t.shape[0]
    tn = COMBINE_TOKENS
    lanes = TILE_TABLE_LANES
    assert N // tn + 1 <= lanes and N % tn == 0 and N % min(CUMSUM_CHUNK, N) == 0
    w = min(CUMSUM_CHUNK, N)
    tri = (np.arange(w)[:, None] <= np.arange(w)[None, :]).astype(np.float32)
    ltri = (np.arange(E)[None, :] < np.arange(E)[:, None]).astype(np.float32)
    tok = np.arange(N)[:, None]
    tile = np.arange(lanes)[None, :]
    before = (tok < tile * tn).astype(np.float32)
    inside = (tok // tn == tile).astype(np.float32)
    const = lambda shape: pl.BlockSpec(shape, lambda b: (0,) * len(shape))
    per_tok = pl.BlockSpec((None, E, N), lambda b, *_: (b, 0, 0))
    posi, key, slab, p0, kmax = pl.pallas_call(
        functools.partial(_route_select_kernel, cap=cap),
        grid=(B,),
        in_specs=[pl.BlockSpec((E, N), lambda b: (0, b)), const((w, w)), const((E, E)),
                  const((N, lanes)), const((N, lanes))],
        out_specs=[per_tok, per_tok, per_tok,
                   pl.BlockSpec((None, E, lanes), lambda b: (b, 0, 0)),
                   pl.BlockSpec((None, 1, lanes), lambda b: (b, 0, 0))],
        out_shape=[jax.ShapeDtypeStruct((B, E, N), F32)] * 3 + [
            jax.ShapeDtypeStruct((B, E, lanes), jnp.int32), jax.ShapeDtypeStruct((B, 1, lanes), jnp.int32)],
        compiler_params=_cparams("parallel"),
        name="moe_route_select",
    )(aff_t, jnp.asarray(tri, BF16), jnp.asarray(ltri, BF16), jnp.asarray(before, BF16),
      jnp.asarray(inside, BF16))
    width = min(ROUTE_BLOCK_LANES, N)
    sel3 = pl.BlockSpec((None, E, cap, 1), lambda b, *_: (b, 0, 0, 0))
    rows2 = pl.BlockSpec((None, E, cap), lambda b, *_: (b, 0, 0))
    idx, gate, rank = pl.pallas_call(
        functools.partial(_route_compact_kernel, cap=cap, tn=tn, lanes=lanes, width=width),
        grid_spec=pltpu.PrefetchScalarGridSpec(
            num_scalar_prefetch=1,
            grid=(B,),
            in_specs=[pl.BlockSpec((E, N), lambda b, *_: (0, b)), per_tok, per_tok, per_tok],
            out_specs=[rows2, sel3, rows2],
        ),
        out_shape=[jax.ShapeDtypeStruct((B, E, cap), jnp.int32),
                   jax.ShapeDtypeStruct((B, E, cap, 1), F32),
                   jax.ShapeDtypeStruct((B, E, cap), jnp.int32)],
        compiler_params=_cparams("parallel"),
        name="moe_route_compact",
    )(p0.reshape(-1), aff_t, posi, key, slab)
    return idx, gate, rank, p0, kmax


def _combine_kernel(rank_ref, p0_ref, kmax_ref, ys_hbm, h_ref, g2_ref, o_ref, rbuf, sem, *,
                    n_exp, cap, tn, lanes):
    b, t = pl.program_id(0), pl.program_id(1)
    nt = pl.num_programs(1)
    step = b * nt + t
    slot = step % 2
    S = rbuf.shape[1] // (n_exp * tn)

    def tile_hits(bb, tt):
        def body(e, tot):
            q = (bb * n_exp + e) * lanes + tt
            return tot + p0_ref[q + 1] - p0_ref[q]
        return lax.fori_loop(0, n_exp, body, 0)

    def fetch(bb, tt, sl):
        km = kmax_ref[bb * lanes + tt]
        for k in range(n_exp):
            @pl.when(k < km)
            def _():
                rbuf[sl, k * tn * S:(k + 1) * tn * S, :] = jnp.zeros((tn * S, LANES), jnp.uint32)

        def copy_row(row, priority=0):
            src = pl.ds(pl.multiple_of(row * S, S), S)
            dst = pl.ds(pl.multiple_of(rank_ref[row] * S, S), S)
            pltpu.make_async_copy(ys_hbm.at[src, :], rbuf.at[sl, dst, :], sem.at[sl]).start(priority=priority)

        def e_body(e, carry):
            q = (bb * n_exp + e) * lanes + tt
            lo = (bb * n_exp + e) * cap + p0_ref[q]
            n = p0_ref[q + 1] - p0_ref[q]
            groups = n // HIT_UNROLL

            def g_body(g, carry2):
                for u in range(HIT_UNROLL):
                    copy_row(lo + g * HIT_UNROLL + u, priority=u % 2)
                return carry2

            def r_body(c, carry2):
                copy_row(lo + c)
                return carry2

            lax.fori_loop(0, groups, g_body, 0)
            lax.fori_loop(groups * HIT_UNROLL, n, r_body, 0)
            return carry

        lax.fori_loop(0, n_exp, e_body, 0)

    @pl.when(step == 0)
    def _():
        fetch(b, t, slot)

    @pl.when(step + 1 < pl.num_programs(0) * nt)
    def _():
        nxt = step + 1
        fetch(nxt // nt, nxt % nt, 1 - slot)

    hits = tile_hits(b, t)

    nrows = hits * S
    bulk = pl.multiple_of((nrows // 8) * 8, 8)

    @pl.when(bulk > 0)
    def _():
        pltpu.make_async_copy(ys_hbm.at[pl.ds(0, bulk), :], rbuf.at[slot, pl.ds(0, bulk), :],
                              sem.at[slot]).wait()

    def wait_row(i, carry):
        pltpu.make_async_copy(ys_hbm.at[pl.ds(0, 1), :], rbuf.at[slot, pl.ds(0, 1), :], sem.at[slot]).wait()
        return carry

    lax.fori_loop(0, nrows - bulk, wait_row, 0)

    km = kmax_ref[b * lanes + t]
    g2 = g2_ref[...]
    rg = COMBINE_ROW_GROUPS
    for r in range(0, tn // 8, rg):
        def k_body(k, tot):
            new = []
            for i in range(rg):
                hi, lo = _load_token_tiles(rbuf, (slot,), k * tn + (r + i) * 8, 8, S)
                new += hi + lo
            return tuple(t_ + p for t_, p in zip(tot, new))

        tot = lax.fori_loop(0, km, k_body, (jnp.zeros((8, LANES), F32),) * (2 * S * rg))
        for i in range(rg):
            rows = slice((r + i) * 8, (r + i + 1) * 8)
            for s in range(2 * S):
                cols = slice(s * LANES, (s + 1) * LANES)
                o_ref[rows, cols] = h_ref[rows, cols] + g2[:, cols] * tot[i * 2 * S + s]


def moe_combine(ys, h, mods, k_gate, group_rows, group_base, rank, p0, kmax, B, N):
    R, D = h.shape
    E, cap = rank.shape[1], rank.shape[2]
    tn = COMBINE_TOKENS
    lanes = TILE_TABLE_LANES
    nt = N // tn

    def grp(b, t):
        return group_base + ((b * nt + t) * tn) // group_rows

    kern = functools.partial(_combine_kernel, n_exp=E, cap=cap, tn=tn, lanes=lanes)
    grid_spec = pltpu.PrefetchScalarGridSpec(
        num_scalar_prefetch=3,
        grid=(B, nt),
        in_specs=[
            pl.BlockSpec(memory_space=pl.ANY),
            pl.BlockSpec((tn, D), lambda b, t, *_: (b * nt + t, 0)),
            pl.BlockSpec((None, None, 1, D), lambda b, t, *_: (k_gate, grp(b, t), 0, 0)),
        ],
        out_specs=pl.BlockSpec((tn, D), lambda b, t, *_: (b * nt + t, 0)),
        scratch_shapes=[pltpu.VMEM((2, E * tn * (D // (2 * LANES)), LANES), jnp.uint32),
                        pltpu.SemaphoreType.DMA((2,))],
    )
    return pl.pallas_call(
        kern,
        grid_spec=grid_spec,
        out_shape=jax.ShapeDtypeStruct((R, D), F32),
        compiler_params=_cparams("arbitrary", "arbitrary"),
        name="moe_combine",
    )(rank.reshape(-1), p0.reshape(-1), kmax.reshape(-1), ys, h, mods)


def ec_moe(h, g, mods, k_shift, k_scale, k_gate, group_rows, group_base, B, router, layer, w1, w3, w2):
    R, D = h.shape
    N = R // B
    E = router.shape[1]
    cap = CAPACITY_FACTOR * N // E
    a, aff_t = moe_prep(h, g, mods, k_shift, k_scale, group_rows, group_base, router.T.astype(BF16))
    idx, gate, rank, p0, kmax = moe_route(aff_t, B, N, cap)
    ys = expert_ffn(a, idx.reshape(-1), layer, w1, w3, w2, gate, B, N)
    return moe_combine(ys.reshape(-1, LANES), h, mods, k_gate, group_rows, group_base,
                       rank, p0, kmax, B, N)


FFT_MINOR = 256
FFT_GROUP = 16


@functools.lru_cache(maxsize=None)
def _dft_tables(L):
    n = 2 * L
    n1 = n // FFT_MINOR
    nb = n1 // 2
    two_pi = 2.0 * math.pi
    a = np.arange(n1)
    ang = two_pi * ((a[:, None] * a[None, :]) % n1) / n1
    cos1, sin1 = np.cos(ang), np.sin(ang)
    eye = np.eye(FFT_GROUP)

    def kron(m):
        return np.kron(m, eye)

    def const(m):
        return jnp.asarray(m.astype(np.float32).astype(BF16))

    f1 = np.arange(n1)[:, None, None]
    f2 = np.arange(FFT_MINOR)[None, :, None]
    n2 = np.arange(FFT_MINOR)[None, None, :]
    ang = two_pi * ((n2 * (f1 + n1 * f2)) % n) / n
    gr, gi = np.cos(ang), -np.sin(ang)
    return dict(
        n1=n1, nb=nb,
        m_fwd_full=const(np.concatenate([kron(cos1), kron(-sin1)], axis=0)),
        m_fwd_half=const(np.concatenate([kron(cos1[:, :nb]), kron(-sin1[:, :nb])], axis=0)),
        m_inv=const(np.concatenate([kron(cos1[:nb]), kron(sin1[:nb])], axis=0)),
        g=const(np.concatenate([gr, gi], axis=1)),
        gt=const(np.concatenate([gr.transpose(0, 2, 1), gi.transpose(0, 2, 1)], axis=1)),
    )


HALO = 8


def _hyena_in_kernel(x_ref, xp_ref, xn_ref, g_ref, sh_ref, sc_ref, w0_ref, w1_ref, w2_ref, b_ref, cw_ref,
                     cb_ref, v_ref, x0_ref, a_ref, ah_ref, *, seq_len):
    i, j = pl.program_id(0), pl.program_id(1)
    tm = x_ref.shape[0]

    @pl.when(j == 0)
    def _():
        g, sh, sc = g_ref[...], sh_ref[...], sc_ref[...]
        a_ref[...] = _norm_mod(x_ref[...], g, sh, sc).astype(BF16)
        ah_ref[0:HALO, :] = _norm_mod(xp_ref[...], g, sh, sc)
        ah_ref[HALO:, :] = _norm_mod(xn_ref[...], g, sh, sc)

    first = (i * tm) % seq_len == 0
    last = ((i + 1) * tm) % seq_len == 0
    a = a_ref[...]
    ah = ah_ref[...].astype(BF16)
    row = lax.broadcasted_iota(jnp.int32, (tm, v_ref.shape[1]), 0)

    def section(k, w_ref):
        w = w_ref[...]
        p = jnp.dot(a, w, preferred_element_type=F32) + b_ref[k]
        ph = jnp.dot(ah, w, preferred_element_type=F32) + b_ref[k]
        above = jnp.where(first, 0.0, ph[HALO - 1:HALO])
        below = jnp.where(last, 0.0, ph[HALO:HALO + 1])
        cw = cw_ref[k]
        prev = jnp.where(row == 0, above, pltpu.roll(p, 1, 0))
        nxt = jnp.where(row == tm - 1, below, pltpu.roll(p, tm - 1, 0))
        return prev * cw[0:1] + p * cw[1:2] + nxt * cw[2:3] + cb_ref[k]

    x0 = section(0, w0_ref)
    x1 = section(1, w1_ref)
    v = section(2, w2_ref)
    v_ref[...] = (v * x1).astype(v_ref.dtype)
    x0_ref[...] = x0.astype(x0_ref.dtype)


def hyena_in_proj(x, g, mods, k_shift, k_scale, group_rows, group_base, seq_len, w, b, conv_w, conv_b,
                  tm=1024, tc=512):
    R, D = x.shape
    tm = _tile(math.gcd(group_rows, seq_len), tm, HALO)
    tc = _tile(D, tc)
    nc = D // tc
    cw = conv_w.reshape(HYENA_SHORT, 3, D).transpose(1, 0, 2)

    def grp(i):
        return group_base + (i * tm) // group_rows

    def wspec(k):
        return pl.BlockSpec((D, tc), lambda i, j: (0, k * nc + j))

    mod = lambda k: pl.BlockSpec((None, None, 1, D), lambda i, j: (k, grp(i), 0, 0))
    per_proj = lambda rows: pl.BlockSpec((3, rows, tc), lambda i, j: (0, 0, j))
    kern = functools.partial(_hyena_in_kernel, seq_len=seq_len)
    return pl.pallas_call(
        kern,
        grid=(R // tm, nc),
        in_specs=[pl.BlockSpec((tm, D), lambda i, j: (i, 0)),
                  pl.BlockSpec((HALO, D), lambda i, j: (jnp.maximum(i * (tm // HALO) - 1, 0), 0)),
                  pl.BlockSpec((HALO, D), lambda i, j: (jnp.minimum((i + 1) * (tm // HALO), R // HALO - 1), 0)),
                  pl.BlockSpec((1, D), lambda i, j: (0, 0)), mod(k_shift), mod(k_scale),
                  wspec(0), wspec(1), wspec(2), per_proj(1), per_proj(HYENA_SHORT), per_proj(1)],
        out_specs=[pl.BlockSpec((tm, tc), lambda i, j: (i, j))] * 2,
        out_shape=[jax.ShapeDtypeStruct((R, D), BF16)] * 2,
        scratch_shapes=[pltpu.VMEM((tm, D), BF16), pltpu.VMEM((2 * HALO, D), F32)],
        compiler_params=_cparams("parallel", "arbitrary"),
        name="hyena_in_proj",
    )(x, x, x, g.reshape(1, D), mods, mods, w, w, w, b.reshape(3, 1, D), cw, conv_b.reshape(3, 1, D))


def _filter_kernel(emb_ref, w1_ref, b1_ref, f1_ref, w2_ref, b2_ref, f2_ref, w3_ref, b3_ref, f3_ref,
                   wo_ref, dl_ref, k_ref, norm_ref, *, L):
    i = pl.program_id(0)
    tr = emb_ref.shape[0]
    hp = lax.Precision.HIGHEST
    emb = emb_ref[...]
    h = jnp.sin(f1_ref[...] * (jnp.dot(emb, w1_ref[...], precision=hp, preferred_element_type=F32) + b1_ref[...]))
    h = jnp.sin(f2_ref[...] * (jnp.dot(h, w2_ref[...], precision=hp, preferred_element_type=F32) + b2_ref[...]))
    h = jnp.sin(f3_ref[...] * (jnp.dot(h, w3_ref[...], precision=hp, preferred_element_type=F32) + b3_ref[...]))
    k = jnp.dot(h, wo_ref[...], precision=hp, preferred_element_type=F32)
    k = k * jnp.exp(-emb[:, 0:1] * dl_ref[...])
    row = i * tr + lax.broadcasted_iota(jnp.int32, (tr, 1), 0)
    k = jnp.where(row == L, 0.0, k)
    k_ref[...] = k

    @pl.when(i == 0)
    def _():
        norm_ref[...] = jnp.zeros_like(norm_ref)

    norm_ref[...] += jnp.sum(jnp.abs(k), axis=0, keepdims=True)


def hyena_filter(L, D, w1, b1, f1, w2, b2, f2, w3, b3, f3, w_out, tr=512):
    n = 2 * L
    tr = _tile(L, tr, 8)
    P = 128
    bands = (HYENA_EMB - 1) // 2
    d = np.arange(n)
    d = np.where(d <= L, d, n - d).clip(0, L - 1)
    t01 = np.linspace(0.0, 1.0, L)[d]
    wang = 2 * math.pi * d / L
    fr = np.linspace(1e-4, bands - 1, bands)
    emb = np.zeros((n, P), np.float32)
    emb[:, 0] = t01
    emb[:, 1:1 + bands] = np.cos(fr[None, :] * wang[:, None])
    emb[:, 1 + bands:1 + 2 * bands] = -np.sin(fr[None, :] * wang[:, None])

    def padw(w):
        return jnp.zeros((P, P), F32).at[:w.shape[0], :w.shape[1]].set(w)

    def padv(v):
        return jnp.zeros((1, P), F32).at[0, :v.shape[0]].set(v)

    wo = jnp.zeros((P, 2 * D), F32).at[:w_out.shape[0]].set(w_out)
    max_decay = math.log(HYENA_TARGET) / HYENA_FAST_DECAY
    min_decay = math.log(HYENA_TARGET) / HYENA_SLOW_DECAY
    deltas = jnp.abs(jnp.linspace(min_decay, max_decay, D, dtype=F32)).reshape(1, D)
    small = pl.BlockSpec((P, P), lambda i: (0, 0))
    vec = pl.BlockSpec((1, P), lambda i: (0, 0))
    kern = functools.partial(_filter_kernel, L=L)
    return pl.pallas_call(
        kern,
        grid=(n // tr,),
        in_specs=[pl.BlockSpec((tr, P), lambda i: (i, 0)),
                  small, vec, vec, small, vec, vec, small, vec, vec,
                  pl.BlockSpec((P, D), lambda i: (0, (i * tr) // L)),
                  pl.BlockSpec((1, D), lambda i: (0, 0))],
        out_specs=[pl.BlockSpec((tr, D), lambda i: (i, 0)), pl.BlockSpec((1, D), lambda i: (0, 0))],
        out_shape=[jax.ShapeDtypeStruct((n, D), F32), jax.ShapeDtypeStruct((1, D), F32)],
        compiler_params=_cparams("arbitrary"),
        name="hyena_filter",
    )(jnp.asarray(emb), padw(w1), padv(b1), padv(f1), padw(w2), padv(b2), padv(f2),
      padw(w3), padv(b3), padv(f3), wo, deltas)


def _fwd_major_kernel(*refs, n1, nb, has_imag):
    if has_imag:
        zr_ref, zi_ref, m_ref, ar_ref, ai_ref = refs
    else:
        zr_ref, m_ref, ar_ref, ai_ref = refs
    m = m_ref[...]
    half = n1 * FFT_GROUP

    def body(g, carry):
        def rows(blk):
            return pl.ds(pl.multiple_of(blk * FFT_MINOR + g * FFT_GROUP, FFT_GROUP), FFT_GROUP)

        def slab(ref):
            return jnp.concatenate([ref[rows(blk), :] for blk in range(nb)], axis=0).astype(BF16)

        pr = jnp.dot(m, slab(zr_ref), preferred_element_type=F32)
        if has_imag:
            pi = jnp.dot(m, slab(zi_ref), preferred_element_type=F32)
            ar, ai = pr[:half] - pi[half:], pi[:half] + pr[half:]
        else:
            ar, ai = pr[:half], pr[half:]
        for f1 in range(n1):
            sl = slice(f1 * FFT_GROUP, (f1 + 1) * FFT_GROUP)
            ar_ref[rows(f1), :] = ar[sl].astype(BF16)
            ai_ref[rows(f1), :] = ai[sl].astype(BF16)
        return carry

    lax.fori_loop(0, FFT_MINOR // FFT_GROUP, body, 0)


def fwd_major(z, tabs, pairs, tc=256):
    n1, nb = tabs["n1"], tabs["nb"]
    n = n1 * FFT_MINOR
    Bz, Lz, D = z.shape
    tc = _tile(D, tc)
    if pairs:
        P = Bz // 2
        m = tabs["m_fwd_half"]
        ins = [z, z, m]
        in_specs = [pl.BlockSpec((None, Lz, tc), lambda p, c: (2 * p, 0, c)),
                    pl.BlockSpec((None, Lz, tc), lambda p, c: (2 * p + 1, 0, c))]
        nblk = nb
    else:
        P = 1
        m = tabs["m_fwd_full"]
        ins = [z, m]
        in_specs = [pl.BlockSpec((None, Lz, tc), lambda p, c: (0, 0, c))]
        nblk = n1
    in_specs.append(pl.BlockSpec(m.shape, lambda p, c: (0, 0)))
    kern = functools.partial(_fwd_major_kernel, n1=n1, nb=nblk, has_imag=pairs)
    return pl.pallas_call(
        kern,
        grid=(P, D // tc),
        in_specs=in_specs,
        out_specs=[pl.BlockSpec((None, n, tc), lambda p, c: (p, 0, c))] * 2,
        out_shape=[jax.ShapeDtypeStruct((P, n, D), BF16)] * 2,
        compiler_params=_cparams("parallel", "parallel"),
        name="hyena_fwd_major",
    )(*ins)


def _spectrum_kernel(ar_ref, ai_ref, g_ref, s_ref, kr_ref, ki_ref):
    g = g_ref[...]
    h = FFT_MINOR
    pr = jnp.dot(g, ar_ref[...], preferred_element_type=F32)
    pi = jnp.dot(g, ai_ref[...], preferred_element_type=F32)
    s = s_ref[...]
    kr_ref[...] = (pr[:h] - pi[h:]) * s
    ki_ref[...] = (pi[:h] + pr[h:]) * s


def filter_spectrum(ar, ai, tabs, scale, tc=1024):
    _, n, D = ar.shape
    tc = _tile(D, tc)
    blk = pl.BlockSpec((None, FFT_MINOR, tc), lambda f, c: (0, f, c))
    out = pl.BlockSpec((FFT_MINOR, tc), lambda f, c: (f, c))
    return pl.pallas_call(
        _spectrum_kernel,
        grid=(tabs["n1"], D // tc),
        in_specs=[blk, blk, pl.BlockSpec((None, 2 * FFT_MINOR, FFT_MINOR), lambda f, c: (f, 0, 0)),
                  pl.BlockSpec((1, tc), lambda f, c: (0, c))],
        out_specs=[out, out],
        out_shape=[jax.ShapeDtypeStruct((n, D), F32)] * 2,
        compiler_params=_cparams("parallel", "parallel"),
        name="hyena_filter_spectrum",
    )(ar, ai, tabs["g"], scale)


def _minor_kernel(ar_ref, ai_ref, g_ref, gt_ref, kr_ref, ki_ref, br_ref, bi_ref):
    h = FFT_MINOR
    g = g_ref[...]
    pr = jnp.dot(g, ar_ref[...], preferred_element_type=F32)
    pi = jnp.dot(g, ai_ref[...], preferred_element_type=F32)
    xr, xi = pr[:h] - pi[h:], pi[:h] + pr[h:]
    kr, ki = kr_ref[...], ki_ref[...]
    yr = (xr * kr - xi * ki).astype(BF16)
    yi = (xr * ki + xi * kr).astype(BF16)
    gt = gt_ref[...]
    qr = jnp.dot(gt, yr, preferred_element_type=F32)
    qi = jnp.dot(gt, yi, preferred_element_type=F32)
    br_ref[...] = (qr[:h] + qi[h:]).astype(BF16)
    bi_ref[...] = (qi[:h] - qr[h:]).astype(BF16)


def minor_conv(ar, ai, tabs, kr, ki, tc=1024):
    P, n, D = ar.shape
    tc = _tile(D, tc)
    blk = pl.BlockSpec((None, FFT_MINOR, tc), lambda f, c, p: (p, f, c))
    tab = pl.BlockSpec((None, 2 * FFT_MINOR, FFT_MINOR), lambda f, c, p: (f, 0, 0))
    kblk = pl.BlockSpec((FFT_MINOR, tc), lambda f, c, p: (f, c))
    return pl.pallas_call(
        _minor_kernel,
        grid=(tabs["n1"], D // tc, P),
        in_specs=[blk, blk, tab, tab, kblk, kblk],
        out_specs=[blk, blk],
        out_shape=[jax.ShapeDtypeStruct((P, n, D), BF16)] * 2,
        compiler_params=_cparams("parallel", "parallel", "parallel"),
        name="hyena_minor_conv",
    )(ar, ai, tabs["g"], tabs["gt"], kr, ki)


def _inv_major_kernel(br_ref, bi_ref, m_ref, v_ref, x0_ref, bias_ref, o_ref, *, n1, nb):
    m = m_ref[...]
    half = nb * FFT_GROUP
    bias = bias_ref[...]

    def body(g, carry):
        def rows(blk):
            return pl.ds(pl.multiple_of(blk * FFT_MINOR + g * FFT_GROUP, FFT_GROUP), FFT_GROUP)

        def slab(ref):
            return jnp.concatenate([ref[rows(f1), :] for f1 in range(n1)], axis=0)

        pr = jnp.dot(m, slab(br_ref), preferred_element_type=F32)
        pi = jnp.dot(m, slab(bi_ref), preferred_element_type=F32)
        ys = (pr[:half] - pi[half:], pi[:half] + pr[half:])
        for s in range(2):
            for blk in range(nb):
                y = ys[s][blk * FFT_GROUP:(blk + 1) * FFT_GROUP]
                v = v_ref[s, rows(blk), :].astype(F32)
                x0 = x0_ref[s, rows(blk), :].astype(F32)
                o_ref[s, rows(blk), :] = ((y + v * bias) * x0).astype(o_ref.dtype)
        return carry

    lax.fori_loop(0, FFT_MINOR // FFT_GROUP, body, 0)


def inv_major(br, bi, tabs, v, x0, bias, tc=256):
    n1, nb = tabs["n1"], tabs["nb"]
    P, n, D = br.shape
    B, L, _ = v.shape
    tc = _tile(D, tc)
    m = tabs["m_inv"]
    blk = pl.BlockSpec((None, n, tc), lambda p, c: (p, 0, c))
    pair = pl.BlockSpec((2, L, tc), lambda p, c: (p, 0, c))
    kern = functools.partial(_inv_major_kernel, n1=n1, nb=nb)
    return pl.pallas_call(
        kern,
        grid=(P, D // tc),
        in_specs=[blk, blk, pl.BlockSpec(m.shape, lambda p, c: (0, 0)), pair, pair,
                  pl.BlockSpec((1, tc), lambda p, c: (0, c))],
        out_specs=pair,
        out_shape=jax.ShapeDtypeStruct((B, L, D), BF16),
        compiler_params=_cparams("parallel", "parallel"),
        name="hyena_inv_major",
    )(br, bi, m, v, x0, bias.reshape(1, D))


def hyena_mix(v, x0, B, L, D, filt, bias):
    assert B % 2 == 0 and L % FFT_MINOR == 0
    tabs = _dft_tables(L)
    v, x0 = v.reshape(B, L, D), x0.reshape(B, L, D)
    k_raw, k_norm = hyena_filter(L, D, *filt)
    kar, kai = fwd_major(k_raw[None], tabs, pairs=False)
    kr, ki = filter_spectrum(kar, kai, tabs, 1.0 / (2 * L * k_norm))
    ar, ai = fwd_major(v, tabs, pairs=True)
    br, bi = minor_conv(ar, ai, tabs, kr, ki)
    z = inv_major(br, bi, tabs, v, x0, bias)
    return z.reshape(B * L, D)


def kernel(x, c, ctx, c_ctx, ada_w, ada_b, norm1_g, norm2_g, na_w_qkv, na_w_o, na_q_g, na_k_g, na_rpb, hy_w_in, hy_b_in, hy_conv_w, hy_conv_b, hy_f_w1, hy_f_b1, hy_f_freq1, hy_f_w2, hy_f_b2, hy_f_freq2, hy_f_w3, hy_f_b3, hy_f_freq3, hy_f_wout, hy_bias, hy_w_out, hy_b_out, moe_router, moe_w1, moe_w3, moe_w2):
    B, N, D = x.shape
    CTX = ctx.shape[1]
    depth = ada_w.shape[0]
    mixer = [i % N_MIXERS for i in range(depth)]

    cond = jnp.concatenate([c, c_ctx[None, :], jnp.zeros((8 - B - 1, D), F32)], axis=0)
    mods_all = ada_all(cond, ada_w, ada_b)
    mods_all = mods_all.reshape(depth, 8, 6, 1, D).transpose(0, 2, 1, 3, 4)

    h = x.reshape(B * N, D)
    hc = ctx.reshape(B * CTX, D)
    zeros_d = jnp.zeros((D,), F32)
    w1, w3, w2 = moe_w1, moe_w3, moe_w2
    for i in range(depth):
        j = i // N_MIXERS
        ctx_stream = any(mixer[l] == 0 for l in range(i + 1, depth))
        ctx_in = ctx_stream or mixer[i] == 0
        mods = mods_all[i]
        if mixer[i] == 0:
            wqkv = na_w_qkv[j].astype(BF16)
            wo = na_w_o[j].astype(BF16)
            hg = jnp.stack([na_q_g[j], na_k_g[j]]).reshape(2, 1, HEAD_DIM)
            zeros_e = jnp.zeros((3 * D,), F32)
            qs = HEAD_DIM ** -0.5
            qkv = nm_matmul(h, norm1_g[i], mods, 0, 1, N, 0, wqkv, zeros_e, hg, qs)
            qkv_c = nm_matmul(hc, norm1_g[i], mods, 0, 1, B * CTX, B, wqkv, zeros_e, hg, qs)
            bias = _na_bias_tables(na_rpb[j], N // GRID_W)
            o, oc = na_attention(qkv, qkv_c, bias, B, N, CTX, D)
            h = matmul_residual(o, wo, zeros_d, mods, 2, N, 0, h)
            if ctx_stream:
                hc = matmul_residual(oc, wo, zeros_d, mods, 2, B * CTX, B, hc)
        else:
            win = hy_w_in[j].astype(BF16)
            wout = hy_w_out[j].astype(BF16)
            filt = (hy_f_w1[j], hy_f_b1[j], hy_f_freq1[j], hy_f_w2[j], hy_f_b2[j], hy_f_freq2[j],
                    hy_f_w3[j], hy_f_b3[j], hy_f_freq3[j], hy_f_wout[j])
            v, x0 = hyena_in_proj(h, norm1_g[i], mods, 0, 1, N, 0, N, win, hy_b_in[j],
                                  hy_conv_w[j], hy_conv_b[j])
            z = hyena_mix(v, x0, B, N, D, filt, hy_bias[j])
            h = matmul_residual(z, wout, hy_b_out[j], mods, 2, N, 0, h)
            if ctx_stream:
                vc, x0c = hyena_in_proj(hc, norm1_g[i], mods, 0, 1, B * CTX, B, CTX, win, hy_b_in[j],
                                        hy_conv_w[j], hy_conv_b[j])
                zc = hyena_mix(vc, x0c, B, CTX, D, filt, hy_bias[j])
                hc = matmul_residual(zc, wout, hy_b_out[j], mods, 2, B * CTX, B, hc)
        h = ec_moe(h, norm2_g[i], mods, 3, 4, 5, N, 0, B, moe_router[i], i, w1, w3, w2)
        if ctx_stream:
            hc = ec_moe(hc, norm2_g[i], mods, 3, 4, 5, B * CTX, B, B, moe_router[i], i, w1, w3, w2)
    return h.reshape(B, N, D)
```

```python
import functools
import math

import jax
import jax.numpy as jnp
import numpy as np
from jax import lax
from jax.experimental import pallas as pl
from jax.experimental.pallas import tpu as pltpu

F32 = jnp.float32
BF16 = jnp.bfloat16

GRID_W = 64
N_MIXERS = 2
NORM_EPS = 1e-6
NEG_INF = -1e30
HEAD_DIM = 128
LANES = 128
WIN_ROWS = 8
WIN_COLS = 16
HYENA_SHORT = 3
HYENA_EMB = 33
HYENA_FAST_DECAY = 0.3
HYENA_SLOW_DECAY = 1.5
HYENA_TARGET = 1e-2
N_EXPERTS = 16
CAPACITY_FACTOR = 2

Q_ROWS = 4
BAND_ROWS = 12

VMEM_LIMIT = 56 * 1024 * 1024


def _cparams(*sem):
    return pltpu.CompilerParams(dimension_semantics=sem, vmem_limit_bytes=VMEM_LIMIT)


def _tile(n, want, unit=128):
    if n <= want:
        return n
    t = (want // unit) * unit
    while n % t:
        t -= unit
    return t


def _ada_kernel(c_ref, w_ref, b_ref, o_ref):
    c = c_ref[...]
    s = c * jax.nn.sigmoid(c)
    o_ref[...] = jnp.dot(s.astype(BF16), w_ref[...].astype(BF16),
                         preferred_element_type=F32) + b_ref[...]


def ada_all(cond, ada_w, ada_b, tn=1024):
    L, D, E = ada_w.shape
    tn = _tile(E, tn)
    return pl.pallas_call(
        _ada_kernel,
        grid=(L, E // tn),
        in_specs=[
            pl.BlockSpec((8, D), lambda l, j: (0, 0)),
            pl.BlockSpec((None, D, tn), lambda l, j: (l, 0, j)),
            pl.BlockSpec((None, 1, tn), lambda l, j: (l, 0, j)),
        ],
        out_specs=pl.BlockSpec((None, 8, tn), lambda l, j: (l, 0, j)),
        out_shape=jax.ShapeDtypeStruct((L, 8, E), F32),
        compiler_params=_cparams("parallel", "parallel"),
        name="ada",
    )(cond, ada_w, ada_b.reshape(L, 1, E))


def _norm_mod(x, g, sh, sc):
    y = x * lax.rsqrt(jnp.mean(x * x, axis=-1, keepdims=True) + NORM_EPS) * g
    return y * (1.0 + sc) + sh


def _nm_matmul_kernel(x_ref, g_ref, sh_ref, sc_ref, w_ref, b_ref, hg_ref, o_ref, a_ref, *,
                      n_norm_sections, section_cols, q_scale):
    j = pl.program_id(1)

    @pl.when(j == 0)
    def _():
        a_ref[...] = _norm_mod(x_ref[...], g_ref[...], sh_ref[...], sc_ref[...]).astype(BF16)

    acc = jnp.dot(a_ref[...], w_ref[...], preferred_element_type=F32) + b_ref[...]
    tn = acc.shape[1]
    if n_norm_sections == 0:
        o_ref[...] = acc.astype(o_ref.dtype)
        return
    sec = (j * tn) // section_cols

    @pl.when(sec < n_norm_sections)
    def _():
        hg = hg_ref[...]
        mult = jnp.where(sec == 0, q_scale, 1.0).astype(F32)
        for h in range(tn // HEAD_DIM):
            c = acc[:, h * HEAD_DIM:(h + 1) * HEAD_DIM]
            c = c * lax.rsqrt(jnp.mean(c * c, axis=-1, keepdims=True) + NORM_EPS) * hg
            o_ref[:, h * HEAD_DIM:(h + 1) * HEAD_DIM] = (c * mult).astype(o_ref.dtype)

    @pl.when(sec >= n_norm_sections)
    def _():
        o_ref[...] = acc.astype(o_ref.dtype)


def nm_matmul(x, g, mods, k_shift, k_scale, group_rows, group_base, w, b, head_g=None,
              q_scale=1.0, out_dtype=BF16, tm=1024, tn=1024):
    R, D = x.shape
    E = w.shape[1]
    tm = _tile(group_rows, tm, 8)
    tn = _tile(D, tn)
    n_norm = 0 if head_g is None else 2
    if head_g is None:
        head_g = jnp.ones((2, 1, HEAD_DIM), F32)

    def grp(i):
        return group_base + (i * tm) // group_rows

    def hg_map(i, j):
        return (jnp.minimum((j * tn) // D, 1), 0, 0)

    kern = functools.partial(_nm_matmul_kernel, n_norm_sections=n_norm, section_cols=D,
                             q_scale=q_scale)
    return pl.pallas_call(
        kern,
        grid=(R // tm, E // tn),
        in_specs=[
            pl.BlockSpec((tm, D), lambda i, j: (i, 0)),
            pl.BlockSpec((1, D), lambda i, j: (0, 0)),
            pl.BlockSpec((None, None, 1, D), lambda i, j: (k_shift, grp(i), 0, 0)),
            pl.BlockSpec((None, None, 1, D), lambda i, j: (k_scale, grp(i), 0, 0)),
            pl.BlockSpec((D, tn), lambda i, j: (0, j)),
            pl.BlockSpec((1, tn), lambda i, j: (0, j)),
            pl.BlockSpec((None, 1, HEAD_DIM), hg_map),
        ],
        out_specs=pl.BlockSpec((tm, tn), lambda i, j: (i, j)),
        out_shape=jax.ShapeDtypeStruct((R, E), out_dtype),
        scratch_shapes=[pltpu.VMEM((tm, D), BF16)],
        compiler_params=_cparams("parallel", "arbitrary"),
        name="nm_matmul",
    )(x, g.reshape(1, D), mods, mods, w, b.reshape(1, E), head_g)


def _mm_res_kernel(x_ref, w_ref, b_ref, gate_ref, res_ref, o_ref):
    y = jnp.dot(x_ref[...], w_ref[...], preferred_element_type=F32) + b_ref[...]
    o_ref[...] = res_ref[...] + gate_ref[...] * y


def matmul_residual(x, w, b, mods, k_gate, group_rows, group_base, res, tm=1024, tn=1024):
    R, K = x.shape
    E = w.shape[1]
    tm = _tile(group_rows, tm, 8)
    tn = _tile(E, tn)

    def grp(i):
        return group_base + (i * tm) // group_rows

    return pl.pallas_call(
        _mm_res_kernel,
        grid=(R // tm, E // tn),
        in_specs=[
            pl.BlockSpec((tm, K), lambda i, j: (i, 0)),
            pl.BlockSpec((K, tn), lambda i, j: (0, j)),
            pl.BlockSpec((1, tn), lambda i, j: (0, j)),
            pl.BlockSpec((None, None, 1, tn), lambda i, j: (k_gate, grp(i), 0, j)),
            pl.BlockSpec((tm, tn), lambda i, j: (i, j)),
        ],
        out_specs=pl.BlockSpec((tm, tn), lambda i, j: (i, j)),
        out_shape=jax.ShapeDtypeStruct((R, E), F32),
        compiler_params=_cparams("parallel", "parallel"),
        name="matmul_residual",
    )(x, w, b.reshape(1, E), mods, res)


def _na_bias_tables(rpb, rows):
    W = GRID_W
    per = LANES // W
    groups = BAND_ROWS // per
    cols = np.arange(W)
    cs = np.clip(cols - WIN_COLS // 2, 0, W - WIN_COLS)
    in_win = (cols[None, :] >= cs[:, None]) & (cols[None, :] < cs[:, None] + WIN_COLS)
    dc = np.clip(cols[None, :] - cols[:, None] + WIN_COLS - 1, 0, 2 * WIN_COLS - 2)
    n_dr, n_dc = 2 * WIN_ROWS - 1, 2 * WIN_COLS - 1
    onehot_r = np.zeros((3, Q_ROWS, BAND_ROWS, n_dr), np.float32)
    mask = np.zeros((3, Q_ROWS, W, BAND_ROWS, W), bool)
    for var, r0 in enumerate((0, Q_ROWS, rows - Q_ROWS)):
        bs = _band_start(r0, rows)
        for i in range(Q_ROWS):
            r = r0 + i
            rs = int(np.clip(r - WIN_ROWS // 2, 0, rows - WIN_ROWS))
            for jj in range(BAND_ROWS):
                kr = bs + jj
                if rs <= kr < rs + WIN_ROWS:
                    onehot_r[var, i, jj, kr - r + WIN_ROWS - 1] = 1.0
                    mask[var, i, :, jj, :] = in_win
    onehot_c = (dc[:, :, None] == np.arange(n_dc)).astype(np.float32)
    pick = np.einsum('pP,qkc->qpkPc', np.eye(per, dtype=np.float32), onehot_c).reshape(W, LANES, per * n_dc)
    onehot_r = onehot_r.reshape(3, Q_ROWS, groups, per, n_dr)
    mask = mask.reshape(3, Q_ROWS, W, groups, per, W).transpose(0, 3, 1, 2, 4, 5).reshape(
        3, groups, Q_ROWS * W, LANES)
    hp = lax.Precision.HIGHEST
    H = rpb.shape[0]
    t = jnp.einsum('hrc,vijpr->hvjipc', rpb.astype(F32), onehot_r, precision=hp)
    t = t.reshape(H, 3, groups, Q_ROWS, per * n_dc)
    t = jnp.einsum('hvjix,qyx->hvjiqy', t, pick, precision=hp)
    return jnp.where(mask[None], t.reshape(H, 3, groups, Q_ROWS * W, LANES), NEG_INF)


def _band_start(r0, rows):
    return int(np.clip(r0 - WIN_ROWS // 2, 0, rows - BAND_ROWS))


def _na_kernel(q_ref, k_ref, v_ref, qc_ref, kc_ref, vc_ref, bias_ref, o_ref, oc_ref, *, rows):
    W = GRID_W
    nblk = rows // Q_ROWS
    kc = kc_ref[...]
    vc = vc_ref[...]
    nt = (((1,), (1,)), ((), ()))

    def attend(q, parts):
        ss = [lax.dot_general(q, k, nt, preferred_element_type=F32) + (0.0 if bias is None else bias)
              for k, _, bias in parts]
        m = functools.reduce(jnp.maximum, [jnp.max(s, axis=-1, keepdims=True) for s in ss])
        ps = [jnp.exp(s - m) for s in ss]
        den = sum(jnp.sum(p, axis=-1, keepdims=True) for p in ps)
        o = sum(jnp.dot(p.astype(BF16), v, preferred_element_type=F32)
                for p, (_, v, _) in zip(ps, parts))
        return o / den

    def body(blk, carry):
        r0 = blk * Q_ROWS
        bs = jnp.clip(r0 - WIN_ROWS // 2, 0, rows - BAND_ROWS)
        var = jnp.where(blk == 0, 0, jnp.where(blk == nblk - 1, 2, 1))
        q0 = pl.multiple_of(r0 * W, Q_ROWS * W)
        k0 = pl.multiple_of(bs * W, W)
        q = q_ref[pl.ds(q0, Q_ROWS * W), :]
        kb = k_ref[pl.ds(k0, BAND_ROWS * W), :]
        vb = v_ref[pl.ds(k0, BAND_ROWS * W), :]
        bias = jnp.concatenate([bias_ref[var, c] for c in range(bias_ref.shape[1])], axis=1)
        o = attend(q, [(kb, vb, bias), (kc, vc, None)])
        o_ref[pl.ds(q0, Q_ROWS * W), :] = o.astype(o_ref.dtype)
        return carry

    lax.fori_loop(0, nblk, body, 0, unroll=2)
    oc_ref[...] = attend(qc_ref[...], [(kc, vc, None)]).astype(oc_ref.dtype)


def na_attention(qkv, qkv_c, bias, B, N, CTX, D):
    H = D // HEAD_DIM
    rows = N // GRID_W
    kern = functools.partial(_na_kernel, rows=rows)

    def spec(n, sec):
        return pl.BlockSpec((n, HEAD_DIM), lambda h, b: (b, sec * H + h))

    return pl.pallas_call(
        kern,
        grid=(H, B),
        in_specs=[spec(N, 0), spec(N, 1), spec(N, 2), spec(CTX, 0), spec(CTX, 1), spec(CTX, 2),
                  pl.BlockSpec((None,) + bias.shape[1:], lambda h, b: (h, 0, 0, 0, 0))],
        out_specs=[pl.BlockSpec((N, HEAD_DIM), lambda h, b: (b, h)),
                   pl.BlockSpec((CTX, HEAD_DIM), lambda h, b: (b, h))],
        out_shape=[jax.ShapeDtypeStruct((B * N, D), BF16),
                   jax.ShapeDtypeStruct((B * CTX, D), BF16)],
        compiler_params=_cparams("parallel", "parallel"),
        name="na_attention",
    )(qkv, qkv, qkv, qkv_c, qkv_c, qkv_c, bias)


def _expert_kernel(idx_ref, a_hbm, w1_ref, w3_ref, w2_ref, gate_ref, o_ref, xbuf, xs, yacc, sem, *,
                   n_tok, cap, bg):
    e, g, f = pl.program_id(0), pl.program_id(1), pl.program_id(2)
    n_exp, ng, nf = pl.num_programs(0), pl.num_programs(1), pl.num_programs(2)
    step = e * ng + g
    slot = step % 2
    nsteps = n_exp * ng
    rows = bg * cap
    S = xbuf.shape[1] // rows
    part = rows // nf

    def copy_row(ee, gg, sl, r0, u=0, priority=0):
        unit = math.gcd(part, cap)
        b = gg * bg + (r0 + (u // unit) * unit) // cap
        c = (r0 + (u // unit) * unit) % cap + u % unit
        row = b * n_tok + idx_ref[(b * n_exp + ee) * cap + c]
        src = pl.ds(pl.multiple_of(row * S, S), S)
        dst = pl.ds(pl.multiple_of((r0 + u) * S, S), S)
        pltpu.make_async_copy(a_hbm.at[src, :], xbuf.at[sl, dst, :], sem.at[sl]).start(priority=priority)

    def wait_rows(sl):
        pltpu.make_async_copy(a_hbm.at[pl.ds(0, rows * S), :], xbuf.at[sl], sem.at[sl]).wait()

    @pl.when((step == 0) & (f == 0))
    def _():
        def body(r, carry):
            copy_row(e, g, slot, r)
            return carry

        lax.fori_loop(0, rows, body, 0, unroll=8)

    @pl.when(f == 0)
    def _():
        wait_rows(slot)
        hi, lo = _load_token_tiles(xbuf, (slot,), 0, rows, S)
        xs[...] = jnp.concatenate([p.astype(BF16) for p in hi + lo], axis=1)
        yacc[...] = jnp.zeros_like(yacc)

    nxt = (step + 1) % nsteps
    for u in range(part):
        copy_row(nxt // ng, nxt % ng, 1 - slot, f * part, u, priority=u % 2)

    x = xs[...]
    h1 = jnp.dot(x, w1_ref[...].astype(BF16), preferred_element_type=F32)
    h3 = jnp.dot(x, w3_ref[...].astype(BF16), preferred_element_type=F32)
    hid = (h1 * jax.nn.sigmoid(h1) * h3).astype(BF16)
    w2 = w2_ref[...].astype(BF16)
    D = w2.shape[1]
    cw = min(D, EXPERT_OUT_COLS)
    for c in range(D // cw):
        cols = slice(c * cw, (c + 1) * cw)
        yacc[:, cols] += jnp.dot(hid, w2[:, cols], preferred_element_type=F32)

    @pl.when(f == nf - 1)
    def _():
        packed = _pack_bf16_pair(yacc[...] * gate_ref[...].reshape(rows, 1))
        for i in range(bg):
            _store_token_tiles(o_ref, (i,), packed[i * cap:(i + 1) * cap])

    @pl.when((step == nsteps - 1) & (f == nf - 1))
    def _():
        wait_rows(1 - slot)


def _pack_bf16_pair(y):
    half = y.shape[1] // 2
    hi = pltpu.bitcast(y[:, :half].astype(BF16).astype(F32), jnp.uint32)
    lo = pltpu.bitcast(y[:, half:].astype(BF16).astype(F32), jnp.uint32)
    return hi | (lo >> 16)


def _unpack_bf16_pair(u):
    hi = pltpu.bitcast(u & jnp.uint32(0xFFFF0000), F32)
    lo = pltpu.bitcast(u << 16, F32)
    return hi, lo


def _store_token_tiles(ref, lead, packed):
    rows, S = packed.shape[0], packed.shape[1] // LANES
    for s in range(S):
        ref[(*lead, pl.ds(s, rows, stride=S), slice(None))] = packed[:, s * LANES:(s + 1) * LANES]


def _load_token_tiles(ref, lead, first, rows, S):
    pieces = [_unpack_bf16_pair(ref[(*lead, pl.ds(first * S + s, rows, stride=S), slice(None))])
              for s in range(S)]
    return [p[0] for p in pieces], [p[1] for p in pieces]


def expert_ffn(a, idx_flat, layer, w1, w3, w2, gate, B, n_tok):
    _, E, D, Fh = w1.shape
    C = gate.shape[2]
    S = D // (2 * LANES)
    bg = max(1, min(B, EXPERT_ROWS // C))
    while B % bg:
        bg -= 1
    fc = _tile(Fh, EXPERT_HIDDEN_CHUNK)
    assert (bg * C) % (Fh // fc) == 0
    kern = functools.partial(_expert_kernel, n_tok=n_tok, cap=C, bg=bg)
    grid_spec = pltpu.PrefetchScalarGridSpec(
        num_scalar_prefetch=1,
        grid=(E, B // bg, Fh // fc),
        in_specs=[
            pl.BlockSpec(memory_space=pl.ANY),
            pl.BlockSpec((None, None, D, fc), lambda e, g, f, idx: (layer, e, 0, f)),
            pl.BlockSpec((None, None, D, fc), lambda e, g, f, idx: (layer, e, 0, f)),
            pl.BlockSpec((None, None, fc, D), lambda e, g, f, idx: (layer, e, f, 0)),
            pl.BlockSpec((bg, None, C, 1), lambda e, g, f, idx: (g, e, 0, 0)),
        ],
        out_specs=pl.BlockSpec((bg, None, C * S, LANES), lambda e, g, f, idx: (g, e, 0, 0)),
        scratch_shapes=[pltpu.VMEM((2, bg * C * S, LANES), jnp.uint32), pltpu.VMEM((bg * C, D), BF16),
                        pltpu.VMEM((bg * C, D), F32), pltpu.SemaphoreType.DMA((2,))],
    )
    return pl.pallas_call(
        kern,
        grid_spec=grid_spec,
        out_shape=jax.ShapeDtypeStruct((B, E, C * S, LANES), jnp.uint32),
        compiler_params=_cparams("arbitrary", "arbitrary", "arbitrary"),
        name="expert_ffn",
    )(idx_flat, a, w1, w3, w2, gate)


def _moe_prep_kernel(x_ref, g_ref, sh_ref, sc_ref, r_ref, a_ref, aff_ref):
    a = _norm_mod(x_ref[...], g_ref[...], sh_ref[...], sc_ref[...])
    _store_token_tiles(a_ref, (), _pack_bf16_pair(a))
    logits = lax.dot_general(r_ref[...], a.astype(BF16), (((1,), (1,)), ((), ())),
                             preferred_element_type=F32)
    m = jnp.max(logits, axis=0, keepdims=True)
    p = jnp.exp(logits - m)
    aff_ref[...] = p / jnp.sum(p, axis=0, keepdims=True)


def moe_prep(x, g, mods, k_shift, k_scale, group_rows, group_base, router_t, tm=512):
    R, D = x.shape
    E = router_t.shape[0]
    S = D // (2 * LANES)
    tm = _tile(group_rows, tm, 128)

    def grp(i):
        return group_base + (i * tm) // group_rows

    return pl.pallas_call(
        _moe_prep_kernel,
        grid=(R // tm,),
        in_specs=[
            pl.BlockSpec((tm, D), lambda i: (i, 0)),
            pl.BlockSpec((1, D), lambda i: (0, 0)),
            pl.BlockSpec((None, None, 1, D), lambda i: (k_shift, grp(i), 0, 0)),
            pl.BlockSpec((None, None, 1, D), lambda i: (k_scale, grp(i), 0, 0)),
            pl.BlockSpec((E, D), lambda i: (0, 0)),
        ],
        out_specs=[pl.BlockSpec((tm * S, LANES), lambda i: (i, 0)),
                   pl.BlockSpec((E, tm), lambda i: (0, i))],
        out_shape=[jax.ShapeDtypeStruct((R * S, LANES), jnp.uint32), jax.ShapeDtypeStruct((E, R), F32)],
        compiler_params=_cparams("parallel"),
        name="moe_prep",
    )(x, g.reshape(1, D), mods, mods, router_t)


EXPERT_ROWS = 1024
EXPERT_HIDDEN_CHUNK = 256
EXPERT_OUT_COLS = 512
COMBINE_TOKENS = 128
HIT_UNROLL = 4
COMBINE_ROW_GROUPS = 2
CUMSUM_CHUNK = 256
ROUTE_BLOCK_LANES = 512
TILE_TABLE_LANES = 128


def _route_select_kernel(aff_ref, tri_ref, ltri_ref, lt_ref, in_ref,
                         posi_ref, key_ref, slab_ref, p0_ref, kmax_ref, *, cap):
    E, N = aff_ref.shape
    aff = aff_ref[...]
    bits = pltpu.bitcast(aff, jnp.int32)
    cur = jnp.zeros((E, 1), jnp.int32)
    for bit in range(30, -1, -1):
        cand = cur | (1 << bit)
        cnt = jnp.sum(jnp.where(bits >= cand, 1.0, 0.0), axis=1, keepdims=True)
        cur = jnp.where(cnt >= cap, cand, cur)
    gt = bits > cur
    eq = bits == cur
    need = cap - jnp.sum(jnp.where(gt, 1.0, 0.0), axis=1, keepdims=True)

    def cumsum_incl(x):
        tri = tri_ref[...]
        w = min(CUMSUM_CHUNK, N)
        off = jnp.zeros((E, 1), F32)
        outs = []
        for c in range(N // w):
            xc = x[:, c * w:(c + 1) * w]
            outs.append(jnp.dot(xc.astype(BF16), tri[:w, :w], preferred_element_type=F32) + off)
            off = off + jnp.sum(xc, axis=1, keepdims=True)
        return jnp.concatenate(outs, axis=1) if len(outs) > 1 else outs[0]

    eqf = jnp.where(eq, 1.0, 0.0)
    tie_rank = cumsum_incl(eqf) - eqf
    sel = gt | (eq & (tie_rank < need))
    self_ = jnp.where(sel, 1.0, 0.0)
    pos_incl = cumsum_incl(self_)
    posi_ref[...] = pos_incl
    key_ref[...] = jnp.where(sel, pos_incl - self_, -1.0)
    selb = self_.astype(BF16)
    erank = jnp.dot(ltri_ref[...], selb, preferred_element_type=F32)
    in_tile = lax.broadcasted_iota(jnp.int32, (E, N), 1) & (COMBINE_TOKENS - 1)
    slab_ref[...] = erank * COMBINE_TOKENS + in_tile.astype(F32)
    p0_ref[...] = jnp.dot(selb, lt_ref[...], preferred_element_type=F32).astype(jnp.int32)
    count = jnp.sum(self_, axis=0, keepdims=True)
    kk = lax.broadcasted_iota(jnp.int32, (E, 1), 0).astype(F32)
    over = jnp.where(count > kk, 1.0, 0.0).astype(BF16)
    per_tile = jnp.dot(over, in_ref[...], preferred_element_type=F32)
    kmax_ref[...] = jnp.sum(jnp.where(per_tile > 0.0, 1.0, 0.0), axis=0, keepdims=True).astype(jnp.int32)


def _route_compact_kernel(p0_ref, aff_ref, posi_ref, key_ref, slab_ref, idx_ref, gate_ref, rank_ref, *,
                          cap, tn, lanes, width):
    b = pl.program_id(0)
    E, N = aff_ref.shape
    ch = min(LANES, cap)
    tpb = width // tn
    diag = lax.broadcasted_iota(jnp.int32, (ch, ch), 0) == lax.broadcasted_iota(jnp.int32, (ch, ch), 1)

    def as_row(col):
        return jnp.sum(jnp.where(diag, col, 0.0), axis=0, keepdims=True)

    def e_body(e, carry):
        q = (b * E + e) * lanes

        def c_body(cc, tiles):
            c0 = cc * ch
            t_lo = lax.while_loop(lambda t: p0_ref[q + t + 1] <= c0, lambda t: t + 1, tiles[0])
            t_hi = lax.while_loop(lambda t: p0_ref[q + t] < c0 + ch, lambda t: t + 1, tiles[1])
            j_lo = t_lo // tpb
            j_hi = (t_hi + tpb - 1) // tpb
            ccol = (c0 + lax.broadcasted_iota(jnp.int32, (ch, 1), 0)).astype(F32)

            def j_body(j, acc):
                row = pl.ds(e, 1)
                blk = pl.ds(pl.multiple_of(j * width, width), width)
                pi, ky, af, sl = posi_ref[row, blk], key_ref[row, blk], aff_ref[row, blk], slab_ref[row, blk]
                acc = list(acc)
                for u in range(width // LANES):
                    part = slice(u * LANES, (u + 1) * LANES)
                    hit = ky[:, part] == ccol
                    acc[0] = acc[0] + jnp.where(pi[:, part] <= ccol, 1.0, 0.0)
                    acc[1] = acc[1] + jnp.where(hit, af[:, part], 0.0)
                    acc[2] = acc[2] + jnp.where(hit, sl[:, part], 0.0)
                return tuple(acc)

            zero = jnp.zeros((ch, LANES), F32)
            iv, gv, rv = [jnp.sum(a, axis=1, keepdims=True)
                          for a in lax.fori_loop(j_lo, j_hi, j_body, (zero, zero, zero))]
            idx_ref[pl.ds(e, 1), pl.ds(c0, ch)] = as_row(iv).astype(jnp.int32) + j_lo * width
            gate_ref[e, pl.ds(c0, ch), :] = gv
            rank_ref[pl.ds(e, 1), pl.ds(c0, ch)] = as_row(rv).astype(jnp.int32)
            return (t_lo, t_hi)

        tiles = (0, 0)
        for cc in range(cap // ch):
            tiles = c_body(cc, tiles)
        return carry

    for e in range(E):
        e_body(e, 0)


def moe_route(aff_t, B, N, cap):
    E = aff_t.shape[0]
    tn = COMBINE_TOKENS
    lanes = TILE_TABLE_LANES
    assert N // tn + 1 <= lanes and N % tn == 0 and N % min(CUMSUM_CHUNK, N) == 0
    w = min(CUMSUM_CHUNK, N)
    tri = (np.arange(w)[:, None] <= np.arange(w)[None, :]).astype(np.float32)
    ltri = (np.arange(E)[None, :] < np.arange(E)[:, None]).astype(np.float32)
    tok = np.arange(N)[:, None]
    tile = np.arange(lanes)[None, :]
    before = (tok < tile * tn).astype(np.float32)
    inside = (tok // tn == tile).astype(np.float32)
    const = lambda shape: pl.BlockSpec(shape, lambda b: (0,) * len(shape))
    per_tok = pl.BlockSpec((None, E, N), lambda b, *_: (b, 0, 0))
    posi, key, slab, p0, kmax = pl.pallas_call(
        functools.partial(_route_select_kernel, cap=cap),
        grid=(B,),
        in_specs=[pl.BlockSpec((E, N), lambda b: (0, b)), const((w, w)), const((E, E)),
                  const((N, lanes)), const((N, lanes))],
        out_specs=[per_tok, per_tok, per_tok,
                   pl.BlockSpec((None, E, lanes), lambda b: (b, 0, 0)),
                   pl.BlockSpec((None, 1, lanes), lambda b: (b, 0, 0))],
        out_shape=[jax.ShapeDtypeStruct((B, E, N), F32)] * 3 + [
            jax.ShapeDtypeStruct((B, E, lanes), jnp.int32), jax.ShapeDtypeStruct((B, 1, lanes), jnp.int32)],
        compiler_params=_cparams("parallel"),
        name="moe_route_select",
    )(aff_t, jnp.asarray(tri, BF16), jnp.asarray(ltri, BF16), jnp.asarray(before, BF16),
      jnp.asarray(inside, BF16))
    width = min(ROUTE_BLOCK_LANES, N)
    sel3 = pl.BlockSpec((None, E, cap, 1), lambda b, *_: (b, 0, 0, 0))
    rows2 = pl.BlockSpec((None, E, cap), lambda b, *_: (b, 0, 0))
    idx, gate, rank = pl.pallas_call(
        functools.partial(_route_compact_kernel, cap=cap, tn=tn, lanes=lanes, width=width),
        grid_spec=pltpu.PrefetchScalarGridSpec(
            num_scalar_prefetch=1,
            grid=(B,),
            in_specs=[pl.BlockSpec((E, N), lambda b, *_: (0, b)), per_tok, per_tok, per_tok],
            out_specs=[rows2, sel3, rows2],
        ),
        out_shape=[jax.ShapeDtypeStruct((B, E, cap), jnp.int32),
                   jax.ShapeDtypeStruct((B, E, cap, 1), F32),
                   jax.ShapeDtypeStruct((B, E, cap), jnp.int32)],
        compiler_params=_cparams("parallel"),
        name="moe_route_compact",
    )(p0.reshape(-1), aff_t, posi, key, slab)
    return idx, gate, rank, p0, kmax


def _combine_kernel(rank_ref, p0_ref, kmax_ref, ys_hbm, h_ref, g2_ref, o_ref, rbuf, sem, *,
                    n_exp, cap, tn, lanes):
    b, t = pl.program_id(0), pl.program_id(1)
    nt = pl.num_programs(1)
    step = b * nt + t
    slot = step % 2
    S = rbuf.shape[1] // (n_exp * tn)

    def tile_hits(bb, tt):
        def body(e, tot):
            q = (bb * n_exp + e) * lanes + tt
            return tot + p0_ref[q + 1] - p0_ref[q]
        return lax.fori_loop(0, n_exp, body, 0)

    def fetch(bb, tt, sl):
        km = kmax_ref[bb * lanes + tt]
        for k in range(n_exp):
            @pl.when(k < km)
            def _():
                rbuf[sl, k * tn * S:(k + 1) * tn * S, :] = jnp.zeros((tn * S, LANES), jnp.uint32)

        def copy_row(row, priority=0):
            src = pl.ds(pl.multiple_of(row * S, S), S)
            dst = pl.ds(pl.multiple_of(rank_ref[row] * S, S), S)
            pltpu.make_async_copy(ys_hbm.at[src, :], rbuf.at[sl, dst, :], sem.at[sl]).start(priority=priority)

        def e_body(e, carry):
            q = (bb * n_exp + e) * lanes + tt
            lo = (bb * n_exp + e) * cap + p0_ref[q]
            n = p0_ref[q + 1] - p0_ref[q]
            groups = n // HIT_UNROLL

            def g_body(g, carry2):
                for u in range(HIT_UNROLL):
                    copy_row(lo + g * HIT_UNROLL + u, priority=u % 2)
                return carry2

            def r_body(c, carry2):
                copy_row(lo + c)
                return carry2

            lax.fori_loop(0, groups, g_body, 0)
            lax.fori_loop(groups * HIT_UNROLL, n, r_body, 0)
            return carry

        lax.fori_loop(0, n_exp, e_body, 0)

    @pl.when(step == 0)
    def _():
        fetch(b, t, slot)

    @pl.when(step + 1 < pl.num_programs(0) * nt)
    def _():
        nxt = step + 1
        fetch(nxt // nt, nxt % nt, 1 - slot)

    hits = tile_hits(b, t)

    nrows = hits * S
    bulk = pl.multiple_of((nrows // 8) * 8, 8)

    @pl.when(bulk > 0)
    def _():
        pltpu.make_async_copy(ys_hbm.at[pl.ds(0, bulk), :], rbuf.at[slot, pl.ds(0, bulk), :],
                              sem.at[slot]).wait()

    def wait_row(i, carry):
        pltpu.make_async_copy(ys_hbm.at[pl.ds(0, 1), :], rbuf.at[slot, pl.ds(0, 1), :], sem.at[slot]).wait()
        return carry

    lax.fori_loop(0, nrows - bulk, wait_row, 0)

    km = kmax_ref[b * lanes + t]
    g2 = g2_ref[...]
    rg = COMBINE_ROW_GROUPS
    for r in range(0, tn // 8, rg):
        def k_body(k, tot):
            new = []
            for i in range(rg):
                hi, lo = _load_token_tiles(rbuf, (slot,), k * tn + (r + i) * 8, 8, S)
                new += hi + lo
            return tuple(t_ + p for t_, p in zip(tot, new))

        tot = lax.fori_loop(0, km, k_body, (jnp.zeros((8, LANES), F32),) * (2 * S * rg))
        for i in range(rg):
            rows = slice((r + i) * 8, (r + i + 1) * 8)
            for s in range(2 * S):
                cols = slice(s * LANES, (s + 1) * LANES)
                o_ref[rows, cols] = h_ref[rows, cols] + g2[:, cols] * tot[i * 2 * S + s]


def moe_combine(ys, h, mods, k_gate, group_rows, group_base, rank, p0, kmax, B, N):
    R, D = h.shape
    E, cap = rank.shape[1], rank.shape[2]
    tn = COMBINE_TOKENS
    lanes = TILE_TABLE_LANES
    nt = N // tn

    def grp(b, t):
        return group_base + ((b * nt + t) * tn) // group_rows

    kern = functools.partial(_combine_kernel, n_exp=E, cap=cap, tn=tn, lanes=lanes)
    grid_spec = pltpu.PrefetchScalarGridSpec(
        num_scalar_prefetch=3,
        grid=(B, nt),
        in_specs=[
            pl.BlockSpec(memory_space=pl.ANY),
            pl.BlockSpec((tn, D), lambda b, t, *_: (b * nt + t, 0)),
            pl.BlockSpec((None, None, 1, D), lambda b, t, *_: (k_gate, grp(b, t), 0, 0)),
        ],
        out_specs=pl.BlockSpec((tn, D), lambda b, t, *_: (b * nt + t, 0)),
        scratch_shapes=[pltpu.VMEM((2, E * tn * (D // (2 * LANES)), LANES), jnp.uint32),
                        pltpu.SemaphoreType.DMA((2,))],
    )
    return pl.pallas_call(
        kern,
        grid_spec=grid_spec,
        out_shape=jax.ShapeDtypeStruct((R, D), F32),
        compiler_params=_cparams("arbitrary", "arbitrary"),
        name="moe_combine",
    )(rank.reshape(-1), p0.reshape(-1), kmax.reshape(-1), ys, h, mods)


def ec_moe(h, g, mods, k_shift, k_scale, k_gate, group_rows, group_base, B, router, layer, w1, w3, w2):
    R, D = h.shape
    N = R // B
    E = router.shape[1]
    cap = CAPACITY_FACTOR * N // E
    a, aff_t = moe_prep(h, g, mods, k_shift, k_scale, group_rows, group_base, router.T.astype(BF16))
    idx, gate, rank, p0, kmax = moe_route(aff_t, B, N, cap)
    ys = expert_ffn(a, idx.reshape(-1), layer, w1, w3, w2, gate, B, N)
    return moe_combine(ys.reshape(-1, LANES), h, mods, k_gate, group_rows, group_base,
                       rank, p0, kmax, B, N)


FFT_MINOR = 256
FFT_GROUP = 16


@functools.lru_cache(maxsize=None)
def _dft_tables(L):
    n = 2 * L
    n1 = n // FFT_MINOR
    nb = n1 // 2
    two_pi = 2.0 * math.pi
    a = np.arange(n1)
    ang = two_pi * ((a[:, None] * a[None, :]) % n1) / n1
    cos1, sin1 = np.cos(ang), np.sin(ang)
    eye = np.eye(FFT_GROUP)

    def kron(m):
        return np.kron(m, eye)

    def const(m):
        return jnp.asarray(m.astype(np.float32).astype(BF16))

    f1 = np.arange(n1)[:, None, None]
    f2 = np.arange(FFT_MINOR)[None, :, None]
    n2 = np.arange(FFT_MINOR)[None, None, :]
    ang = two_pi * ((n2 * (f1 + n1 * f2)) % n) / n
    gr, gi = np.cos(ang), -np.sin(ang)
    return dict(
        n1=n1, nb=nb,
        m_fwd_full=const(np.concatenate([kron(cos1), kron(-sin1)], axis=0)),
        m_fwd_half=const(np.concatenate([kron(cos1[:, :nb]), kron(-sin1[:, :nb])], axis=0)),
        m_inv=const(np.concatenate([kron(cos1[:nb]), kron(sin1[:nb])], axis=0)),
        g=const(np.concatenate([gr, gi], axis=1)),
        gt=const(np.concatenate([gr.transpose(0, 2, 1), gi.transpose(0, 2, 1)], axis=1)),
    )


HALO = 8


def _hyena_in_kernel(x_ref, xp_ref, xn_ref, g_ref, sh_ref, sc_ref, w0_ref, w1_ref, w2_ref, b_ref, cw_ref,
                     cb_ref, v_ref, x0_ref, a_ref, ah_ref, *, seq_len):
    i, j = pl.program_id(0), pl.program_id(1)
    tm = x_ref.shape[0]

    @pl.when(j == 0)
    def _():
        g, sh, sc = g_ref[...], sh_ref[...], sc_ref[...]
        a_ref[...] = _norm_mod(x_ref[...], g, sh, sc).astype(BF16)
        ah_ref[0:HALO, :] = _norm_mod(xp_ref[...], g, sh, sc)
        ah_ref[HALO:, :] = _norm_mod(xn_ref[...], g, sh, sc)

    first = (i * tm) % seq_len == 0
    last = ((i + 1) * tm) % seq_len == 0
    a = a_ref[...]
    ah = ah_ref[...].astype(BF16)
    row = lax.broadcasted_iota(jnp.int32, (tm, v_ref.shape[1]), 0)

    def section(k, w_ref):
        w = w_ref[...]
        p = jnp.dot(a, w, preferred_element_type=F32) + b_ref[k]
        ph = jnp.dot(ah, w, preferred_element_type=F32) + b_ref[k]
        above = jnp.where(first, 0.0, ph[HALO - 1:HALO])
        below = jnp.where(last, 0.0, ph[HALO:HALO + 1])
        cw = cw_ref[k]
        prev = jnp.where(row == 0, above, pltpu.roll(p, 1, 0))
        nxt = jnp.where(row == tm - 1, below, pltpu.roll(p, tm - 1, 0))
        return prev * cw[0:1] + p * cw[1:2] + nxt * cw[2:3] + cb_ref[k]

    x0 = section(0, w0_ref)
    x1 = section(1, w1_ref)
    v = section(2, w2_ref)
    v_ref[...] = (v * x1).astype(v_ref.dtype)
    x0_ref[...] = x0.astype(x0_ref.dtype)


def hyena_in_proj(x, g, mods, k_shift, k_scale, group_rows, group_base, seq_len, w, b, conv_w, conv_b,
                  tm=1024, tc=512):
    R, D = x.shape
    tm = _tile(math.gcd(group_rows, seq_len), tm, HALO)
    tc = _tile(D, tc)
    nc = D // tc
    cw = conv_w.reshape(HYENA_SHORT, 3, D).transpose(1, 0, 2)

    def grp(i):
        return group_base + (i * tm) // group_rows

    def wspec(k):
        return pl.BlockSpec((D, tc), lambda i, j: (0, k * nc + j))

    mod = lambda k: pl.BlockSpec((None, None, 1, D), lambda i, j: (k, grp(i), 0, 0))
    per_proj = lambda rows: pl.BlockSpec((3, rows, tc), lambda i, j: (0, 0, j))
    kern = functools.partial(_hyena_in_kernel, seq_len=seq_len)
    return pl.pallas_call(
        kern,
        grid=(R // tm, nc),
        in_specs=[pl.BlockSpec((tm, D), lambda i, j: (i, 0)),
                  pl.BlockSpec((HALO, D), lambda i, j: (jnp.maximum(i * (tm // HALO) - 1, 0), 0)),
                  pl.BlockSpec((HALO, D), lambda i, j: (jnp.minimum((i + 1) * (tm // HALO), R // HALO - 1), 0)),
                  pl.BlockSpec((1, D), lambda i, j: (0, 0)), mod(k_shift), mod(k_scale),
                  wspec(0), wspec(1), wspec(2), per_proj(1), per_proj(HYENA_SHORT), per_proj(1)],
        out_specs=[pl.BlockSpec((tm, tc), lambda i, j: (i, j))] * 2,
        out_shape=[jax.ShapeDtypeStruct((R, D), BF16)] * 2,
        scratch_shapes=[pltpu.VMEM((tm, D), BF16), pltpu.VMEM((2 * HALO, D), F32)],
        compiler_params=_cparams("parallel", "arbitrary"),
        name="hyena_in_proj",
    )(x, x, x, g.reshape(1, D), mods, mods, w, w, w, b.reshape(3, 1, D), cw, conv_b.reshape(3, 1, D))


def _filter_kernel(emb_ref, w1_ref, b1_ref, f1_ref, w2_ref, b2_ref, f2_ref, w3_ref, b3_ref, f3_ref,
                   wo_ref, dl_ref, k_ref, norm_ref, *, L):
    i = pl.program_id(0)
    tr = emb_ref.shape[0]
    hp = lax.Precision.HIGHEST
    emb = emb_ref[...]
    h = jnp.sin(f1_ref[...] * (jnp.dot(emb, w1_ref[...], precision=hp, preferred_element_type=F32) + b1_ref[...]))
    h = jnp.sin(f2_ref[...] * (jnp.dot(h, w2_ref[...], precision=hp, preferred_element_type=F32) + b2_ref[...]))
    h = jnp.sin(f3_ref[...] * (jnp.dot(h, w3_ref[...], precision=hp, preferred_element_type=F32) + b3_ref[...]))
    k = jnp.dot(h, wo_ref[...], precision=hp, preferred_element_type=F32)
    k = k * jnp.exp(-emb[:, 0:1] * dl_ref[...])
    row = i * tr + lax.broadcasted_iota(jnp.int32, (tr, 1), 0)
    k = jnp.where(row == L, 0.0, k)
    k_ref[...] = k

    @pl.when(i == 0)
    def _():
        norm_ref[...] = jnp.zeros_like(norm_ref)

    norm_ref[...] += jnp.sum(jnp.abs(k), axis=0, keepdims=True)


def hyena_filter(L, D, w1, b1, f1, w2, b2, f2, w3, b3, f3, w_out, tr=512):
    n = 2 * L
    tr = _tile(L, tr, 8)
    P = 128
    bands = (HYENA_EMB - 1) // 2
    d = np.arange(n)
    d = np.where(d <= L, d, n - d).clip(0, L - 1)
    t01 = np.linspace(0.0, 1.0, L)[d]
    wang = 2 * math.pi * d / L
    fr = np.linspace(1e-4, bands - 1, bands)
    emb = np.zeros((n, P), np.float32)
    emb[:, 0] = t01
    emb[:, 1:1 + bands] = np.cos(fr[None, :] * wang[:, None])
    emb[:, 1 + bands:1 + 2 * bands] = -np.sin(fr[None, :] * wang[:, None])

    def padw(w):
        return jnp.zeros((P, P), F32).at[:w.shape[0], :w.shape[1]].set(w)

    def padv(v):
        return jnp.zeros((1, P), F32).at[0, :v.shape[0]].set(v)

    wo = jnp.zeros((P, 2 * D), F32).at[:w_out.shape[0]].set(w_out)
    max_decay = math.log(HYENA_TARGET) / HYENA_FAST_DECAY
    min_decay = math.log(HYENA_TARGET) / HYENA_SLOW_DECAY
    deltas = jnp.abs(jnp.linspace(min_decay, max_decay, D, dtype=F32)).reshape(1, D)
    small = pl.BlockSpec((P, P), lambda i: (0, 0))
    vec = pl.BlockSpec((1, P), lambda i: (0, 0))
    kern = functools.partial(_filter_kernel, L=L)
    return pl.pallas_call(
        kern,
        grid=(n // tr,),
        in_specs=[pl.BlockSpec((tr, P), lambda i: (i, 0)),
                  small, vec, vec, small, vec, vec, small, vec, vec,
                  pl.BlockSpec((P, D), lambda i: (0, (i * tr) // L)),
                  pl.BlockSpec((1, D), lambda i: (0, 0))],
        out_specs=[pl.BlockSpec((tr, D), lambda i: (i, 0)), pl.BlockSpec((1, D), lambda i: (0, 0))],
        out_shape=[jax.ShapeDtypeStruct((n, D), F32), jax.ShapeDtypeStruct((1, D), F32)],
        compiler_params=_cparams("arbitrary"),
        name="hyena_filter",
    )(jnp.asarray(emb), padw(w1), padv(b1), padv(f1), padw(w2), padv(b2), padv(f2),
      padw(w3), padv(b3), padv(f3), wo, deltas)


def _fwd_major_kernel(*refs, n1, nb, has_imag):
    if has_imag:
        zr_ref, zi_ref, m_ref, ar_ref, ai_ref = refs
    else:
        zr_ref, m_ref, ar_ref, ai_ref = refs
    m = m_ref[...]
    half = n1 * FFT_GROUP

    def body(g, carry):
        def rows(blk):
            return pl.ds(pl.multiple_of(blk * FFT_MINOR + g * FFT_GROUP, FFT_GROUP), FFT_GROUP)

        def slab(ref):
            return jnp.concatenate([ref[rows(blk), :] for blk in range(nb)], axis=0).astype(BF16)

        pr = jnp.dot(m, slab(zr_ref), preferred_element_type=F32)
        if has_imag:
            pi = jnp.dot(m, slab(zi_ref), preferred_element_type=F32)
            ar, ai = pr[:half] - pi[half:], pi[:half] + pr[half:]
        else:
            ar, ai = pr[:half], pr[half:]
        for f1 in range(n1):
            sl = slice(f1 * FFT_GROUP, (f1 + 1) * FFT_GROUP)
            ar_ref[rows(f1), :] = ar[sl].astype(BF16)
            ai_ref[rows(f1), :] = ai[sl].astype(BF16)
        return carry

    lax.fori_loop(0, FFT_MINOR // FFT_GROUP, body, 0)


def fwd_major(z, tabs, pairs, tc=256):
    n1, nb = tabs["n1"], tabs["nb"]
    n = n1 * FFT_MINOR
    Bz, Lz, D = z.shape
    tc = _tile(D, tc)
    if pairs:
        P = Bz // 2
        m = tabs["m_fwd_half"]
        ins = [z, z, m]
        in_specs = [pl.BlockSpec((None, Lz, tc), lambda p, c: (2 * p, 0, c)),
                    pl.BlockSpec((None, Lz, tc), lambda p, c: (2 * p + 1, 0, c))]
        nblk = nb
    else:
        P = 1
        m = tabs["m_fwd_full"]
        ins = [z, m]
        in_specs = [pl.BlockSpec((None, Lz, tc), lambda p, c: (0, 0, c))]
        nblk = n1
    in_specs.append(pl.BlockSpec(m.shape, lambda p, c: (0, 0)))
    kern = functools.partial(_fwd_major_kernel, n1=n1, nb=nblk, has_imag=pairs)
    return pl.pallas_call(
        kern,
        grid=(P, D // tc),
        in_specs=in_specs,
        out_specs=[pl.BlockSpec((None, n, tc), lambda p, c: (p, 0, c))] * 2,
        out_shape=[jax.ShapeDtypeStruct((P, n, D), BF16)] * 2,
        compiler_params=_cparams("parallel", "parallel"),
        name="hyena_fwd_major",
    )(*ins)


def _spectrum_kernel(ar_ref, ai_ref, g_ref, s_ref, kr_ref, ki_ref):
    g = g_ref[...]
    h = FFT_MINOR
    pr = jnp.dot(g, ar_ref[...], preferred_element_type=F32)
    pi = jnp.dot(g, ai_ref[...], preferred_element_type=F32)
    s = s_ref[...]
    kr_ref[...] = (pr[:h] - pi[h:]) * s
    ki_ref[...] = (pi[:h] + pr[h:]) * s


def filter_spectrum(ar, ai, tabs, scale, tc=1024):
    _, n, D = ar.shape
    tc = _tile(D, tc)
    blk = pl.BlockSpec((None, FFT_MINOR, tc), lambda f, c: (0, f, c))
    out = pl.BlockSpec((FFT_MINOR, tc), lambda f, c: (f, c))
    return pl.pallas_call(
        _spectrum_kernel,
        grid=(tabs["n1"], D // tc),
        in_specs=[blk, blk, pl.BlockSpec((None, 2 * FFT_MINOR, FFT_MINOR), lambda f, c: (f, 0, 0)),
                  pl.BlockSpec((1, tc), lambda f, c: (0, c))],
        out_specs=[out, out],
        out_shape=[jax.ShapeDtypeStruct((n, D), F32)] * 2,
        compiler_params=_cparams("parallel", "parallel"),
        name="hyena_filter_spectrum",
    )(ar, ai, tabs["g"], scale)


def _minor_kernel(ar_ref, ai_ref, g_ref, gt_ref, kr_ref, ki_ref, br_ref, bi_ref):
    h = FFT_MINOR
    g = g_ref[...]
    pr = jnp.dot(g, ar_ref[...], preferred_element_type=F32)
    pi = jnp.dot(g, ai_ref[...], preferred_element_type=F32)
    xr, xi = pr[:h] - pi[h:], pi[:h] + pr[h:]
    kr, ki = kr_ref[...], ki_ref[...]
    yr = (xr * kr - xi * ki).astype(BF16)
    yi = (xr * ki + xi * kr).astype(BF16)
    gt = gt_ref[...]
    qr = jnp.dot(gt, yr, preferred_element_type=F32)
    qi = jnp.dot(gt, yi, preferred_element_type=F32)
    br_ref[...] = (qr[:h] + qi[h:]).astype(BF16)
    bi_ref[...] = (qi[:h] - qr[h:]).astype(BF16)


def minor_conv(ar, ai, tabs, kr, ki, tc=1024):
    P, n, D = ar.shape
    tc = _tile(D, tc)
    blk = pl.BlockSpec((None, FFT_MINOR, tc), lambda f, c, p: (p, f, c))
    tab = pl.BlockSpec((None, 2 * FFT_MINOR, FFT_MINOR), lambda f, c, p: (f, 0, 0))
    kblk = pl.BlockSpec((FFT_MINOR, tc), lambda f, c, p: (f, c))
    return pl.pallas_call(
        _minor_kernel,
        grid=(tabs["n1"], D // tc, P),
        in_specs=[blk, blk, tab, tab, kblk, kblk],
        out_specs=[blk, blk],
        out_shape=[jax.ShapeDtypeStruct((P, n, D), BF16)] * 2,
        compiler_params=_cparams("parallel", "parallel", "parallel"),
        name="hyena_minor_conv",
    )(ar, ai, tabs["g"], tabs["gt"], kr, ki)


def _inv_major_kernel(br_ref, bi_ref, m_ref, v_ref, x0_ref, bias_ref, o_ref, *, n1, nb):
    m = m_ref[...]
    half = nb * FFT_GROUP
    bias = bias_ref[...]

    def body(g, carry):
        def rows(blk):
            return pl.ds(pl.multiple_of(blk * FFT_MINOR + g * FFT_GROUP, FFT_GROUP), FFT_GROUP)

        def slab(ref):
            return jnp.concatenate([ref[rows(f1), :] for f1 in range(n1)], axis=0)

        pr = jnp.dot(m, slab(br_ref), preferred_element_type=F32)
        pi = jnp.dot(m, slab(bi_ref), preferred_element_type=F32)
        ys = (pr[:half] - pi[half:], pi[:half] + pr[half:])
        for s in range(2):
            for blk in range(nb):
                y = ys[s][blk * FFT_GROUP:(blk + 1) * FFT_GROUP]
                v = v_ref[s, rows(blk), :].astype(F32)
                x0 = x0_ref[s, rows(blk), :].astype(F32)
                o_ref[s, rows(blk), :] = ((y + v * bias) * x0).astype(o_ref.dtype)
        return carry

    lax.fori_loop(0, FFT_MINOR // FFT_GROUP, body, 0)


def inv_major(br, bi, tabs, v, x0, bias, tc=256):
    n1, nb = tabs["n1"], tabs["nb"]
    P, n, D = br.shape
    B, L, _ = v.shape
    tc = _tile(D, tc)
    m = tabs["m_inv"]
    blk = pl.BlockSpec((None, n, tc), lambda p, c: (p, 0, c))
    pair = pl.BlockSpec((2, L, tc), lambda p, c: (p, 0, c))
    kern = functools.partial(_inv_major_kernel, n1=n1, nb=nb)
    return pl.pallas_call(
        kern,
        grid=(P, D // tc),
        in_specs=[blk, blk, pl.BlockSpec(m.shape, lambda p, c: (0, 0)), pair, pair,
                  pl.BlockSpec((1, tc), lambda p, c: (0, c))],
        out_specs=pair,
        out_shape=jax.ShapeDtypeStruct((B, L, D), BF16),
        compiler_params=_cparams("parallel", "parallel"),
        name="hyena_inv_major",
    )(br, bi, m, v, x0, bias.reshape(1, D))


def hyena_mix(v, x0, B, L, D, filt, bias):
    assert B % 2 == 0 and L % FFT_MINOR == 0
    tabs = _dft_tables(L)
    v, x0 = v.reshape(B, L, D), x0.reshape(B, L, D)
    k_raw, k_norm = hyena_filter(L, D, *filt)
    kar, kai = fwd_major(k_raw[None], tabs, pairs=False)
    kr, ki = filter_spectrum(kar, kai, tabs, 1.0 / (2 * L * k_norm))
    ar, ai = fwd_major(v, tabs, pairs=True)
    br, bi = minor_conv(ar, ai, tabs, kr, ki)
    z = inv_major(br, bi, tabs, v, x0, bias)
    return z.reshape(B * L, D)


def kernel(x, c, ctx, c_ctx, ada_w, ada_b, norm1_g, norm2_g, na_w_qkv, na_w_o, na_q_g, na_k_g, na_rpb, hy_w_in, hy_b_in, hy_conv_w, hy_conv_b, hy_f_w1, hy_f_b1, hy_f_freq1, hy_f_w2, hy_f_b2, hy_f_freq2, hy_f_w3, hy_f_b3, hy_f_freq3, hy_f_wout, hy_bias, hy_w_out, hy_b_out, moe_router, moe_w1, moe_w3, moe_w2):
    B, N, D = x.shape
    CTX = ctx.shape[1]
    depth = ada_w.shape[0]
    mixer = [i % N_MIXERS for i in range(depth)]

    cond = jnp.concatenate([c, c_ctx[None, :], jnp.zeros((8 - B - 1, D), F32)], axis=0)
    mods_all = ada_all(cond, ada_w, ada_b)
    mods_all = mods_all.reshape(depth, 8, 6, 1, D).transpose(0, 2, 1, 3, 4)

    h = x.reshape(B * N, D)
    hc = ctx.reshape(B * CTX, D)
    zeros_d = jnp.zeros((D,), F32)
    w1, w3, w2 = moe_w1, moe_w3, moe_w2
    for i in range(depth):
        j = i // N_MIXERS
        ctx_stream = any(mixer[l] == 0 for l in range(i + 1, depth))
        ctx_in = ctx_stream or mixer[i] == 0
        mods = mods_all[i]
        if mixer[i] == 0:
            wqkv = na_w_qkv[j].astype(BF16)
            wo = na_w_o[j].astype(BF16)
            hg = jnp.stack([na_q_g[j], na_k_g[j]]).reshape(2, 1, HEAD_DIM)
            zeros_e = jnp.zeros((3 * D,), F32)
            qs = HEAD_DIM ** -0.5
            qkv = nm_matmul(h, norm1_g[i], mods, 0, 1, N, 0, wqkv, zeros_e, hg, qs)
            qkv_c = nm_matmul(hc, norm1_g[i], mods, 0, 1, B * CTX, B, wqkv, zeros_e, hg, qs)
            bias = _na_bias_tables(na_rpb[j], N // GRID_W)
            o, oc = na_attention(qkv, qkv_c, bias, B, N, CTX, D)
            h = matmul_residual(o, wo, zeros_d, mods, 2, N, 0, h)
            if ctx_stream:
                hc = matmul_residual(oc, wo, zeros_d, mods, 2, B * CTX, B, hc)
        else:
            win = hy_w_in[j].astype(BF16)
            wout = hy_w_out[j].astype(BF16)
            filt = (hy_f_w1[j], hy_f_b1[j], hy_f_freq1[j], hy_f_w2[j], hy_f_b2[j], hy_f_freq2[j],
                    hy_f_w3[j], hy_f_b3[j], hy_f_freq3[j], hy_f_wout[j])
            v, x0 = hyena_in_proj(h, norm1_g[i], mods, 0, 1, N, 0, N, win, hy_b_in[j],
                                  hy_conv_w[j], hy_conv_b[j])
            z = hyena_mix(v, x0, B, N, D, filt, hy_bias[j])
            h = matmul_residual(z, wout, hy_b_out[j], mods, 2, N, 0, h)
            if ctx_stream:
                vc, x0c = hyena_in_proj(hc, norm1_g[i], mods, 0, 1, B * CTX, B, CTX, win, hy_b_in[j],
                                        hy_conv_w[j], hy_conv_b[j])
                zc = hyena_mix(vc, x0c, B, CTX, D, filt, hy_bias[j])
                hc = matmul_residual(zc, wout, hy_b_out[j], mods, 2, B * CTX, B, hc)
        h = ec_moe(h, norm2_g[i], mods, 3, 4, 5, N, 0, B, moe_router[i], i, w1, w3, w2)
        if ctx_stream:
            hc = ec_moe(hc, norm2_g[i], mods, 3, 4, 5, B * CTX, B, B, moe_router[i], i, w1, w3, w2)
    return h.reshape(B, N, D)
```

```python
import functools
import math

import jax
import jax.numpy as jnp
import numpy as np
from jax import lax
from jax.experimental import pallas as pl
from jax.experimental.pallas import tpu as pltpu

F32 = jnp.float32
BF16 = jnp.bfloat16

GRID_W = 64
N_MIXERS = 2
NORM_EPS = 1e-6
NEG_INF = -1e30
HEAD_DIM = 128
LANES = 128
WIN_ROWS = 8
WIN_COLS = 16
HYENA_SHORT = 3
HYENA_EMB = 33
HYENA_FAST_DECAY = 0.3
HYENA_SLOW_DECAY = 1.5
HYENA_TARGET = 1e-2
CAPACITY_FACTOR = 2

Q_ROWS = 4
BAND_ROWS = 12

VMEM_LIMIT = 56 * 1024 * 1024


def _cparams(*sem):
    return pltpu.CompilerParams(dimension_semantics=sem, vmem_limit_bytes=VMEM_LIMIT)


def _tile(n, want, unit=128):
    if n <= want:
        return n
    t = (want // unit) * unit
    while n % t:
        t -= unit
    return t


def _ada_kernel(c_ref, w_ref, b_ref, o_ref):
    c = c_ref[...]
    s = c * jax.nn.sigmoid(c)
    o_ref[...] = jnp.dot(s.astype(BF16), w_ref[...].astype(BF16),
                         preferred_element_type=F32) + b_ref[...]


def ada_all(cond, ada_w, ada_b, tn=1024):
    L, D, E = ada_w.shape
    tn = _tile(E, tn)
    return pl.pallas_call(
        _ada_kernel,
        grid=(L, E // tn),
        in_specs=[
            pl.BlockSpec((8, D), lambda l, j: (0, 0)),
            pl.BlockSpec((None, D, tn), lambda l, j: (l, 0, j)),
            pl.BlockSpec((None, 1, tn), lambda l, j: (l, 0, j)),
        ],
        out_specs=pl.BlockSpec((None, 8, tn), lambda l, j: (l, 0, j)),
        out_shape=jax.ShapeDtypeStruct((L, 8, E), F32),
        compiler_params=_cparams("parallel", "parallel"),
        name="ada",
    )(cond, ada_w, ada_b.reshape(L, 1, E))


def _norm_mod(x, g, sh, sc):
    y = x * lax.rsqrt(jnp.mean(x * x, axis=-1, keepdims=True) + NORM_EPS) * g
    return y * (1.0 + sc) + sh


def _nm_matmul_kernel(x_ref, g_ref, sh_ref, sc_ref, w_ref, b_ref, hg_ref, o_ref, a_ref, *,
                      n_norm_sections, section_cols, q_scale):
    j = pl.program_id(1)

    @pl.when(j == 0)
    def _():
        a_ref[...] = _norm_mod(x_ref[...], g_ref[...], sh_ref[...], sc_ref[...]).astype(BF16)

    acc = jnp.dot(a_ref[...], w_ref[...], preferred_element_type=F32) + b_ref[...]
    tn = acc.shape[1]
    sec = (j * tn) // section_cols

    @pl.when(sec < n_norm_sections)
    def _():
        hg = hg_ref[...]
        mult = jnp.where(sec == 0, q_scale, 1.0).astype(F32)
        for h in range(tn // HEAD_DIM):
            c = acc[:, h * HEAD_DIM:(h + 1) * HEAD_DIM]
            c = c * lax.rsqrt(jnp.mean(c * c, axis=-1, keepdims=True) + NORM_EPS) * hg
            o_ref[h] = (c * mult).astype(o_ref.dtype)

    @pl.when(sec >= n_norm_sections)
    def _():
        for h in range(tn // HEAD_DIM):
            o_ref[h] = acc[:, h * HEAD_DIM:(h + 1) * HEAD_DIM].astype(o_ref.dtype)


def nm_matmul(x, g, mods, k_shift, k_scale, group_rows, group_base, w, b, head_g,
              q_scale=1.0, out_dtype=BF16, tm=1024, tn=1024):
    R, D = x.shape
    E = w.shape[1]
    tm = _tile(group_rows, tm, 8)
    tn = _tile(D, tn)
    n_norm = 2

    def grp(i):
        return group_base + (i * tm) // group_rows

    def hg_map(i, j):
        return (jnp.minimum((j * tn) // D, 1), 0, 0)

    kern = functools.partial(_nm_matmul_kernel, n_norm_sections=n_norm, section_cols=D,
                             q_scale=q_scale)
    return pl.pallas_call(
        kern,
        grid=(R // tm, E // tn),
        in_specs=[
            pl.BlockSpec((tm, D), lambda i, j: (i, 0)),
            pl.BlockSpec((1, D), lambda i, j: (0, 0)),
            pl.BlockSpec((None, None, 1, D), lambda i, j: (k_shift, grp(i), 0, 0)),
            pl.BlockSpec((None, None, 1, D), lambda i, j: (k_scale, grp(i), 0, 0)),
            pl.BlockSpec((D, tn), lambda i, j: (0, j)),
            pl.BlockSpec((1, tn), lambda i, j: (0, j)),
            pl.BlockSpec((None, 1, HEAD_DIM), hg_map),
        ],
        out_specs=pl.BlockSpec((tn // HEAD_DIM, tm, HEAD_DIM), lambda i, j: (j, i, 0)),
        out_shape=jax.ShapeDtypeStruct((E // HEAD_DIM, R, HEAD_DIM), out_dtype),
        scratch_shapes=[pltpu.VMEM((tm, D), BF16)],
        compiler_params=_cparams("parallel", "arbitrary"),
        name="nm_matmul",
    )(x, g.reshape(1, D), mods, mods, w, b.reshape(1, E), head_g)


def _mm_res_kernel(x_ref, w_ref, b_ref, gate_ref, res_ref, o_ref):
    if len(x_ref.shape) == 3:
        x = jnp.concatenate([x_ref[h] for h in range(x_ref.shape[0])], axis=1)
    else:
        x = x_ref[...]
    y = jnp.dot(x, w_ref[...], preferred_element_type=F32) + b_ref[...]
    o_ref[...] = res_ref[...] + gate_ref[...] * y


def matmul_residual(x, w, b, mods, k_gate, group_rows, group_base, res, tm=1024, tn=1024):
    K, E = w.shape
    R = res.shape[0]
    tm = _tile(group_rows, tm, 8)
    tn = _tile(E, tn)

    def grp(i):
        return group_base + (i * tm) // group_rows

    if x.ndim == 3:
        x_spec = pl.BlockSpec((x.shape[0], tm, x.shape[2]), lambda i, j: (0, i, 0))
    else:
        x_spec = pl.BlockSpec((tm, K), lambda i, j: (i, 0))
    return pl.pallas_call(
        _mm_res_kernel,
        grid=(R // tm, E // tn),
        in_specs=[
            x_spec,
            pl.BlockSpec((K, tn), lambda i, j: (0, j)),
            pl.BlockSpec((1, tn), lambda i, j: (0, j)),
            pl.BlockSpec((None, None, 1, tn), lambda i, j: (k_gate, grp(i), 0, j)),
            pl.BlockSpec((tm, tn), lambda i, j: (i, j)),
        ],
        out_specs=pl.BlockSpec((tm, tn), lambda i, j: (i, j)),
        out_shape=jax.ShapeDtypeStruct((R, E), F32),
        compiler_params=_cparams("parallel", "parallel"),
        name="matmul_residual",
    )(x, w, b.reshape(1, E), mods, res)


def _na_bias_tables(rpb, rows):
    W = GRID_W
    per = LANES // W
    groups = BAND_ROWS // per
    cols = np.arange(W)
    cs = np.clip(cols - WIN_COLS // 2, 0, W - WIN_COLS)
    in_win = (cols[None, :] >= cs[:, None]) & (cols[None, :] < cs[:, None] + WIN_COLS)
    dc = np.clip(cols[None, :] - cols[:, None] + WIN_COLS - 1, 0, 2 * WIN_COLS - 2)
    n_dr, n_dc = 2 * WIN_ROWS - 1, 2 * WIN_COLS - 1
    onehot_r = np.zeros((3, Q_ROWS, BAND_ROWS, n_dr), np.float32)
    mask = np.zeros((3, Q_ROWS, W, BAND_ROWS, W), bool)
    for var, r0 in enumerate((0, Q_ROWS, rows - Q_ROWS)):
        bs = _band_start(r0, rows)
        for i in range(Q_ROWS):
            r = r0 + i
            rs = int(np.clip(r - WIN_ROWS // 2, 0, rows - WIN_ROWS))
            for jj in range(BAND_ROWS):
                kr = bs + jj
                if rs <= kr < rs + WIN_ROWS:
                    onehot_r[var, i, jj, kr - r + WIN_ROWS - 1] = 1.0
                    mask[var, i, :, jj, :] = in_win
    onehot_c = (dc[:, :, None] == np.arange(n_dc)).astype(np.float32)
    pick = np.einsum('pP,qkc->qpkPc', np.eye(per, dtype=np.float32), onehot_c).reshape(W, LANES, per * n_dc)
    onehot_r = onehot_r.reshape(3, Q_ROWS, groups, per, n_dr)
    mask = mask.reshape(3, Q_ROWS, W, groups, per, W).transpose(0, 3, 1, 2, 4, 5).reshape(
        3, groups, Q_ROWS * W, LANES)
    hp = lax.Precision.HIGHEST
    H = rpb.shape[0]
    t = jnp.einsum('hrc,vijpr->hvjipc', rpb.astype(F32), onehot_r, precision=hp)
    t = t.reshape(H, 3, groups, Q_ROWS, per * n_dc)
    t = jnp.einsum('hvjix,qyx->hvjiqy', t, pick, precision=hp)
    return jnp.where(mask[None], t.reshape(H, 3, groups, Q_ROWS * W, LANES), NEG_INF)


def _band_start(r0, rows):
    return int(np.clip(r0 - WIN_ROWS // 2, 0, rows - BAND_ROWS))


def _na_kernel(q_ref, k_ref, v_ref, qc_ref, kc_ref, vc_ref, bias_ref, o_ref, oc_ref, *, rows):
    W = GRID_W
    nblk = rows // Q_ROWS
    kc = kc_ref[...]
    vc = vc_ref[...]
    nt = (((1,), (1,)), ((), ()))

    def attend(q, parts):
        ss = [lax.dot_general(q, k, nt, preferred_element_type=F32) + (0.0 if bias is None else bias)
              for k, _, bias in parts]
        m = functools.reduce(jnp.maximum, [jnp.max(s, axis=-1, keepdims=True) for s in ss])
        ps = [jnp.exp(s - m) for s in ss]
        den = sum(jnp.sum(p, axis=-1, keepdims=True) for p in ps)
        o = sum(jnp.dot(p.astype(BF16), v, preferred_element_type=F32)
                for p, (_, v, _) in zip(ps, parts))
        return o / den

    def body(blk, carry):
        r0 = blk * Q_ROWS
        bs = jnp.clip(r0 - WIN_ROWS // 2, 0, rows - BAND_ROWS)
        var = jnp.where(blk == 0, 0, jnp.where(blk == nblk - 1, 2, 1))
        q0 = pl.multiple_of(r0 * W, Q_ROWS * W)
        k0 = pl.multiple_of(bs * W, W)
        q = q_ref[pl.ds(q0, Q_ROWS * W), :]
        kb = k_ref[pl.ds(k0, BAND_ROWS * W), :]
        vb = v_ref[pl.ds(k0, BAND_ROWS * W), :]
        bias = jnp.concatenate([bias_ref[var, c] for c in range(bias_ref.shape[1])], axis=1)
        o = attend(q, [(kb, vb, bias), (kc, vc, None)])
        o_ref[pl.ds(q0, Q_ROWS * W), :] = o.astype(o_ref.dtype)
        return carry

    lax.fori_loop(0, nblk, body, 0, unroll=2)
    oc_ref[...] = attend(qc_ref[...], [(kc, vc, None)]).astype(oc_ref.dtype)


def na_attention(qkv, qkv_c, bias, B, N, CTX, D):
    H = D // HEAD_DIM
    rows = N // GRID_W
    kern = functools.partial(_na_kernel, rows=rows)

    def spec(n, sec):
        return pl.BlockSpec((None, n, HEAD_DIM), lambda h, b: (sec * H + h, b, 0))

    return pl.pallas_call(
        kern,
        grid=(H, B),
        in_specs=[spec(N, 0), spec(N, 1), spec(N, 2), spec(CTX, 0), spec(CTX, 1), spec(CTX, 2),
                  pl.BlockSpec((None,) + bias.shape[1:], lambda h, b: (h, 0, 0, 0, 0))],
        out_specs=[spec(N, 0), spec(CTX, 0)],
        out_shape=[jax.ShapeDtypeStruct((H, B * N, HEAD_DIM), BF16),
                   jax.ShapeDtypeStruct((H, B * CTX, HEAD_DIM), BF16)],
        compiler_params=_cparams("parallel", "parallel"),
        name="na_attention",
    )(qkv, qkv, qkv, qkv_c, qkv_c, qkv_c, bias)


def _expert_kernel(idx_ref, a_hbm, w1_ref, w3_ref, w2_ref, gate_ref, o_ref, xbuf, xs, yacc, sem, *,
                   n_tok, cap, bg):
    e, g, f = pl.program_id(0), pl.program_id(1), pl.program_id(2)
    n_exp, ng, nf = pl.num_programs(0), pl.num_programs(1), pl.num_programs(2)
    step = e * ng + g
    slot = step % 2
    nsteps = n_exp * ng
    rows = bg * cap
    S = xbuf.shape[1] // rows
    part = rows // nf

    def copy_row(ee, gg, sl, r0, u=0, priority=0):
        unit = math.gcd(part, cap)
        b = gg * bg + (r0 + (u // unit) * unit) // cap
        c = (r0 + (u // unit) * unit) % cap + u % unit
        row = b * n_tok + idx_ref[(b * n_exp + ee) * cap + c]
        src = pl.ds(pl.multiple_of(row * S, S), S)
        dst = pl.ds(pl.multiple_of((r0 + u) * S, S), S)
        pltpu.make_async_copy(a_hbm.at[src, :], xbuf.at[sl, dst, :], sem.at[sl]).start(priority=priority)

    def wait_rows(sl):
        pltpu.make_async_copy(a_hbm.at[pl.ds(0, rows * S), :], xbuf.at[sl], sem.at[sl]).wait()

    @pl.when((step == 0) & (f == 0))
    def _():
        def body(r, carry):
            copy_row(e, g, slot, r)
            return carry

        lax.fori_loop(0, rows, body, 0, unroll=8)

    @pl.when(f == 0)
    def _():
        wait_rows(slot)
        hi, lo = _load_token_tiles(xbuf, (slot,), 0, rows, S)
        xs[...] = jnp.concatenate([p.astype(BF16) for p in hi + lo], axis=1)
        yacc[...] = jnp.zeros_like(yacc)

    nxt = (step + 1) % nsteps
    for u in range(part):
        copy_row(nxt // ng, nxt % ng, 1 - slot, f * part, u, priority=u % 2)

    x = xs[...]
    h1 = jnp.dot(x, w1_ref[...].astype(BF16), preferred_element_type=F32)
    h3 = jnp.dot(x, w3_ref[...].astype(BF16), preferred_element_type=F32)
    hid = (h1 * jax.nn.sigmoid(h1) * h3).astype(BF16)
    w2 = w2_ref[...].astype(BF16)
    D = w2.shape[1]
    cw = min(D, EXPERT_OUT_COLS)
    for c in range(D // cw):
        cols = slice(c * cw, (c + 1) * cw)
        yacc[:, cols] += jnp.dot(hid, w2[:, cols], preferred_element_type=F32)

    @pl.when(f == nf - 1)
    def _():
        packed = _pack_bf16_pair(yacc[...] * gate_ref[...].reshape(rows, 1))
        for i in range(bg):
            _store_token_tiles(o_ref, (i,), packed[i * cap:(i + 1) * cap])

    @pl.when((step == nsteps - 1) & (f == nf - 1))
    def _():
        wait_rows(1 - slot)


def _pack_bf16_pair(y):
    half = y.shape[1] // 2
    hi = pltpu.bitcast(y[:, :half].astype(BF16).astype(F32), jnp.uint32)
    lo = pltpu.bitcast(y[:, half:].astype(BF16).astype(F32), jnp.uint32)
    return hi | (lo >> 16)


def _unpack_bf16_pair(u):
    hi = pltpu.bitcast(u & jnp.uint32(0xFFFF0000), F32)
    lo = pltpu.bitcast(u << 16, F32)
    return hi, lo


def _store_token_tiles(ref, lead, packed):
    rows, S = packed.shape[0], packed.shape[1] // LANES
    for s in range(S):
        ref[(*lead, pl.ds(s, rows, stride=S), slice(None))] = packed[:, s * LANES:(s + 1) * LANES]


def _load_token_tiles(ref, lead, first, rows, S):
    pieces = [_unpack_bf16_pair(ref[(*lead, pl.ds(first * S + s, rows, stride=S), slice(None))])
              for s in range(S)]
    return [p[0] for p in pieces], [p[1] for p in pieces]


def expert_ffn(a, idx_flat, layer, w1, w3, w2, gate, B, n_tok):
    _, E, D, Fh = w1.shape
    C = gate.shape[2]
    S = D // (2 * LANES)
    bg = max(1, min(B, EXPERT_ROWS // C))
    while B % bg:
        bg -= 1
    fc = _tile(Fh, EXPERT_HIDDEN_CHUNK)
    assert (bg * C) % (Fh // fc) == 0
    kern = functools.partial(_expert_kernel, n_tok=n_tok, cap=C, bg=bg)
    grid_spec = pltpu.PrefetchScalarGridSpec(
        num_scalar_prefetch=1,
        grid=(E, B // bg, Fh // fc),
        in_specs=[
            pl.BlockSpec(memory_space=pl.ANY),
            pl.BlockSpec((None, None, D, fc), lambda e, g, f, idx: (layer, e, 0, f)),
            pl.BlockSpec((None, None, D, fc), lambda e, g, f, idx: (layer, e, 0, f)),
            pl.BlockSpec((None, None, fc, D), lambda e, g, f, idx: (layer, e, f, 0)),
            pl.BlockSpec((bg, None, C, 1), lambda e, g, f, idx: (g, e, 0, 0)),
        ],
        out_specs=pl.BlockSpec((bg, None, C * S, LANES), lambda e, g, f, idx: (g, e, 0, 0)),
        scratch_shapes=[pltpu.VMEM((2, bg * C * S, LANES), jnp.uint32), pltpu.VMEM((bg * C, D), BF16),
                        pltpu.VMEM((bg * C, D), F32), pltpu.SemaphoreType.DMA((2,))],
    )
    return pl.pallas_call(
        kern,
        grid_spec=grid_spec,
        out_shape=jax.ShapeDtypeStruct((B, E, C * S, LANES), jnp.uint32),
        compiler_params=_cparams("arbitrary", "arbitrary", "arbitrary"),
        name="expert_ffn",
    )(idx_flat, a, w1, w3, w2, gate)


def _moe_prep_kernel(x_ref, g_ref, sh_ref, sc_ref, r_ref, a_ref, aff_ref):
    a = _norm_mod(x_ref[...], g_ref[...], sh_ref[...], sc_ref[...])
    _store_token_tiles(a_ref, (), _pack_bf16_pair(a))
    logits = lax.dot_general(r_ref[...], a.astype(BF16), (((1,), (1,)), ((), ())),
                             preferred_element_type=F32)
    m = jnp.max(logits, axis=0, keepdims=True)
    p = jnp.exp(logits - m)
    aff_ref[...] = p / jnp.sum(p, axis=0, keepdims=True)


def moe_prep(x, g, mods, k_shift, k_scale, group_rows, group_base, router_t, tm=512):
    R, D = x.shape
    E = router_t.shape[0]
    S = D // (2 * LANES)
    tm = _tile(group_rows, tm, 128)

    def grp(i):
        return group_base + (i * tm) // group_rows

    return pl.pallas_call(
        _moe_prep_kernel,
        grid=(R // tm,),
        in_specs=[
            pl.BlockSpec((tm, D), lambda i: (i, 0)),
            pl.BlockSpec((1, D), lambda i: (0, 0)),
            pl.BlockSpec((None, None, 1, D), lambda i: (k_shift, grp(i), 0, 0)),
            pl.BlockSpec((None, None, 1, D), lambda i: (k_scale, grp(i), 0, 0)),
            pl.BlockSpec((E, D), lambda i: (0, 0)),
        ],
        out_specs=[pl.BlockSpec((tm * S, LANES), lambda i: (i, 0)),
                   pl.BlockSpec((E, tm), lambda i: (0, i))],
        out_shape=[jax.ShapeDtypeStruct((R * S, LANES), jnp.uint32), jax.ShapeDtypeStruct((E, R), F32)],
        compiler_params=_cparams("parallel"),
        name="moe_prep",
    )(x, g.reshape(1, D), mods, mods, router_t)


EXPERT_ROWS = 1024
EXPERT_HIDDEN_CHUNK = 256
EXPERT_OUT_COLS = 512
COMBINE_TOKENS = 128
HIT_UNROLL = 4
COMBINE_ROW_GROUPS = 2
CUMSUM_CHUNK = 256
ROUTE_BLOCK_LANES = 512
TILE_TABLE_LANES = 128


def _route_select_kernel(aff_ref, tri_ref, ltri_ref, lt_ref, in_ref,
                         posi_ref, key_ref, slab_ref, p0_ref, kmax_ref, *, cap):
    E, N = aff_ref.shape
    aff = aff_ref[...]
    bits = pltpu.bitcast(aff, jnp.int32)
    cur = jnp.zeros((E, 1), jnp.int32)
    for bit in range(30, -1, -1):
        cand = cur | (1 << bit)
        cnt = jnp.sum(jnp.where(bits >= cand, 1.0, 0.0), axis=1, keepdims=True)
        cur = jnp.where(cnt >= cap, cand, cur)
    gt = bits > cur
    eq = bits == cur
    need = cap - jnp.sum(jnp.where(gt, 1.0, 0.0), axis=1, keepdims=True)

    def cumsum_incl(x):
        tri = tri_ref[...]
        w = min(CUMSUM_CHUNK, N)
        off = jnp.zeros((E, 1), F32)
        outs = []
        for c in range(N // w):
            xc = x[:, c * w:(c + 1) * w]
            outs.append(jnp.dot(xc.astype(BF16), tri[:w, :w], preferred_element_type=F32) + off)
            off = off + jnp.sum(xc, axis=1, keepdims=True)
        return jnp.concatenate(outs, axis=1) if len(outs) > 1 else outs[0]

    eqf = jnp.where(eq, 1.0, 0.0)
    tie_rank = cumsum_incl(eqf) - eqf
    sel = gt | (eq & (tie_rank < need))
    self_ = jnp.where(sel, 1.0, 0.0)
    pos_incl = cumsum_incl(self_)
    posi_ref[...] = pos_incl
    key_ref[...] = jnp.where(sel, pos_incl - self_, -1.0)
    selb = self_.astype(BF16)
    erank = jnp.dot(ltri_ref[...], selb, preferred_element_type=F32)
    in_tile = lax.broadcasted_iota(jnp.int32, (E, N), 1) & (COMBINE_TOKENS - 1)
    slab_ref[...] = erank * COMBINE_TOKENS + in_tile.astype(F32)
    p0_ref[...] = jnp.dot(selb, lt_ref[...], preferred_element_type=F32).astype(jnp.int32)
    count = jnp.sum(self_, axis=0, keepdims=True)
    kk = lax.broadcasted_iota(jnp.int32, (E, 1), 0).astype(F32)
    over = jnp.where(count > kk, 1.0, 0.0).astype(BF16)
    per_tile = jnp.dot(over, in_ref[...], preferred_element_type=F32)
    kmax_ref[...] = jnp.sum(jnp.where(per_tile > 0.0, 1.0, 0.0), axis=0, keepdims=True).astype(jnp.int32)


def _route_compact_kernel(p0_ref, aff_ref, posi_ref, key_ref, slab_ref, idx_ref, gate_ref, rank_ref, *,
                          cap, tn, lanes, width):
    b = pl.program_id(0)
    E, N = aff_ref.shape
    ch = min(LANES, cap)
    tpb = width // tn
    diag = lax.broadcasted_iota(jnp.int32, (ch, ch), 0) == lax.broadcasted_iota(jnp.int32, (ch, ch), 1)

    def as_row(col):
        return jnp.sum(jnp.where(diag, col, 0.0), axis=0, keepdims=True)

    def e_body(e, carry):
        q = (b * E + e) * lanes

        def c_body(cc, tiles):
            c0 = cc * ch
            t_lo = lax.while_loop(lambda t: p0_ref[q + t + 1] <= c0, lambda t: t + 1, tiles[0])
            t_hi = lax.while_loop(lambda t: p0_ref[q + t] < c0 + ch, lambda t: t + 1, tiles[1])
            j_lo = t_lo // tpb
            j_hi = (t_hi + tpb - 1) // tpb
            ccol = (c0 + lax.broadcasted_iota(jnp.int32, (ch, 1), 0)).astype(F32)

            def j_body(j, acc):
                row = pl.ds(e, 1)
                blk = pl.ds(pl.multiple_of(j * width, width), width)
                pi, ky, af, sl = posi_ref[row, blk], key_ref[row, blk], aff_ref[row, blk], slab_ref[row, blk]
                acc = list(acc)
                for u in range(width // LANES):
                    part = slice(u * LANES, (u + 1) * LANES)
                    hit = ky[:, part] == ccol
                    acc[0] = acc[0] + jnp.where(pi[:, part] <= ccol, 1.0, 0.0)
                    acc[1] = acc[1] + jnp.where(hit, af[:, part], 0.0)
                    acc[2] = acc[2] + jnp.where(hit, sl[:, part], 0.0)
                return tuple(acc)

            zero = jnp.zeros((ch, LANES), F32)
            iv, gv, rv = [jnp.sum(a, axis=1, keepdims=True)
                          for a in lax.fori_loop(j_lo, j_hi, j_body, (zero, zero, zero))]
            idx_ref[pl.ds(e, 1), pl.ds(c0, ch)] = as_row(iv).astype(jnp.int32) + j_lo * width
            gate_ref[e, pl.ds(c0, ch), :] = gv
            rank_ref[pl.ds(e, 1), pl.ds(c0, ch)] = as_row(rv).astype(jnp.int32)
            return (t_lo, t_hi)

        tiles = (0, 0)
        for cc in range(cap // ch):
            tiles = c_body(cc, tiles)
        return carry

    for e in range(E):
        e_body(e, 0)


def moe_route(aff_t, B, N, cap):
    E = aff_t.shape[0]
    tn = COMBINE_TOKENS
    lanes = TILE_TABLE_LANES
    assert N // tn + 1 <= lanes and N % tn == 0 and N % min(CUMSUM_CHUNK, N) == 0
    w = min(CUMSUM_CHUNK, N)
    tri = (np.arange(w)[:, None] <= np.arange(w)[None, :]).astype(np.float32)
    ltri = (np.arange(E)[None, :] < np.arange(E)[:, None]).astype(np.float32)
    tok = np.arange(N)[:, None]
    tile = np.arange(lanes)[None, :]
    before = (tok < tile * tn).astype(np.float32)
    inside = (tok // tn == tile).astype(np.float32)
    const = lambda shape: pl.BlockSpec(shape, lambda b: (0,) * len(shape))
    per_tok = pl.BlockSpec((None, E, N), lambda b, *_: (b, 0, 0))
    posi, key, slab, p0, kmax = pl.pallas_call(
        functools.partial(_route_select_kernel, cap=cap),
        grid=(B,),
        in_specs=[pl.BlockSpec((E, N), lambda b: (0, b)), const((w, w)), const((E, E)),
                  const((N, lanes)), const((N, lanes))],
        out_specs=[per_tok, per_tok, per_tok,
                   pl.BlockSpec((None, E, lanes), lambda b: (b, 0, 0)),
                   pl.BlockSpec((None, 1, lanes), lambda b: (b, 0, 0))],
        out_shape=[jax.ShapeDtypeStruct((B, E, N), F32)] * 3 + [
            jax.ShapeDtypeStruct((B, E, lanes), jnp.int32), jax.ShapeDtypeStruct((B, 1, lanes), jnp.int32)],
        compiler_params=_cparams("parallel"),
        name="moe_route_select",
    )(aff_t, jnp.asarray(tri, BF16), jnp.asarray(ltri, BF16), jnp.asarray(before, BF16),
      jnp.asarray(inside, BF16))
    width = min(ROUTE_BLOCK_LANES, N)
    sel3 = pl.BlockSpec((None, E, cap, 1), lambda b, *_: (b, 0, 0, 0))
    rows2 = pl.BlockSpec((None, E, cap), lambda b, *_: (b, 0, 0))
    idx, gate, rank = pl.pallas_call(
        functools.partial(_route_compact_kernel, cap=cap, tn=tn, lanes=lanes, width=width),
        grid_spec=pltpu.PrefetchScalarGridSpec(
            num_scalar_prefetch=1,
            grid=(B,),
            in_specs=[pl.BlockSpec((E, N), lambda b, *_: (0, b)), per_tok, per_tok, per_tok],
            out_specs=[rows2, sel3, rows2],
        ),
        out_shape=[jax.ShapeDtypeStruct((B, E, cap), jnp.int32),
                   jax.ShapeDtypeStruct((B, E, cap, 1), F32),
                   jax.ShapeDtypeStruct((B, E, cap), jnp.int32)],
        compiler_params=_cparams("parallel"),
        name="moe_route_compact",
    )(p0.reshape(-1), aff_t, posi, key, slab)
    return idx, gate, rank, p0, kmax


def _combine_kernel(rank_ref, p0_ref, kmax_ref, ys_hbm, h_ref, g2_ref, o_ref, rbuf, sem, *,
                    n_exp, cap, tn, lanes):
    b, t = pl.program_id(0), pl.program_id(1)
    nt = pl.num_programs(1)
    step = b * nt + t
    slot = step % 2
    S = rbuf.shape[1] // (n_exp * tn)

    def tile_hits(bb, tt):
        def body(e, tot):
            q = (bb * n_exp + e) * lanes + tt
            return tot + p0_ref[q + 1] - p0_ref[q]
        return lax.fori_loop(0, n_exp, body, 0)

    def fetch(bb, tt, sl):
        km = kmax_ref[bb * lanes + tt]
        for k in range(n_exp):
            @pl.when(k < km)
            def _():
                rbuf[sl, k * tn * S:(k + 1) * tn * S, :] = jnp.zeros((tn * S, LANES), jnp.uint32)

        def copy_row(row, priority=0):
            src = pl.ds(pl.multiple_of(row * S, S), S)
            dst = pl.ds(pl.multiple_of(rank_ref[row] * S, S), S)
            pltpu.make_async_copy(ys_hbm.at[src, :], rbuf.at[sl, dst, :], sem.at[sl]).start(priority=priority)

        def e_body(e, carry):
            q = (bb * n_exp + e) * lanes + tt
            lo = (bb * n_exp + e) * cap + p0_ref[q]
            n = p0_ref[q + 1] - p0_ref[q]
            groups = n // HIT_UNROLL

            def g_body(g, carry2):
                for u in range(HIT_UNROLL):
                    copy_row(lo + g * HIT_UNROLL + u, priority=u % 2)
                return carry2

            def r_body(c, carry2):
                copy_row(lo + c)
                return carry2

            lax.fori_loop(0, groups, g_body, 0)
            lax.fori_loop(groups * HIT_UNROLL, n, r_body, 0)
            return carry

        lax.fori_loop(0, n_exp, e_body, 0)

    @pl.when(step == 0)
    def _():
        fetch(b, t, slot)

    @pl.when(step + 1 < pl.num_programs(0) * nt)
    def _():
        nxt = step + 1
        fetch(nxt // nt, nxt % nt, 1 - slot)

    hits = tile_hits(b, t)

    nrows = hits * S
    bulk = pl.multiple_of((nrows // 8) * 8, 8)

    @pl.when(bulk > 0)
    def _():
        pltpu.make_async_copy(ys_hbm.at[pl.ds(0, bulk), :], rbuf.at[slot, pl.ds(0, bulk), :],
                              sem.at[slot]).wait()

    def wait_row(i, carry):
        pltpu.make_async_copy(ys_hbm.at[pl.ds(0, 1), :], rbuf.at[slot, pl.ds(0, 1), :], sem.at[slot]).wait()
        return carry

    lax.fori_loop(0, nrows - bulk, wait_row, 0)

    km = kmax_ref[b * lanes + t]
    g2 = g2_ref[...]
    rg = COMBINE_ROW_GROUPS
    for r in range(0, tn // 8, rg):
        def k_body(k, tot):
            new = []
            for i in range(rg):
                hi, lo = _load_token_tiles(rbuf, (slot,), k * tn + (r + i) * 8, 8, S)
                new += hi + lo
            return tuple(t_ + p for t_, p in zip(tot, new))

        tot = lax.fori_loop(0, km, k_body, (jnp.zeros((8, LANES), F32),) * (2 * S * rg))
        for i in range(rg):
            rows = slice((r + i) * 8, (r + i + 1) * 8)
            for s in range(2 * S):
                cols = slice(s * LANES, (s + 1) * LANES)
                o_ref[rows, cols] = h_ref[rows, cols] + g2[:, cols] * tot[i * 2 * S + s]


def moe_combine(ys, h, mods, k_gate, group_rows, group_base, rank, p0, kmax, B, N):
    R, D = h.shape
    E, cap = rank.shape[1], rank.shape[2]
    tn = COMBINE_TOKENS
    lanes = TILE_TABLE_LANES
    nt = N // tn

    def grp(b, t):
        return group_base + ((b * nt + t) * tn) // group_rows

    kern = functools.partial(_combine_kernel, n_exp=E, cap=cap, tn=tn, lanes=lanes)
    grid_spec = pltpu.PrefetchScalarGridSpec(
        num_scalar_prefetch=3,
        grid=(B, nt),
        in_specs=[
            pl.BlockSpec(memory_space=pl.ANY),
            pl.BlockSpec((tn, D), lambda b, t, *_: (b * nt + t, 0)),
            pl.BlockSpec((None, None, 1, D), lambda b, t, *_: (k_gate, grp(b, t), 0, 0)),
        ],
        out_specs=pl.BlockSpec((tn, D), lambda b, t, *_: (b * nt + t, 0)),
        scratch_shapes=[pltpu.VMEM((2, E * tn * (D // (2 * LANES)), LANES), jnp.uint32),
                        pltpu.SemaphoreType.DMA((2,))],
    )
    return pl.pallas_call(
        kern,
        grid_spec=grid_spec,
        out_shape=jax.ShapeDtypeStruct((R, D), F32),
        compiler_params=_cparams("arbitrary", "arbitrary"),
        name="moe_combine",
    )(rank.reshape(-1), p0.reshape(-1), kmax.reshape(-1), ys, h, mods)


def ec_moe(h, g, mods, k_shift, k_scale, k_gate, group_rows, group_base, B, router, layer, w1, w3, w2):
    R, D = h.shape
    N = R // B
    E = router.shape[1]
    cap = CAPACITY_FACTOR * N // E
    a, aff_t = moe_prep(h, g, mods, k_shift, k_scale, group_rows, group_base, router.T.astype(BF16))
    idx, gate, rank, p0, kmax = moe_route(aff_t, B, N, cap)
    ys = expert_ffn(a, idx.reshape(-1), layer, w1, w3, w2, gate, B, N)
    return moe_combine(ys.reshape(-1, LANES), h, mods, k_gate, group_rows, group_base,
                       rank, p0, kmax, B, N)


FFT_MINOR = 256
FFT_GROUP = 16


@functools.lru_cache(maxsize=None)
def _dft_tables(L):
    n = 2 * L
    n1 = n // FFT_MINOR
    nb = n1 // 2
    two_pi = 2.0 * math.pi
    a = np.arange(n1)
    ang = two_pi * ((a[:, None] * a[None, :]) % n1) / n1
    cos1, sin1 = np.cos(ang), np.sin(ang)
    eye = np.eye(FFT_GROUP)

    def kron(m):
        return np.kron(m, eye)

    def const(m):
        return jnp.asarray(m.astype(np.float32).astype(BF16))

    f1 = np.arange(n1)[:, None, None]
    f2 = np.arange(FFT_MINOR)[None, :, None]
    n2 = np.arange(FFT_MINOR)[None, None, :]
    ang = two_pi * ((n2 * (f1 + n1 * f2)) % n) / n
    gr, gi = np.cos(ang), -np.sin(ang)
    return dict(
        n1=n1, nb=nb,
        m_fwd_full=const(np.concatenate([kron(cos1), kron(-sin1)], axis=0)),
        m_fwd_half=const(np.concatenate([kron(cos1[:, :nb]), kron(-sin1[:, :nb])], axis=0)),
        m_inv=const(np.concatenate([kron(cos1[:nb]), kron(sin1[:nb])], axis=0)),
        g=const(np.concatenate([gr, gi], axis=1)),
        gt=const(np.concatenate([gr.transpose(0, 2, 1), gi.transpose(0, 2, 1)], axis=1)),
    )


HALO = 8


def _hyena_in_kernel(x_ref, xp_ref, xn_ref, g_ref, sh_ref, sc_ref, w0_ref, w1_ref, w2_ref, b_ref, cw_ref,
                     cb_ref, v_ref, x0_ref, a_ref, ah_ref, *, seq_len):
    i, j = pl.program_id(0), pl.program_id(1)
    tm = x_ref.shape[0]

    @pl.when(j == 0)
    def _():
        g, sh, sc = g_ref[...], sh_ref[...], sc_ref[...]
        a_ref[...] = _norm_mod(x_ref[...], g, sh, sc).astype(BF16)
        ah_ref[0:HALO, :] = _norm_mod(xp_ref[...], g, sh, sc)
        ah_ref[HALO:, :] = _norm_mod(xn_ref[...], g, sh, sc)

    first = (i * tm) % seq_len == 0
    last = ((i + 1) * tm) % seq_len == 0
    a = a_ref[...]
    ah = ah_ref[...].astype(BF16)
    row = lax.broadcasted_iota(jnp.int32, (tm, v_ref.shape[1]), 0)

    def section(k, w_ref):
        w = w_ref[...]
        p = jnp.dot(a, w, preferred_element_type=F32) + b_ref[k]
        ph = jnp.dot(ah, w, preferred_element_type=F32) + b_ref[k]
        above = jnp.where(first, 0.0, ph[HALO - 1:HALO])
        below = jnp.where(last, 0.0, ph[HALO:HALO + 1])
        cw = cw_ref[k]
        prev = jnp.where(row == 0, above, pltpu.roll(p, 1, 0))
        nxt = jnp.where(row == tm - 1, below, pltpu.roll(p, tm - 1, 0))
        return prev * cw[0:1] + p * cw[1:2] + nxt * cw[2:3] + cb_ref[k]

    x0 = section(0, w0_ref)
    x1 = section(1, w1_ref)
    v = section(2, w2_ref)
    v_ref[...] = (v * x1).astype(v_ref.dtype)
    x0_ref[...] = x0.astype(x0_ref.dtype)


def hyena_in_proj(x, g, mods, k_shift, k_scale, group_rows, group_base, seq_len, w, b, conv_w, conv_b,
                  tm=1024, tc=512):
    R, D = x.shape
    tm = _tile(math.gcd(group_rows, seq_len), tm, HALO)
    tc = _tile(D, tc)
    nc = D // tc
    cw = conv_w.reshape(HYENA_SHORT, 3, D).transpose(1, 0, 2)

    def grp(i):
        return group_base + (i * tm) // group_rows

    def wspec(k):
        return pl.BlockSpec((D, tc), lambda i, j: (0, k * nc + j))

    mod = lambda k: pl.BlockSpec((None, None, 1, D), lambda i, j: (k, grp(i), 0, 0))
    per_proj = lambda rows: pl.BlockSpec((3, rows, tc), lambda i, j: (0, 0, j))
    kern = functools.partial(_hyena_in_kernel, seq_len=seq_len)
    return pl.pallas_call(
        kern,
        grid=(R // tm, nc),
        in_specs=[pl.BlockSpec((tm, D), lambda i, j: (i, 0)),
                  pl.BlockSpec((HALO, D), lambda i, j: (jnp.maximum(i * (tm // HALO) - 1, 0), 0)),
                  pl.BlockSpec((HALO, D), lambda i, j: (jnp.minimum((i + 1) * (tm // HALO), R // HALO - 1), 0)),
                  pl.BlockSpec((1, D), lambda i, j: (0, 0)), mod(k_shift), mod(k_scale),
                  wspec(0), wspec(1), wspec(2), per_proj(1), per_proj(HYENA_SHORT), per_proj(1)],
        out_specs=[pl.BlockSpec((tm, tc), lambda i, j: (i, j))] * 2,
        out_shape=[jax.ShapeDtypeStruct((R, D), BF16)] * 2,
        scratch_shapes=[pltpu.VMEM((tm, D), BF16), pltpu.VMEM((2 * HALO, D), F32)],
        compiler_params=_cparams("parallel", "arbitrary"),
        name="hyena_in_proj",
    )(x, x, x, g.reshape(1, D), mods, mods, w, w, w, b.reshape(3, 1, D), cw, conv_b.reshape(3, 1, D))


def _filter_kernel(emb_ref, w1_ref, b1_ref, f1_ref, w2_ref, b2_ref, f2_ref, w3_ref, b3_ref, f3_ref,
                   wo_ref, dl_ref, k_ref, norm_ref, *, L):
    i = pl.program_id(0)
    tr = emb_ref.shape[0]
    hp = lax.Precision.HIGHEST
    emb = emb_ref[...]
    h = jnp.sin(f1_ref[...] * (jnp.dot(emb, w1_ref[...], precision=hp, preferred_element_type=F32) + b1_ref[...]))
    h = jnp.sin(f2_ref[...] * (jnp.dot(h, w2_ref[...], precision=hp, preferred_element_type=F32) + b2_ref[...]))
    h = jnp.sin(f3_ref[...] * (jnp.dot(h, w3_ref[...], precision=hp, preferred_element_type=F32) + b3_ref[...]))
    k = jnp.dot(h, wo_ref[...], precision=hp, preferred_element_type=F32)
    k = k * jnp.exp(-emb[:, 0:1] * dl_ref[...])
    row = i * tr + lax.broadcasted_iota(jnp.int32, (tr, 1), 0)
    k = jnp.where(row == L, 0.0, k)
    k_ref[...] = k

    @pl.when(i == 0)
    def _():
        norm_ref[...] = jnp.zeros_like(norm_ref)

    norm_ref[...] += jnp.sum(jnp.abs(k), axis=0, keepdims=True)


def hyena_filter(L, D, w1, b1, f1, w2, b2, f2, w3, b3, f3, w_out, tr=512):
    n = 2 * L
    tr = _tile(L, tr, 8)
    P = 128
    bands = (HYENA_EMB - 1) // 2
    d = np.arange(n)
    d = np.where(d <= L, d, n - d).clip(0, L - 1)
    t01 = np.linspace(0.0, 1.0, L)[d]
    wang = 2 * math.pi * d / L
    fr = np.linspace(1e-4, bands - 1, bands)
    emb = np.zeros((n, P), np.float32)
    emb[:, 0] = t01
    emb[:, 1:1 + bands] = np.cos(fr[None, :] * wang[:, None])
    emb[:, 1 + bands:1 + 2 * bands] = -np.sin(fr[None, :] * wang[:, None])

    def padw(w):
        return jnp.zeros((P, P), F32).at[:w.shape[0], :w.shape[1]].set(w)

    def padv(v):
        return jnp.zeros((1, P), F32).at[0, :v.shape[0]].set(v)

    wo = jnp.zeros((P, 2 * D), F32).at[:w_out.shape[0]].set(w_out)
    max_decay = math.log(HYENA_TARGET) / HYENA_FAST_DECAY
    min_decay = math.log(HYENA_TARGET) / HYENA_SLOW_DECAY
    deltas = jnp.abs(jnp.linspace(min_decay, max_decay, D, dtype=F32)).reshape(1, D)
    small = pl.BlockSpec((P, P), lambda i: (0, 0))
    vec = pl.BlockSpec((1, P), lambda i: (0, 0))
    kern = functools.partial(_filter_kernel, L=L)
    return pl.pallas_call(
        kern,
        grid=(n // tr,),
        in_specs=[pl.BlockSpec((tr, P), lambda i: (i, 0)),
                  small, vec, vec, small, vec, vec, small, vec, vec,
                  pl.BlockSpec((P, D), lambda i: (0, (i * tr) // L)),
                  pl.BlockSpec((1, D), lambda i: (0, 0))],
        out_specs=[pl.BlockSpec((tr, D), lambda i: (i, 0)), pl.BlockSpec((1, D), lambda i: (0, 0))],
        out_shape=[jax.ShapeDtypeStruct((n, D), F32), jax.ShapeDtypeStruct((1, D), F32)],
        compiler_params=_cparams("arbitrary"),
        name="hyena_filter",
    )(jnp.asarray(emb), padw(w1), padv(b1), padv(f1), padw(w2), padv(b2), padv(f2),
      padw(w3), padv(b3), padv(f3), wo, deltas)


def _fwd_major_kernel(*refs, n1, nb, has_imag):
    if has_imag:
        zr_ref, zi_ref, m_ref, ar_ref, ai_ref = refs
    else:
        zr_ref, m_ref, ar_ref, ai_ref = refs
    m = m_ref[...]
    half = n1 * FFT_GROUP

    def body(g, carry):
        def rows(blk):
            return pl.ds(pl.multiple_of(blk * FFT_MINOR + g * FFT_GROUP, FFT_GROUP), FFT_GROUP)

        def slab(ref):
            return jnp.concatenate([ref[rows(blk), :] for blk in range(nb)], axis=0).astype(BF16)

        pr = jnp.dot(m, slab(zr_ref), preferred_element_type=F32)
        if has_imag:
            pi = jnp.dot(m, slab(zi_ref), preferred_element_type=F32)
            ar, ai = pr[:half] - pi[half:], pi[:half] + pr[half:]
        else:
            ar, ai = pr[:half], pr[half:]
        for f1 in range(n1):
            sl = slice(f1 * FFT_GROUP, (f1 + 1) * FFT_GROUP)
            ar_ref[rows(f1), :] = ar[sl].astype(BF16)
            ai_ref[rows(f1), :] = ai[sl].astype(BF16)
        return carry

    lax.fori_loop(0, FFT_MINOR // FFT_GROUP, body, 0)


def fwd_major(z, tabs, pairs, tc=256):
    n1, nb = tabs["n1"], tabs["nb"]
    n = n1 * FFT_MINOR
    Bz, Lz, D = z.shape
    tc = _tile(D, tc)
    if pairs:
        P = Bz // 2
        m = tabs["m_fwd_half"]
        ins = [z, z, m]
        in_specs = [pl.BlockSpec((None, Lz, tc), lambda p, c: (2 * p, 0, c)),
                    pl.BlockSpec((None, Lz, tc), lambda p, c: (2 * p + 1, 0, c))]
        nblk = nb
    else:
        P = 1
        m = tabs["m_fwd_full"]
        ins = [z, m]
        in_specs = [pl.BlockSpec((None, Lz, tc), lambda p, c: (0, 0, c))]
        nblk = n1
    in_specs.append(pl.BlockSpec(m.shape, lambda p, c: (0, 0)))
    kern = functools.partial(_fwd_major_kernel, n1=n1, nb=nblk, has_imag=pairs)
    return pl.pallas_call(
        kern,
        grid=(P, D // tc),
        in_specs=in_specs,
        out_specs=[pl.BlockSpec((None, n, tc), lambda p, c: (p, 0, c))] * 2,
        out_shape=[jax.ShapeDtypeStruct((P, n, D), BF16)] * 2,
        compiler_params=_cparams("parallel", "parallel"),
        name="hyena_fwd_major",
    )(*ins)


def _spectrum_kernel(ar_ref, ai_ref, g_ref, s_ref, kr_ref, ki_ref):
    g = g_ref[...]
    h = FFT_MINOR
    pr = jnp.dot(g, ar_ref[...], preferred_element_type=F32)
    pi = jnp.dot(g, ai_ref[...], preferred_element_type=F32)
    s = s_ref[...]
    kr_ref[...] = (pr[:h] - pi[h:]) * s
    ki_ref[...] = (pi[:h] + pr[h:]) * s


def filter_spectrum(ar, ai, tabs, scale, tc=1024):
    _, n, D = ar.shape
    tc = _tile(D, tc)
    blk = pl.BlockSpec((None, FFT_MINOR, tc), lambda f, c: (0, f, c))
    out = pl.BlockSpec((FFT_MINOR, tc), lambda f, c: (f, c))
    return pl.pallas_call(
        _spectrum_kernel,
        grid=(tabs["n1"], D // tc),
        in_specs=[blk, blk, pl.BlockSpec((None, 2 * FFT_MINOR, FFT_MINOR), lambda f, c: (f, 0, 0)),
                  pl.BlockSpec((1, tc), lambda f, c: (0, c))],
        out_specs=[out, out],
        out_shape=[jax.ShapeDtypeStruct((n, D), F32)] * 2,
        compiler_params=_cparams("parallel", "parallel"),
        name="hyena_filter_spectrum",
    )(ar, ai, tabs["g"], scale)


def _minor_kernel(ar_ref, ai_ref, g_ref, gt_ref, kr_ref, ki_ref, br_ref, bi_ref):
    h = FFT_MINOR
    g = g_ref[...]
    pr = jnp.dot(g, ar_ref[...], preferred_element_type=F32)
    pi = jnp.dot(g, ai_ref[...], preferred_element_type=F32)
    xr, xi = pr[:h] - pi[h:], pi[:h] + pr[h:]
    kr, ki = kr_ref[...], ki_ref[...]
    yr = (xr * kr - xi * ki).astype(BF16)
    yi = (xr * ki + xi * kr).astype(BF16)
    gt = gt_ref[...]
    qr = jnp.dot(gt, yr, preferred_element_type=F32)
    qi = jnp.dot(gt, yi, preferred_element_type=F32)
    br_ref[...] = (qr[:h] + qi[h:]).astype(BF16)
    bi_ref[...] = (qi[:h] - qr[h:]).astype(BF16)


def minor_conv(ar, ai, tabs, kr, ki, tc=1024):
    P, n, D = ar.shape
    tc = _tile(D, tc)
    blk = pl.BlockSpec((None, FFT_MINOR, tc), lambda f, c, p: (p, f, c))
    tab = pl.BlockSpec((None, 2 * FFT_MINOR, FFT_MINOR), lambda f, c, p: (f, 0, 0))
    kblk = pl.BlockSpec((FFT_MINOR, tc), lambda f, c, p: (f, c))
    return pl.pallas_call(
        _minor_kernel,
        grid=(tabs["n1"], D // tc, P),
        in_specs=[blk, blk, tab, tab, kblk, kblk],
        out_specs=[blk, blk],
        out_shape=[jax.ShapeDtypeStruct((P, n, D), BF16)] * 2,
        compiler_params=_cparams("parallel", "parallel", "parallel"),
        name="hyena_minor_conv",
    )(ar, ai, tabs["g"], tabs["gt"], kr, ki)


def _inv_major_kernel(br_ref, bi_ref, m_ref, v_ref, x0_ref, bias_ref, o_ref, *, n1, nb):
    m = m_ref[...]
    half = nb * FFT_GROUP
    bias = bias_ref[...]

    def body(g, carry):
        def rows(blk):
            return pl.ds(pl.multiple_of(blk * FFT_MINOR + g * FFT_GROUP, FFT_GROUP), FFT_GROUP)

        def slab(ref):
            return jnp.concatenate([ref[rows(f1), :] for f1 in range(n1)], axis=0)

        pr = jnp.dot(m, slab(br_ref), preferred_element_type=F32)
        pi = jnp.dot(m, slab(bi_ref), preferred_element_type=F32)
        ys = (pr[:half] - pi[half:], pi[:half] + pr[half:])
        for s in range(2):
            for blk in range(nb):
                y = ys[s][blk * FFT_GROUP:(blk + 1) * FFT_GROUP]
                v = v_ref[s, rows(blk), :].astype(F32)
                x0 = x0_ref[s, rows(blk), :].astype(F32)
                o_ref[s, rows(blk), :] = ((y + v * bias) * x0).astype(o_ref.dtype)
        return carry

    lax.fori_loop(0, FFT_MINOR // FFT_GROUP, body, 0)


def inv_major(br, bi, tabs, v, x0, bias, tc=256):
    n1, nb = tabs["n1"], tabs["nb"]
    P, n, D = br.shape
    B, L, _ = v.shape
    tc = _tile(D, tc)
    m = tabs["m_inv"]
    blk = pl.BlockSpec((None, n, tc), lambda p, c: (p, 0, c))
    pair = pl.BlockSpec((2, L, tc), lambda p, c: (p, 0, c))
    kern = functools.partial(_inv_major_kernel, n1=n1, nb=nb)
    return pl.pallas_call(
        kern,
        grid=(P, D // tc),
        in_specs=[blk, blk, pl.BlockSpec(m.shape, lambda p, c: (0, 0)), pair, pair,
                  pl.BlockSpec((1, tc), lambda p, c: (0, c))],
        out_specs=pair,
        out_shape=jax.ShapeDtypeStruct((B, L, D), BF16),
        compiler_params=_cparams("parallel", "parallel"),
        name="hyena_inv_major",
    )(br, bi, m, v, x0, bias.reshape(1, D))


def hyena_mix(v, x0, B, L, D, filt, bias):
    assert B % 2 == 0 and L % FFT_MINOR == 0
    tabs = _dft_tables(L)
    v, x0 = v.reshape(B, L, D), x0.reshape(B, L, D)
    k_raw, k_norm = hyena_filter(L, D, *filt)
    kar, kai = fwd_major(k_raw[None], tabs, pairs=False)
    kr, ki = filter_spectrum(kar, kai, tabs, 1.0 / (2 * L * k_norm))
    ar, ai = fwd_major(v, tabs, pairs=True)
    br, bi = minor_conv(ar, ai, tabs, kr, ki)
    z = inv_major(br, bi, tabs, v, x0, bias)
    return z.reshape(B * L, D)


def kernel(x, c, ctx, c_ctx, ada_w, ada_b, norm1_g, norm2_g, na_w_qkv, na_w_o, na_q_g, na_k_g, na_rpb, hy_w_in, hy_b_in, hy_conv_w, hy_conv_b, hy_f_w1, hy_f_b1, hy_f_freq1, hy_f_w2, hy_f_b2, hy_f_freq2, hy_f_w3, hy_f_b3, hy_f_freq3, hy_f_wout, hy_bias, hy_w_out, hy_b_out, moe_router, moe_w1, moe_w3, moe_w2):
    B, N, D = x.shape
    CTX = ctx.shape[1]
    depth = ada_w.shape[0]
    mixer = [i % N_MIXERS for i in range(depth)]

    cond = jnp.concatenate([c, c_ctx[None, :], jnp.zeros((8 - B - 1, D), F32)], axis=0)
    mods_all = ada_all(cond, ada_w, ada_b)
    mods_all = mods_all.reshape(depth, 8, 6, 1, D).transpose(0, 2, 1, 3, 4)

    h = x.reshape(B * N, D)
    hc = ctx.reshape(B * CTX, D)
    zeros_d = jnp.zeros((D,), F32)
    w1, w3, w2 = moe_w1, moe_w3, moe_w2
    for i in range(depth):
        j = i // N_MIXERS
        ctx_stream = any(mixer[l] == 0 for l in range(i + 1, depth))
        ctx_in = ctx_stream or mixer[i] == 0
        mods = mods_all[i]
        if mixer[i] == 0:
            wqkv = na_w_qkv[j].astype(BF16)
            wo = na_w_o[j].astype(BF16)
            hg = jnp.stack([na_q_g[j], na_k_g[j]]).reshape(2, 1, HEAD_DIM)
            zeros_e = jnp.zeros((3 * D,), F32)
            qs = HEAD_DIM ** -0.5
            qkv = nm_matmul(h, norm1_g[i], mods, 0, 1, N, 0, wqkv, zeros_e, hg, qs)
            qkv_c = nm_matmul(hc, norm1_g[i], mods, 0, 1, B * CTX, B, wqkv, zeros_e, hg, qs)
            bias = _na_bias_tables(na_rpb[j], N // GRID_W)
            o, oc = na_attention(qkv, qkv_c, bias, B, N, CTX, D)
            h = matmul_residual(o, wo, zeros_d, mods, 2, N, 0, h)
            if ctx_stream:
                hc = matmul_residual(oc, wo, zeros_d, mods, 2, B * CTX, B, hc)
        else:
            win = hy_w_in[j].astype(BF16)
            wout = hy_w_out[j].astype(BF16)
            filt = (hy_f_w1[j], hy_f_b1[j], hy_f_freq1[j], hy_f_w2[j], hy_f_b2[j], hy_f_freq2[j],
                    hy_f_w3[j], hy_f_b3[j], hy_f_freq3[j], hy_f_wout[j])
            v, x0 = hyena_in_proj(h, norm1_g[i], mods, 0, 1, N, 0, N, win, hy_b_in[j],
                                  hy_conv_w[j], hy_conv_b[j])
            z = hyena_mix(v, x0, B, N, D, filt, hy_bias[j])
            h = matmul_residual(z, wout, hy_b_out[j], mods, 2, N, 0, h)
            if ctx_stream:
                vc, x0c = hyena_in_proj(hc, norm1_g[i], mods, 0, 1, B * CTX, B, CTX, win, hy_b_in[j],
                                        hy_conv_w[j], hy_conv_b[j])
                zc = hyena_mix(vc, x0c, B, CTX, D, filt, hy_bias[j])
                hc = matmul_residual(zc, wout, hy_b_out[j], mods, 2, B * CTX, B, hc)
        h = ec_moe(h, norm2_g[i], mods, 3, 4, 5, N, 0, B, moe_router[i], i, w1, w3, w2)
        if ctx_stream:
            hc = ec_moe(hc, norm2_g[i], mods, 3, 4, 5, B * CTX, B, B, moe_router[i], i, w1, w3, w2)
    return h.reshape(B, N, D)
```

```python
import functools
import math

import jax
import jax.numpy as jnp
import numpy as np
from jax import lax
from jax.experimental import pallas as pl
from jax.experimental.pallas import tpu as pltpu

F32 = jnp.float32
BF16 = jnp.bfloat16

GRID_W = 64
N_MIXERS = 2
NORM_EPS = 1e-6
NEG_INF = -1e30
HEAD_DIM = 128
LANES = 128
WIN_ROWS = 8
WIN_COLS = 16
HYENA_SHORT = 3
HYENA_EMB = 33
HYENA_FAST_DECAY = 0.3
HYENA_SLOW_DECAY = 1.5
HYENA_TARGET = 1e-2
CAPACITY_FACTOR = 2

Q_ROWS = 4
BAND_ROWS = 12

VMEM_LIMIT = 56 * 1024 * 1024


def _cparams(*sem):
    return pltpu.CompilerParams(dimension_semantics=sem, vmem_limit_bytes=VMEM_LIMIT)


def _tile(n, want, unit=128):
    if n <= want:
        return n
    t = (want // unit) * unit
    while n % t:
        t -= unit
    return t


def _ada_kernel(c_ref, w_ref, b_ref, o_ref):
    c = c_ref[...]
    s = c * jax.nn.sigmoid(c)
    o_ref[...] = jnp.dot(s.astype(BF16), w_ref[...].astype(BF16),
                         preferred_element_type=F32) + b_ref[...]


def ada_all(cond, ada_w, ada_b, tn=1024):
    L, D, E = ada_w.shape
    tn = _tile(E, tn)
    return pl.pallas_call(
        _ada_kernel,
        grid=(L, E // tn),
        in_specs=[
            pl.BlockSpec((8, D), lambda l, j: (0, 0)),
            pl.BlockSpec((None, D, tn), lambda l, j: (l, 0, j)),
            pl.BlockSpec((None, 1, tn), lambda l, j: (l, 0, j)),
        ],
        out_specs=pl.BlockSpec((None, 8, tn), lambda l, j: (l, 0, j)),
        out_shape=jax.ShapeDtypeStruct((L, 8, E), F32),
        compiler_params=_cparams("parallel", "parallel"),
        name="ada",
    )(cond, ada_w, ada_b.reshape(L, 1, E))


def _norm_mod(x, g, sh, sc):
    y = x * lax.rsqrt(jnp.mean(x * x, axis=-1, keepdims=True) + NORM_EPS) * g
    return y * (1.0 + sc) + sh


def _nm_matmul_kernel(x_ref, g_ref, sh_ref, sc_ref, w_ref, b_ref, hg_ref, o_ref, a_ref, *,
                      n_norm_sections, section_cols, q_scale):
    j = pl.program_id(1)

    @pl.when(j == 0)
    def _():
        a_ref[...] = _norm_mod(x_ref[...], g_ref[...], sh_ref[...], sc_ref[...]).astype(BF16)

    acc = jnp.dot(a_ref[...], w_ref[...], preferred_element_type=F32) + b_ref[...]
    tn = acc.shape[1]
    sec = (j * tn) // section_cols

    @pl.when(sec < n_norm_sections)
    def _():
        hg = hg_ref[...]
        mult = jnp.where(sec == 0, q_scale, 1.0).astype(F32)
        for h in range(tn // HEAD_DIM):
            c = acc[:, h * HEAD_DIM:(h + 1) * HEAD_DIM]
            c = c * lax.rsqrt(jnp.mean(c * c, axis=-1, keepdims=True) + NORM_EPS) * hg
            o_ref[h] = (c * mult).astype(o_ref.dtype)

    @pl.when(sec >= n_norm_sections)
    def _():
        for h in range(tn // HEAD_DIM):
            o_ref[h] = acc[:, h * HEAD_DIM:(h + 1) * HEAD_DIM].astype(o_ref.dtype)


def nm_matmul(x, g, mods, k_shift, k_scale, group_rows, group_base, w, b, head_g,
              q_scale=1.0, out_dtype=BF16, tm=1024, tn=1024):
    R, D = x.shape
    E = w.shape[1]
    tm = _tile(group_rows, tm, 8)
    tn = _tile(D, tn)
    n_norm = 2

    def grp(i):
        return group_base + (i * tm) // group_rows

    def hg_map(i, j):
        return (jnp.minimum((j * tn) // D, 1), 0, 0)

    kern = functools.partial(_nm_matmul_kernel, n_norm_sections=n_norm, section_cols=D,
                             q_scale=q_scale)
    return pl.pallas_call(
        kern,
        grid=(R // tm, E // tn),
        in_specs=[
            pl.BlockSpec((tm, D), lambda i, j: (i, 0)),
            pl.BlockSpec((1, D), lambda i, j: (0, 0)),
            pl.BlockSpec((None, None, 1, D), lambda i, j: (k_shift, grp(i), 0, 0)),
            pl.BlockSpec((None, None, 1, D), lambda i, j: (k_scale, grp(i), 0, 0)),
            pl.BlockSpec((D, tn), lambda i, j: (0, j)),
            pl.BlockSpec((1, tn), lambda i, j: (0, j)),
            pl.BlockSpec((None, 1, HEAD_DIM), hg_map),
        ],
        out_specs=pl.BlockSpec((tn // HEAD_DIM, tm, HEAD_DIM), lambda i, j: (j, i, 0)),
        out_shape=jax.ShapeDtypeStruct((E // HEAD_DIM, R, HEAD_DIM), out_dtype),
        scratch_shapes=[pltpu.VMEM((tm, D), BF16)],
        compiler_params=_cparams("parallel", "arbitrary"),
        name="nm_matmul",
    )(x, g.reshape(1, D), mods, mods, w, b.reshape(1, E), head_g)


def _mm_res_kernel(x_ref, w_ref, b_ref, gate_ref, res_ref, o_ref):
    if len(x_ref.shape) == 3:
        x = jnp.concatenate([x_ref[h] for h in range(x_ref.shape[0])], axis=1)
    else:
        x = x_ref[...]
    y = jnp.dot(x, w_ref[...], preferred_element_type=F32) + b_ref[...]
    o_ref[...] = res_ref[...] + gate_ref[...] * y


def matmul_residual(x, w, b, mods, k_gate, group_rows, group_base, res, tm=1024, tn=1024):
    K, E = w.shape
    R = res.shape[0]
    tm = _tile(group_rows, tm, 8)
    tn = _tile(E, tn)

    def grp(i):
        return group_base + (i * tm) // group_rows

    if x.ndim == 3:
        x_spec = pl.BlockSpec((x.shape[0], tm, x.shape[2]), lambda i, j: (0, i, 0))
    else:
        x_spec = pl.BlockSpec((tm, K), lambda i, j: (i, 0))
    return pl.pallas_call(
        _mm_res_kernel,
        grid=(R // tm, E // tn),
        in_specs=[
            x_spec,
            pl.BlockSpec((K, tn), lambda i, j: (0, j)),
            pl.BlockSpec((1, tn), lambda i, j: (0, j)),
            pl.BlockSpec((None, None, 1, tn), lambda i, j: (k_gate, grp(i), 0, j)),
            pl.BlockSpec((tm, tn), lambda i, j: (i, j)),
        ],
        out_specs=pl.BlockSpec((tm, tn), lambda i, j: (i, j)),
        out_shape=jax.ShapeDtypeStruct((R, E), F32),
        compiler_params=_cparams("parallel", "parallel"),
        name="matmul_residual",
    )(x, w, b.reshape(1, E), mods, res)


def _na_bias_tables(rpb, rows):
    W = GRID_W
    per = LANES // W
    groups = BAND_ROWS // per
    cols = np.arange(W)
    cs = np.clip(cols - WIN_COLS // 2, 0, W - WIN_COLS)
    in_win = (cols[None, :] >= cs[:, None]) & (cols[None, :] < cs[:, None] + WIN_COLS)
    dc = np.clip(cols[None, :] - cols[:, None] + WIN_COLS - 1, 0, 2 * WIN_COLS - 2)
    n_dr, n_dc = 2 * WIN_ROWS - 1, 2 * WIN_COLS - 1
    onehot_r = np.zeros((3, Q_ROWS, BAND_ROWS, n_dr), np.float32)
    mask = np.zeros((3, Q_ROWS, W, BAND_ROWS, W), bool)
    for var, r0 in enumerate((0, Q_ROWS, rows - Q_ROWS)):
        bs = _band_start(r0, rows)
        for i in range(Q_ROWS):
            r = r0 + i
            rs = int(np.clip(r - WIN_ROWS // 2, 0, rows - WIN_ROWS))
            for jj in range(BAND_ROWS):
                kr = bs + jj
                if rs <= kr < rs + WIN_ROWS:
                    onehot_r[var, i, jj, kr - r + WIN_ROWS - 1] = 1.0
                    mask[var, i, :, jj, :] = in_win
    onehot_c = (dc[:, :, None] == np.arange(n_dc)).astype(np.float32)
    pick = np.einsum('pP,qkc->qpkPc', np.eye(per, dtype=np.float32), onehot_c).reshape(W, LANES, per * n_dc)
    onehot_r = onehot_r.reshape(3, Q_ROWS, groups, per, n_dr)
    mask = mask.reshape(3, Q_ROWS, W, groups, per, W).transpose(0, 3, 1, 2, 4, 5).reshape(
        3, groups, Q_ROWS * W, LANES)
    hp = lax.Precision.HIGHEST
    H = rpb.shape[0]
    t = jnp.einsum('hrc,vijpr->hvjipc', rpb.astype(F32), onehot_r, precision=hp)
    t = t.reshape(H, 3, groups, Q_ROWS, per * n_dc)
    t = jnp.einsum('hvjix,qyx->hvjiqy', t, pick, precision=hp)
    return jnp.where(mask[None], t.reshape(H, 3, groups, Q_ROWS * W, LANES), NEG_INF)


def _band_start(r0, rows):
    return int(np.clip(r0 - WIN_ROWS // 2, 0, rows - BAND_ROWS))


def _na_kernel(q_ref, k_ref, v_ref, qc_ref, kc_ref, vc_ref, bias_ref, o_ref, oc_ref, kt_ref, *, rows):
    W = GRID_W
    nblk = rows // Q_ROWS
    kt_ref[...] = k_ref[...].T
    kc = kc_ref[...].T
    vc = vc_ref[...]

    def attend(q, parts):
        ss = [jnp.dot(q, kt, preferred_element_type=F32) + (0.0 if bias is None else bias)
              for kt, _, bias in parts]
        m = functools.reduce(jnp.maximum, [jnp.max(s, axis=-1, keepdims=True) for s in ss])
        ps = [jnp.exp(s - m) for s in ss]
        den = sum(jnp.sum(p, axis=-1, keepdims=True) for p in ps)
        o = sum(jnp.dot(p.astype(BF16), v, preferred_element_type=F32)
                for p, (_, v, _) in zip(ps, parts))
        return o / den

    def body(blk, carry):
        r0 = blk * Q_ROWS
        bs = jnp.clip(r0 - WIN_ROWS // 2, 0, rows - BAND_ROWS)
        var = jnp.where(blk == 0, 0, jnp.where(blk == nblk - 1, 2, 1))
        q0 = pl.multiple_of(r0 * W, Q_ROWS * W)
        k0 = pl.multiple_of(bs * W, LANES)
        q = q_ref[pl.ds(q0, Q_ROWS * W), :]
        kb = kt_ref[:, pl.ds(k0, BAND_ROWS * W)]
        vb = v_ref[pl.ds(k0, BAND_ROWS * W), :]
        bias = jnp.concatenate([bias_ref[var, c] for c in range(bias_ref.shape[1])], axis=1)
        o = attend(q, [(kb, vb, bias), (kc, vc, None)])
        o_ref[pl.ds(q0, Q_ROWS * W), :] = o.astype(o_ref.dtype)
        return carry

    lax.fori_loop(0, nblk, body, 0, unroll=2)
    oc_ref[...] = attend(qc_ref[...], [(kc, vc, None)]).astype(oc_ref.dtype)


def na_attention(qkv, qkv_c, bias, B, N, CTX, D):
    H = D // HEAD_DIM
    rows = N // GRID_W
    per = LANES // GRID_W
    assert Q_ROWS % per == 0 and (WIN_ROWS // 2) % per == 0 and (rows - BAND_ROWS) % per == 0
    kern = functools.partial(_na_kernel, rows=rows)

    def spec(n, sec):
        return pl.BlockSpec((None, n, HEAD_DIM), lambda h, b: (sec * H + h, b, 0))

    return pl.pallas_call(
        kern,
        grid=(H, B),
        in_specs=[spec(N, 0), spec(N, 1), spec(N, 2), spec(CTX, 0), spec(CTX, 1), spec(CTX, 2),
                  pl.BlockSpec((None,) + bias.shape[1:], lambda h, b: (h, 0, 0, 0, 0))],
        out_specs=[spec(N, 0), spec(CTX, 0)],
        out_shape=[jax.ShapeDtypeStruct((H, B * N, HEAD_DIM), BF16),
                   jax.ShapeDtypeStruct((H, B * CTX, HEAD_DIM), BF16)],
        scratch_shapes=[pltpu.VMEM((HEAD_DIM, N), BF16)],
        compiler_params=_cparams("parallel", "parallel"),
        name="na_attention",
    )(qkv, qkv, qkv, qkv_c, qkv_c, qkv_c, bias)


def _expert_kernel(idx_ref, a_hbm, w1_ref, w3_ref, w2_ref, gate_ref, o_ref, xbuf, xs, yacc, sem, *,
                   n_tok, cap, bg):
    e, g, f = pl.program_id(0), pl.program_id(1), pl.program_id(2)
    n_exp, ng, nf = pl.num_programs(0), pl.num_programs(1), pl.num_programs(2)
    step = e * ng + g
    slot = step % 2
    nsteps = n_exp * ng
    rows = bg * cap
    S = xbuf.shape[1] // rows
    part = rows // nf

    def copy_row(ee, gg, sl, r0, u=0, priority=0):
        unit = math.gcd(part, cap)
        b = gg * bg + (r0 + (u // unit) * unit) // cap
        c = (r0 + (u // unit) * unit) % cap + u % unit
        row = b * n_tok + idx_ref[(b * n_exp + ee) * cap + c]
        src = pl.ds(pl.multiple_of(row * S, S), S)
        dst = pl.ds(pl.multiple_of((r0 + u) * S, S), S)
        pltpu.make_async_copy(a_hbm.at[src, :], xbuf.at[sl, dst, :], sem.at[sl]).start(priority=priority)

    def wait_rows(sl):
        pltpu.make_async_copy(a_hbm.at[pl.ds(0, rows * S), :], xbuf.at[sl], sem.at[sl]).wait()

    @pl.when((step == 0) & (f == 0))
    def _():
        def body(r, carry):
            copy_row(e, g, slot, r)
            return carry

        lax.fori_loop(0, rows, body, 0, unroll=8)

    @pl.when(f == 0)
    def _():
        wait_rows(slot)
        hi, lo = _load_token_tiles(xbuf, (slot,), 0, rows, S)
        xs[...] = jnp.concatenate([p.astype(BF16) for p in hi + lo], axis=1)
        yacc[...] = jnp.zeros_like(yacc)

    nxt = (step + 1) % nsteps
    for u in range(part):
        copy_row(nxt // ng, nxt % ng, 1 - slot, f * part, u, priority=u % 2)

    x = xs[...]
    h1 = jnp.dot(x, w1_ref[...].astype(BF16), preferred_element_type=F32)
    h3 = jnp.dot(x, w3_ref[...].astype(BF16), preferred_element_type=F32)
    hid = (h1 * jax.nn.sigmoid(h1) * h3).astype(BF16)
    w2 = w2_ref[...].astype(BF16)
    D = w2.shape[1]
    cw = min(D, EXPERT_OUT_COLS)
    for c in range(D // cw):
        cols = slice(c * cw, (c + 1) * cw)
        yacc[:, cols] += jnp.dot(hid, w2[:, cols], preferred_element_type=F32)

    @pl.when(f == nf - 1)
    def _():
        packed = _pack_bf16_pair(yacc[...] * gate_ref[...].reshape(rows, 1))
        for i in range(bg):
            _store_token_tiles(o_ref, (i,), packed[i * cap:(i + 1) * cap])

    @pl.when((step == nsteps - 1) & (f == nf - 1))
    def _():
        wait_rows(1 - slot)


def _pack_bf16_pair(y):
    half = y.shape[1] // 2
    hi = pltpu.bitcast(y[:, :half].astype(BF16).astype(F32), jnp.uint32)
    lo = pltpu.bitcast(y[:, half:].astype(BF16).astype(F32), jnp.uint32)
    return hi | (lo >> 16)


def _unpack_bf16_pair(u):
    hi = pltpu.bitcast(u & jnp.uint32(0xFFFF0000), F32)
    lo = pltpu.bitcast(u << 16, F32)
    return hi, lo


def _store_token_tiles(ref, lead, packed):
    rows, S = packed.shape[0], packed.shape[1] // LANES
    for s in range(S):
        ref[(*lead, pl.ds(s, rows, stride=S), slice(None))] = packed[:, s * LANES:(s + 1) * LANES]


def _load_token_tiles(ref, lead, first, rows, S):
    pieces = [_unpack_bf16_pair(ref[(*lead, pl.ds(first * S + s, rows, stride=S), slice(None))])
              for s in range(S)]
    return [p[0] for p in pieces], [p[1] for p in pieces]


def expert_ffn(a, idx_flat, layer, w1, w3, w2, gate, B, n_tok):
    _, E, D, Fh = w1.shape
    C = gate.shape[2]
    S = D // (2 * LANES)
    bg = max(1, min(B, EXPERT_ROWS // C))
    while B % bg:
        bg -= 1
    fc = _tile(Fh, EXPERT_HIDDEN_CHUNK)
    assert (bg * C) % (Fh // fc) == 0
    kern = functools.partial(_expert_kernel, n_tok=n_tok, cap=C, bg=bg)
    grid_spec = pltpu.PrefetchScalarGridSpec(
        num_scalar_prefetch=1,
        grid=(E, B // bg, Fh // fc),
        in_specs=[
            pl.BlockSpec(memory_space=pl.ANY),
            pl.BlockSpec((None, None, D, fc), lambda e, g, f, idx: (layer, e, 0, f)),
            pl.BlockSpec((None, None, D, fc), lambda e, g, f, idx: (layer, e, 0, f)),
            pl.BlockSpec((None, None, fc, D), lambda e, g, f, idx: (layer, e, f, 0)),
            pl.BlockSpec((bg, None, C, 1), lambda e, g, f, idx: (g, e, 0, 0)),
        ],
        out_specs=pl.BlockSpec((bg, None, C * S, LANES), lambda e, g, f, idx: (g, e, 0, 0)),
        scratch_shapes=[pltpu.VMEM((2, bg * C * S, LANES), jnp.uint32), pltpu.VMEM((bg * C, D), BF16),
                        pltpu.VMEM((bg * C, D), F32), pltpu.SemaphoreType.DMA((2,))],
    )
    return pl.pallas_call(
        kern,
        grid_spec=grid_spec,
        out_shape=jax.ShapeDtypeStruct((B, E, C * S, LANES), jnp.uint32),
        compiler_params=_cparams("arbitrary", "arbitrary", "arbitrary"),
        name="expert_ffn",
    )(idx_flat, a, w1, w3, w2, gate)


def _moe_prep_kernel(x_ref, g_ref, sh_ref, sc_ref, r_ref, a_ref, aff_ref):
    a = _norm_mod(x_ref[...], g_ref[...], sh_ref[...], sc_ref[...])
    _store_token_tiles(a_ref, (), _pack_bf16_pair(a))
    logits = lax.dot_general(r_ref[...], a.astype(BF16), (((1,), (1,)), ((), ())),
                             preferred_element_type=F32)
    m = jnp.max(logits, axis=0, keepdims=True)
    p = jnp.exp(logits - m)
    aff_ref[...] = p / jnp.sum(p, axis=0, keepdims=True)


def moe_prep(x, g, mods, k_shift, k_scale, group_rows, group_base, router_t, tm=512):
    R, D = x.shape
    E = router_t.shape[0]
    S = D // (2 * LANES)
    tm = _tile(group_rows, tm, 128)

    def grp(i):
        return group_base + (i * tm) // group_rows

    return pl.pallas_call(
        _moe_prep_kernel,
        grid=(R // tm,),
        in_specs=[
            pl.BlockSpec((tm, D), lambda i: (i, 0)),
            pl.BlockSpec((1, D), lambda i: (0, 0)),
            pl.BlockSpec((None, None, 1, D), lambda i: (k_shift, grp(i), 0, 0)),
            pl.BlockSpec((None, None, 1, D), lambda i: (k_scale, grp(i), 0, 0)),
            pl.BlockSpec((E, D), lambda i: (0, 0)),
        ],
        out_specs=[pl.BlockSpec((tm * S, LANES), lambda i: (i, 0)),
                   pl.BlockSpec((E, tm), lambda i: (0, i))],
        out_shape=[jax.ShapeDtypeStruct((R * S, LANES), jnp.uint32), jax.ShapeDtypeStruct((E, R), F32)],
        compiler_params=_cparams("parallel"),
        name="moe_prep",
    )(x, g.reshape(1, D), mods, mods, router_t)


EXPERT_ROWS = 1024
EXPERT_HIDDEN_CHUNK = 256
EXPERT_OUT_COLS = 512
COMBINE_TOKENS = 128
HIT_UNROLL = 4
COMBINE_ROW_GROUPS = 2
CUMSUM_CHUNK = 256
ROUTE_BLOCK_LANES = 512
TILE_TABLE_LANES = 128


def _route_select_kernel(aff_ref, tri_ref, ltri_ref, lt_ref, in_ref,
                         posi_ref, key_ref, slab_ref, p0_ref, kmax_ref, *, cap):
    E, N = aff_ref.shape
    aff = aff_ref[...]
    bits = pltpu.bitcast(aff, jnp.int32)
    cur = jnp.zeros((E, 1), jnp.int32)
    for bit in range(30, -1, -1):
        cand = cur | (1 << bit)
        cnt = jnp.sum(jnp.where(bits >= cand, 1.0, 0.0), axis=1, keepdims=True)
        cur = jnp.where(cnt >= cap, cand, cur)
    gt = bits > cur
    eq = bits == cur
    need = cap - jnp.sum(jnp.where(gt, 1.0, 0.0), axis=1, keepdims=True)

    def cumsum_incl(x):
        tri = tri_ref[...]
        w = min(CUMSUM_CHUNK, N)
        off = jnp.zeros((E, 1), F32)
        outs = []
        for c in range(N // w):
            xc = x[:, c * w:(c + 1) * w]
            outs.append(jnp.dot(xc.astype(BF16), tri[:w, :w], preferred_element_type=F32) + off)
            off = off + jnp.sum(xc, axis=1, keepdims=True)
        return jnp.concatenate(outs, axis=1) if len(outs) > 1 else outs[0]

    eqf = jnp.where(eq, 1.0, 0.0)
    tie_rank = cumsum_incl(eqf) - eqf
    sel = gt | (eq & (tie_rank < need))
    self_ = jnp.where(sel, 1.0, 0.0)
    pos_incl = cumsum_incl(self_)
    posi_ref[...] = pos_incl
    key_ref[...] = jnp.where(sel, pos_incl - self_, -1.0)
    selb = self_.astype(BF16)
    erank = jnp.dot(ltri_ref[...], selb, preferred_element_type=F32)
    in_tile = lax.broadcasted_iota(jnp.int32, (E, N), 1) & (COMBINE_TOKENS - 1)
    slab_ref[...] = erank * COMBINE_TOKENS + in_tile.astype(F32)
    p0_ref[...] = jnp.dot(selb, lt_ref[...], preferred_element_type=F32).astype(jnp.int32)
    count = jnp.sum(self_, axis=0, keepdims=True)
    kk = lax.broadcasted_iota(jnp.int32, (E, 1), 0).astype(F32)
    over = jnp.where(count > kk, 1.0, 0.0).astype(BF16)
    per_tile = jnp.dot(over, in_ref[...], preferred_element_type=F32)
    kmax_ref[...] = jnp.sum(jnp.where(per_tile > 0.0, 1.0, 0.0), axis=0, keepdims=True).astype(jnp.int32)


def _route_compact_kernel(p0_ref, aff_ref, posi_ref, key_ref, slab_ref, idx_ref, gate_ref, rank_ref, *,
                          cap, tn, lanes, width):
    b = pl.program_id(0)
    E, N = aff_ref.shape
    ch = min(LANES, cap)
    tpb = width // tn
    diag = lax.broadcasted_iota(jnp.int32, (ch, ch), 0) == lax.broadcasted_iota(jnp.int32, (ch, ch), 1)

    def as_row(col):
        return jnp.sum(jnp.where(diag, col, 0.0), axis=0, keepdims=True)

    def e_body(e, carry):
        q = (b * E + e) * lanes

        def c_body(cc, tiles):
            c0 = cc * ch
            t_lo = lax.while_loop(lambda t: p0_ref[q + t + 1] <= c0, lambda t: t + 1, tiles[0])
            t_hi = lax.while_loop(lambda t: p0_ref[q + t] < c0 + ch, lambda t: t + 1, tiles[1])
            j_lo = t_lo // tpb
            j_hi = (t_hi + tpb - 1) // tpb
            ccol = (c0 + lax.broadcasted_iota(jnp.int32, (ch, 1), 0)).astype(F32)

            def j_body(j, acc):
                row = pl.ds(e, 1)
                blk = pl.ds(pl.multiple_of(j * width, width), width)
                pi, ky, af, sl = posi_ref[row, blk], key_ref[row, blk], aff_ref[row, blk], slab_ref[row, blk]
                acc = list(acc)
                for u in range(width // LANES):
                    part = slice(u * LANES, (u + 1) * LANES)
                    hit = ky[:, part] == ccol
                    acc[0] = acc[0] + jnp.where(pi[:, part] <= ccol, 1.0, 0.0)
                    acc[1] = acc[1] + jnp.where(hit, af[:, part], 0.0)
                    acc[2] = acc[2] + jnp.where(hit, sl[:, part], 0.0)
                return tuple(acc)

            zero = jnp.zeros((ch, LANES), F32)
            iv, gv, rv = [jnp.sum(a, axis=1, keepdims=True)
                          for a in lax.fori_loop(j_lo, j_hi, j_body, (zero, zero, zero))]
            idx_ref[pl.ds(e, 1), pl.ds(c0, ch)] = as_row(iv).astype(jnp.int32) + j_lo * width
            gate_ref[e, pl.ds(c0, ch), :] = gv
            rank_ref[pl.ds(e, 1), pl.ds(c0, ch)] = as_row(rv).astype(jnp.int32)
            return (t_lo, t_hi)

        tiles = (0, 0)
        for cc in range(cap // ch):
            tiles = c_body(cc, tiles)
        return carry

    for e in range(E):
        e_body(e, 0)


def moe_route(aff_t, B, N, cap):
    E = aff_t.shape[0]
    tn = COMBINE_TOKENS
    lanes = TILE_TABLE_LANES
    assert N // tn + 1 <= lanes and N % tn == 0 and N % min(CUMSUM_CHUNK, N) == 0
    w = min(CUMSUM_CHUNK, N)
    tri = (np.arange(w)[:, None] <= np.arange(w)[None, :]).astype(np.float32)
    ltri = (np.arange(E)[None, :] < np.arange(E)[:, None]).astype(np.float32)
    tok = np.arange(N)[:, None]
    tile = np.arange(lanes)[None, :]
    before = (tok < tile * tn).astype(np.float32)
    inside = (tok // tn == tile).astype(np.float32)
    const = lambda shape: pl.BlockSpec(shape, lambda b: (0,) * len(shape))
    per_tok = pl.BlockSpec((None, E, N), lambda b, *_: (b, 0, 0))
    posi, key, slab, p0, kmax = pl.pallas_call(
        functools.partial(_route_select_kernel, cap=cap),
        grid=(B,),
        in_specs=[pl.BlockSpec((E, N), lambda b: (0, b)), const((w, w)), const((E, E)),
                  const((N, lanes)), const((N, lanes))],
        out_specs=[per_tok, per_tok, per_tok,
                   pl.BlockSpec((None, E, lanes), lambda b: (b, 0, 0)),
                   pl.BlockSpec((None, 1, lanes), lambda b: (b, 0, 0))],
        out_shape=[jax.ShapeDtypeStruct((B, E, N), F32)] * 3 + [
            jax.ShapeDtypeStruct((B, E, lanes), jnp.int32), jax.ShapeDtypeStruct((B, 1, lanes), jnp.int32)],
        compiler_params=_cparams("parallel"),
        name="moe_route_select",
    )(aff_t, jnp.asarray(tri, BF16), jnp.asarray(ltri, BF16), jnp.asarray(before, BF16),
      jnp.asarray(inside, BF16))
    width = min(ROUTE_BLOCK_LANES, N)
    sel3 = pl.BlockSpec((None, E, cap, 1), lambda b, *_: (b, 0, 0, 0))
    rows2 = pl.BlockSpec((None, E, cap), lambda b, *_: (b, 0, 0))
    idx, gate, rank = pl.pallas_call(
        functools.partial(_route_compact_kernel, cap=cap, tn=tn, lanes=lanes, width=width),
        grid_spec=pltpu.PrefetchScalarGridSpec(
            num_scalar_prefetch=1,
            grid=(B,),
            in_specs=[pl.BlockSpec((E, N), lambda b, *_: (0, b)), per_tok, per_tok, per_tok],
            out_specs=[rows2, sel3, rows2],
        ),
        out_shape=[jax.ShapeDtypeStruct((B, E, cap), jnp.int32),
                   jax.ShapeDtypeStruct((B, E, cap, 1), F32),
                   jax.ShapeDtypeStruct((B, E, cap), jnp.int32)],
        compiler_params=_cparams("parallel"),
        name="moe_route_compact",
    )(p0.reshape(-1), aff_t, posi, key, slab)
    return idx, gate, rank, p0, kmax


def _combine_kernel(rank_ref, p0_ref, kmax_ref, ys_hbm, h_ref, g2_ref, o_ref, rbuf, sem, *,
                    n_exp, cap, tn, lanes):
    b, t = pl.program_id(0), pl.program_id(1)
    nt = pl.num_programs(1)
    step = b * nt + t
    slot = step % 2
    S = rbuf.shape[1] // (n_exp * tn)

    def tile_hits(bb, tt):
        def body(e, tot):
            q = (bb * n_exp + e) * lanes + tt
            return tot + p0_ref[q + 1] - p0_ref[q]
        return lax.fori_loop(0, n_exp, body, 0)

    def fetch(bb, tt, sl):
        km = kmax_ref[bb * lanes + tt]
        for k in range(n_exp):
            @pl.when(k < km)
            def _():
                rbuf[sl, k * tn * S:(k + 1) * tn * S, :] = jnp.zeros((tn * S, LANES), jnp.uint32)

        def copy_row(row, priority=0):
            src = pl.ds(pl.multiple_of(row * S, S), S)
            dst = pl.ds(pl.multiple_of(rank_ref[row] * S, S), S)
            pltpu.make_async_copy(ys_hbm.at[src, :], rbuf.at[sl, dst, :], sem.at[sl]).start(priority=priority)

        def e_body(e, carry):
            q = (bb * n_exp + e) * lanes + tt
            lo = (bb * n_exp + e) * cap + p0_ref[q]
            n = p0_ref[q + 1] - p0_ref[q]
            groups = n // HIT_UNROLL

            def g_body(g, carry2):
                for u in range(HIT_UNROLL):
                    copy_row(lo + g * HIT_UNROLL + u, priority=u % 2)
                return carry2

            def r_body(c, carry2):
                copy_row(lo + c)
                return carry2

            lax.fori_loop(0, groups, g_body, 0)
            lax.fori_loop(groups * HIT_UNROLL, n, r_body, 0)
            return carry

        lax.fori_loop(0, n_exp, e_body, 0)

    @pl.when(step == 0)
    def _():
        fetch(b, t, slot)

    @pl.when(step + 1 < pl.num_programs(0) * nt)
    def _():
        nxt = step + 1
        fetch(nxt // nt, nxt % nt, 1 - slot)

    hits = tile_hits(b, t)

    nrows = hits * S
    bulk = pl.multiple_of((nrows // 8) * 8, 8)

    @pl.when(bulk > 0)
    def _():
        pltpu.make_async_copy(ys_hbm.at[pl.ds(0, bulk), :], rbuf.at[slot, pl.ds(0, bulk), :],
                              sem.at[slot]).wait()

    def wait_row(i, carry):
        pltpu.make_async_copy(ys_hbm.at[pl.ds(0, 1), :], rbuf.at[slot, pl.ds(0, 1), :], sem.at[slot]).wait()
        return carry

    lax.fori_loop(0, nrows - bulk, wait_row, 0)

    km = kmax_ref[b * lanes + t]
    g2 = g2_ref[...]
    rg = COMBINE_ROW_GROUPS
    for r in range(0, tn // 8, rg):
        def k_body(k, tot):
            new = []
            for i in range(rg):
                hi, lo = _load_token_tiles(rbuf, (slot,), k * tn + (r + i) * 8, 8, S)
                new += hi + lo
            return tuple(t_ + p for t_, p in zip(tot, new))

        tot = lax.fori_loop(0, km, k_body, (jnp.zeros((8, LANES), F32),) * (2 * S * rg))
        for i in range(rg):
            rows = slice((r + i) * 8, (r + i + 1) * 8)
            for s in range(2 * S):
                cols = slice(s * LANES, (s + 1) * LANES)
                o_ref[rows, cols] = h_ref[rows, cols] + g2[:, cols] * tot[i * 2 * S + s]


def moe_combine(ys, h, mods, k_gate, group_rows, group_base, rank, p0, kmax, B, N):
    R, D = h.shape
    E, cap = rank.shape[1], rank.shape[2]
    tn = COMBINE_TOKENS
    lanes = TILE_TABLE_LANES
    nt = N // tn

    def grp(b, t):
        return group_base + ((b * nt + t) * tn) // group_rows

    kern = functools.partial(_combine_kernel, n_exp=E, cap=cap, tn=tn, lanes=lanes)
    grid_spec = pltpu.PrefetchScalarGridSpec(
        num_scalar_prefetch=3,
        grid=(B, nt),
        in_specs=[
            pl.BlockSpec(memory_space=pl.ANY),
            pl.BlockSpec((tn, D), lambda b, t, *_: (b * nt + t, 0)),
            pl.BlockSpec((None, None, 1, D), lambda b, t, *_: (k_gate, grp(b, t), 0, 0)),
        ],
        out_specs=pl.BlockSpec((tn, D), lambda b, t, *_: (b * nt + t, 0)),
        scratch_shapes=[pltpu.VMEM((2, E * tn * (D // (2 * LANES)), LANES), jnp.uint32),
                        pltpu.SemaphoreType.DMA((2,))],
    )
    return pl.pallas_call(
        kern,
        grid_spec=grid_spec,
        out_shape=jax.ShapeDtypeStruct((R, D), F32),
        compiler_params=_cparams("arbitrary", "arbitrary"),
        name="moe_combine",
    )(rank.reshape(-1), p0.reshape(-1), kmax.reshape(-1), ys, h, mods)


def ec_moe(h, g, mods, k_shift, k_scale, k_gate, group_rows, group_base, B, router, layer, w1, w3, w2):
    R, D = h.shape
    N = R // B
    E = router.shape[1]
    cap = CAPACITY_FACTOR * N // E
    a, aff_t = moe_prep(h, g, mods, k_shift, k_scale, group_rows, group_base, router.T.astype(BF16))
    idx, gate, rank, p0, kmax = moe_route(aff_t, B, N, cap)
    ys = expert_ffn(a, idx.reshape(-1), layer, w1, w3, w2, gate, B, N)
    return moe_combine(ys.reshape(-1, LANES), h, mods, k_gate, group_rows, group_base,
                       rank, p0, kmax, B, N)


FFT_MINOR = 256
FFT_GROUP = 16


@functools.lru_cache(maxsize=None)
def _dft_tables(L):
    n = 2 * L
    n1 = n // FFT_MINOR
    nb = n1 // 2
    two_pi = 2.0 * math.pi
    a = np.arange(n1)
    ang = two_pi * ((a[:, None] * a[None, :]) % n1) / n1
    cos1, sin1 = np.cos(ang), np.sin(ang)
    eye = np.eye(FFT_GROUP)

    def kron(m):
        return np.kron(m, eye)

    def const(m):
        return jnp.asarray(m.astype(np.float32).astype(BF16))

    f1 = np.arange(n1)[:, None, None]
    f2 = np.arange(FFT_MINOR)[None, :, None]
    n2 = np.arange(FFT_MINOR)[None, None, :]
    ang = two_pi * ((n2 * (f1 + n1 * f2)) % n) / n
    gr, gi = np.cos(ang), -np.sin(ang)
    return dict(
        n1=n1, nb=nb,
        m_fwd_full=const(np.concatenate([kron(cos1), kron(-sin1)], axis=0)),
        m_fwd_half=const(np.concatenate([kron(cos1[:, :nb]), kron(-sin1[:, :nb])], axis=0)),
        m_inv=const(np.concatenate([kron(cos1[:nb]), kron(sin1[:nb])], axis=0)),
        g=const(np.concatenate([gr, gi], axis=1)),
        gt=const(np.concatenate([gr.transpose(0, 2, 1), gi.transpose(0, 2, 1)], axis=1)),
    )


HALO = 8


def _hyena_in_kernel(x_ref, xp_ref, xn_ref, g_ref, sh_ref, sc_ref, w0_ref, w1_ref, w2_ref, b_ref, cw_ref,
                     cb_ref, v_ref, x0_ref, a_ref, ah_ref, *, seq_len):
    i, j = pl.program_id(0), pl.program_id(1)
    tm = x_ref.shape[0]

    @pl.when(j == 0)
    def _():
        g, sh, sc = g_ref[...], sh_ref[...], sc_ref[...]
        a_ref[...] = _norm_mod(x_ref[...], g, sh, sc).astype(BF16)
        ah_ref[0:HALO, :] = _norm_mod(xp_ref[...], g, sh, sc)
        ah_ref[HALO:, :] = _norm_mod(xn_ref[...], g, sh, sc)

    first = (i * tm) % seq_len == 0
    last = ((i + 1) * tm) % seq_len == 0
    a = a_ref[...]
    ah = ah_ref[...].astype(BF16)
    row = lax.broadcasted_iota(jnp.int32, (tm, v_ref.shape[1]), 0)

    def section(k, w_ref):
        w = w_ref[...]
        p = jnp.dot(a, w, preferred_element_type=F32) + b_ref[k]
        ph = jnp.dot(ah, w, preferred_element_type=F32) + b_ref[k]
        above = jnp.where(first, 0.0, ph[HALO - 1:HALO])
        below = jnp.where(last, 0.0, ph[HALO:HALO + 1])
        cw = cw_ref[k]
        prev = jnp.where(row == 0, above, pltpu.roll(p, 1, 0))
        nxt = jnp.where(row == tm - 1, below, pltpu.roll(p, tm - 1, 0))
        return prev * cw[0:1] + p * cw[1:2] + nxt * cw[2:3] + cb_ref[k]

    x0 = section(0, w0_ref)
    x1 = section(1, w1_ref)
    v = section(2, w2_ref)
    v_ref[...] = (v * x1).astype(v_ref.dtype)
    x0_ref[...] = x0.astype(x0_ref.dtype)


def hyena_in_proj(x, g, mods, k_shift, k_scale, group_rows, group_base, seq_len, w, b, conv_w, conv_b,
                  tm=1024, tc=512):
    R, D = x.shape
    tm = _tile(math.gcd(group_rows, seq_len), tm, HALO)
    tc = _tile(D, tc)
    nc = D // tc
    cw = conv_w.reshape(HYENA_SHORT, 3, D).transpose(1, 0, 2)

    def grp(i):
        return group_base + (i * tm) // group_rows

    def wspec(k):
        return pl.BlockSpec((D, tc), lambda i, j: (0, k * nc + j))

    mod = lambda k: pl.BlockSpec((None, None, 1, D), lambda i, j: (k, grp(i), 0, 0))
    per_proj = lambda rows: pl.BlockSpec((3, rows, tc), lambda i, j: (0, 0, j))
    kern = functools.partial(_hyena_in_kernel, seq_len=seq_len)
    return pl.pallas_call(
        kern,
        grid=(R // tm, nc),
        in_specs=[pl.BlockSpec((tm, D), lambda i, j: (i, 0)),
                  pl.BlockSpec((HALO, D), lambda i, j: (jnp.maximum(i * (tm // HALO) - 1, 0), 0)),
                  pl.BlockSpec((HALO, D), lambda i, j: (jnp.minimum((i + 1) * (tm // HALO), R // HALO - 1), 0)),
                  pl.BlockSpec((1, D), lambda i, j: (0, 0)), mod(k_shift), mod(k_scale),
                  wspec(0), wspec(1), wspec(2), per_proj(1), per_proj(HYENA_SHORT), per_proj(1)],
        out_specs=[pl.BlockSpec((tm, tc), lambda i, j: (i, j))] * 2,
        out_shape=[jax.ShapeDtypeStruct((R, D), BF16)] * 2,
        scratch_shapes=[pltpu.VMEM((tm, D), BF16), pltpu.VMEM((2 * HALO, D), F32)],
        compiler_params=_cparams("parallel", "arbitrary"),
        name="hyena_in_proj",
    )(x, x, x, g.reshape(1, D), mods, mods, w, w, w, b.reshape(3, 1, D), cw, conv_b.reshape(3, 1, D))


def _filter_kernel(emb_ref, w1_ref, b1_ref, f1_ref, w2_ref, b2_ref, f2_ref, w3_ref, b3_ref, f3_ref,
                   wo_ref, dl_ref, k_ref, norm_ref, *, L):
    i = pl.program_id(0)
    tr = emb_ref.shape[0]
    hp = lax.Precision.HIGHEST
    emb = emb_ref[...]
    h = jnp.sin(f1_ref[...] * (jnp.dot(emb, w1_ref[...], precision=hp, preferred_element_type=F32) + b1_ref[...]))
    h = jnp.sin(f2_ref[...] * (jnp.dot(h, w2_ref[...], precision=hp, preferred_element_type=F32) + b2_ref[...]))
    h = jnp.sin(f3_ref[...] * (jnp.dot(h, w3_ref[...], precision=hp, preferred_element_type=F32) + b3_ref[...]))
    k = jnp.dot(h, wo_ref[...], precision=hp, preferred_element_type=F32)
    k = k * jnp.exp(-emb[:, 0:1] * dl_ref[...])
    row = i * tr + lax.broadcasted_iota(jnp.int32, (tr, 1), 0)
    k = jnp.where(row == L, 0.0, k)
    k_ref[...] = k

    @pl.when(i == 0)
    def _():
        norm_ref[...] = jnp.zeros_like(norm_ref)

    norm_ref[...] += jnp.sum(jnp.abs(k), axis=0, keepdims=True)


def hyena_filter(L, D, w1, b1, f1, w2, b2, f2, w3, b3, f3, w_out, tr=512):
    n = 2 * L
    tr = _tile(L, tr, 8)
    P = 128
    bands = (HYENA_EMB - 1) // 2
    d = np.arange(n)
    d = np.where(d <= L, d, n - d).clip(0, L - 1)
    t01 = np.linspace(0.0, 1.0, L)[d]
    wang = 2 * math.pi * d / L
    fr = np.linspace(1e-4, bands - 1, bands)
    emb = np.zeros((n, P), np.float32)
    emb[:, 0] = t01
    emb[:, 1:1 + bands] = np.cos(fr[None, :] * wang[:, None])
    emb[:, 1 + bands:1 + 2 * bands] = -np.sin(fr[None, :] * wang[:, None])

    def padw(w):
        return jnp.zeros((P, P), F32).at[:w.shape[0], :w.shape[1]].set(w)

    def padv(v):
        return jnp.zeros((1, P), F32).at[0, :v.shape[0]].set(v)

    wo = jnp.zeros((P, 2 * D), F32).at[:w_out.shape[0]].set(w_out)
    max_decay = math.log(HYENA_TARGET) / HYENA_FAST_DECAY
    min_decay = math.log(HYENA_TARGET) / HYENA_SLOW_DECAY
    deltas = jnp.abs(jnp.linspace(min_decay, max_decay, D, dtype=F32)).reshape(1, D)
    small = pl.BlockSpec((P, P), lambda i: (0, 0))
    vec = pl.BlockSpec((1, P), lambda i: (0, 0))
    kern = functools.partial(_filter_kernel, L=L)
    return pl.pallas_call(
        kern,
        grid=(n // tr,),
        in_specs=[pl.BlockSpec((tr, P), lambda i: (i, 0)),
                  small, vec, vec, small, vec, vec, small, vec, vec,
                  pl.BlockSpec((P, D), lambda i: (0, (i * tr) // L)),
                  pl.BlockSpec((1, D), lambda i: (0, 0))],
        out_specs=[pl.BlockSpec((tr, D), lambda i: (i, 0)), pl.BlockSpec((1, D), lambda i: (0, 0))],
        out_shape=[jax.ShapeDtypeStruct((n, D), F32), jax.ShapeDtypeStruct((1, D), F32)],
        compiler_params=_cparams("arbitrary"),
        name="hyena_filter",
    )(jnp.asarray(emb), padw(w1), padv(b1), padv(f1), padw(w2), padv(b2), padv(f2),
      padw(w3), padv(b3), padv(f3), wo, deltas)


def _fwd_major_kernel(*refs, n1, nb, has_imag):
    if has_imag:
        zr_ref, zi_ref, m_ref, ar_ref, ai_ref = refs
    else:
        zr_ref, m_ref, ar_ref, ai_ref = refs
    m = m_ref[...]
    half = n1 * FFT_GROUP

    def body(g, carry):
        def rows(blk):
            return pl.ds(pl.multiple_of(blk * FFT_MINOR + g * FFT_GROUP, FFT_GROUP), FFT_GROUP)

        def slab(ref):
            return jnp.concatenate([ref[rows(blk), :] for blk in range(nb)], axis=0).astype(BF16)

        pr = jnp.dot(m, slab(zr_ref), preferred_element_type=F32)
        if has_imag:
            pi = jnp.dot(m, slab(zi_ref), preferred_element_type=F32)
            ar, ai = pr[:half] - pi[half:], pi[:half] + pr[half:]
        else:
            ar, ai = pr[:half], pr[half:]
        for f1 in range(n1):
            sl = slice(f1 * FFT_GROUP, (f1 + 1) * FFT_GROUP)
            ar_ref[rows(f1), :] = ar[sl].astype(BF16)
            ai_ref[rows(f1), :] = ai[sl].astype(BF16)
        return carry

    lax.fori_loop(0, FFT_MINOR // FFT_GROUP, body, 0)


def fwd_major(z, tabs, pairs, tc=256):
    n1, nb = tabs["n1"], tabs["nb"]
    n = n1 * FFT_MINOR
    Bz, Lz, D = z.shape
    tc = _tile(D, tc)
    if pairs:
        P = Bz // 2
        m = tabs["m_fwd_half"]
        ins = [z, z, m]
        in_specs = [pl.BlockSpec((None, Lz, tc), lambda p, c: (2 * p, 0, c)),
                    pl.BlockSpec((None, Lz, tc), lambda p, c: (2 * p + 1, 0, c))]
        nblk = nb
    else:
        P = 1
        m = tabs["m_fwd_full"]
        ins = [z, m]
        in_specs = [pl.BlockSpec((None, Lz, tc), lambda p, c: (0, 0, c))]
        nblk = n1
    in_specs.append(pl.BlockSpec(m.shape, lambda p, c: (0, 0)))
    kern = functools.partial(_fwd_major_kernel, n1=n1, nb=nblk, has_imag=pairs)
    return pl.pallas_call(
        kern,
        grid=(P, D // tc),
        in_specs=in_specs,
        out_specs=[pl.BlockSpec((None, n, tc), lambda p, c: (p, 0, c))] * 2,
        out_shape=[jax.ShapeDtypeStruct((P, n, D), BF16)] * 2,
        compiler_params=_cparams("parallel", "parallel"),
        name="hyena_fwd_major",
    )(*ins)


def _spectrum_kernel(ar_ref, ai_ref, g_ref, s_ref, kr_ref, ki_ref):
    g = g_ref[...]
    h = FFT_MINOR
    pr = jnp.dot(g, ar_ref[...], preferred_element_type=F32)
    pi = jnp.dot(g, ai_ref[...], preferred_element_type=F32)
    s = s_ref[...]
    kr_ref[...] = (pr[:h] - pi[h:]) * s
    ki_ref[...] = (pi[:h] + pr[h:]) * s


def filter_spectrum(ar, ai, tabs, scale, tc=1024):
    _, n, D = ar.shape
    tc = _tile(D, tc)
    blk = pl.BlockSpec((None, FFT_MINOR, tc), lambda f, c: (0, f, c))
    out = pl.BlockSpec((FFT_MINOR, tc), lambda f, c: (f, c))
    return pl.pallas_call(
        _spectrum_kernel,
        grid=(tabs["n1"], D // tc),
        in_specs=[blk, blk, pl.BlockSpec((None, 2 * FFT_MINOR, FFT_MINOR), lambda f, c: (f, 0, 0)),
                  pl.BlockSpec((1, tc), lambda f, c: (0, c))],
        out_specs=[out, out],
        out_shape=[jax.ShapeDtypeStruct((n, D), F32)] * 2,
        compiler_params=_cparams("parallel", "parallel"),
        name="hyena_filter_spectrum",
    )(ar, ai, tabs["g"], scale)


def _minor_kernel(ar_ref, ai_ref, g_ref, gt_ref, kr_ref, ki_ref, br_ref, bi_ref):
    h = FFT_MINOR
    g = g_ref[...]
    pr = jnp.dot(g, ar_ref[...], preferred_element_type=F32)
    pi = jnp.dot(g, ai_ref[...], preferred_element_type=F32)
    xr, xi = pr[:h] - pi[h:], pi[:h] + pr[h:]
    kr, ki = kr_ref[...], ki_ref[...]
    yr = (xr * kr - xi * ki).astype(BF16)
    yi = (xr * ki + xi * kr).astype(BF16)
    gt = gt_ref[...]
    qr = jnp.dot(gt, yr, preferred_element_type=F32)
    qi = jnp.dot(gt, yi, preferred_element_type=F32)
    br_ref[...] = (qr[:h] + qi[h:]).astype(BF16)
    bi_ref[...] = (qi[:h] - qr[h:]).astype(BF16)


def minor_conv(ar, ai, tabs, kr, ki, tc=1024):
    P, n, D = ar.shape
    tc = _tile(D, tc)
    blk = pl.BlockSpec((None, FFT_MINOR, tc), lambda f, c, p: (p, f, c))
    tab = pl.BlockSpec((None, 2 * FFT_MINOR, FFT_MINOR), lambda f, c, p: (f, 0, 0))
    kblk = pl.BlockSpec((FFT_MINOR, tc), lambda f, c, p: (f, c))
    return pl.pallas_call(
        _minor_kernel,
        grid=(tabs["n1"], D // tc, P),
        in_specs=[blk, blk, tab, tab, kblk, kblk],
        out_specs=[blk, blk],
        out_shape=[jax.ShapeDtypeStruct((P, n, D), BF16)] * 2,
        compiler_params=_cparams("parallel", "parallel", "parallel"),
        name="hyena_minor_conv",
    )(ar, ai, tabs["g"], tabs["gt"], kr, ki)


def _inv_major_kernel(br_ref, bi_ref, m_ref, v_ref, x0_ref, bias_ref, o_ref, *, n1, nb):
    m = m_ref[...]
    half = nb * FFT_GROUP
    bias = bias_ref[...]

    def body(g, carry):
        def rows(blk):
            return pl.ds(pl.multiple_of(blk * FFT_MINOR + g * FFT_GROUP, FFT_GROUP), FFT_GROUP)

        def slab(ref):
            return jnp.concatenate([ref[rows(f1), :] for f1 in range(n1)], axis=0)

        pr = jnp.dot(m, slab(br_ref), preferred_element_type=F32)
        pi = jnp.dot(m, slab(bi_ref), preferred_element_type=F32)
        ys = (pr[:half] - pi[half:], pi[:half] + pr[half:])
        for s in range(2):
            for blk in range(nb):
                y = ys[s][blk * FFT_GROUP:(blk + 1) * FFT_GROUP]
                v = v_ref[s, rows(blk), :].astype(F32)
                x0 = x0_ref[s, rows(blk), :].astype(F32)
                o_ref[s, rows(blk), :] = ((y + v * bias) * x0).astype(o_ref.dtype)
        return carry

    lax.fori_loop(0, FFT_MINOR // FFT_GROUP, body, 0)


def inv_major(br, bi, tabs, v, x0, bias, tc=256):
    n1, nb = tabs["n1"], tabs["nb"]
    P, n, D = br.shape
    B, L, _ = v.shape
    tc = _tile(D, tc)
    m = tabs["m_inv"]
    blk = pl.BlockSpec((None, n, tc), lambda p, c: (p, 0, c))
    pair = pl.BlockSpec((2, L, tc), lambda p, c: (p, 0, c))
    kern = functools.partial(_inv_major_kernel, n1=n1, nb=nb)
    return pl.pallas_call(
        kern,
        grid=(P, D // tc),
        in_specs=[blk, blk, pl.BlockSpec(m.shape, lambda p, c: (0, 0)), pair, pair,
                  pl.BlockSpec((1, tc), lambda p, c: (0, c))],
        out_specs=pair,
        out_shape=jax.ShapeDtypeStruct((B, L, D), BF16),
        compiler_params=_cparams("parallel", "parallel"),
        name="hyena_inv_major",
    )(br, bi, m, v, x0, bias.reshape(1, D))


def hyena_mix(v, x0, B, L, D, filt, bias):
    assert B % 2 == 0 and L % FFT_MINOR == 0
    tabs = _dft_tables(L)
    v, x0 = v.reshape(B, L, D), x0.reshape(B, L, D)
    k_raw, k_norm = hyena_filter(L, D, *filt)
    kar, kai = fwd_major(k_raw[None], tabs, pairs=False)
    kr, ki = filter_spectrum(kar, kai, tabs, 1.0 / (2 * L * k_norm))
    ar, ai = fwd_major(v, tabs, pairs=True)
    br, bi = minor_conv(ar, ai, tabs, kr, ki)
    z = inv_major(br, bi, tabs, v, x0, bias)
    return z.reshape(B * L, D)


def kernel(x, c, ctx, c_ctx, ada_w, ada_b, norm1_g, norm2_g, na_w_qkv, na_w_o, na_q_g, na_k_g, na_rpb, hy_w_in, hy_b_in, hy_conv_w, hy_conv_b, hy_f_w1, hy_f_b1, hy_f_freq1, hy_f_w2, hy_f_b2, hy_f_freq2, hy_f_w3, hy_f_b3, hy_f_freq3, hy_f_wout, hy_bias, hy_w_out, hy_b_out, moe_router, moe_w1, moe_w3, moe_w2):
    B, N, D = x.shape
    CTX = ctx.shape[1]
    depth = ada_w.shape[0]
    mixer = [i % N_MIXERS for i in range(depth)]

    cond = jnp.concatenate([c, c_ctx[None, :], jnp.zeros((8 - B - 1, D), F32)], axis=0)
    mods_all = ada_all(cond, ada_w, ada_b)
    mods_all = mods_all.reshape(depth, 8, 6, 1, D).transpose(0, 2, 1, 3, 4)

    h = x.reshape(B * N, D)
    hc = ctx.reshape(B * CTX, D)
    zeros_d = jnp.zeros((D,), F32)
    w1, w3, w2 = moe_w1, moe_w3, moe_w2
    for i in range(depth):
        j = i // N_MIXERS
        ctx_stream = any(mixer[l] == 0 for l in range(i + 1, depth))
        ctx_in = ctx_stream or mixer[i] == 0
        mods = mods_all[i]
        if mixer[i] == 0:
            wqkv = na_w_qkv[j].astype(BF16)
            wo = na_w_o[j].astype(BF16)
            hg = jnp.stack([na_q_g[j], na_k_g[j]]).reshape(2, 1, HEAD_DIM)
            zeros_e = jnp.zeros((3 * D,), F32)
            qs = HEAD_DIM ** -0.5
            qkv = nm_matmul(h, norm1_g[i], mods, 0, 1, N, 0, wqkv, zeros_e, hg, qs)
            qkv_c = nm_matmul(hc, norm1_g[i], mods, 0, 1, B * CTX, B, wqkv, zeros_e, hg, qs)
            bias = _na_bias_tables(na_rpb[j], N // GRID_W)
            o, oc = na_attention(qkv, qkv_c, bias, B, N, CTX, D)
            h = matmul_residual(o, wo, zeros_d, mods, 2, N, 0, h)
            if ctx_stream:
                hc = matmul_residual(oc, wo, zeros_d, mods, 2, B * CTX, B, hc)
        else:
            win = hy_w_in[j].astype(BF16)
            wout = hy_w_out[j].astype(BF16)
            filt = (hy_f_w1[j], hy_f_b1[j], hy_f_freq1[j], hy_f_w2[j], hy_f_b2[j], hy_f_freq2[j],
                    hy_f_w3[j], hy_f_b3[j], hy_f_freq3[j], hy_f_wout[j])
            v, x0 = hyena_in_proj(h, norm1_g[i], mods, 0, 1, N, 0, N, win, hy_b_in[j],
                                  hy_conv_w[j], hy_conv_b[j])
            z = hyena_mix(v, x0, B, N, D, filt, hy_bias[j])
            h = matmul_residual(z, wout, hy_b_out[j], mods, 2, N, 0, h)
            if ctx_stream:
                vc, x0c = hyena_in_proj(hc, norm1_g[i], mods, 0, 1, B * CTX, B, CTX, win, hy_b_in[j],
                                        hy_conv_w[j], hy_conv_b[j])
                zc = hyena_mix(vc, x0c, B, CTX, D, filt, hy_bias[j])
                hc = matmul_residual(zc, wout, hy_b_out[j], mods, 2, B * CTX, B, hc)
        h = ec_moe(h, norm2_g[i], mods, 3, 4, 5, N, 0, B, moe_router[i], i, w1, w3, w2)
        if ctx_stream:
            hc = ec_moe(hc, norm2_g[i], mods, 3, 4, 5, B * CTX, B, B, moe_router[i], i, w1, w3, w2)
    return h.reshape(B, N, D)
```

```python
import functools
import math

import jax
import jax.numpy as jnp
import numpy as np
from jax import lax
from jax.experimental import pallas as pl
from jax.experimental.pallas import tpu as pltpu

F32 = jnp.float32
BF16 = jnp.bfloat16

GRID_W = 64
N_MIXERS = 2
NORM_EPS = 1e-6
NEG_INF = -1e30
HEAD_DIM = 128
LANES = 128
WIN_ROWS = 8
WIN_COLS = 16
HYENA_SHORT = 3
HYENA_EMB = 33
HYENA_FAST_DECAY = 0.3
HYENA_SLOW_DECAY = 1.5
HYENA_TARGET = 1e-2
N_EXPERTS = 16
CAPACITY_FACTOR = 2

Q_ROWS = 4
BAND_ROWS = 12

VMEM_LIMIT = 56 * 1024 * 1024


def _cparams(*sem):
    return pltpu.CompilerParams(dimension_semantics=sem, vmem_limit_bytes=VMEM_LIMIT)


def _tile(n, want, unit=128):
    if n <= want:
        return n
    t = (want // unit) * unit
    while n % t:
        t -= unit
    return t


def _ada_kernel(c_ref, w_ref, b_ref, o_ref):
    c = c_ref[...]
    s = c * jax.nn.sigmoid(c)
    o_ref[...] = jnp.dot(s.astype(BF16), w_ref[...].astype(BF16),
                         preferred_element_type=F32) + b_ref[...]


def ada_all(cond, ada_w, ada_b, tn=1024):
    L, D, E = ada_w.shape
    tn = _tile(E, tn)
    return pl.pallas_call(
        _ada_kernel,
        grid=(L, E // tn),
        in_specs=[
            pl.BlockSpec((8, D), lambda l, j: (0, 0)),
            pl.BlockSpec((None, D, tn), lambda l, j: (l, 0, j)),
            pl.BlockSpec((None, 1, tn), lambda l, j: (l, 0, j)),
        ],
        out_specs=pl.BlockSpec((None, 8, tn), lambda l, j: (l, 0, j)),
        out_shape=jax.ShapeDtypeStruct((L, 8, E), F32),
        compiler_params=_cparams("parallel", "parallel"),
        name="ada",
    )(cond, ada_w, ada_b.reshape(L, 1, E))


def _norm_mod(x, g, sh, sc):
    y = x * lax.rsqrt(jnp.mean(x * x, axis=-1, keepdims=True) + NORM_EPS) * g
    return y * (1.0 + sc) + sh


def _nm_matmul_kernel(x_ref, g_ref, sh_ref, sc_ref, w_ref, b_ref, hg_ref, o_ref, a_ref, *,
                      n_norm_sections, section_cols, q_scale):
    j = pl.program_id(1)

    @pl.when(j == 0)
    def _():
        a_ref[...] = _norm_mod(x_ref[...], g_ref[...], sh_ref[...], sc_ref[...]).astype(BF16)

    acc = jnp.dot(a_ref[...], w_ref[...], preferred_element_type=F32) + b_ref[...]
    tn = acc.shape[1]
    if n_norm_sections == 0:
        o_ref[...] = acc.astype(o_ref.dtype)
        return
    sec = (j * tn) // section_cols

    @pl.when(sec < n_norm_sections)
    def _():
        hg = hg_ref[...]
        mult = jnp.where(sec == 0, q_scale, 1.0).astype(F32)
        for h in range(tn // HEAD_DIM):
            c = acc[:, h * HEAD_DIM:(h + 1) * HEAD_DIM]
            c = c * lax.rsqrt(jnp.mean(c * c, axis=-1, keepdims=True) + NORM_EPS) * hg
            o_ref[:, h * HEAD_DIM:(h + 1) * HEAD_DIM] = (c * mult).astype(o_ref.dtype)

    @pl.when(sec >= n_norm_sections)
    def _():
        o_ref[...] = acc.astype(o_ref.dtype)


def nm_matmul(x, g, mods, k_shift, k_scale, group_rows, group_base, w, b, head_g=None,
              q_scale=1.0, out_dtype=BF16, tm=1024, tn=1024):
    R, D = x.shape
    E = w.shape[1]
    tm = _tile(group_rows, tm, 8)
    tn = _tile(D, tn)
    n_norm = 0 if head_g is None else 2
    if head_g is None:
        head_g = jnp.ones((2, 1, HEAD_DIM), F32)

    def grp(i):
        return group_base + (i * tm) // group_rows

    def hg_map(i, j):
        return (jnp.minimum((j * tn) // D, 1), 0, 0)

    kern = functools.partial(_nm_matmul_kernel, n_norm_sections=n_norm, section_cols=D,
                             q_scale=q_scale)
    return pl.pallas_call(
        kern,
        grid=(R // tm, E // tn),
        in_specs=[
            pl.BlockSpec((tm, D), lambda i, j: (i, 0)),
            pl.BlockSpec((1, D), lambda i, j: (0, 0)),
            pl.BlockSpec((None, None, 1, D), lambda i, j: (k_shift, grp(i), 0, 0)),
            pl.BlockSpec((None, None, 1, D), lambda i, j: (k_scale, grp(i), 0, 0)),
            pl.BlockSpec((D, tn), lambda i, j: (0, j)),
            pl.BlockSpec((1, tn), lambda i, j: (0, j)),
            pl.BlockSpec((None, 1, HEAD_DIM), hg_map),
        ],
        out_specs=pl.BlockSpec((tm, tn), lambda i, j: (i, j)),
        out_shape=jax.ShapeDtypeStruct((R, E), out_dtype),
        scratch_shapes=[pltpu.VMEM((tm, D), BF16)],
        compiler_params=_cparams("parallel", "arbitrary"),
        name="nm_matmul",
    )(x, g.reshape(1, D), mods, mods, w, b.reshape(1, E), head_g)


def _mm_res_kernel(x_ref, w_ref, b_ref, gate_ref, res_ref, o_ref):
    y = jnp.dot(x_ref[...], w_ref[...], preferred_element_type=F32) + b_ref[...]
    o_ref[...] = res_ref[...] + gate_ref[...] * y


def matmul_residual(x, w, b, mods, k_gate, group_rows, group_base, res, tm=1024, tn=1024):
    R, K = x.shape
    E = w.shape[1]
    tm = _tile(group_rows, tm, 8)
    tn = _tile(E, tn)

    def grp(i):
        return group_base + (i * tm) // group_rows

    return pl.pallas_call(
        _mm_res_kernel,
        grid=(R // tm, E // tn),
        in_specs=[
            pl.BlockSpec((tm, K), lambda i, j: (i, 0)),
            pl.BlockSpec((K, tn), lambda i, j: (0, j)),
            pl.BlockSpec((1, tn), lambda i, j: (0, j)),
            pl.BlockSpec((None, None, 1, tn), lambda i, j: (k_gate, grp(i), 0, j)),
            pl.BlockSpec((tm, tn), lambda i, j: (i, j)),
        ],
        out_specs=pl.BlockSpec((tm, tn), lambda i, j: (i, j)),
        out_shape=jax.ShapeDtypeStruct((R, E), F32),
        compiler_params=_cparams("parallel", "parallel"),
        name="matmul_residual",
    )(x, w, b.reshape(1, E), mods, res)


def _na_bias_tables(rpb, rows):
    W = GRID_W
    per = LANES // W
    groups = BAND_ROWS // per
    cols = np.arange(W)
    cs = np.clip(cols - WIN_COLS // 2, 0, W - WIN_COLS)
    in_win = (cols[None, :] >= cs[:, None]) & (cols[None, :] < cs[:, None] + WIN_COLS)
    dc = np.clip(cols[None, :] - cols[:, None] + WIN_COLS - 1, 0, 2 * WIN_COLS - 2)
    n_dr, n_dc = 2 * WIN_ROWS - 1, 2 * WIN_COLS - 1
    onehot_r = np.zeros((3, Q_ROWS, BAND_ROWS, n_dr), np.float32)
    mask = np.zeros((3, Q_ROWS, W, BAND_ROWS, W), bool)
    for var, r0 in enumerate((0, Q_ROWS, rows - Q_ROWS)):
        bs = _band_start(r0, rows)
        for i in range(Q_ROWS):
            r = r0 + i
            rs = int(np.clip(r - WIN_ROWS // 2, 0, rows - WIN_ROWS))
            for jj in range(BAND_ROWS):
                kr = bs + jj
                if rs <= kr < rs + WIN_ROWS:
                    onehot_r[var, i, jj, kr - r + WIN_ROWS - 1] = 1.0
                    mask[var, i, :, jj, :] = in_win
    onehot_c = (dc[:, :, None] == np.arange(n_dc)).astype(np.float32)
    pick = np.einsum('pP,qkc->qpkPc', np.eye(per, dtype=np.float32), onehot_c).reshape(W, LANES, per * n_dc)
    onehot_r = onehot_r.reshape(3, Q_ROWS, groups, per, n_dr)
    mask = mask.reshape(3, Q_ROWS, W, groups, per, W).transpose(0, 3, 1, 2, 4, 5).reshape(
        3, groups, Q_ROWS * W, LANES)
    hp = lax.Precision.HIGHEST
    H = rpb.shape[0]
    t = jnp.einsum('hrc,vijpr->hvjipc', rpb.astype(F32), onehot_r, precision=hp)
    t = t.reshape(H, 3, groups, Q_ROWS, per * n_dc)
    t = jnp.einsum('hvjix,qyx->hvjiqy', t, pick, precision=hp)
    return jnp.where(mask[None], t.reshape(H, 3, groups, Q_ROWS * W, LANES), NEG_INF)


def _band_start(r0, rows):
    return int(np.clip(r0 - WIN_ROWS // 2, 0, rows - BAND_ROWS))


def _na_kernel(q_ref, k_ref, v_ref, qc_ref, kc_ref, vc_ref, bias_ref, o_ref, oc_ref, *, rows):
    W = GRID_W
    nblk = rows // Q_ROWS
    kc = kc_ref[...]
    vc = vc_ref[...]
    nt = (((1,), (1,)), ((), ()))

    def attend(q, parts):
        ss = [lax.dot_general(q, k, nt, preferred_element_type=F32) + (0.0 if bias is None else bias)
              for k, _, bias in parts]
        m = functools.reduce(jnp.maximum, [jnp.max(s, axis=-1, keepdims=True) for s in ss])
        ps = [jnp.exp(s - m) for s in ss]
        den = sum(jnp.sum(p, axis=-1, keepdims=True) for p in ps)
        o = sum(jnp.dot(p.astype(BF16), v, preferred_element_type=F32)
                for p, (_, v, _) in zip(ps, parts))
        return o / den

    def body(blk, carry):
        r0 = blk * Q_ROWS
        bs = jnp.clip(r0 - WIN_ROWS // 2, 0, rows - BAND_ROWS)
        var = jnp.where(blk == 0, 0, jnp.where(blk == nblk - 1, 2, 1))
        q0 = pl.multiple_of(r0 * W, Q_ROWS * W)
        k0 = pl.multiple_of(bs * W, W)
        q = q_ref[pl.ds(q0, Q_ROWS * W), :]
        kb = k_ref[pl.ds(k0, BAND_ROWS * W), :]
        vb = v_ref[pl.ds(k0, BAND_ROWS * W), :]
        bias = jnp.concatenate([bias_ref[var, c] for c in range(bias_ref.shape[1])], axis=1)
        o = attend(q, [(kb, vb, bias), (kc, vc, None)])
        o_ref[pl.ds(q0, Q_ROWS * W), :] = o.astype(o_ref.dtype)
        return carry

    lax.fori_loop(0, nblk, body, 0, unroll=2)
    oc_ref[...] = attend(qc_ref[...], [(kc, vc, None)]).astype(oc_ref.dtype)


def na_attention(qkv, qkv_c, bias, B, N, CTX, D):
    H = D // HEAD_DIM
    rows = N // GRID_W
    kern = functools.partial(_na_kernel, rows=rows)

    def spec(n, sec):
        return pl.BlockSpec((n, HEAD_DIM), lambda h, b: (b, sec * H + h))

    return pl.pallas_call(
        kern,
        grid=(H, B),
        in_specs=[spec(N, 0), spec(N, 1), spec(N, 2), spec(CTX, 0), spec(CTX, 1), spec(CTX, 2),
                  pl.BlockSpec((None,) + bias.shape[1:], lambda h, b: (h, 0, 0, 0, 0))],
        out_specs=[pl.BlockSpec((N, HEAD_DIM), lambda h, b: (b, h)),
                   pl.BlockSpec((CTX, HEAD_DIM), lambda h, b: (b, h))],
        out_shape=[jax.ShapeDtypeStruct((B * N, D), BF16),
                   jax.ShapeDtypeStruct((B * CTX, D), BF16)],
        compiler_params=_cparams("parallel", "parallel"),
        name="na_attention",
    )(qkv, qkv, qkv, qkv_c, qkv_c, qkv_c, bias)


def _expert_kernel(idx_ref, a_hbm, w1_ref, w3_ref, w2_ref, gate_ref, o_ref, xbuf, xs, yacc, sem, *,
                   n_tok, cap, bg):
    e, g, f = pl.program_id(0), pl.program_id(1), pl.program_id(2)
    n_exp, ng, nf = pl.num_programs(0), pl.num_programs(1), pl.num_programs(2)
    step = e * ng + g
    slot = step % 2
    nsteps = n_exp * ng
    rows = bg * cap
    S = xbuf.shape[1] // rows
    part = rows // nf

    def copy_row(ee, gg, sl, r0, u=0, priority=0):
        unit = math.gcd(part, cap)
        b = gg * bg + (r0 + (u // unit) * unit) // cap
        c = (r0 + (u // unit) * unit) % cap + u % unit
        row = b * n_tok + idx_ref[(b * n_exp + ee) * cap + c]
        src = pl.ds(pl.multiple_of(row * S, S), S)
        dst = pl.ds(pl.multiple_of((r0 + u) * S, S), S)
        pltpu.make_async_copy(a_hbm.at[src, :], xbuf.at[sl, dst, :], sem.at[sl]).start(priority=priority)

    def wait_rows(sl):
        pltpu.make_async_copy(a_hbm.at[pl.ds(0, rows * S), :], xbuf.at[sl], sem.at[sl]).wait()

    @pl.when((step == 0) & (f == 0))
    def _():
        def body(r, carry):
            copy_row(e, g, slot, r)
            return carry

        lax.fori_loop(0, rows, body, 0, unroll=8)

    @pl.when(f == 0)
    def _():
        wait_rows(slot)
        hi, lo = _load_token_tiles(xbuf, (slot,), 0, rows, S)
        xs[...] = jnp.concatenate([p.astype(BF16) for p in hi + lo], axis=1)
        yacc[...] = jnp.zeros_like(yacc)

    nxt = (step + 1) % nsteps
    for u in range(part):
        copy_row(nxt // ng, nxt % ng, 1 - slot, f * part, u, priority=1)

    x = xs[...]
    h1 = jnp.dot(x, w1_ref[...].astype(BF16), preferred_element_type=F32)
    h3 = jnp.dot(x, w3_ref[...].astype(BF16), preferred_element_type=F32)
    hid = (h1 * jax.nn.sigmoid(h1) * h3).astype(BF16)
    w2 = w2_ref[...].astype(BF16)
    D = w2.shape[1]
    cw = min(D, EXPERT_OUT_COLS)
    for c in range(D // cw):
        cols = slice(c * cw, (c + 1) * cw)
        yacc[:, cols] += jnp.dot(hid, w2[:, cols], preferred_element_type=F32)

    @pl.when(f == nf - 1)
    def _():
        packed = _pack_bf16_pair(yacc[...] * gate_ref[...].reshape(rows, 1))
        for i in range(bg):
            _store_token_tiles(o_ref, (i,), packed[i * cap:(i + 1) * cap])

    @pl.when((step == nsteps - 1) & (f == nf - 1))
    def _():
        wait_rows(1 - slot)


def _pack_bf16_pair(y):
    half = y.shape[1] // 2
    hi = pltpu.bitcast(y[:, :half].astype(BF16).astype(F32), jnp.uint32)
    lo = pltpu.bitcast(y[:, half:].astype(BF16).astype(F32), jnp.uint32)
    return hi | (lo >> 16)


def _unpack_bf16_pair(u):
    hi = pltpu.bitcast(u & jnp.uint32(0xFFFF0000), F32)
    lo = pltpu.bitcast(u << 16, F32)
    return hi, lo


def _store_token_tiles(ref, lead, packed):
    rows, S = packed.shape[0], packed.shape[1] // LANES
    for s in range(S):
        ref[(*lead, pl.ds(s, rows, stride=S), slice(None))] = packed[:, s * LANES:(s + 1) * LANES]


def _load_token_tiles(ref, lead, first, rows, S):
    pieces = [_unpack_bf16_pair(ref[(*lead, pl.ds(first * S + s, rows, stride=S), slice(None))])
              for s in range(S)]
    return [p[0] for p in pieces], [p[1] for p in pieces]


def expert_ffn(a, idx_flat, layer, w1, w3, w2, gate, B, n_tok):
    _, E, D, Fh = w1.shape
    C = gate.shape[2]
    S = D // (2 * LANES)
    bg = max(1, min(B, EXPERT_ROWS // C))
    while B % bg:
        bg -= 1
    fc = _tile(Fh, EXPERT_HIDDEN_CHUNK)
    assert (bg * C) % (Fh // fc) == 0
    kern = functools.partial(_expert_kernel, n_tok=n_tok, cap=C, bg=bg)
    grid_spec = pltpu.PrefetchScalarGridSpec(
        num_scalar_prefetch=1,
        grid=(E, B // bg, Fh // fc),
        in_specs=[
            pl.BlockSpec(memory_space=pl.ANY),
            pl.BlockSpec((None, None, D, fc), lambda e, g, f, idx: (layer, e, 0, f)),
            pl.BlockSpec((None, None, D, fc), lambda e, g, f, idx: (layer, e, 0, f)),
            pl.BlockSpec((None, None, fc, D), lambda e, g, f, idx: (layer, e, f, 0)),
            pl.BlockSpec((bg, None, C, 1), lambda e, g, f, idx: (g, e, 0, 0)),
        ],
        out_specs=pl.BlockSpec((bg, None, C * S, LANES), lambda e, g, f, idx: (g, e, 0, 0)),
        scratch_shapes=[pltpu.VMEM((2, bg * C * S, LANES), jnp.uint32), pltpu.VMEM((bg * C, D), BF16),
                        pltpu.VMEM((bg * C, D), F32), pltpu.SemaphoreType.DMA((2,))],
    )
    return pl.pallas_call(
        kern,
        grid_spec=grid_spec,
        out_shape=jax.ShapeDtypeStruct((B, E, C * S, LANES), jnp.uint32),
        compiler_params=_cparams("arbitrary", "arbitrary", "arbitrary"),
        name="expert_ffn",
    )(idx_flat, a, w1, w3, w2, gate)


def _moe_prep_kernel(x_ref, g_ref, sh_ref, sc_ref, r_ref, a_ref, aff_ref):
    a = _norm_mod(x_ref[...], g_ref[...], sh_ref[...], sc_ref[...])
    _store_token_tiles(a_ref, (), _pack_bf16_pair(a))
    logits = lax.dot_general(r_ref[...], a.astype(BF16), (((1,), (1,)), ((), ())),
                             preferred_element_type=F32)
    m = jnp.max(logits, axis=0, keepdims=True)
    p = jnp.exp(logits - m)
    aff_ref[...] = p / jnp.sum(p, axis=0, keepdims=True)


def moe_prep(x, g, mods, k_shift, k_scale, group_rows, group_base, router_t, tm=512):
    R, D = x.shape
    E = router_t.shape[0]
    S = D // (2 * LANES)
    tm = _tile(group_rows, tm, 128)

    def grp(i):
        return group_base + (i * tm) // group_rows

    return pl.pallas_call(
        _moe_prep_kernel,
        grid=(R // tm,),
        in_specs=[
            pl.BlockSpec((tm, D), lambda i: (i, 0)),
            pl.BlockSpec((1, D), lambda i: (0, 0)),
            pl.BlockSpec((None, None, 1, D), lambda i: (k_shift, grp(i), 0, 0)),
            pl.BlockSpec((None, None, 1, D), lambda i: (k_scale, grp(i), 0, 0)),
            pl.BlockSpec((E, D), lambda i: (0, 0)),
        ],
        out_specs=[pl.BlockSpec((tm * S, LANES), lambda i: (i, 0)),
                   pl.BlockSpec((E, tm), lambda i: (0, i))],
        out_shape=[jax.ShapeDtypeStruct((R * S, LANES), jnp.uint32), jax.ShapeDtypeStruct((E, R), F32)],
        compiler_params=_cparams("parallel"),
        name="moe_prep",
    )(x, g.reshape(1, D), mods, mods, router_t)


EXPERT_ROWS = 1024
EXPERT_HIDDEN_CHUNK = 256
EXPERT_OUT_COLS = 512
COMBINE_TOKENS = 128
HIT_UNROLL = 4
COMBINE_ROW_GROUPS = 2
CUMSUM_CHUNK = 256
ROUTE_BLOCK_LANES = 512
TILE_TABLE_LANES = 128


def _route_select_kernel(aff_ref, tri_ref, ltri_ref, lt_ref, in_ref,
                         posi_ref, key_ref, slab_ref, p0_ref, kmax_ref, *, cap):
    E, N = aff_ref.shape
    aff = aff_ref[...]
    bits = pltpu.bitcast(aff, jnp.int32)
    cur = jnp.zeros((E, 1), jnp.int32)
    for bit in range(30, -1, -1):
        cand = cur | (1 << bit)
        cnt = jnp.sum(jnp.where(bits >= cand, 1.0, 0.0), axis=1, keepdims=True)
        cur = jnp.where(cnt >= cap, cand, cur)
    gt = bits > cur
    eq = bits == cur
    need = cap - jnp.sum(jnp.where(gt, 1.0, 0.0), axis=1, keepdims=True)

    def cumsum_incl(x):
        tri = tri_ref[...]
        w = min(CUMSUM_CHUNK, N)
        off = jnp.zeros((E, 1), F32)
        outs = []
        for c in range(N // w):
            xc = x[:, c * w:(c + 1) * w]
            outs.append(jnp.dot(xc.astype(BF16), tri[:w, :w], preferred_element_type=F32) + off)
            off = off + jnp.sum(xc, axis=1, keepdims=True)
        return jnp.concatenate(outs, axis=1) if len(outs) > 1 else outs[0]

    eqf = jnp.where(eq, 1.0, 0.0)
    tie_rank = cumsum_incl(eqf) - eqf
    sel = gt | (eq & (tie_rank < need))
    self_ = jnp.where(sel, 1.0, 0.0)
    pos_incl = cumsum_incl(self_)
    posi_ref[...] = pos_incl
    key_ref[...] = jnp.where(sel, pos_incl - self_, -1.0)
    selb = self_.astype(BF16)
    erank = jnp.dot(ltri_ref[...], selb, preferred_element_type=F32)
    in_tile = lax.broadcasted_iota(jnp.int32, (E, N), 1) & (COMBINE_TOKENS - 1)
    slab_ref[...] = erank * COMBINE_TOKENS + in_tile.astype(F32)
    p0_ref[...] = jnp.dot(selb, lt_ref[...], preferred_element_type=F32).astype(jnp.int32)
    count = jnp.sum(self_, axis=0, keepdims=True)
    kk = lax.broadcasted_iota(jnp.int32, (E, 1), 0).astype(F32)
    over = jnp.where(count > kk, 1.0, 0.0).astype(BF16)
    per_tile = jnp.dot(over, in_ref[...], preferred_element_type=F32)
    kmax_ref[...] = jnp.sum(jnp.where(per_tile > 0.0, 1.0, 0.0), axis=0, keepdims=True).astype(jnp.int32)


def _route_compact_kernel(p0_ref, aff_ref, posi_ref, key_ref, slab_ref, idx_ref, gate_ref, rank_ref, *,
                          cap, tn, lanes, width):
    b = pl.program_id(0)
    E, N = aff_ref.shape
    ch = min(LANES, cap)
    tpb = width // tn
    diag = lax.broadcasted_iota(jnp.int32, (ch, ch), 0) == lax.broadcasted_iota(jnp.int32, (ch, ch), 1)

    def as_row(col):
        return jnp.sum(jnp.where(diag, col, 0.0), axis=0, keepdims=True)

    def e_body(e, carry):
        q = (b * E + e) * lanes

        def c_body(cc, tiles):
            c0 = cc * ch
            t_lo = lax.while_loop(lambda t: p0_ref[q + t + 1] <= c0, lambda t: t + 1, tiles[0])
            t_hi = lax.while_loop(lambda t: p0_ref[q + t] < c0 + ch, lambda t: t + 1, tiles[1])
            j_lo = t_lo // tpb
            j_hi = (t_hi + tpb - 1) // tpb
            ccol = (c0 + lax.broadcasted_iota(jnp.int32, (ch, 1), 0)).astype(F32)

            def j_body(j, acc):
                row = pl.ds(e, 1)
                blk = pl.ds(pl.multiple_of(j * width, width), width)
                pi, ky, af, sl = posi_ref[row, blk], key_ref[row, blk], aff_ref[row, blk], slab_ref[row, blk]
                acc = list(acc)
                for u in range(width // LANES):
                    part = slice(u * LANES, (u + 1) * LANES)
                    hit = ky[:, part] == ccol
                    acc[0] = acc[0] + jnp.where(pi[:, part] <= ccol, 1.0, 0.0)
                    acc[1] = acc[1] + jnp.where(hit, af[:, part], 0.0)
                    acc[2] = acc[2] + jnp.where(hit, sl[:, part], 0.0)
                return tuple(acc)

            zero = jnp.zeros((ch, LANES), F32)
            iv, gv, rv = [jnp.sum(a, axis=1, keepdims=True)
                          for a in lax.fori_loop(j_lo, j_hi, j_body, (zero, zero, zero))]
            idx_ref[pl.ds(e, 1), pl.ds(c0, ch)] = as_row(iv).astype(jnp.int32) + j_lo * width
            gate_ref[e, pl.ds(c0, ch), :] = gv
            rank_ref[pl.ds(e, 1), pl.ds(c0, ch)] = as_row(rv).astype(jnp.int32)
            return (t_lo, t_hi)

        tiles = (0, 0)
        for cc in range(cap // ch):
            tiles = c_body(cc, tiles)
        return carry

    for e in range(E):
        e_body(e, 0)


def moe_route(aff_t, B, N, cap):
    E = aff_t.shape[0]
    tn = COMBINE_TOKENS
    lanes = TILE_TABLE_LANES
    assert N // tn + 1 <= lanes and N % tn == 0 and N % min(CUMSUM_CHUNK, N) == 0
    w = min(CUMSUM_CHUNK, N)
    tri = (np.arange(w)[:, None] <= np.arange(w)[None, :]).astype(np.float32)
    ltri = (np.arange(E)[None, :] < np.arange(E)[:, None]).astype(np.float32)
    tok = np.arange(N)[:, None]
    tile = np.arange(lanes)[None, :]
    before = (tok < tile * tn).astype(np.float32)
    inside = (tok // tn == tile).astype(np.float32)
    const = lambda shape: pl.BlockSpec(shape, lambda b: (0,) * len(shape))
    per_tok = pl.BlockSpec((None, E, N), lambda b, *_: (b, 0, 0))
    posi, key, slab, p0, kmax = pl.pallas_call(
        functools.partial(_route_select_kernel, cap=cap),
        grid=(B,),
        in_specs=[pl.BlockSpec((E, N), lambda b: (0, b)), const((w, w)), const((E, E)),
                  const((N, lanes)), const((N, lanes))],
        out_specs=[per_tok, per_tok, per_tok,
                   pl.BlockSpec((None, E, lanes), lambda b: (b, 0, 0)),
                   pl.BlockSpec((None, 1, lanes), lambda b: (b, 0, 0))],
        out_shape=[jax.ShapeDtypeStruct((B, E, N), F32)] * 3 + [
            jax.ShapeDtypeStruct((B, E, lanes), jnp.int32), jax.ShapeDtypeStruct((B, 1, lanes), jnp.int32)],
        compiler_params=_cparams("parallel"),
        name="moe_route_select",
    )(aff_t, jnp.asarray(tri, BF16), jnp.asarray(ltri, BF16), jnp.asarray(before, BF16),
      jnp.asarray(inside, BF16))
    width = min(ROUTE_BLOCK_LANES, N)
    sel3 = pl.BlockSpec((None, E, cap, 1), lambda b, *_: (b, 0, 0, 0))
    rows2 = pl.BlockSpec((None, E, cap), lambda b, *_: (b, 0, 0))
    idx, gate, rank = pl.pallas_call(
        functools.partial(_route_compact_kernel, cap=cap, tn=tn, lanes=lanes, width=width),
        grid_spec=pltpu.PrefetchScalarGridSpec(
            num_scalar_prefetch=1,
            grid=(B,),
            in_specs=[pl.BlockSpec((E, N), lambda b, *_: (0, b)), per_tok, per_tok, per_tok],
            out_specs=[rows2, sel3, rows2],
        ),
        out_shape=[jax.ShapeDtypeStruct((B, E, cap), jnp.int32),
                   jax.ShapeDtypeStruct((B, E, cap, 1), F32),
                   jax.ShapeDtypeStruct((B, E, cap), jnp.int32)],
        compiler_params=_cparams("parallel"),
        name="moe_route_compact",
    )(p0.reshape(-1), aff_t, posi, key, slab)
    return idx, gate, rank, p0, kmax


def _combine_kernel(rank_ref, p0_ref, kmax_ref, ys_hbm, h_ref, g2_ref, o_ref, rbuf, sem, *,
                    n_exp, cap, tn, lanes):
    b, t = pl.program_id(0), pl.program_id(1)
    nt = pl.num_programs(1)
    step = b * nt + t
    slot = step % 2
    S = rbuf.shape[1] // (n_exp * tn)

    def tile_hits(bb, tt):
        def body(e, tot):
            q = (bb * n_exp + e) * lanes + tt
            return tot + p0_ref[q + 1] - p0_ref[q]
        return lax.fori_loop(0, n_exp, body, 0)

    def fetch(bb, tt, sl):
        km = kmax_ref[bb * lanes + tt]
        for k in range(n_exp):
            @pl.when(k < km)
            def _():
                rbuf[sl, k * tn * S:(k + 1) * tn * S, :] = jnp.zeros((tn * S, LANES), jnp.uint32)

        def copy_row(row, priority=0):
            src = pl.ds(pl.multiple_of(row * S, S), S)
            dst = pl.ds(pl.multiple_of(rank_ref[row] * S, S), S)
            pltpu.make_async_copy(ys_hbm.at[src, :], rbuf.at[sl, dst, :], sem.at[sl]).start(priority=priority)

        def e_body(e, carry):
            q = (bb * n_exp + e) * lanes + tt
            lo = (bb * n_exp + e) * cap + p0_ref[q]
            n = p0_ref[q + 1] - p0_ref[q]
            groups = n // HIT_UNROLL

            def g_body(g, carry2):
                for u in range(HIT_UNROLL):
                    copy_row(lo + g * HIT_UNROLL + u, priority=u % 2)
                return carry2

            def r_body(c, carry2):
                copy_row(lo + c)
                return carry2

            lax.fori_loop(0, groups, g_body, 0)
            lax.fori_loop(groups * HIT_UNROLL, n, r_body, 0)
            return carry

        lax.fori_loop(0, n_exp, e_body, 0)

    @pl.when(step == 0)
    def _():
        fetch(b, t, slot)

    @pl.when(step + 1 < pl.num_programs(0) * nt)
    def _():
        nxt = step + 1
        fetch(nxt // nt, nxt % nt, 1 - slot)

    hits = tile_hits(b, t)

    nrows = hits * S
    bulk = pl.multiple_of((nrows // 8) * 8, 8)

    @pl.when(bulk > 0)
    def _():
        pltpu.make_async_copy(ys_hbm.at[pl.ds(0, bulk), :], rbuf.at[slot, pl.ds(0, bulk), :],
                              sem.at[slot]).wait()

    def wait_row(i, carry):
        pltpu.make_async_copy(ys_hbm.at[pl.ds(0, 1), :], rbuf.at[slot, pl.ds(0, 1), :], sem.at[slot]).wait()
        return carry

    lax.fori_loop(0, nrows - bulk, wait_row, 0)

    km = kmax_ref[b * lanes + t]
    g2 = g2_ref[...]
    rg = COMBINE_ROW_GROUPS
    for r in range(0, tn // 8, rg):
        def k_body(k, tot):
            new = []
            for i in range(rg):
                hi, lo = _load_token_tiles(rbuf, (slot,), k * tn + (r + i) * 8, 8, S)
                new += hi + lo
            return tuple(t_ + p for t_, p in zip(tot, new))

        tot = lax.fori_loop(0, km, k_body, (jnp.zeros((8, LANES), F32),) * (2 * S * rg))
        for i in range(rg):
            rows = slice((r + i) * 8, (r + i + 1) * 8)
            for s in range(2 * S):
                cols = slice(s * LANES, (s + 1) * LANES)
                o_ref[rows, cols] = h_ref[rows, cols] + g2[:, cols] * tot[i * 2 * S + s]


def moe_combine(ys, h, mods, k_gate, group_rows, group_base, rank, p0, kmax, B, N):
    R, D = h.shape
    E, cap = rank.shape[1], rank.shape[2]
    tn = COMBINE_TOKENS
    lanes = TILE_TABLE_LANES
    nt = N // tn

    def grp(b, t):
        return group_base + ((b * nt + t) * tn) // group_rows

    kern = functools.partial(_combine_kernel, n_exp=E, cap=cap, tn=tn, lanes=lanes)
    grid_spec = pltpu.PrefetchScalarGridSpec(
        num_scalar_prefetch=3,
        grid=(B, nt),
        in_specs=[
            pl.BlockSpec(memory_space=pl.ANY),
            pl.BlockSpec((tn, D), lambda b, t, *_: (b * nt + t, 0)),
            pl.BlockSpec((None, None, 1, D), lambda b, t, *_: (k_gate, grp(b, t), 0, 0)),
        ],
        out_specs=pl.BlockSpec((tn, D), lambda b, t, *_: (b * nt + t, 0)),
        scratch_shapes=[pltpu.VMEM((2, E * tn * (D // (2 * LANES)), LANES), jnp.uint32),
                        pltpu.SemaphoreType.DMA((2,))],
    )
    return pl.pallas_call(
        kern,
        grid_spec=grid_spec,
        out_shape=jax.ShapeDtypeStruct((R, D), F32),
        compiler_params=_cparams("arbitrary", "arbitrary"),
        name="moe_combine",
    )(rank.reshape(-1), p0.reshape(-1), kmax.reshape(-1), ys, h, mods)


def ec_moe(h, g, mods, k_shift, k_scale, k_gate, group_rows, group_base, B, router, layer, w1, w3, w2):
    R, D = h.shape
    N = R // B
    E = router.shape[1]
    cap = CAPACITY_FACTOR * N // E
    a, aff_t = moe_prep(h, g, mods, k_shift, k_scale, group_rows, group_base, router.T.astype(BF16))
    idx, gate, rank, p0, kmax = moe_route(aff_t, B, N, cap)
    ys = expert_ffn(a, idx.reshape(-1), layer, w1, w3, w2, gate, B, N)
    return moe_combine(ys.reshape(-1, LANES), h, mods, k_gate, group_rows, group_base,
                       rank, p0, kmax, B, N)


FFT_MINOR = 256
FFT_GROUP = 16


@functools.lru_cache(maxsize=None)
def _dft_tables(L):
    n = 2 * L
    n1 = n // FFT_MINOR
    nb = n1 // 2
    two_pi = 2.0 * math.pi
    a = np.arange(n1)
    ang = two_pi * ((a[:, None] * a[None, :]) % n1) / n1
    cos1, sin1 = np.cos(ang), np.sin(ang)
    eye = np.eye(FFT_GROUP)

    def kron(m):
        return np.kron(m, eye)

    def const(m):
        return jnp.asarray(m.astype(np.float32).astype(BF16))

    f1 = np.arange(n1)[:, None, None]
    f2 = np.arange(FFT_MINOR)[None, :, None]
    n2 = np.arange(FFT_MINOR)[None, None, :]
    ang = two_pi * ((n2 * (f1 + n1 * f2)) % n) / n
    gr, gi = np.cos(ang), -np.sin(ang)
    return dict(
        n1=n1, nb=nb,
        m_fwd_full=const(np.concatenate([kron(cos1), kron(-sin1)], axis=0)),
        m_fwd_half=const(np.concatenate([kron(cos1[:, :nb]), kron(-sin1[:, :nb])], axis=0)),
        m_inv=const(np.concatenate([kron(cos1[:nb]), kron(sin1[:nb])], axis=0)),
        g=const(np.concatenate([gr, gi], axis=1)),
        gt=const(np.concatenate([gr.transpose(0, 2, 1), gi.transpose(0, 2, 1)], axis=1)),
    )


HALO = 8


def _hyena_in_kernel(x_ref, xp_ref, xn_ref, g_ref, sh_ref, sc_ref, w0_ref, w1_ref, w2_ref, b_ref, cw_ref,
                     cb_ref, v_ref, x0_ref, a_ref, ah_ref, *, seq_len):
    i, j = pl.program_id(0), pl.program_id(1)
    tm = x_ref.shape[0]

    @pl.when(j == 0)
    def _():
        g, sh, sc = g_ref[...], sh_ref[...], sc_ref[...]
        a_ref[...] = _norm_mod(x_ref[...], g, sh, sc).astype(BF16)
        ah_ref[0:HALO, :] = _norm_mod(xp_ref[...], g, sh, sc)
        ah_ref[HALO:, :] = _norm_mod(xn_ref[...], g, sh, sc)

    first = (i * tm) % seq_len == 0
    last = ((i + 1) * tm) % seq_len == 0
    a = a_ref[...]
    ah = ah_ref[...].astype(BF16)
    row = lax.broadcasted_iota(jnp.int32, (tm, v_ref.shape[1]), 0)

    def section(k, w_ref):
        w = w_ref[...]
        p = jnp.dot(a, w, preferred_element_type=F32) + b_ref[k]
        ph = jnp.dot(ah, w, preferred_element_type=F32) + b_ref[k]
        above = jnp.where(first, 0.0, ph[HALO - 1:HALO])
        below = jnp.where(last, 0.0, ph[HALO:HALO + 1])
        cw = cw_ref[k]
        prev = jnp.where(row == 0, above, pltpu.roll(p, 1, 0))
        nxt = jnp.where(row == tm - 1, below, pltpu.roll(p, tm - 1, 0))
        return prev * cw[0:1] + p * cw[1:2] + nxt * cw[2:3] + cb_ref[k]

    x0 = section(0, w0_ref)
    x1 = section(1, w1_ref)
    v = section(2, w2_ref)
    v_ref[...] = (v * x1).astype(v_ref.dtype)
    x0_ref[...] = x0.astype(x0_ref.dtype)


def hyena_in_proj(x, g, mods, k_shift, k_scale, group_rows, group_base, seq_len, w, b, conv_w, conv_b,
                  tm=1024, tc=512):
    R, D = x.shape
    tm = _tile(math.gcd(group_rows, seq_len), tm, HALO)
    tc = _tile(D, tc)
    nc = D // tc
    cw = conv_w.reshape(HYENA_SHORT, 3, D).transpose(1, 0, 2)

    def grp(i):
        return group_base + (i * tm) // group_rows

    def wspec(k):
        return pl.BlockSpec((D, tc), lambda i, j: (0, k * nc + j))

    mod = lambda k: pl.BlockSpec((None, None, 1, D), lambda i, j: (k, grp(i), 0, 0))
    per_proj = lambda rows: pl.BlockSpec((3, rows, tc), lambda i, j: (0, 0, j))
    kern = functools.partial(_hyena_in_kernel, seq_len=seq_len)
    return pl.pallas_call(
        kern,
        grid=(R // tm, nc),
        in_specs=[pl.BlockSpec((tm, D), lambda i, j: (i, 0)),
                  pl.BlockSpec((HALO, D), lambda i, j: (jnp.maximum(i * (tm // HALO) - 1, 0), 0)),
                  pl.BlockSpec((HALO, D), lambda i, j: (jnp.minimum((i + 1) * (tm // HALO), R // HALO - 1), 0)),
                  pl.BlockSpec((1, D), lambda i, j: (0, 0)), mod(k_shift), mod(k_scale),
                  wspec(0), wspec(1), wspec(2), per_proj(1), per_proj(HYENA_SHORT), per_proj(1)],
        out_specs=[pl.BlockSpec((tm, tc), lambda i, j: (i, j))] * 2,
        out_shape=[jax.ShapeDtypeStruct((R, D), BF16)] * 2,
        scratch_shapes=[pltpu.VMEM((tm, D), BF16), pltpu.VMEM((2 * HALO, D), F32)],
        compiler_params=_cparams("parallel", "arbitrary"),
        name="hyena_in_proj",
    )(x, x, x, g.reshape(1, D), mods, mods, w, w, w, b.reshape(3, 1, D), cw, conv_b.reshape(3, 1, D))


def _filter_kernel(emb_ref, w1_ref, b1_ref, f1_ref, w2_ref, b2_ref, f2_ref, w3_ref, b3_ref, f3_ref,
                   wo_ref, dl_ref, k_ref, norm_ref, *, L):
    i = pl.program_id(0)
    tr = emb_ref.shape[0]
    hp = lax.Precision.HIGHEST
    emb = emb_ref[...]
    h = jnp.sin(f1_ref[...] * (jnp.dot(emb, w1_ref[...], precision=hp, preferred_element_type=F32) + b1_ref[...]))
    h = jnp.sin(f2_ref[...] * (jnp.dot(h, w2_ref[...], precision=hp, preferred_element_type=F32) + b2_ref[...]))
    h = jnp.sin(f3_ref[...] * (jnp.dot(h, w3_ref[...], precision=hp, preferred_element_type=F32) + b3_ref[...]))
    k = jnp.dot(h, wo_ref[...], precision=hp, preferred_element_type=F32)
    k = k * jnp.exp(-emb[:, 0:1] * dl_ref[...])
    row = i * tr + lax.broadcasted_iota(jnp.int32, (tr, 1), 0)
    k = jnp.where(row == L, 0.0, k)
    k_ref[...] = k

    @pl.when(i == 0)
    def _():
        norm_ref[...] = jnp.zeros_like(norm_ref)

    norm_ref[...] += jnp.sum(jnp.abs(k), axis=0, keepdims=True)


def hyena_filter(L, D, w1, b1, f1, w2, b2, f2, w3, b3, f3, w_out, tr=512):
    n = 2 * L
    tr = _tile(L, tr, 8)
    P = 128
    bands = (HYENA_EMB - 1) // 2
    d = np.arange(n)
    d = np.where(d <= L, d, n - d).clip(0, L - 1)
    t01 = np.linspace(0.0, 1.0, L)[d]
    wang = 2 * math.pi * d / L
    fr = np.linspace(1e-4, bands - 1, bands)
    emb = np.zeros((n, P), np.float32)
    emb[:, 0] = t01
    emb[:, 1:1 + bands] = np.cos(fr[None, :] * wang[:, None])
    emb[:, 1 + bands:1 + 2 * bands] = -np.sin(fr[None, :] * wang[:, None])

    def padw(w):
        return jnp.zeros((P, P), F32).at[:w.shape[0], :w.shape[1]].set(w)

    def padv(v):
        return jnp.zeros((1, P), F32).at[0, :v.shape[0]].set(v)

    wo = jnp.zeros((P, 2 * D), F32).at[:w_out.shape[0]].set(w_out)
    max_decay = math.log(HYENA_TARGET) / HYENA_FAST_DECAY
    min_decay = math.log(HYENA_TARGET) / HYENA_SLOW_DECAY
    deltas = jnp.abs(jnp.linspace(min_decay, max_decay, D, dtype=F32)).reshape(1, D)
    small = pl.BlockSpec((P, P), lambda i: (0, 0))
    vec = pl.BlockSpec((1, P), lambda i: (0, 0))
    kern = functools.partial(_filter_kernel, L=L)
    return pl.pallas_call(
        kern,
        grid=(n // tr,),
        in_specs=[pl.BlockSpec((tr, P), lambda i: (i, 0)),
                  small, vec, vec, small, vec, vec, small, vec, vec,
                  pl.BlockSpec((P, D), lambda i: (0, (i * tr) // L)),
                  pl.BlockSpec((1, D), lambda i: (0, 0))],
        out_specs=[pl.BlockSpec((tr, D), lambda i: (i, 0)), pl.BlockSpec((1, D), lambda i: (0, 0))],
        out_shape=[jax.ShapeDtypeStruct((n, D), F32), jax.ShapeDtypeStruct((1, D), F32)],
        compiler_params=_cparams("arbitrary"),
        name="hyena_filter",
    )(jnp.asarray(emb), padw(w1), padv(b1), padv(f1), padw(w2), padv(b2), padv(f2),
      padw(w3), padv(b3), padv(f3), wo, deltas)


def _fwd_major_kernel(*refs, n1, nb, has_imag):
    if has_imag:
        zr_ref, zi_ref, m_ref, ar_ref, ai_ref = refs
    else:
        zr_ref, m_ref, ar_ref, ai_ref = refs
    m = m_ref[...]
    half = n1 * FFT_GROUP

    def body(g, carry):
        def rows(blk):
            return pl.ds(pl.multiple_of(blk * FFT_MINOR + g * FFT_GROUP, FFT_GROUP), FFT_GROUP)

        def slab(ref):
            return jnp.concatenate([ref[rows(blk), :] for blk in range(nb)], axis=0).astype(BF16)

        pr = jnp.dot(m, slab(zr_ref), preferred_element_type=F32)
        if has_imag:
            pi = jnp.dot(m, slab(zi_ref), preferred_element_type=F32)
            ar, ai = pr[:half] - pi[half:], pi[:half] + pr[half:]
        else:
            ar, ai = pr[:half], pr[half:]
        for f1 in range(n1):
            sl = slice(f1 * FFT_GROUP, (f1 + 1) * FFT_GROUP)
            ar_ref[rows(f1), :] = ar[sl].astype(BF16)
            ai_ref[rows(f1), :] = ai[sl].astype(BF16)
        return carry

    lax.fori_loop(0, FFT_MINOR // FFT_GROUP, body, 0)


def fwd_major(z, tabs, pairs, tc=256):
    n1, nb = tabs["n1"], tabs["nb"]
    n = n1 * FFT_MINOR
    Bz, Lz, D = z.shape
    tc = _tile(D, tc)
    if pairs:
        P = Bz // 2
        m = tabs["m_fwd_half"]
        ins = [z, z, m]
        in_specs = [pl.BlockSpec((None, Lz, tc), lambda p, c: (2 * p, 0, c)),
                    pl.BlockSpec((None, Lz, tc), lambda p, c: (2 * p + 1, 0, c))]
        nblk = nb
    else:
        P = 1
        m = tabs["m_fwd_full"]
        ins = [z, m]
        in_specs = [pl.BlockSpec((None, Lz, tc), lambda p, c: (0, 0, c))]
        nblk = n1
    in_specs.append(pl.BlockSpec(m.shape, lambda p, c: (0, 0)))
    kern = functools.partial(_fwd_major_kernel, n1=n1, nb=nblk, has_imag=pairs)
    return pl.pallas_call(
        kern,
        grid=(P, D // tc),
        in_specs=in_specs,
        out_specs=[pl.BlockSpec((None, n, tc), lambda p, c: (p, 0, c))] * 2,
        out_shape=[jax.ShapeDtypeStruct((P, n, D), BF16)] * 2,
        compiler_params=_cparams("parallel", "parallel"),
        name="hyena_fwd_major",
    )(*ins)


def _spectrum_kernel(ar_ref, ai_ref, g_ref, s_ref, kr_ref, ki_ref):
    g = g_ref[...]
    h = FFT_MINOR
    pr = jnp.dot(g, ar_ref[...], preferred_element_type=F32)
    pi = jnp.dot(g, ai_ref[...], preferred_element_type=F32)
    s = s_ref[...]
    kr_ref[...] = (pr[:h] - pi[h:]) * s
    ki_ref[...] = (pi[:h] + pr[h:]) * s


def filter_spectrum(ar, ai, tabs, scale, tc=1024):
    _, n, D = ar.shape
    tc = _tile(D, tc)
    blk = pl.BlockSpec((None, FFT_MINOR, tc), lambda f, c: (0, f, c))
    out = pl.BlockSpec((FFT_MINOR, tc), lambda f, c: (f, c))
    return pl.pallas_call(
        _spectrum_kernel,
        grid=(tabs["n1"], D // tc),
        in_specs=[blk, blk, pl.BlockSpec((None, 2 * FFT_MINOR, FFT_MINOR), lambda f, c: (f, 0, 0)),
                  pl.BlockSpec((1, tc), lambda f, c: (0, c))],
        out_specs=[out, out],
        out_shape=[jax.ShapeDtypeStruct((n, D), F32)] * 2,
        compiler_params=_cparams("parallel", "parallel"),
        name="hyena_filter_spectrum",
    )(ar, ai, tabs["g"], scale)


def _minor_kernel(ar_ref, ai_ref, g_ref, gt_ref, kr_ref, ki_ref, br_ref, bi_ref):
    h = FFT_MINOR
    g = g_ref[...]
    pr = jnp.dot(g, ar_ref[...], preferred_element_type=F32)
    pi = jnp.dot(g, ai_ref[...], preferred_element_type=F32)
    xr, xi = pr[:h] - pi[h:], pi[:h] + pr[h:]
    kr, ki = kr_ref[...], ki_ref[...]
    yr = (xr * kr - xi * ki).astype(BF16)
    yi = (xr * ki + xi * kr).astype(BF16)
    gt = gt_ref[...]
    qr = jnp.dot(gt, yr, preferred_element_type=F32)
    qi = jnp.dot(gt, yi, preferred_element_type=F32)
    br_ref[...] = (qr[:h] + qi[h:]).astype(BF16)
    bi_ref[...] = (qi[:h] - qr[h:]).astype(BF16)


def minor_conv(ar, ai, tabs, kr, ki, tc=1024):
    P, n, D = ar.shape
    tc = _tile(D, tc)
    blk = pl.BlockSpec((None, FFT_MINOR, tc), lambda f, c, p: (p, f, c))
    tab = pl.BlockSpec((None, 2 * FFT_MINOR, FFT_MINOR), lambda f, c, p: (f, 0, 0))
    kblk = pl.BlockSpec((FFT_MINOR, tc), lambda f, c, p: (f, c))
    return pl.pallas_call(
        _minor_kernel,
        grid=(tabs["n1"], D // tc, P),
        in_specs=[blk, blk, tab, tab, kblk, kblk],
        out_specs=[blk, blk],
        out_shape=[jax.ShapeDtypeStruct((P, n, D), BF16)] * 2,
        compiler_params=_cparams("parallel", "parallel", "parallel"),
        name="hyena_minor_conv",
    )(ar, ai, tabs["g"], tabs["gt"], kr, ki)


def _inv_major_kernel(br_ref, bi_ref, m_ref, v_ref, x0_ref, bias_ref, o_ref, *, n1, nb):
    m = m_ref[...]
    half = nb * FFT_GROUP
    bias = bias_ref[...]

    def body(g, carry):
        def rows(blk):
            return pl.ds(pl.multiple_of(blk * FFT_MINOR + g * FFT_GROUP, FFT_GROUP), FFT_GROUP)

        def slab(ref):
            return jnp.concatenate([ref[rows(f1), :] for f1 in range(n1)], axis=0)

        pr = jnp.dot(m, slab(br_ref), preferred_element_type=F32)
        pi = jnp.dot(m, slab(bi_ref), preferred_element_type=F32)
        ys = (pr[:half] - pi[half:], pi[:half] + pr[half:])
        for s in range(2):
            for blk in range(nb):
                y = ys[s][blk * FFT_GROUP:(blk + 1) * FFT_GROUP]
                v = v_ref[s, rows(blk), :].astype(F32)
                x0 = x0_ref[s, rows(blk), :].astype(F32)
                o_ref[s, rows(blk), :] = ((y + v * bias) * x0).astype(o_ref.dtype)
        return carry

    lax.fori_loop(0, FFT_MINOR // FFT_GROUP, body, 0)


def inv_major(br, bi, tabs, v, x0, bias, tc=256):
    n1, nb = tabs["n1"], tabs["nb"]
    P, n, D = br.shape
    B, L, _ = v.shape
    tc = _tile(D, tc)
    m = tabs["m_inv"]
    blk = pl.BlockSpec((None, n, tc), lambda p, c: (p, 0, c))
    pair = pl.BlockSpec((2, L, tc), lambda p, c: (p, 0, c))
    kern = functools.partial(_inv_major_kernel, n1=n1, nb=nb)
    return pl.pallas_call(
        kern,
        grid=(P, D // tc),
        in_specs=[blk, blk, pl.BlockSpec(m.shape, lambda p, c: (0, 0)), pair, pair,
                  pl.BlockSpec((1, tc), lambda p, c: (0, c))],
        out_specs=pair,
        out_shape=jax.ShapeDtypeStruct((B, L, D), BF16),
        compiler_params=_cparams("parallel", "parallel"),
        name="hyena_inv_major",
    )(br, bi, m, v, x0, bias.reshape(1, D))


def hyena_mix(v, x0, B, L, D, filt, bias):
    assert B % 2 == 0 and L % FFT_MINOR == 0
    tabs = _dft_tables(L)
    v, x0 = v.reshape(B, L, D), x0.reshape(B, L, D)
    k_raw, k_norm = hyena_filter(L, D, *filt)
    kar, kai = fwd_major(k_raw[None], tabs, pairs=False)
    kr, ki = filter_spectrum(kar, kai, tabs, 1.0 / (2 * L * k_norm))
    ar, ai = fwd_major(v, tabs, pairs=True)
    br, bi = minor_conv(ar, ai, tabs, kr, ki)
    z = inv_major(br, bi, tabs, v, x0, bias)
    return z.reshape(B * L, D)


def kernel(x, c, ctx, c_ctx, ada_w, ada_b, norm1_g, norm2_g, na_w_qkv, na_w_o, na_q_g, na_k_g, na_rpb, hy_w_in, hy_b_in, hy_conv_w, hy_conv_b, hy_f_w1, hy_f_b1, hy_f_freq1, hy_f_w2, hy_f_b2, hy_f_freq2, hy_f_w3, hy_f_b3, hy_f_freq3, hy_f_wout, hy_bias, hy_w_out, hy_b_out, moe_router, moe_w1, moe_w3, moe_w2):
    B, N, D = x.shape
    CTX = ctx.shape[1]
    depth = ada_w.shape[0]
    mixer = [i % N_MIXERS for i in range(depth)]

    cond = jnp.concatenate([c, c_ctx[None, :], jnp.zeros((8 - B - 1, D), F32)], axis=0)
    mods_all = ada_all(cond, ada_w, ada_b)
    mods_all = mods_all.reshape(depth, 8, 6, 1, D).transpose(0, 2, 1, 3, 4)

    h = x.reshape(B * N, D)
    hc = ctx.reshape(B * CTX, D)
    zeros_d = jnp.zeros((D,), F32)
    w1, w3, w2 = moe_w1, moe_w3, moe_w2
    for i in range(depth):
        j = i // N_MIXERS
        ctx_stream = any(mixer[l] == 0 for l in range(i + 1, depth))
        ctx_in = ctx_stream or mixer[i] == 0
        mods = mods_all[i]
        if mixer[i] == 0:
            wqkv = na_w_qkv[j].astype(BF16)
            wo = na_w_o[j].astype(BF16)
            hg = jnp.stack([na_q_g[j], na_k_g[j]]).reshape(2, 1, HEAD_DIM)
            zeros_e = jnp.zeros((3 * D,), F32)
            qs = HEAD_DIM ** -0.5
            qkv = nm_matmul(h, norm1_g[i], mods, 0, 1, N, 0, wqkv, zeros_e, hg, qs)
            qkv_c = nm_matmul(hc, norm1_g[i], mods, 0, 1, B * CTX, B, wqkv, zeros_e, hg, qs)
            bias = _na_bias_tables(na_rpb[j], N // GRID_W)
            o, oc = na_attention(qkv, qkv_c, bias, B, N, CTX, D)
            h = matmul_residual(o, wo, zeros_d, mods, 2, N, 0, h)
            if ctx_stream:
                hc = matmul_residual(oc, wo, zeros_d, mods, 2, B * CTX, B, hc)
        else:
            win = hy_w_in[j].astype(BF16)
            wout = hy_w_out[j].astype(BF16)
            filt = (hy_f_w1[j], hy_f_b1[j], hy_f_freq1[j], hy_f_w2[j], hy_f_b2[j], hy_f_freq2[j],
                    hy_f_w3[j], hy_f_b3[j], hy_f_freq3[j], hy_f_wout[j])
            v, x0 = hyena_in_proj(h, norm1_g[i], mods, 0, 1, N, 0, N, win, hy_b_in[j],
                                  hy_conv_w[j], hy_conv_b[j])
            z = hyena_mix(v, x0, B, N, D, filt, hy_bias[j])
            h = matmul_residual(z, wout, hy_b_out[j], mods, 2, N, 0, h)
            if ctx_stream:
                vc, x0c = hyena_in_proj(hc, norm1_g[i], mods, 0, 1, B * CTX, B, CTX, win, hy_b_in[j],
                                        hy_conv_w[j], hy_conv_b[j])
                zc = hyena_mix(vc, x0c, B, CTX, D, filt, hy_bias[j])
                hc = matmul_residual(zc, wout, hy_b_out[j], mods, 2, B * CTX, B, hc)
        h = ec_moe(h, norm2_g[i], mods, 3, 4, 5, N, 0, B, moe_router[i], i, w1, w3, w2)
        if ctx_stream:
            hc = ec_moe(hc, norm2_g[i], mods, 3, 4, 5, B * CTX, B, B, moe_router[i], i, w1, w3, w2)
    return h.reshape(B, N, D)
```
